```python
import jax
import jax.numpy as jnp
from jax import lax
import numpy as np

D_MODEL = 2048
BATCH = 2
SEQ = 4096
DEPTH = 1
DEC_BATCH = 32
DEC_SEQ = 1
PAST_LEN = 16384
PAGE_SIZE = 128

DIL_PATTERNS = ((128, 1), (512, 4), (2048, 16))
N_GROUPS = 3
ATT_HEADS = 4
ATT_HEAD_DIM = 128
ATT_SPAN = 128
ATT_SCALE = ATT_HEAD_DIM ** -0.5
ATT_WIDTH = N_GROUPS * ATT_HEADS * ATT_HEAD_DIM
M_HEADS = 4
M_QK_DIM = D_MODEL // (2 * M_HEADS)
M_V_DIM = D_MODEL // M_HEADS
M_QK_WIDTH = M_HEADS * M_QK_DIM
M_V_WIDTH = M_HEADS * M_V_DIM
M_CHUNK = 64
D_FF = ((8 * D_MODEL // 3 + 255) // 256) * 256
CONV_W = 3
RMS_EPS = 1e-6
NEG_INF = -1e30
SPLIT_SIZES = (ATT_WIDTH, ATT_WIDTH, ATT_WIDTH, M_QK_WIDTH, M_QK_WIDTH, M_V_WIDTH, M_V_WIDTH,
               M_HEADS, M_HEADS, D_MODEL, D_MODEL)
IN_COLS = sum(SPLIT_SIZES)

kernel_name = "hybrid_dilated_attn_mlstm_convffn_step"


def rmsnorm(x, g):
    xf = x.astype(jnp.float32)
    y = xf * lax.rsqrt(jnp.mean(xf * xf, axis=-1, keepdims=True) + RMS_EPS)
    return (y * g.astype(jnp.float32)).astype(x.dtype)


def dilated_attn_prompt(q, k, v, dil):
    B, S, G, E = q.shape
    su = S // dil
    blk = ATT_SPAN
    nb = -(-su // blk)
    pad = nb * blk - su

    def to_blocks(a):
        a = a.astype(jnp.float32).reshape(B, su, dil, G, E).transpose(0, 2, 1, 3, 4)
        a = jnp.pad(a, ((0, 0), (0, 0), (0, pad), (0, 0), (0, 0)))
        return a.reshape(B, dil, nb, blk, G, E)

    def with_prev(a):
        prev = jnp.concatenate([jnp.zeros_like(a[:, :, :1]), a[:, :, :-1]], axis=2)
        return jnp.concatenate([prev, a], axis=3)

    qb = to_blocks(q)
    kk = with_prev(to_blocks(k))
    vv = with_prev(to_blocks(v))
    s = jnp.einsum('brcqne,brckne->brcnqk', qb, kk) * ATT_SCALE
    qi = jnp.arange(blk)[:, None]
    kj = jnp.arange(2 * blk)[None, :]
    dist = blk + qi - kj
    cb = jnp.arange(nb)[:, None, None]
    mask = (dist >= 0) & (dist <= blk) & ((cb > 0) | (kj[None] >= blk))
    s = jnp.where(mask[None, None, :, None], s, NEG_INF)
    lse = jax.nn.logsumexp(s, axis=-1)
    p = jnp.exp(s - lse[..., None])
    o = jnp.einsum('brcnqk,brckne->brcqne', p, vv)
    o = o.reshape(B, dil, nb * blk, G, E)[:, :, :su].transpose(0, 2, 1, 3, 4).reshape(B, S, G, E)
    lse = lse.transpose(0, 1, 2, 4, 3).reshape(B, dil, nb * blk, G)[:, :, :su]
    lse = lse.transpose(0, 2, 1, 3).reshape(B, S, G)
    return o, lse


def dilated_attn_sample(q, k_new, v_new, k_buf, v_buf, dil):
    Bd, L, G, E = q.shape
    wb = k_buf.shape[1]
    kc = jnp.concatenate([k_buf.astype(jnp.float32), k_new.astype(jnp.float32)], axis=1)
    vc = jnp.concatenate([v_buf.astype(jnp.float32), v_new.astype(jnp.float32)], axis=1)
    idx = wb + jnp.arange(L)[:, None] - dil * jnp.arange(ATT_SPAN + 1)[None, :]
    valid = idx >= 0
    idx = jnp.maximum(idx, 0)
    kg = kc[:, idx]
    vg = vc[:, idx]
    s = jnp.einsum('blne,bljne->blnj', q.astype(jnp.float32), kg) * ATT_SCALE
    s = jnp.where(valid[None, :, None, :], s, NEG_INF)
    lse = jax.nn.logsumexp(s, axis=-1)
    p = jnp.exp(s - lse[..., None])
    o = jnp.einsum('blnj,bljne->blne', p, vg)
    return o, lse


def mlstm_chunkwise(q, k, v, logi, logf, c0, n0, m0):
    B, S, H, _ = q.shape
    L = min(M_CHUNK, S)
    nc = -(-S // L)
    pad = nc * L - S

    def chunks(a, fill):
        a = jnp.pad(a, [(0, 0), (0, pad)] + [(0, 0)] * (a.ndim - 2), constant_values=fill)
        return jnp.moveaxis(a.reshape(B, nc, L, *a.shape[2:]), 1, 0)

    xs = (chunks(q, 0.0), chunks(k, 0.0), chunks(v, 0.0), chunks(logi, NEG_INF), chunks(logf, 0.0))
    causal = jnp.tril(jnp.ones((L, L), dtype=bool))

    def step(carry, xc):
        c, n, m = carry
        qc, kc, vc, li, lf = xc
        b = jnp.cumsum(lf, axis=1)
        dmat = b[:, :, None] - b[:, None] + li[:, None]
        dmat = jnp.where(causal[None, :, :, None], dmat, NEG_INF)
        inter = b + m[:, None]
        mt = jnp.maximum(inter, dmat.max(axis=2))
        a = jnp.exp(dmat - mt[:, :, None]) * jnp.einsum('btnd,bsnd->btsn', qc, kc)
        w_inter = jnp.exp(inter - mt)
        num = jnp.einsum('btsn,bsnv->btnv', a, vc) + w_inter[..., None] * jnp.einsum('btnd,bndv->btnv', qc, c)
        den = a.sum(axis=2) + w_inter * jnp.einsum('btnd,bnd->btn', qc, n)
        hc = num / jnp.maximum(jnp.abs(den), jnp.exp(-mt))[..., None]
        b_end = b[:, -1]
        g = b_end[:, None] - b + li
        m_new = jnp.maximum(b_end + m, g.max(axis=1))
        decay = jnp.exp(b_end + m - m_new)
        ws = jnp.exp(g - m_new[:, None])
        c_new = decay[..., None, None] * c + jnp.einsum('bsn,bsnd,bsnv->bndv', ws, kc, vc)
        n_new = decay[..., None] * n + jnp.einsum('bsn,bsnd->bnd', ws, kc)
        return (c_new, n_new, m_new), hc

    (c1, n1, m1), hs = lax.scan(step, (c0, n0, m0), xs)
    h = jnp.moveaxis(hs, 0, 1).reshape(B, nc * L, H, -1)[:, :S]
    return h, c1, n1, m1


def hybrid_layer(x, cache, norm_mix, w_in, q_norm, k_norm, b_igate, b_fgate, w_proj_att,
                 w_proj_mlstm, w_out, norm_ffn, w_ffn_in, conv_w, conv_b, w_ffn_down):
    f32 = jnp.float32
    B, L, _ = x.shape
    h = rmsnorm(x, norm_mix)
    z = h @ w_in
    split_points = tuple(int(v) for v in np.cumsum(SPLIT_SIZES)[:-1])
    qa, ka, va, qm, km, vm, om, im, fm, ga, gb = jnp.split(z, split_points, axis=-1)

    att_shape = (B, L, N_GROUPS, ATT_HEADS, ATT_HEAD_DIM)
    qa = rmsnorm(qa.reshape(att_shape), q_norm)
    ka = rmsnorm(ka.reshape(att_shape), k_norm)
    va = va.reshape(att_shape)
    outs, lses, new_state = [], [], []
    for gi, (win, dil) in enumerate(DIL_PATTERNS):
        if cache is None:
            o, lse = dilated_attn_prompt(qa[:, :, gi], ka[:, :, gi], va[:, :, gi], dil)
            keep = min(win, L)
            new_state += [ka[:, L - keep:, gi], va[:, L - keep:, gi]]
        else:
            o, lse = dilated_attn_sample(qa[:, :, gi], ka[:, :, gi], va[:, :, gi],
                                         cache[2 * gi], cache[2 * gi + 1], dil)
            new_state += [ka[:, :, gi], va[:, :, gi]]
        outs.append(o)
        lses.append(lse)
    wgt = jax.nn.softmax(jnp.stack(lses, axis=2), axis=2)
    att = jnp.einsum('blgn,blgne->blne', wgt, jnp.stack(outs, axis=2))
    att = att.reshape(B, L, ATT_HEADS * ATT_HEAD_DIM)

    qm = qm.reshape(B, L, M_HEADS, M_QK_DIM).astype(f32)
    km = km.reshape(B, L, M_HEADS, M_QK_DIM).astype(f32) * (M_QK_DIM ** -0.5)
    vm = vm.reshape(B, L, M_HEADS, M_V_DIM).astype(f32)
    logi = im.astype(f32) + b_igate.astype(f32)
    logf = jax.nn.log_sigmoid(fm.astype(f32) + b_fgate.astype(f32))
    if cache is None:
        c0 = jnp.zeros((B, M_HEADS, M_QK_DIM, M_V_DIM), f32)
        n0 = jnp.zeros((B, M_HEADS, M_QK_DIM), f32)
        m0 = jnp.zeros((B, M_HEADS), f32)
    else:
        c0, n0, m0 = cache[6].astype(f32), cache[7].astype(f32), cache[8].astype(f32)
    hm, c1, n1, m1 = mlstm_chunkwise(qm, km, vm, logi, logf, c0, n0, m0)
    hm = jax.nn.sigmoid(om.astype(f32)) * hm.reshape(B, L, M_V_WIDTH)

    merged = (jax.nn.sigmoid(ga.astype(f32)) * (att @ w_proj_att)
              + jax.nn.sigmoid(gb.astype(f32)) * (hm @ w_proj_mlstm))
    x1 = (x.astype(f32) + merged @ w_out).astype(x.dtype)

    h2 = rmsnorm(x1, norm_ffn)
    u, g = jnp.split(h2 @ w_ffn_in, 2, axis=-1)
    if cache is None:
        buf = jnp.zeros((B, CONV_W - 1, D_FF), g.dtype)
    else:
        buf = cache[9].astype(g.dtype)
    gc = jnp.concatenate([buf, g], axis=1)
    gconv = conv_b + sum(gc[:, j:j + L] * conv_w[j] for j in range(CONV_W))
    y = (x1.astype(f32) + (jax.nn.gelu(gconv) * u) @ w_ffn_down).astype(x.dtype)

    new_state += [c1, n1, m1, gc[:, L:]]
    return y, new_state


def setup_inputs(seed: int = 0) -> dict:
    key = jax.random.key(seed)
    ks = iter(jax.random.split(key, 32))

    def nrm(shape, scale):
        return jax.random.normal(next(ks), shape, jnp.float32) * scale

    rows = [min(w, PAST_LEN) for w, _ in DIL_PATTERNS]

    def cshape(r):
        return (DEPTH, DEC_BATCH, r, ATT_HEADS, ATT_HEAD_DIM)

    return {
        'x_prompt': nrm((BATCH, SEQ, D_MODEL), 1.0),
        'x_sample': nrm((DEC_BATCH, DEC_SEQ, D_MODEL), 1.0),
        'cache_k_w128': nrm(cshape(rows[0]), 1.0),
        'cache_v_w128': nrm(cshape(rows[0]), 1.0),
        'cache_k_w512': nrm(cshape(rows[1]), 1.0),
        'cache_v_w512': nrm(cshape(rows[1]), 1.0),
        'cache_k_w2048': nrm(cshape(rows[2]), 1.0),
        'cache_v_w2048': nrm(cshape(rows[2]), 1.0),
        'state_mlstm_C': nrm((DEPTH, DEC_BATCH, M_HEADS, M_QK_DIM, M_V_DIM), 0.3),
        'state_mlstm_n': nrm((DEPTH, DEC_BATCH, M_HEADS, M_QK_DIM), 0.3),
        'state_mlstm_m': nrm((DEPTH, DEC_BATCH, M_HEADS), 1.0),
        'state_ffn_conv': nrm((DEPTH, DEC_BATCH, CONV_W - 1, D_FF), 1.0),
        'norm_mix': 1.0 + nrm((DEPTH, D_MODEL), 0.02),
        'w_in': nrm((DEPTH, D_MODEL, IN_COLS), D_MODEL ** -0.5),
        'q_norm': 1.0 + nrm((DEPTH, ATT_HEAD_DIM), 0.02),
        'k_norm': 1.0 + nrm((DEPTH, ATT_HEAD_DIM), 0.02),
        'b_igate': nrm((DEPTH, M_HEADS), 0.1),
        'b_fgate': 3.0 + nrm((DEPTH, M_HEADS), 0.1),
        'w_proj_att': nrm((DEPTH, ATT_HEADS * ATT_HEAD_DIM, D_MODEL), (ATT_HEADS * ATT_HEAD_DIM) ** -0.5),
        'w_proj_mlstm': nrm((DEPTH, M_V_WIDTH, D_MODEL), M_V_WIDTH ** -0.5),
        'w_out': nrm((DEPTH, D_MODEL, D_MODEL), D_MODEL ** -0.5),
        'norm_ffn': 1.0 + nrm((DEPTH, D_MODEL), 0.02),
        'w_ffn_in': nrm((DEPTH, D_MODEL, 2 * D_FF), D_MODEL ** -0.5),
        'conv_w': nrm((DEPTH, CONV_W, D_FF), CONV_W ** -0.5),
        'conv_b': nrm((DEPTH, D_FF), 0.02),
        'w_ffn_down': nrm((DEPTH, D_FF, D_MODEL), D_FF ** -0.5),
    }


def reference(x_prompt, x_sample, cache_k_w128, cache_v_w128, cache_k_w512, cache_v_w512,
              cache_k_w2048, cache_v_w2048, state_mlstm_C, state_mlstm_n, state_mlstm_m,
              state_ffn_conv, norm_mix, w_in, q_norm, k_norm, b_igate, b_fgate, w_proj_att,
              w_proj_mlstm, w_out, norm_ffn, w_ffn_in, conv_w, conv_b, w_ffn_down):
    caches = (cache_k_w128, cache_v_w128, cache_k_w512, cache_v_w512, cache_k_w2048, cache_v_w2048,
              state_mlstm_C, state_mlstm_n, state_mlstm_m, state_ffn_conv)
    weights = (norm_mix, w_in, q_norm, k_norm, b_igate, b_fgate, w_proj_att, w_proj_mlstm, w_out,
               norm_ffn, w_ffn_in, conv_w, conv_b, w_ffn_down)
    y_prompt, y_sample = x_prompt, x_sample
    p_layers, s_layers = [], []
    for layer in range(DEPTH):
        lw = [w[layer] for w in weights]
        y_prompt, p_new = hybrid_layer(y_prompt, None, *lw)
        y_sample, s_new = hybrid_layer(y_sample, [c[layer] for c in caches], *lw)
        p_layers.append(p_new)
        s_layers.append(s_new)
    (p_k128, p_v128, p_k512, p_v512, p_k2048, p_v2048,
     p_C, p_n, p_m, p_conv) = [jnp.stack(a) for a in zip(*p_layers)]
    (s_k128, s_v128, s_k512, s_v512, s_k2048, s_v2048,
     s_C, s_n, s_m, s_conv) = [jnp.stack(a) for a in zip(*s_layers)]
    return (y_prompt, y_sample,
            p_k128, p_v128, p_k512, p_v512, p_k2048, p_v2048, p_C, p_n, p_m, p_conv,
            s_k128, s_v128, s_k512, s_v512, s_k2048, s_v2048, s_C, s_n, s_m, s_conv)
```

```python
import functools

import numpy as np
import jax
import jax.numpy as jnp
from jax import lax
from jax.experimental import pallas as pl
from jax.experimental.pallas import tpu as pltpu

F32 = jnp.float32
BF16 = jnp.bfloat16

RMS_EPS = 1e-6
NEG_INF = -1e30
LANES = 128
VMEM_LIMIT = 56 * 1024 * 1024

D_MODEL = 2048
DIL_PATTERNS = ((128, 1), (512, 4), (2048, 16))
N_GROUPS = 3
ATT_HEADS = 4
ATT_HEAD_DIM = 128
ATT_SPAN = 128
ATT_SCALE = ATT_HEAD_DIM ** -0.5
ATT_GROUP_W = ATT_HEADS * ATT_HEAD_DIM
ATT_WIDTH = N_GROUPS * ATT_GROUP_W
M_HEADS = 4
M_QK_DIM = D_MODEL // (2 * M_HEADS)
M_V_DIM = D_MODEL // M_HEADS
M_QK_WIDTH = M_HEADS * M_QK_DIM
M_V_WIDTH = M_HEADS * M_V_DIM
M_K_SCALE = M_QK_DIM ** -0.5
D_FF = ((8 * D_MODEL // 3 + 255) // 256) * 256
CONV_W = 3
SPLIT_SIZES = (ATT_WIDTH, ATT_WIDTH, ATT_WIDTH, M_QK_WIDTH, M_QK_WIDTH, M_V_WIDTH, M_V_WIDTH,
               M_HEADS, M_HEADS, D_MODEL, D_MODEL)

COL_GA = 0
COL_GB = COL_GA + D_MODEL
COL_OM = COL_GB + D_MODEL
COL_VM = COL_OM + M_V_WIDTH
COL_QA = COL_VM + M_V_WIDTH
COL_KA = COL_QA + ATT_WIDTH
COL_VA = COL_KA + ATT_WIDTH
COL_QM = COL_VA + ATT_WIDTH
COL_KM = COL_QM + M_QK_WIDTH
COL_IF = COL_KM + M_QK_WIDTH
IF_PAD = 512
NZ = COL_IF + IF_PAD


def _cparams(sem):
    return pltpu.CompilerParams(dimension_semantics=sem, vmem_limit_bytes=VMEM_LIMIT)


def _log_sigmoid(x):
    return jnp.minimum(x, 0.0) - jnp.log(1.0 + jnp.exp(-jnp.abs(x)))


def _in_proj_kernel(x_ref, g_ref, w_ref, qkg_ref, o_ref, h_ref, *, norm_lo, norm_hi):
    j = pl.program_id(1)

    @pl.when(j == 0)
    def _():
        x = x_ref[...]
        ms = jnp.mean(x * x, axis=-1, keepdims=True)
        h_ref[...] = (x * lax.rsqrt(ms + RMS_EPS) * g_ref[...]).astype(BF16)

    o_ref[...] = jnp.dot(h_ref[...], w_ref[...], preferred_element_type=F32)

    @pl.when((j >= norm_lo) & (j < norm_hi))
    def _():
        for c in range(o_ref.shape[1] // LANES):
            sl = slice(c * LANES, (c + 1) * LANES)
            zc = o_ref[:, sl]
            ms = jnp.mean(zc * zc, axis=-1, keepdims=True)
            o_ref[:, sl] = zc * lax.rsqrt(ms + RMS_EPS) * qkg_ref[:, sl]


def _in_proj(x, gain, w, qk_gain, *, tm, tn):
    m, d = x.shape
    nz = w.shape[1]
    assert m % tm == 0 and nz % tn == 0 and COL_QA % tn == 0 and COL_VA % tn == 0
    kern = functools.partial(_in_proj_kernel, norm_lo=COL_QA // tn, norm_hi=COL_VA // tn)
    return pl.pallas_call(
        kern,
        grid=(m // tm, nz // tn),
        in_specs=[
            pl.BlockSpec((tm, d), lambda i, j: (i, 0)),
            pl.BlockSpec((1, d), lambda i, j: (0, 0)),
            pl.BlockSpec((d, tn), lambda i, j: (0, j)),
            pl.BlockSpec((1, tn), lambda i, j: (0, j)),
        ],
        out_specs=pl.BlockSpec((tm, tn), lambda i, j: (i, j)),
        out_shape=jax.ShapeDtypeStruct((m, nz), F32),
        scratch_shapes=[pltpu.VMEM((tm, d), BF16)],
        compiler_params=_cparams(("parallel", "arbitrary")),
        name="in_proj",
    )(x, gain, w, qk_gain)


def _attn_prompt_kernel(q_ref, kp_ref, kc_ref, vp_ref, vc_ref, o_ref, lse_ref):
    c = pl.program_id(2)
    blk = q_ref.shape[0]
    qi = lax.broadcasted_iota(jnp.int32, (blk, blk), 0)
    kj = lax.broadcasted_iota(jnp.int32, (blk, blk), 1)
    cur_ok = kj <= qi
    prev_ok = kj >= qi
    prev_bias = jnp.where(c > 0, 0.0, NEG_INF)
    lane = lax.broadcasted_iota(jnp.int32, (blk, LANES), 1)
    lse_tile = jnp.zeros((blk, LANES), F32)
    nt = (((1,), (1,)), ((), ()))
    for n in range(ATT_HEADS):
        sl = slice(n * ATT_HEAD_DIM, (n + 1) * ATT_HEAD_DIM)
        q = q_ref[:, sl].astype(BF16)
        s_c = lax.dot_general(q, kc_ref[:, sl].astype(BF16), nt, preferred_element_type=F32) * ATT_SCALE
        s_p = lax.dot_general(q, kp_ref[:, sl].astype(BF16), nt, preferred_element_type=F32) * ATT_SCALE
        s_c = jnp.where(cur_ok, s_c, NEG_INF)
        s_p = jnp.where(prev_ok, s_p + prev_bias, NEG_INF)
        m = jnp.maximum(jnp.max(s_c, axis=-1, keepdims=True), jnp.max(s_p, axis=-1, keepdims=True))
        p_c = jnp.exp(s_c - m)
        p_p = jnp.exp(s_p - m)
        l = jnp.sum(p_c, axis=-1, keepdims=True) + jnp.sum(p_p, axis=-1, keepdims=True)
        o = (jnp.dot(p_c.astype(BF16), vc_ref[:, sl].astype(BF16), preferred_element_type=F32)
             + jnp.dot(p_p.astype(BF16), vp_ref[:, sl].astype(BF16), preferred_element_type=F32))
        o_ref[:, sl] = o / l
        lse_tile = jnp.where(lane == n, m + jnp.log(l), lse_tile)
    lse_ref[...] = lse_tile


def _attn_prompt_group(z, batch, seq, gi, dil):
    su = seq // dil
    blk = ATT_SPAN
    assert su % blk == 0
    nb = su // blk
    per_row = NZ // ATT_GROUP_W
    z3 = z.reshape(batch, su, dil * NZ)
    qc = COL_QA // ATT_GROUP_W + gi
    kc = COL_KA // ATT_GROUP_W + gi
    vc = COL_VA // ATT_GROUP_W + gi

    def cur(col):
        return pl.BlockSpec((None, blk, ATT_GROUP_W), lambda b, r, c: (b, c, r * per_row + col))

    def prev(col):
        return pl.BlockSpec((None, blk, ATT_GROUP_W), lambda b, r, c: (b, jnp.maximum(c - 1, 0), r * per_row + col))

    o, lse = pl.pallas_call(
        _attn_prompt_kernel,
        grid=(batch, dil, nb),
        in_specs=[cur(qc), prev(kc), cur(kc), prev(vc), cur(vc)],
        out_specs=[pl.BlockSpec((None, blk, ATT_GROUP_W), lambda b, r, c: (b, c, r)),
                   pl.BlockSpec((None, blk, LANES), lambda b, r, c: (b, c, r))],
        out_shape=[jax.ShapeDtypeStruct((batch, su, dil * ATT_GROUP_W), F32),
                   jax.ShapeDtypeStruct((batch, su, dil * LANES), F32)],
        compiler_params=_cparams(("parallel", "parallel", "arbitrary")),
        name=f"attn_prompt_g{gi}",
    )(z3, z3, z3, z3, z3)
    return o.reshape(batch * seq, ATT_GROUP_W), lse.reshape(batch * seq, LANES)


def _attn_sample_kernel(z_ref, k1, v1, k2, v2, k3, v3, o1, o2, o3, l1, l2, l3):
    bb = z_ref.shape[0]
    bufs = ((k1, v1, o1, l1), (k2, v2, o2, l2), (k3, v3, o3, l3))
    lane = lax.broadcasted_iota(jnp.int32, (1, LANES), 1)

    def body(b, carry):
        for gi, (k_ref, v_ref, o_ref, l_ref) in enumerate(bufs):
            lse_row = jnp.zeros((1, LANES), F32)
            for n in range(ATT_HEADS):
                off = gi * ATT_GROUP_W + n * ATT_HEAD_DIM
                sl = slice(n * ATT_HEAD_DIM, (n + 1) * ATT_HEAD_DIM)
                q = z_ref[b, :, COL_QA + off:COL_QA + off + ATT_HEAD_DIM]
                k_new = z_ref[b, :, COL_KA + off:COL_KA + off + ATT_HEAD_DIM]
                v_new = z_ref[b, :, COL_VA + off:COL_VA + off + ATT_HEAD_DIM]
                kb = k_ref[b, :, sl]
                vb = v_ref[b, :, sl]
                s = jnp.sum(kb * q, axis=-1, keepdims=True) * ATT_SCALE
                s_new = jnp.sum(k_new * q, axis=-1, keepdims=True) * ATT_SCALE
                m = jnp.maximum(jnp.max(s, axis=0, keepdims=True), s_new)
                p = jnp.exp(s - m)
                p_new = jnp.exp(s_new - m)
                l = jnp.sum(p, axis=0, keepdims=True) + p_new
                o = jnp.sum(p * vb, axis=0, keepdims=True) + p_new * v_new
                o_ref[b, :, sl] = o / l
                lse_row = jnp.where(lane == n, m + jnp.log(l), lse_row)
            l_ref[b] = lse_row
        return carry

    lax.fori_loop(0, bb, body, 0)


def _attn_sample(z_s, caches, *, bb):
    bd = z_s.shape[0]
    assert bd % bb == 0
    ins, in_specs = [z_s.reshape(bd, 1, NZ)], [pl.BlockSpec((bb, 1, NZ), lambda i: (i, 0, 0))]
    for (k_buf, v_buf), (win, dil) in zip(caches, DIL_PATTERNS):
        assert k_buf.shape[1] == ATT_SPAN * dil
        for buf in (k_buf, v_buf):
            ins.append(buf.reshape(bd, ATT_SPAN, dil * ATT_GROUP_W))
            in_specs.append(pl.BlockSpec((bb, ATT_SPAN, ATT_GROUP_W), lambda i: (i, 0, 0)))
    outs = pl.pallas_call(
        _attn_sample_kernel,
        grid=(bd // bb,),
        in_specs=in_specs,
        out_specs=[pl.BlockSpec((bb, 1, ATT_GROUP_W), lambda i: (i, 0, 0))] * 3
                  + [pl.BlockSpec((bb, 1, LANES), lambda i: (i, 0, 0))] * 3,
        out_shape=[jax.ShapeDtypeStruct((bd, 1, ATT_GROUP_W), F32)] * 3
                  + [jax.ShapeDtypeStruct((bd, 1, LANES), F32)] * 3,
        compiler_params=_cparams(("parallel",)),
        name="attn_sample",
    )(*ins)
    return [o[:, 0, :] for o in outs[:3]], [l[:, 0, :] for l in outs[3:]]


def _mlstm_prompt_kernel(q_ref, k_ref, v_ref, om_ref, g_ref, bias_ref, h_ref, c_out, n_out, m_out,
                         c_s, n_s, m_s):
    hd = pl.program_id(1)
    ci = pl.program_id(2)
    L = q_ref.shape[0]

    @pl.when(ci == 0)
    def _():
        c_s[...] = jnp.zeros_like(c_s)
        n_s[...] = jnp.zeros_like(n_s)
        m_s[...] = jnp.zeros_like(m_s)

    gates = g_ref[...] + bias_ref[...]
    lane = lax.broadcasted_iota(jnp.int32, (L, LANES), 1)
    li_col = jnp.sum(jnp.where(lane == hd, gates, 0.0), axis=1, keepdims=True)
    lf_col = _log_sigmoid(jnp.sum(jnp.where(lane == hd + M_HEADS, gates, 0.0), axis=1, keepdims=True))
    gates_t = gates.T
    sub = lax.broadcasted_iota(jnp.int32, (LANES, L), 0)
    li_row = jnp.sum(jnp.where(sub == hd, gates_t, 0.0), axis=0, keepdims=True)
    lf_row = _log_sigmoid(jnp.sum(jnp.where(sub == hd + M_HEADS, gates_t, 0.0), axis=0, keepdims=True))

    ti = lax.broadcasted_iota(jnp.int32, (L, L), 0)
    si = lax.broadcasted_iota(jnp.int32, (L, L), 1)
    causal = si <= ti
    b_col = jnp.sum(jnp.where(causal, lf_row, 0.0), axis=1, keepdims=True)
    b_row = jnp.sum(jnp.where(ti <= si, lf_col, 0.0), axis=0, keepdims=True)
    b_end = jnp.sum(lf_row, axis=1, keepdims=True)

    m_prev = m_s[...]
    dmat = jnp.where(causal, b_col - b_row + li_row, NEG_INF)
    inter = b_col + m_prev
    mt = jnp.maximum(inter, jnp.max(dmat, axis=1, keepdims=True))

    q = q_ref[...]
    k = k_ref[...] * M_K_SCALE
    qb = q.astype(BF16)
    vb = v_ref[...].astype(BF16)
    qk = lax.dot_general(qb, k.astype(BF16), (((1,), (1,)), ((), ())), preferred_element_type=F32)
    a = jnp.exp(dmat - mt) * qk
    w_inter = jnp.exp(inter - mt)
    num = (jnp.dot(a.astype(BF16), vb, preferred_element_type=F32)
           + w_inter * jnp.dot(qb, c_s[...].astype(BF16), preferred_element_type=F32))
    den = jnp.sum(a, axis=1, keepdims=True) + w_inter * jnp.sum(q * n_s[...], axis=1, keepdims=True)
    h = num / jnp.maximum(jnp.abs(den), jnp.exp(-mt))
    h_ref[...] = (jax.nn.sigmoid(om_ref[...]) * h).astype(h_ref.dtype)

    g_col = b_end - b_col + li_col
    g_row = b_end - b_row + li_row
    m_new = jnp.maximum(b_end + m_prev, jnp.max(g_row, axis=1, keepdims=True))
    decay = jnp.exp(b_end + m_prev - m_new)
    kw = jnp.exp(g_col - m_new) * k
    c_s[...] = decay * c_s[...] + jnp.dot(kw.T.astype(BF16), vb, preferred_element_type=F32)
    n_s[...] = decay * n_s[...] + jnp.sum(kw, axis=0, keepdims=True)
    m_s[...] = m_new

    @pl.when(ci == pl.num_programs(2) - 1)
    def _():
        c_out[...] = c_s[...]
        n_out[...] = n_s[...]
        m_out[...] = jnp.broadcast_to(m_s[...], m_out.shape)


def _mlstm_prompt(z, gate_bias, batch, seq, *, chunk):
    assert seq % chunk == 0
    nc = seq // chunk
    z3 = z.reshape(batch, seq, NZ)
    qk_blk = lambda col: pl.BlockSpec((None, chunk, M_QK_DIM), lambda b, h, c: (b, c, col // M_QK_DIM + h))
    v_blk = lambda col: pl.BlockSpec((None, chunk, M_V_DIM), lambda b, h, c: (b, c, col // M_V_DIM + h))
    hm, c1, n1, m1 = pl.pallas_call(
        _mlstm_prompt_kernel,
        grid=(batch, M_HEADS, nc),
        in_specs=[qk_blk(COL_QM), qk_blk(COL_KM), v_blk(COL_VM), v_blk(COL_OM),
                  pl.BlockSpec((None, chunk, LANES), lambda b, h, c: (b, c, COL_IF // LANES)),
                  pl.BlockSpec((1, LANES), lambda b, h, c: (0, 0))],
        out_specs=[pl.BlockSpec((None, chunk, M_V_DIM), lambda b, h, c: (b, c, h)),
                   pl.BlockSpec((None, None, M_QK_DIM, M_V_DIM), lambda b, h, c: (b, h, 0, 0)),
                   pl.BlockSpec((None, None, 1, M_QK_DIM), lambda b, h, c: (b, h, 0, 0)),
                   pl.BlockSpec((None, None, 1, LANES), lambda b, h, c: (b, h, 0, 0))],
        out_shape=[jax.ShapeDtypeStruct((batch, seq, M_V_WIDTH), BF16),
                   jax.ShapeDtypeStruct((batch, M_HEADS, M_QK_DIM, M_V_DIM), F32),
                   jax.ShapeDtypeStruct((batch, M_HEADS, 1, M_QK_DIM), F32),
                   jax.ShapeDtypeStruct((batch, M_HEADS, 1, LANES), F32)],
        scratch_shapes=[pltpu.VMEM((M_QK_DIM, M_V_DIM), F32), pltpu.VMEM((1, M_QK_DIM), F32),
                        pltpu.VMEM((1, 1), F32)],
        compiler_params=_cparams(("parallel", "parallel", "arbitrary")),
        name="mlstm_prompt",
    )(z3, z3, z3, z3, z3, gate_bias)
    return hm.reshape(batch * seq, M_V_WIDTH), c1, n1[:, :, 0, :], m1[:, :, 0, 0]


def _row_to_col(row):
    n = row.shape[1]
    return jnp.broadcast_to(row, (LANES, n)).T[:, 0:1]


def _mlstm_sample_kernel(z_ref, bias_ref, c_ref, n_ref, m_ref, h_ref, c_out, n_out, m_out):
    gates = z_ref[:, COL_IF:COL_IF + LANES] + bias_ref[...]
    lane = lax.broadcasted_iota(jnp.int32, (1, LANES), 1)
    m_row = jnp.zeros((1, LANES), F32)
    for h in range(M_HEADS):
        q = z_ref[:, COL_QM + h * M_QK_DIM:COL_QM + (h + 1) * M_QK_DIM]
        k = z_ref[:, COL_KM + h * M_QK_DIM:COL_KM + (h + 1) * M_QK_DIM] * M_K_SCALE
        v = z_ref[:, COL_VM + h * M_V_DIM:COL_VM + (h + 1) * M_V_DIM]
        om = z_ref[:, COL_OM + h * M_V_DIM:COL_OM + (h + 1) * M_V_DIM]
        li = gates[:, h:h + 1]
        lf = _log_sigmoid(gates[:, M_HEADS + h:M_HEADS + h + 1])
        m0 = m_ref[:, h:h + 1]
        c0 = c_ref[h]
        n0 = n_ref[h:h + 1, :]
        inter = lf + m0
        mt = jnp.maximum(inter, li)
        a = jnp.exp(li - mt) * jnp.sum(q * k, axis=1, keepdims=True)
        w_inter = jnp.exp(inter - mt)
        q_c = jnp.sum(_row_to_col(q) * c0, axis=0, keepdims=True)
        num = a * v + w_inter * q_c
        den = a + w_inter * jnp.sum(q * n0, axis=1, keepdims=True)
        hv = num / jnp.maximum(jnp.abs(den), jnp.exp(-mt))
        h_ref[:, h * M_V_DIM:(h + 1) * M_V_DIM] = (jax.nn.sigmoid(om) * hv).astype(h_ref.dtype)
        m_new = jnp.maximum(inter, li)
        decay = jnp.exp(inter - m_new)
        ws = jnp.exp(li - m_new)
        c_out[h] = decay * c0 + _row_to_col(ws * k) * v
        n_out[h:h + 1, :] = decay * n0 + ws * k
        m_row = jnp.where(lane == h, m_new, m_row)
    m_out[...] = m_row


def _mlstm_sample(z_s, gate_bias, c0, n0, m0):
    bd = z_s.shape[0]
    hm, c1, n1, m1 = pl.pallas_call(
        _mlstm_sample_kernel,
        grid=(bd,),
        in_specs=[pl.BlockSpec((None, 1, NZ), lambda b: (b, 0, 0)),
                  pl.BlockSpec((1, LANES), lambda b: (0, 0)),
                  pl.BlockSpec((None, M_HEADS, M_QK_DIM, M_V_DIM), lambda b: (b, 0, 0, 0)),
                  pl.BlockSpec((None, M_HEADS, M_QK_DIM), lambda b: (b, 0, 0)),
                  pl.BlockSpec((None, 1, M_HEADS), lambda b: (b, 0, 0))],
        out_specs=[pl.BlockSpec((None, 1, M_V_WIDTH), lambda b: (b, 0, 0)),
                   pl.BlockSpec((None, M_HEADS, M_QK_DIM, M_V_DIM), lambda b: (b, 0, 0, 0)),
                   pl.BlockSpec((None, M_HEADS, M_QK_DIM), lambda b: (b, 0, 0)),
                   pl.BlockSpec((None, 1, LANES), lambda b: (b, 0, 0))],
        out_shape=[jax.ShapeDtypeStruct((bd, 1, M_V_WIDTH), BF16),
                   jax.ShapeDtypeStruct((bd, M_HEADS, M_QK_DIM, M_V_DIM), F32),
                   jax.ShapeDtypeStruct((bd, M_HEADS, M_QK_DIM), F32),
                   jax.ShapeDtypeStruct((bd, 1, LANES), F32)],
        compiler_params=_cparams(("parallel",)),
        name="mlstm_sample",
    )(z_s.reshape(bd, 1, NZ), gate_bias, c0, n0, m0.reshape(bd, 1, M_HEADS))
    return hm.reshape(bd, M_V_WIDTH), c1, n1, m1[:, 0, :M_HEADS]


def _merge_kernel(o1, o2, o3, l1, l2, l3, hm_ref, ga_ref, gb_ref, x_ref, wpa_ref, wpm_ref, wo_ref, g2_ref,
                  x1_ref, h2_ref):
    ls = [l1[...], l2[...], l3[...]]
    mx = jnp.maximum(jnp.maximum(ls[0], ls[1]), ls[2])
    es = [jnp.exp(l - mx) for l in ls]
    tot = es[0] + es[1] + es[2]
    wg = [e / tot for e in es]
    parts = []
    for n in range(ATT_HEADS):
        sl = slice(n * ATT_HEAD_DIM, (n + 1) * ATT_HEAD_DIM)
        parts.append(wg[0][:, n:n + 1] * o1[:, sl] + wg[1][:, n:n + 1] * o2[:, sl] + wg[2][:, n:n + 1] * o3[:, sl])
    att = jnp.concatenate(parts, axis=1).astype(BF16)
    pa = jnp.dot(att, wpa_ref[...], preferred_element_type=F32)
    pm = jnp.dot(hm_ref[...], wpm_ref[...], preferred_element_type=F32)
    merged = jax.nn.sigmoid(ga_ref[...]) * pa + jax.nn.sigmoid(gb_ref[...]) * pm
    x1 = x_ref[...] + jnp.dot(merged.astype(BF16), wo_ref[...], preferred_element_type=F32)
    x1_ref[...] = x1
    ms = jnp.mean(x1 * x1, axis=-1, keepdims=True)
    h2_ref[...] = (x1 * lax.rsqrt(ms + RMS_EPS) * g2_ref[...]).astype(BF16)


def _merge(o_list, lse_list, hm, z, x, wpa, wpm, wo, g2, *, tm):
    m, d = x.shape
    assert m % tm == 0
    row_blk = lambda w: pl.BlockSpec((tm, w), lambda i: (i, 0))
    const = lambda shape: pl.BlockSpec(shape, lambda i: (0, 0), pipeline_mode=pl.Buffered(1))
    return pl.pallas_call(
        _merge_kernel,
        grid=(m // tm,),
        in_specs=[row_blk(ATT_GROUP_W)] * 3 + [row_blk(LANES)] * 3
                 + [row_blk(M_V_WIDTH),
                    pl.BlockSpec((tm, d), lambda i: (i, COL_GA // d)),
                    pl.BlockSpec((tm, d), lambda i: (i, COL_GB // d)),
                    row_blk(d),
                    const(wpa.shape), const(wpm.shape), const(wo.shape), const((1, d))],
        out_specs=[row_blk(d), row_blk(d)],
        out_shape=[jax.ShapeDtypeStruct((m, d), F32), jax.ShapeDtypeStruct((m, d), BF16)],
        compiler_params=_cparams(("parallel",)),
        name="merge",
    )(*o_list, *lse_list, hm, z, z, x, wpa, wpm, wo, g2)


def _ffn_body(h2_ref, wu_ref, wg_ref, cw_ref, cb_ref, wd_ref, x1_ref, y_ref, g_prev2, g_prev1):
    j = pl.program_id(1)
    h2 = h2_ref[...]
    u = jnp.dot(h2, wu_ref[...], preferred_element_type=F32)
    g = jnp.dot(h2, wg_ref[...], preferred_element_type=F32)
    gconv = cb_ref[...] + ((g_prev2(g) * cw_ref[0:1, :] + g_prev1(g) * cw_ref[1:2, :]) + g * cw_ref[2:3, :])
    act = (jax.nn.gelu(gconv) * u).astype(BF16)
    down = jnp.dot(act, wd_ref[...], preferred_element_type=F32)

    @pl.when(j == 0)
    def _():
        y_ref[...] = x1_ref[...] + down

    @pl.when(j > 0)
    def _():
        y_ref[...] += down

    return g


def _ffn_prompt_kernel(h2_ref, wu_ref, wg_ref, cw_ref, cb_ref, wd_ref, x1_ref, y_ref, tail_ref, prev_s,
                       *, tiles_per_seq):
    i = pl.program_id(0)
    j = pl.program_id(1)
    tm = h2_ref.shape[0]
    prev = jnp.where(i % tiles_per_seq == 0, 0.0, prev_s[j])
    p2, p1 = prev[6:7, :], prev[7:8, :]

    def g_prev1(g):
        r = lax.broadcasted_iota(jnp.int32, g.shape, 0)
        return jnp.where(r == 0, p1, pltpu.roll(g, 1, axis=0))

    def g_prev2(g):
        r = lax.broadcasted_iota(jnp.int32, g.shape, 0)
        return jnp.where(r == 0, p2, jnp.where(r == 1, p1, pltpu.roll(g, 2, axis=0)))

    g = _ffn_body(h2_ref, wu_ref, wg_ref, cw_ref, cb_ref, wd_ref, x1_ref, y_ref, g_prev2, g_prev1)
    prev_s[j] = g[tm - 8:tm, :]
    tail_ref[...] = g[tm - 8:tm, :]


def _ffn_sample_kernel(h2_ref, wu_ref, wg_ref, cw_ref, cb_ref, wd_ref, x1_ref, b2_ref, b1_ref, y_ref, g_ref):
    g = _ffn_body(h2_ref, wu_ref, wg_ref, cw_ref, cb_ref, wd_ref, x1_ref, y_ref,
                  lambda g: b2_ref[...], lambda g: b1_ref[...])
    g_ref[...] = g


def _ffn_specs(tm, tf, d, nf):
    return [pl.BlockSpec((tm, d), lambda i, j: (i, 0)),
            pl.BlockSpec((d, tf), lambda i, j: (0, j)),
            pl.BlockSpec((d, tf), lambda i, j: (0, nf + j)),
            pl.BlockSpec((CONV_W, tf), lambda i, j: (0, j)),
            pl.BlockSpec((1, tf), lambda i, j: (0, j)),
            pl.BlockSpec((tf, d), lambda i, j: (j, 0)),
            pl.BlockSpec((tm, d), lambda i, j: (i, 0))]


def _ffn_prompt(h2, x1, w_in, conv_w, conv_b, w_down, seq, *, tm, tf):
    m, d = x1.shape
    assert m % tm == 0 and seq % tm == 0 and D_FF % tf == 0 and tm % 8 == 0
    nf = D_FF // tf
    kern = functools.partial(_ffn_prompt_kernel, tiles_per_seq=seq // tm)
    y, tails = pl.pallas_call(
        kern,
        grid=(m // tm, nf),
        in_specs=_ffn_specs(tm, tf, d, nf),
        out_specs=[pl.BlockSpec((tm, d), lambda i, j: (i, 0)),
                   pl.BlockSpec((None, 8, tf), lambda i, j: (i, 0, j))],
        out_shape=[jax.ShapeDtypeStruct((m, d), F32), jax.ShapeDtypeStruct((m // tm, 8, D_FF), F32)],
        scratch_shapes=[pltpu.VMEM((nf, 8, tf), F32)],
        compiler_params=_cparams(("arbitrary", "arbitrary")),
        name="ffn_prompt",
    )(h2, w_in, w_in, conv_w, conv_b, w_down, x1)
    return y, tails


def _ffn_sample(h2, x1, w_in, conv_w, conv_b, w_down, conv_buf, *, tf):
    m, d = x1.shape
    nf = D_FF // tf
    buf2d = conv_buf.reshape(m, (CONV_W - 1) * D_FF)
    return pl.pallas_call(
        _ffn_sample_kernel,
        grid=(1, nf),
        in_specs=_ffn_specs(m, tf, d, nf) + [pl.BlockSpec((m, tf), lambda i, j: (0, j)),
                                             pl.BlockSpec((m, tf), lambda i, j: (0, nf + j))],
        out_specs=[pl.BlockSpec((m, d), lambda i, j: (i, 0)), pl.BlockSpec((m, tf), lambda i, j: (0, j))],
        out_shape=[jax.ShapeDtypeStruct((m, d), F32), jax.ShapeDtypeStruct((m, D_FF), F32)],
        compiler_params=_cparams(("arbitrary", "arbitrary")),
        name="ffn_sample",
    )(h2, w_in, w_in, conv_w, conv_b, w_down, x1, buf2d, buf2d)


def _prep_w_in(w_in):
    pts = tuple(int(v) for v in np.cumsum(SPLIT_SIZES)[:-1])
    qa, ka, va, qm, km, vm, om, im, fm, ga, gb = jnp.split(w_in, pts, axis=1)
    pad = jnp.zeros((w_in.shape[0], IF_PAD - 2 * M_HEADS), w_in.dtype)
    return jnp.concatenate([ga, gb, om, vm, qa, ka, va, qm, km, im, fm, pad], axis=1).astype(BF16)


def _qk_gain_row(q_norm, k_norm):
    reps = N_GROUPS * ATT_HEADS
    return jnp.concatenate([jnp.zeros((COL_QA,), F32), jnp.tile(q_norm, reps), jnp.tile(k_norm, reps),
                            jnp.zeros((NZ - COL_VA,), F32)])[None, :]


def _layer(x_prompt, x_sample, caches, norm_mix, w_in, q_norm, k_norm, b_igate, b_fgate, w_proj_att,
           w_proj_mlstm, w_out, norm_ffn, w_ffn_in, conv_w, conv_b, w_ffn_down):
    batch, seq, d = x_prompt.shape
    bd = x_sample.shape[0]
    assert x_sample.shape[1] == 1 and d == D_MODEL
    (ck1, cv1, ck2, cv2, ck3, cv3, st_c, st_n, st_m, st_conv) = caches

    w_z = _prep_w_in(w_in)
    qk_gain = _qk_gain_row(q_norm, k_norm)
    g1 = norm_mix[None, :]
    g2 = norm_ffn[None, :]
    gate_bias = jnp.concatenate([b_igate, b_fgate, jnp.zeros((LANES - 2 * M_HEADS,), F32)])[None, :]
    wpa = w_proj_att.astype(BF16)
    wpm = w_proj_mlstm.astype(BF16)
    wo = w_out.astype(BF16)
    w_ff = w_ffn_in.astype(BF16)
    w_dn = w_ffn_down.astype(BF16)
    cb = conv_b[None, :]

    xp = x_prompt.reshape(batch * seq, d)
    xs = x_sample.reshape(bd, d)

    z_p = _in_proj(xp, g1, w_z, qk_gain, tm=1024, tn=1024)
    o_p, lse_p = zip(*[_attn_prompt_group(z_p, batch, seq, gi, dil) for gi, (_, dil) in enumerate(DIL_PATTERNS)])
    hm_p, p_c, p_n, p_m = _mlstm_prompt(z_p, gate_bias, batch, seq, chunk=256)
    x1_p, h2_p = _merge(o_p, lse_p, hm_p, z_p, xp, wpa, wpm, wo, g2, tm=256)
    y_p, tails = _ffn_prompt(h2_p, x1_p, w_ff, conv_w, cb, w_dn, seq, tm=512, tf=512)

    z_p3 = z_p.reshape(batch, seq, NZ)
    p_kv = []
    for gi, (win, _) in enumerate(DIL_PATTERNS):
        keep = min(win, seq)
        for col in (COL_KA, COL_VA):
            lo = col + gi * ATT_GROUP_W
            p_kv.append(z_p3[:, seq - keep:, lo:lo + ATT_GROUP_W].reshape(batch, keep, ATT_HEADS, ATT_HEAD_DIM))
    tiles_per_seq = seq // 512
    p_conv = tails.reshape(batch, tiles_per_seq, 8, D_FF)[:, -1, 8 - (CONV_W - 1):, :]

    z_s = _in_proj(xs, g1, w_z, qk_gain, tm=bd, tn=1024)
    o_s, lse_s = _attn_sample(z_s, [(ck1, cv1), (ck2, cv2), (ck3, cv3)], bb=8)
    hm_s, s_c, s_n, s_m = _mlstm_sample(z_s, gate_bias, st_c, st_n, st_m)
    x1_s, h2_s = _merge(o_s, lse_s, hm_s, z_s, xs, wpa, wpm, wo, g2, tm=bd)
    y_s, g_s = _ffn_sample(h2_s, x1_s, w_ff, conv_w, cb, w_dn, st_conv, tf=512)

    s_kv = []
    for gi in range(N_GROUPS):
        for col in (COL_KA, COL_VA):
            lo = col + gi * ATT_GROUP_W
            s_kv.append(z_s[:, lo:lo + ATT_GROUP_W].reshape(bd, 1, ATT_HEADS, ATT_HEAD_DIM))
    s_conv = jnp.stack([st_conv[:, 1, :], g_s], axis=1)

    p_state = p_kv + [p_c, p_n, p_m, p_conv]
    s_state = s_kv + [s_c, s_n, s_m, s_conv]
    return y_p.reshape(batch, seq, d), y_s.reshape(bd, 1, d), p_state, s_state


def kernel(x_prompt, x_sample, cache_k_w128, cache_v_w128, cache_k_w512, cache_v_w512, cache_k_w2048,
           cache_v_w2048, state_mlstm_C, state_mlstm_n, state_mlstm_m, state_ffn_conv, norm_mix, w_in, q_norm,
           k_norm, b_igate, b_fgate, w_proj_att, w_proj_mlstm, w_out, norm_ffn, w_ffn_in, conv_w, conv_b,
           w_ffn_down):
    assert norm_mix.shape[0] == 1
    caches = [c[0] for c in (cache_k_w128, cache_v_w128, cache_k_w512, cache_v_w512, cache_k_w2048,
                             cache_v_w2048, state_mlstm_C, state_mlstm_n, state_mlstm_m, state_ffn_conv)]
    weights = [w[0] for w in (norm_mix, w_in, q_norm, k_norm, b_igate, b_fgate, w_proj_att, w_proj_mlstm,
                              w_out, norm_ffn, w_ffn_in, conv_w, conv_b, w_ffn_down)]
    y_p, y_s, p_state, s_state = _layer(x_prompt, x_sample, caches, *weights)
    return (y_p, y_s, *[a[None] for a in p_state], *[a[None] for a in s_state])
```

```python
import functools

import numpy as np
import jax
import jax.numpy as jnp
from jax import lax
from jax.experimental import pallas as pl
from jax.experimental.pallas import tpu as pltpu

F32 = jnp.float32
BF16 = jnp.bfloat16

RMS_EPS = 1e-6
NEG_INF = -1e30
LANES = 128
VMEM_LIMIT = 56 * 1024 * 1024

D_MODEL = 2048
DIL_PATTERNS = ((128, 1), (512, 4), (2048, 16))
N_GROUPS = 3
ATT_HEADS = 4
ATT_HEAD_DIM = 128
ATT_SPAN = 128
ATT_SCALE = ATT_HEAD_DIM ** -0.5
ATT_GROUP_W = ATT_HEADS * ATT_HEAD_DIM
ATT_WIDTH = N_GROUPS * ATT_GROUP_W
M_HEADS = 4
M_QK_DIM = D_MODEL // (2 * M_HEADS)
M_V_DIM = D_MODEL // M_HEADS
M_QK_WIDTH = M_HEADS * M_QK_DIM
M_V_WIDTH = M_HEADS * M_V_DIM
M_K_SCALE = M_QK_DIM ** -0.5
D_FF = ((8 * D_MODEL // 3 + 255) // 256) * 256
CONV_W = 3
SPLIT_SIZES = (ATT_WIDTH, ATT_WIDTH, ATT_WIDTH, M_QK_WIDTH, M_QK_WIDTH, M_V_WIDTH, M_V_WIDTH,
               M_HEADS, M_HEADS, D_MODEL, D_MODEL)

A_QA = 0
A_KA = A_QA + ATT_WIDTH
A_VA = A_KA + ATT_WIDTH
A_IF = A_VA + ATT_WIDTH
IF_PAD = 512
NA = A_IF + IF_PAD
B_GA = 0
B_GB = B_GA + D_MODEL
B_OM = B_GB + D_MODEL
B_VM = B_OM + M_V_WIDTH
B_QM = B_VM + M_V_WIDTH
B_KM = B_QM + M_QK_WIDTH
NB = B_KM + M_QK_WIDTH
NZ = NA + NB


def _cparams(sem):
    return pltpu.CompilerParams(dimension_semantics=sem, vmem_limit_bytes=VMEM_LIMIT)


def _log_sigmoid(x):
    return jnp.minimum(x, 0.0) - jnp.log(1.0 + jnp.exp(-jnp.abs(x)))


def _in_proj_kernel(x_ref, g_ref, w_ref, qkg_ref, za_ref, zb_ref, h_ref, *, n_norm, n32):
    j = pl.program_id(1)

    @pl.when(j == 0)
    def _():
        x = x_ref[...]
        ms = jnp.mean(x * x, axis=-1, keepdims=True)
        h_ref[...] = (x * lax.rsqrt(ms + RMS_EPS) * g_ref[...]).astype(BF16)

    @pl.when(j < n32)
    def _():
        za_ref[...] = jnp.dot(h_ref[...], w_ref[...], preferred_element_type=F32)

    @pl.when(j < n_norm)
    def _():
        for c in range(za_ref.shape[1] // LANES):
            sl = slice(c * LANES, (c + 1) * LANES)
            zc = za_ref[:, sl]
            ms = jnp.mean(zc * zc, axis=-1, keepdims=True)
            za_ref[:, sl] = zc * lax.rsqrt(ms + RMS_EPS) * qkg_ref[:, sl]

    @pl.when(j >= n32)
    def _():
        zb_ref[...] = jnp.dot(h_ref[...], w_ref[...], preferred_element_type=F32).astype(zb_ref.dtype)


def _in_proj(x, gain, w, qk_gain, *, tm, tn, zb_dtype):
    m, d = x.shape
    assert w.shape[1] == NZ and m % tm == 0 and NA % tn == 0 and NB % tn == 0 and A_VA % tn == 0
    n32 = NA // tn
    kern = functools.partial(_in_proj_kernel, n_norm=A_VA // tn, n32=n32)
    return pl.pallas_call(
        kern,
        grid=(m // tm, NZ // tn),
        in_specs=[
            pl.BlockSpec((tm, d), lambda i, j: (i, 0)),
            pl.BlockSpec((1, d), lambda i, j: (0, 0)),
            pl.BlockSpec((d, tn), lambda i, j: (0, j)),
            pl.BlockSpec((1, tn), lambda i, j: (0, jnp.minimum(j, n32 - 1))),
        ],
        out_specs=[pl.BlockSpec((tm, tn), lambda i, j: (i, jnp.minimum(j, n32 - 1))),
                   pl.BlockSpec((tm, tn), lambda i, j: (i, jnp.maximum(j - n32, 0)))],
        out_shape=[jax.ShapeDtypeStruct((m, NA), F32), jax.ShapeDtypeStruct((m, NB), zb_dtype)],
        scratch_shapes=[pltpu.VMEM((tm, d), BF16)],
        compiler_params=_cparams(("parallel", "arbitrary")),
        name="in_proj",
    )(x, gain, w, qk_gain)


def _attn_prompt_kernel(q_ref, kp_ref, kc_ref, vp_ref, vc_ref, o_ref, lse_ref, *, dil, nsub):
    c = pl.program_id(1)
    blk = ATT_SPAN
    step = blk * dil
    hw = q_ref.shape[1] // ATT_HEAD_DIM
    qi = lax.broadcasted_iota(jnp.int32, (blk, blk), 0)
    kj = lax.broadcasted_iota(jnp.int32, (blk, blk), 1)
    cur_ok = kj <= qi
    prev_ok = kj >= qi
    first_bias = jnp.where(c > 0, 0.0, NEG_INF)
    lane = lax.broadcasted_iota(jnp.int32, (blk, LANES), 1)
    nt = (((1,), (1,)), ((), ()))

    def rows(base, r):
        return pl.ds(base + r, blk) if dil == 1 else pl.ds(base + r, blk, stride=dil)

    def one(r, s):
        base = s * step
        lse_tile = jnp.zeros((blk, LANES), F32)
        for n in range(hw):
            sl = slice(n * ATT_HEAD_DIM, (n + 1) * ATT_HEAD_DIM)
            q = q_ref[rows(base, r), sl].astype(BF16)
            k_c = kc_ref[rows(base, r), sl].astype(BF16)
            v_c = vc_ref[rows(base, r), sl].astype(BF16)
            if s == 0:
                k_p = kp_ref[rows(0, r), sl].astype(BF16)
                v_p = vp_ref[rows(0, r), sl].astype(BF16)
            else:
                k_p = kc_ref[rows(base - step, r), sl].astype(BF16)
                v_p = vc_ref[rows(base - step, r), sl].astype(BF16)
            s_c = lax.dot_general(q, k_c, nt, preferred_element_type=F32) * ATT_SCALE
            s_p = lax.dot_general(q, k_p, nt, preferred_element_type=F32) * ATT_SCALE
            if s == 0:
                s_p = s_p + first_bias
            s_c = jnp.where(cur_ok, s_c, NEG_INF)
            s_p = jnp.where(prev_ok, s_p, NEG_INF)
            m = jnp.maximum(jnp.max(s_c, axis=-1, keepdims=True), jnp.max(s_p, axis=-1, keepdims=True))
            p_c = jnp.exp(s_c - m)
            p_p = jnp.exp(s_p - m)
            l = jnp.sum(p_c, axis=-1, keepdims=True) + jnp.sum(p_p, axis=-1, keepdims=True)
            o = (jnp.dot(p_c.astype(BF16), v_c, preferred_element_type=F32)
                 + jnp.dot(p_p.astype(BF16), v_p, preferred_element_type=F32))
            o_ref[rows(base, r), sl] = o / l
            lse_tile = jnp.where(lane == n, m + jnp.log(l), lse_tile)
        lse_ref[rows(base, r), :] = lse_tile

    for s in range(nsub):
        if dil == 1:
            one(0, s)
        else:
            def body(r, carry, s=s):
                one(r, s)
                return carry
            lax.fori_loop(0, dil, body, 0)


def _attn_prompt_group(za, batch, seq, gi, dil, *, nsub, hw):
    step = ATT_SPAN * dil
    tc = nsub * step
    assert seq % tc == 0 and ATT_HEADS % hw == 0
    nh = ATT_HEADS // hw
    w = hw * ATT_HEAD_DIM
    z3 = za.reshape(batch, seq, NA)
    qc, kc, vc = [(col + gi * ATT_GROUP_W) // w for col in (A_QA, A_KA, A_VA)]

    def cur(col):
        return pl.BlockSpec((None, tc, w), lambda b, c, h: (b, c, col + h))

    def prev(col):
        return pl.BlockSpec((None, step, w), lambda b, c, h: (b, jnp.maximum(c * nsub - 1, 0), col + h))

    kern = functools.partial(_attn_prompt_kernel, dil=dil, nsub=nsub)
    o, lse = pl.pallas_call(
        kern,
        grid=(batch, seq // tc, nh),
        in_specs=[cur(qc), prev(kc), cur(kc), prev(vc), cur(vc)],
        out_specs=[pl.BlockSpec((None, tc, w), lambda b, c, h: (b, c, h)),
                   pl.BlockSpec((None, tc, LANES), lambda b, c, h: (b, c, h))],
        out_shape=[jax.ShapeDtypeStruct((batch, seq, ATT_GROUP_W), F32),
                   jax.ShapeDtypeStruct((batch, seq, nh * LANES), F32)],
        compiler_params=_cparams(("parallel", "parallel", "parallel")),
        name=f"attn_prompt_g{gi}",
    )(z3, z3, z3, z3, z3)
    return o.reshape(batch * seq, ATT_GROUP_W), lse.reshape(batch * seq, nh * LANES)


def _attn_sample_kernel(z_ref, k1, v1, k2, v2, k3, v3, o1, o2, o3, l1, l2, l3):
    bb = z_ref.shape[0]
    bufs = ((k1, v1, o1, l1), (k2, v2, o2, l2), (k3, v3, o3, l3))
    lane = lax.broadcasted_iota(jnp.int32, (1, LANES), 1)

    def heads(b, col):
        return jnp.concatenate([z_ref[b, :, col + n * ATT_HEAD_DIM:col + (n + 1) * ATT_HEAD_DIM]
                                for n in range(ATT_HEADS)], axis=0)

    def body(b, carry):
        for gi, (k_ref, v_ref, o_ref, l_ref) in enumerate(bufs):
            q = heads(b, A_QA + gi * ATT_GROUP_W)
            k_new = heads(b, A_KA + gi * ATT_GROUP_W)
            v_new = heads(b, A_VA + gi * ATT_GROUP_W)
            kb = k_ref[b]
            vb = v_ref[b]
            s = jnp.sum(kb * q[None], axis=-1, keepdims=True) * ATT_SCALE
            s_new = jnp.sum(k_new * q, axis=-1, keepdims=True) * ATT_SCALE
            m = jnp.maximum(jnp.max(s, axis=0), s_new)
            p = jnp.exp(s - m[None])
            p_new = jnp.exp(s_new - m)
            l = jnp.sum(p, axis=0) + p_new
            o = (jnp.sum(p * vb, axis=0) + p_new * v_new) / l
            lse = m + jnp.log(l)
            lse_row = jnp.zeros((1, LANES), F32)
            for n in range(ATT_HEADS):
                o_ref[b, :, n * ATT_HEAD_DIM:(n + 1) * ATT_HEAD_DIM] = o[n:n + 1, :]
                lse_row = jnp.where(lane == n, lse[n:n + 1, :], lse_row)
            l_ref[b] = lse_row
        return carry

    lax.fori_loop(0, bb, body, 0)


def _attn_sample(za_s, caches, *, bb):
    bd = za_s.shape[0]
    assert bd % bb == 0
    ins, in_specs = [za_s.reshape(bd, 1, NA)], [pl.BlockSpec((bb, 1, NA), lambda i: (i, 0, 0))]
    for (k_buf, v_buf), (win, dil) in zip(caches, DIL_PATTERNS):
        assert k_buf.shape[1:] == (ATT_SPAN * dil, ATT_HEADS, ATT_HEAD_DIM)
        for buf in (k_buf, v_buf):
            ins.append(buf.reshape(bd, ATT_SPAN, dil, ATT_HEADS, ATT_HEAD_DIM))
            in_specs.append(pl.BlockSpec((bb, ATT_SPAN, None, ATT_HEADS, ATT_HEAD_DIM),
                                         lambda i: (i, 0, 0, 0, 0)))
    outs = pl.pallas_call(
        _attn_sample_kernel,
        grid=(bd // bb,),
        in_specs=in_specs,
        out_specs=[pl.BlockSpec((bb, 1, ATT_GROUP_W), lambda i: (i, 0, 0))] * 3
                  + [pl.BlockSpec((bb, 1, LANES), lambda i: (i, 0, 0))] * 3,
        out_shape=[jax.ShapeDtypeStruct((bd, 1, ATT_GROUP_W), F32)] * 3
                  + [jax.ShapeDtypeStruct((bd, 1, LANES), F32)] * 3,
        compiler_params=_cparams(("parallel",)),
        name="attn_sample",
    )(*ins)
    return [o[:, 0, :] for o in outs[:3]], [l[:, 0, :] for l in outs[3:]]


def _mlstm_prompt_kernel(q_ref, k_ref, v_ref, om_ref, g_ref, bias_ref, h_ref, c_out, n_out, m_out,
                         c_s, n_s, m_s):
    hd = pl.program_id(1)
    ci = pl.program_id(2)
    L = q_ref.shape[0]

    @pl.when(ci == 0)
    def _():
        c_s[...] = jnp.zeros_like(c_s)
        n_s[...] = jnp.zeros_like(n_s)
        m_s[...] = jnp.zeros_like(m_s)

    gates = g_ref[...] + bias_ref[...]
    lane = lax.broadcasted_iota(jnp.int32, (L, LANES), 1)
    li_col = jnp.sum(jnp.where(lane == hd, gates, 0.0), axis=1, keepdims=True)
    lf_col = _log_sigmoid(jnp.sum(jnp.where(lane == hd + M_HEADS, gates, 0.0), axis=1, keepdims=True))
    gates_t = gates.T
    sub = lax.broadcasted_iota(jnp.int32, (LANES, L), 0)
    li_row = jnp.sum(jnp.where(sub == hd, gates_t, 0.0), axis=0, keepdims=True)
    lf_row = _log_sigmoid(jnp.sum(jnp.where(sub == hd + M_HEADS, gates_t, 0.0), axis=0, keepdims=True))

    ti = lax.broadcasted_iota(jnp.int32, (L, L), 0)
    si = lax.broadcasted_iota(jnp.int32, (L, L), 1)
    causal = si <= ti
    b_col = jnp.sum(jnp.where(causal, lf_row, 0.0), axis=1, keepdims=True)
    b_row = jnp.sum(jnp.where(ti <= si, lf_col, 0.0), axis=0, keepdims=True)
    b_end = jnp.sum(lf_row, axis=1, keepdims=True)

    m_prev = m_s[...]
    dmat = jnp.where(causal, b_col - b_row + li_row, NEG_INF)
    inter = b_col + m_prev
    mt = jnp.maximum(inter, jnp.max(dmat, axis=1, keepdims=True))

    qb = q_ref[...]
    q = qb.astype(F32)
    k = k_ref[...].astype(F32) * M_K_SCALE
    vb = v_ref[...]
    qk = lax.dot_general(qb, k.astype(BF16), (((1,), (1,)), ((), ())), preferred_element_type=F32)
    a = jnp.exp(dmat - mt) * qk
    w_inter = jnp.exp(inter - mt)
    num = (jnp.dot(a.astype(BF16), vb, preferred_element_type=F32)
           + w_inter * jnp.dot(qb, c_s[...].astype(BF16), preferred_element_type=F32))
    den = jnp.sum(a, axis=1, keepdims=True) + w_inter * jnp.sum(q * n_s[...], axis=1, keepdims=True)
    h = num / jnp.maximum(jnp.abs(den), jnp.exp(-mt))
    h_ref[...] = (jax.nn.sigmoid(om_ref[...].astype(F32)) * h).astype(h_ref.dtype)

    g_col = b_end - b_col + li_col
    g_row = b_end - b_row + li_row
    m_new = jnp.maximum(b_end + m_prev, jnp.max(g_row, axis=1, keepdims=True))
    decay = jnp.exp(b_end + m_prev - m_new)
    kw = jnp.exp(g_col - m_new) * k
    c_s[...] = decay * c_s[...] + jnp.dot(kw.T.astype(BF16), vb, preferred_element_type=F32)
    n_s[...] = decay * n_s[...] + jnp.sum(kw, axis=0, keepdims=True)
    m_s[...] = m_new

    @pl.when(ci == pl.num_programs(2) - 1)
    def _():
        c_out[...] = c_s[...]
        n_out[...] = n_s[...]
        m_out[...] = jnp.broadcast_to(m_s[...], m_out.shape)


def _mlstm_prompt(za, zb, gate_bias, batch, seq, *, chunk):
    assert seq % chunk == 0
    nc = seq // chunk
    za3 = za.reshape(batch, seq, NA)
    zb3 = zb.reshape(batch, seq, NB)
    qk_blk = lambda col: pl.BlockSpec((None, chunk, M_QK_DIM), lambda b, h, c: (b, c, col // M_QK_DIM + h))
    v_blk = lambda col: pl.BlockSpec((None, chunk, M_V_DIM), lambda b, h, c: (b, c, col // M_V_DIM + h))
    hm, c1, n1, m1 = pl.pallas_call(
        _mlstm_prompt_kernel,
        grid=(batch, M_HEADS, nc),
        in_specs=[qk_blk(B_QM), qk_blk(B_KM), v_blk(B_VM), v_blk(B_OM),
                  pl.BlockSpec((None, chunk, LANES), lambda b, h, c: (b, c, A_IF // LANES)),
                  pl.BlockSpec((1, LANES), lambda b, h, c: (0, 0))],
        out_specs=[pl.BlockSpec((None, chunk, M_V_DIM), lambda b, h, c: (b, c, h)),
                   pl.BlockSpec((None, None, M_QK_DIM, M_V_DIM), lambda b, h, c: (b, h, 0, 0)),
                   pl.BlockSpec((None, None, 1, M_QK_DIM), lambda b, h, c: (b, h, 0, 0)),
                   pl.BlockSpec((None, None, 1, LANES), lambda b, h, c: (b, h, 0, 0))],
        out_shape=[jax.ShapeDtypeStruct((batch, seq, M_V_WIDTH), BF16),
                   jax.ShapeDtypeStruct((batch, M_HEADS, M_QK_DIM, M_V_DIM), F32),
                   jax.ShapeDtypeStruct((batch, M_HEADS, 1, M_QK_DIM), F32),
                   jax.ShapeDtypeStruct((batch, M_HEADS, 1, LANES), F32)],
        scratch_shapes=[pltpu.VMEM((M_QK_DIM, M_V_DIM), F32), pltpu.VMEM((1, M_QK_DIM), F32),
                        pltpu.VMEM((1, 1), F32)],
        compiler_params=_cparams(("parallel", "parallel", "arbitrary")),
        name="mlstm_prompt",
    )(zb3, zb3, zb3, zb3, za3, gate_bias)
    return hm.reshape(batch * seq, M_V_WIDTH), c1, n1[:, :, 0, :], m1[:, :, 0, 0]


def _row_to_col(row):
    n = row.shape[1]
    return jnp.broadcast_to(row, (LANES, n)).T[:, 0:1]


def _mlstm_sample_kernel(za_ref, zb_ref, bias_ref, c_ref, n_ref, m_ref, h_ref, c_out, n_out, m_out):
    gates = za_ref[:, A_IF:A_IF + LANES] + bias_ref[...]
    lane = lax.broadcasted_iota(jnp.int32, (1, LANES), 1)
    m_row = jnp.zeros((1, LANES), F32)
    for h in range(M_HEADS):
        q = zb_ref[:, B_QM + h * M_QK_DIM:B_QM + (h + 1) * M_QK_DIM].astype(F32)
        k = zb_ref[:, B_KM + h * M_QK_DIM:B_KM + (h + 1) * M_QK_DIM].astype(F32) * M_K_SCALE
        v = zb_ref[:, B_VM + h * M_V_DIM:B_VM + (h + 1) * M_V_DIM].astype(F32)
        om = zb_ref[:, B_OM + h * M_V_DIM:B_OM + (h + 1) * M_V_DIM].astype(F32)
        li = gates[:, h:h + 1]
        lf = _log_sigmoid(gates[:, M_HEADS + h:M_HEADS + h + 1])
        m0 = m_ref[:, h:h + 1]
        c0 = c_ref[h]
        n0 = n_ref[h:h + 1, :]
        inter = lf + m0
        mt = jnp.maximum(inter, li)
        a = jnp.exp(li - mt) * jnp.sum(q * k, axis=1, keepdims=True)
        w_inter = jnp.exp(inter - mt)
        q_c = jnp.sum(_row_to_col(q) * c0, axis=0, keepdims=True)
        num = a * v + w_inter * q_c
        den = a + w_inter * jnp.sum(q * n0, axis=1, keepdims=True)
        hv = num / jnp.maximum(jnp.abs(den), jnp.exp(-mt))
        h_ref[:, h * M_V_DIM:(h + 1) * M_V_DIM] = (jax.nn.sigmoid(om) * hv).astype(h_ref.dtype)
        m_new = jnp.maximum(inter, li)
        decay = jnp.exp(inter - m_new)
        ws = jnp.exp(li - m_new)
        c_out[h] = decay * c0 + _row_to_col(ws * k) * v
        n_out[h:h + 1, :] = decay * n0 + ws * k
        m_row = jnp.where(lane == h, m_new, m_row)
    m_out[...] = m_row


def _mlstm_sample(za_s, zb_s, gate_bias, c0, n0, m0):
    bd = za_s.shape[0]
    hm, c1, n1, m1 = pl.pallas_call(
        _mlstm_sample_kernel,
        grid=(bd,),
        in_specs=[pl.BlockSpec((None, 1, NA), lambda b: (b, 0, 0)),
                  pl.BlockSpec((None, 1, NB), lambda b: (b, 0, 0)),
                  pl.BlockSpec((1, LANES), lambda b: (0, 0)),
                  pl.BlockSpec((None, M_HEADS, M_QK_DIM, M_V_DIM), lambda b: (b, 0, 0, 0)),
                  pl.BlockSpec((None, M_HEADS, M_QK_DIM), lambda b: (b, 0, 0)),
                  pl.BlockSpec((None, 1, M_HEADS), lambda b: (b, 0, 0))],
        out_specs=[pl.BlockSpec((None, 1, M_V_WIDTH), lambda b: (b, 0, 0)),
                   pl.BlockSpec((None, M_HEADS, M_QK_DIM, M_V_DIM), lambda b: (b, 0, 0, 0)),
                   pl.BlockSpec((None, M_HEADS, M_QK_DIM), lambda b: (b, 0, 0)),
                   pl.BlockSpec((None, 1, LANES), lambda b: (b, 0, 0))],
        out_shape=[jax.ShapeDtypeStruct((bd, 1, M_V_WIDTH), BF16),
                   jax.ShapeDtypeStruct((bd, M_HEADS, M_QK_DIM, M_V_DIM), F32),
                   jax.ShapeDtypeStruct((bd, M_HEADS, M_QK_DIM), F32),
                   jax.ShapeDtypeStruct((bd, 1, LANES), F32)],
        compiler_params=_cparams(("parallel",)),
        name="mlstm_sample",
    )(za_s.reshape(bd, 1, NA), zb_s.reshape(bd, 1, NB), gate_bias, c0, n0, m0.reshape(bd, 1, M_HEADS))
    return hm.reshape(bd, M_V_WIDTH), c1, n1, m1[:, 0, :M_HEADS]


def _merge_kernel(o1, o2, o3, l1, l2, l3, hm_ref, ga_ref, gb_ref, x_ref, wpa_ref, wpm_ref, wo_ref, g2_ref,
                  x1_ref, h2_ref, *, hws):
    os_ = (o1, o2, o3)
    parts = []
    for n in range(ATT_HEADS):
        sl = slice(n * ATT_HEAD_DIM, (n + 1) * ATT_HEAD_DIM)
        ls = []
        for l_ref, hw in zip((l1, l2, l3), hws):
            ln = (n // hw) * LANES + n % hw
            ls.append(l_ref[:, ln:ln + 1])
        mx = jnp.maximum(jnp.maximum(ls[0], ls[1]), ls[2])
        es = [jnp.exp(l - mx) for l in ls]
        tot = es[0] + es[1] + es[2]
        parts.append((es[0] / tot) * o1[:, sl] + (es[1] / tot) * o2[:, sl] + (es[2] / tot) * o3[:, sl])
    att = jnp.concatenate(parts, axis=1).astype(BF16)
    pa = jnp.dot(att, wpa_ref[...], preferred_element_type=F32)
    pm = jnp.dot(hm_ref[...], wpm_ref[...], preferred_element_type=F32)
    merged = (jax.nn.sigmoid(ga_ref[...].astype(F32)) * pa + jax.nn.sigmoid(gb_ref[...].astype(F32)) * pm)
    x1 = x_ref[...] + jnp.dot(merged.astype(BF16), wo_ref[...], preferred_element_type=F32)
    x1_ref[...] = x1
    ms = jnp.mean(x1 * x1, axis=-1, keepdims=True)
    h2_ref[...] = (x1 * lax.rsqrt(ms + RMS_EPS) * g2_ref[...]).astype(BF16)


def _merge(o_list, lse_list, hm, zb, x, wpa, wpm, wo, g2, *, tm):
    m, d = x.shape
    assert m % tm == 0
    row_blk = lambda w: pl.BlockSpec((tm, w), lambda i: (i, 0))
    const = lambda shape: pl.BlockSpec(shape, lambda i: (0, 0), pipeline_mode=pl.Buffered(1))
    hws = tuple(ATT_HEADS * LANES // l.shape[1] for l in lse_list)
    return pl.pallas_call(
        functools.partial(_merge_kernel, hws=hws),
        grid=(m // tm,),
        in_specs=[row_blk(ATT_GROUP_W)] * 3 + [row_blk(l.shape[1]) for l in lse_list]
                 + [row_blk(M_V_WIDTH),
                    pl.BlockSpec((tm, d), lambda i: (i, B_GA // d)),
                    pl.BlockSpec((tm, d), lambda i: (i, B_GB // d)),
                    row_blk(d),
                    const(wpa.shape), const(wpm.shape), const(wo.shape), const((1, d))],
        out_specs=[row_blk(d), row_blk(d)],
        out_shape=[jax.ShapeDtypeStruct((m, d), F32), jax.ShapeDtypeStruct((m, d), BF16)],
        compiler_params=_cparams(("parallel",)),
        name="merge",
    )(*o_list, *lse_list, hm, zb, zb, x, wpa, wpm, wo, g2)


def _ffn_body(h2_ref, wu_ref, wg_ref, cw_ref, cb_ref, wd_ref, x1_ref, y_ref, g_prev2, g_prev1):
    j = pl.program_id(1)
    h2 = h2_ref[...]
    u = jnp.dot(h2, wu_ref[...], preferred_element_type=F32)
    g = jnp.dot(h2, wg_ref[...], preferred_element_type=F32)
    gconv = cb_ref[...] + ((g_prev2(g) * cw_ref[0:1, :] + g_prev1(g) * cw_ref[1:2, :]) + g * cw_ref[2:3, :])
    act = (jax.nn.gelu(gconv) * u).astype(BF16)
    down = jnp.dot(act, wd_ref[...], preferred_element_type=F32)

    @pl.when(j == 0)
    def _():
        y_ref[...] = x1_ref[...] + down

    @pl.when(j > 0)
    def _():
        y_ref[...] += down

    return g


def _ffn_prompt_kernel(h2_ref, wu_ref, wg_ref, cw_ref, cb_ref, wd_ref, x1_ref, y_ref, tail_ref, prev_s,
                       *, tiles_per_seq):
    i = pl.program_id(0)
    j = pl.program_id(1)
    tm = h2_ref.shape[0]
    prev = jnp.where(i % tiles_per_seq == 0, 0.0, prev_s[j])
    p2, p1 = prev[6:7, :], prev[7:8, :]

    def g_prev1(g):
        r = lax.broadcasted_iota(jnp.int32, g.shape, 0)
        return jnp.where(r == 0, p1, pltpu.roll(g, 1, axis=0))

    def g_prev2(g):
        r = lax.broadcasted_iota(jnp.int32, g.shape, 0)
        return jnp.where(r == 0, p2, jnp.where(r == 1, p1, pltpu.roll(g, 2, axis=0)))

    g = _ffn_body(h2_ref, wu_ref, wg_ref, cw_ref, cb_ref, wd_ref, x1_ref, y_ref, g_prev2, g_prev1)
    prev_s[j] = g[tm - 8:tm, :]
    tail_ref[...] = g[tm - 8:tm, :]


def _ffn_sample_kernel(h2_ref, wu_ref, wg_ref, cw_ref, cb_ref, wd_ref, x1_ref, b2_ref, b1_ref, y_ref, g_ref):
    g = _ffn_body(h2_ref, wu_ref, wg_ref, cw_ref, cb_ref, wd_ref, x1_ref, y_ref,
                  lambda g: b2_ref[...], lambda g: b1_ref[...])
    g_ref[...] = g


def _ffn_specs(tm, tf, d, nf):
    return [pl.BlockSpec((tm, d), lambda i, j: (i, 0)),
            pl.BlockSpec((d, tf), lambda i, j: (0, j)),
            pl.BlockSpec((d, tf), lambda i, j: (0, nf + j)),
            pl.BlockSpec((CONV_W, tf), lambda i, j: (0, j)),
            pl.BlockSpec((1, tf), lambda i, j: (0, j)),
            pl.BlockSpec((tf, d), lambda i, j: (j, 0)),
            pl.BlockSpec((tm, d), lambda i, j: (i, 0))]


def _ffn_prompt(h2, x1, w_in, conv_w, conv_b, w_down, seq, *, tm, tf):
    m, d = x1.shape
    assert m % tm == 0 and seq % tm == 0 and D_FF % tf == 0 and tm % 8 == 0
    nf = D_FF // tf
    kern = functools.partial(_ffn_prompt_kernel, tiles_per_seq=seq // tm)
    y, tails = pl.pallas_call(
        kern,
        grid=(m // tm, nf),
        in_specs=_ffn_specs(tm, tf, d, nf),
        out_specs=[pl.BlockSpec((tm, d), lambda i, j: (i, 0)),
                   pl.BlockSpec((None, 8, tf), lambda i, j: (i, 0, j))],
        out_shape=[jax.ShapeDtypeStruct((m, d), F32), jax.ShapeDtypeStruct((m // tm, 8, D_FF), F32)],
        scratch_shapes=[pltpu.VMEM((nf, 8, tf), F32)],
        compiler_params=_cparams(("arbitrary", "arbitrary")),
        name="ffn_prompt",
    )(h2, w_in, w_in, conv_w, conv_b, w_down, x1)
    return y, tails


def _ffn_sample(h2, x1, w_in, conv_w, conv_b, w_down, conv_buf, *, tf):
    m, d = x1.shape
    nf = D_FF // tf
    buf2d = conv_buf.reshape(m, (CONV_W - 1) * D_FF)
    return pl.pallas_call(
        _ffn_sample_kernel,
        grid=(1, nf),
        in_specs=_ffn_specs(m, tf, d, nf) + [pl.BlockSpec((m, tf), lambda i, j: (0, j)),
                                             pl.BlockSpec((m, tf), lambda i, j: (0, nf + j))],
        out_specs=[pl.BlockSpec((m, d), lambda i, j: (i, 0)), pl.BlockSpec((m, tf), lambda i, j: (0, j))],
        out_shape=[jax.ShapeDtypeStruct((m, d), F32), jax.ShapeDtypeStruct((m, D_FF), F32)],
        compiler_params=_cparams(("arbitrary", "arbitrary")),
        name="ffn_sample",
    )(h2, w_in, w_in, conv_w, conv_b, w_down, x1, buf2d, buf2d)


def _prep_w_in(w_in):
    pts = tuple(int(v) for v in np.cumsum(SPLIT_SIZES)[:-1])
    qa, ka, va, qm, km, vm, om, im, fm, ga, gb = jnp.split(w_in, pts, axis=1)
    pad = jnp.zeros((w_in.shape[0], IF_PAD - 2 * M_HEADS), w_in.dtype)
    return jnp.concatenate([qa, ka, va, im, fm, pad, ga, gb, om, vm, qm, km], axis=1).astype(BF16)


def _qk_gain_row(q_norm, k_norm):
    reps = N_GROUPS * ATT_HEADS
    return jnp.concatenate([jnp.tile(q_norm, reps), jnp.tile(k_norm, reps), jnp.zeros((NA - A_VA,), F32)])[None, :]


ATTN_PROMPT_TILING = ((4, 4), (1, 1), (1, 1))


def _layer(x_prompt, x_sample, caches, norm_mix, w_in, q_norm, k_norm, b_igate, b_fgate, w_proj_att,
           w_proj_mlstm, w_out, norm_ffn, w_ffn_in, conv_w, conv_b, w_ffn_down):
    batch, seq, d = x_prompt.shape
    bd = x_sample.shape[0]
    assert x_sample.shape[1] == 1 and d == D_MODEL
    (ck1, cv1, ck2, cv2, ck3, cv3, st_c, st_n, st_m, st_conv) = caches

    w_z = _prep_w_in(w_in)
    qk_gain = _qk_gain_row(q_norm, k_norm)
    g1 = norm_mix[None, :]
    g2 = norm_ffn[None, :]
    gate_bias = jnp.concatenate([b_igate, b_fgate, jnp.zeros((LANES - 2 * M_HEADS,), F32)])[None, :]
    wpa = w_proj_att.astype(BF16)
    wpm = w_proj_mlstm.astype(BF16)
    wo = w_out.astype(BF16)
    w_ff = w_ffn_in.astype(BF16)
    w_dn = w_ffn_down.astype(BF16)
    cb = conv_b[None, :]

    xp = x_prompt.reshape(batch * seq, d)
    xs = x_sample.reshape(bd, d)

    za_p, zb_p = _in_proj(xp, g1, w_z, qk_gain, tm=1024, tn=1024, zb_dtype=BF16)
    o_p, lse_p = zip(*[_attn_prompt_group(za_p, batch, seq, gi, dil, nsub=nsub, hw=hw)
                       for gi, ((_, dil), (nsub, hw)) in enumerate(zip(DIL_PATTERNS, ATTN_PROMPT_TILING))])
    hm_p, p_c, p_n, p_m = _mlstm_prompt(za_p, zb_p, gate_bias, batch, seq, chunk=256)
    x1_p, h2_p = _merge(o_p, lse_p, hm_p, zb_p, xp, wpa, wpm, wo, g2, tm=256)
    y_p, tails = _ffn_prompt(h2_p, x1_p, w_ff, conv_w, cb, w_dn, seq, tm=512, tf=512)

    za_p3 = za_p.reshape(batch, seq, NA)
    p_kv = []
    for gi, (win, _) in enumerate(DIL_PATTERNS):
        keep = min(win, seq)
        for col in (A_KA, A_VA):
            lo = col + gi * ATT_GROUP_W
            p_kv.append(za_p3[:, seq - keep:, lo:lo + ATT_GROUP_W].reshape(batch, keep, ATT_HEADS, ATT_HEAD_DIM))
    tiles_per_seq = seq // 512
    p_conv = tails.reshape(batch, tiles_per_seq, 8, D_FF)[:, -1, 8 - (CONV_W - 1):, :]

    za_s, zb_s = _in_proj(xs, g1, w_z, qk_gain, tm=bd, tn=1024, zb_dtype=F32)
    o_s, lse_s = _attn_sample(za_s, [(ck1, cv1), (ck2, cv2), (ck3, cv3)], bb=4)
    hm_s, s_c, s_n, s_m = _mlstm_sample(za_s, zb_s, gate_bias, st_c, st_n, st_m)
    x1_s, h2_s = _merge(o_s, lse_s, hm_s, zb_s, xs, wpa, wpm, wo, g2, tm=bd)
    y_s, g_s = _ffn_sample(h2_s, x1_s, w_ff, conv_w, cb, w_dn, st_conv, tf=512)

    s_kv = []
    for gi in range(N_GROUPS):
        for col in (A_KA, A_VA):
            lo = col + gi * ATT_GROUP_W
            s_kv.append(za_s[:, lo:lo + ATT_GROUP_W].reshape(bd, 1, ATT_HEADS, ATT_HEAD_DIM))
    s_conv = jnp.stack([st_conv[:, 1, :], g_s], axis=1)

    p_state = p_kv + [p_c, p_n, p_m, p_conv]
    s_state = s_kv + [s_c, s_n, s_m, s_conv]
    return y_p.reshape(batch, seq, d), y_s.reshape(bd, 1, d), p_state, s_state


def kernel(x_prompt, x_sample, cache_k_w128, cache_v_w128, cache_k_w512, cache_v_w512, cache_k_w2048,
           cache_v_w2048, state_mlstm_C, state_mlstm_n, state_mlstm_m, state_ffn_conv, norm_mix, w_in, q_norm,
           k_norm, b_igate, b_fgate, w_proj_att, w_proj_mlstm, w_out, norm_ffn, w_ffn_in, conv_w, conv_b,
           w_ffn_down):
    assert norm_mix.shape[0] == 1
    caches = [c[0] for c in (cache_k_w128, cache_v_w128, cache_k_w512, cache_v_w512, cache_k_w2048,
                             cache_v_w2048, state_mlstm_C, state_mlstm_n, state_mlstm_m, state_ffn_conv)]
    weights = [w[0] for w in (norm_mix, w_in, q_norm, k_norm, b_igate, b_fgate, w_proj_att, w_proj_mlstm,
                              w_out, norm_ffn, w_ffn_in, conv_w, conv_b, w_ffn_down)]
    y_p, y_s, p_state, s_state = _layer(x_prompt, x_sample, caches, *weights)
    return (y_p, y_s, *[a[None] for a in p_state], *[a[None] for a in s_state])
```

```python
import functools

import jax
import jax.numpy as jnp
from jax import lax
from jax.experimental import pallas as pl
from jax.experimental.pallas import tpu as pltpu

F32 = jnp.float32
BF16 = jnp.bfloat16

RMS_EPS = 1e-6
NEG_INF = -1e30
LANES = 128
VMEM_LIMIT = 56 * 1024 * 1024

D_MODEL = 2048
DIL_PATTERNS = ((128, 1), (512, 4), (2048, 16))
N_GROUPS = 3
ATT_HEADS = 4
ATT_HEAD_DIM = 128
ATT_SPAN = 128
ATT_SCALE = ATT_HEAD_DIM ** -0.5
ATT_GROUP_W = ATT_HEADS * ATT_HEAD_DIM
ATT_WIDTH = N_GROUPS * ATT_GROUP_W
M_HEADS = 4
M_QK_DIM = D_MODEL // (2 * M_HEADS)
M_V_DIM = D_MODEL // M_HEADS
M_QK_WIDTH = M_HEADS * M_QK_DIM
M_V_WIDTH = M_HEADS * M_V_DIM
M_K_SCALE = M_QK_DIM ** -0.5
D_FF = ((8 * D_MODEL // 3 + 255) // 256) * 256
CONV_W = 3
SPLIT_SIZES = (ATT_WIDTH, ATT_WIDTH, ATT_WIDTH, M_QK_WIDTH, M_QK_WIDTH, M_V_WIDTH, M_V_WIDTH,
               M_HEADS, M_HEADS, D_MODEL, D_MODEL)

IN_COLS = sum(SPLIT_SIZES)
A_QA = 0
A_KA = A_QA + ATT_WIDTH
A_VA = A_KA + ATT_WIDTH
NA = A_VA + ATT_WIDTH
B_QM = 0
B_KM = B_QM + M_QK_WIDTH
B_VM = B_KM + M_QK_WIDTH
B_OM = B_VM + M_V_WIDTH
NB = B_OM + M_V_WIDTH
T_GA = 2 * M_HEADS
T_GB = T_GA + D_MODEL
T_END = T_GB + D_MODEL


def _cparams(sem):
    return pltpu.CompilerParams(dimension_semantics=sem, vmem_limit_bytes=VMEM_LIMIT)


def _log_sigmoid(x):
    return jnp.minimum(x, 0.0) - jnp.log(1.0 + jnp.exp(-jnp.abs(x)))


def _rmsnorm_kernel(x_ref, g_ref, h_ref):
    x = x_ref[...]
    ms = jnp.mean(x * x, axis=-1, keepdims=True)
    h_ref[...] = (x * lax.rsqrt(ms + RMS_EPS) * g_ref[...]).astype(h_ref.dtype)


def _rmsnorm(x, gain, *, tm):
    m, d = x.shape
    assert m % tm == 0
    return pl.pallas_call(
        _rmsnorm_kernel,
        grid=(m // tm,),
        in_specs=[pl.BlockSpec((tm, d), lambda i: (i, 0)), pl.BlockSpec((1, d), lambda i: (0, 0))],
        out_specs=pl.BlockSpec((tm, d), lambda i: (i, 0)),
        out_shape=jax.ShapeDtypeStruct((m, d), BF16),
        compiler_params=_cparams(("parallel",)),
        name="rmsnorm",
    )(x, gain)


def _in_proj_kernel(h_ref, w_ref, wl_ref, qkg_ref, za_ref, zb_ref, zif_ref, zt_ref, *, n_norm, na, nb, nt):
    j = pl.program_id(1)

    def z_of(w):
        return jnp.dot(h_ref[...], w[...].astype(BF16), preferred_element_type=F32)

    @pl.when(j < na)
    def _():
        za_ref[...] = z_of(w_ref)

    @pl.when(j < n_norm)
    def _():
        for c in range(za_ref.shape[1] // LANES):
            sl = slice(c * LANES, (c + 1) * LANES)
            zc = za_ref[:, sl]
            ms = jnp.mean(zc * zc, axis=-1, keepdims=True)
            za_ref[:, sl] = zc * lax.rsqrt(ms + RMS_EPS) * qkg_ref[:, sl]

    @pl.when((j >= na) & (j < na + nb))
    def _():
        zb_ref[...] = z_of(w_ref).astype(zb_ref.dtype)

    @pl.when(j == na + nb)
    def _():
        z = z_of(w_ref)
        zif_ref[...] = z[:, :LANES]
        zt_ref[...] = z.astype(zt_ref.dtype)

    @pl.when((j > na + nb) & (j < na + nb + nt - 1))
    def _():
        zt_ref[...] = z_of(w_ref).astype(zt_ref.dtype)

    @pl.when(j == na + nb + nt - 1)
    def _():
        zt_ref[:, :LANES] = z_of(wl_ref).astype(zt_ref.dtype)
        zt_ref[:, LANES:] = jnp.zeros((zt_ref.shape[0], zt_ref.shape[1] - LANES), zt_ref.dtype)


def _in_proj(h, w, w_last, qk_gain, *, tm, tn, z_dtype):
    m, d = h.shape
    assert w.shape[1] == IN_COLS and m % tm == 0 and NA % tn == 0 and NB % tn == 0 and A_VA % tn == 0
    assert 0 < IN_COLS % tn <= LANES
    na, nb = NA // tn, NB // tn
    nt = -(-T_END // tn)
    nj = na + nb + nt
    kern = functools.partial(_in_proj_kernel, n_norm=A_VA // tn, na=na, nb=nb, nt=nt)
    return pl.pallas_call(
        kern,
        grid=(m // tm, nj),
        in_specs=[
            pl.BlockSpec((tm, d), lambda i, j: (i, 0)),
            pl.BlockSpec((d, tn), lambda i, j: (0, jnp.minimum(j, nj - 2))),
            pl.BlockSpec((d, LANES), lambda i, j: (0, 0)),
            pl.BlockSpec((1, tn), lambda i, j: (0, jnp.minimum(j, na - 1))),
        ],
        out_specs=[pl.BlockSpec((tm, tn), lambda i, j: (i, jnp.minimum(j, na - 1))),
                   pl.BlockSpec((tm, tn), lambda i, j: (i, jnp.clip(j - na, 0, nb - 1))),
                   pl.BlockSpec((tm, LANES), lambda i, j: (i, 0)),
                   pl.BlockSpec((tm, tn), lambda i, j: (i, jnp.clip(j - na - nb, 0, nt - 1)))],
        out_shape=[jax.ShapeDtypeStruct((m, NA), F32), jax.ShapeDtypeStruct((m, NB), z_dtype),
                   jax.ShapeDtypeStruct((m, LANES), F32), jax.ShapeDtypeStruct((m, nt * tn), z_dtype)],
        compiler_params=_cparams(("parallel", "arbitrary")),
        name="in_proj",
    )(h, w, w_last, qk_gain)


def _attn_prompt_kernel(q_ref, kp_ref, kc_ref, vp_ref, vc_ref, o_ref, lse_ref, *, dil, nsub):
    c = pl.program_id(1)
    blk = ATT_SPAN
    step = blk * dil
    hw = q_ref.shape[1] // ATT_HEAD_DIM
    qi = lax.broadcasted_iota(jnp.int32, (blk, blk), 0)
    kj = lax.broadcasted_iota(jnp.int32, (blk, blk), 1)
    cur_ok = kj <= qi
    prev_ok = kj >= qi
    first_bias = jnp.where(c > 0, 0.0, NEG_INF)
    lane = lax.broadcasted_iota(jnp.int32, (blk, LANES), 1)
    nt = (((1,), (1,)), ((), ()))

    def rows(base, r):
        return pl.ds(base + r, blk) if dil == 1 else pl.ds(base + r, blk, stride=dil)

    def one(r, s):
        base = s * step
        lse_tile = jnp.zeros((blk, LANES), F32)
        for n in range(hw):
            sl = slice(n * ATT_HEAD_DIM, (n + 1) * ATT_HEAD_DIM)
            q = q_ref[rows(base, r), sl].astype(BF16)
            k_c = kc_ref[rows(base, r), sl].astype(BF16)
            v_c = vc_ref[rows(base, r), sl].astype(BF16)
            if s == 0:
                k_p = kp_ref[rows(0, r), sl].astype(BF16)
                v_p = vp_ref[rows(0, r), sl].astype(BF16)
            else:
                k_p = kc_ref[rows(base - step, r), sl].astype(BF16)
                v_p = vc_ref[rows(base - step, r), sl].astype(BF16)
            s_c = lax.dot_general(q, k_c, nt, preferred_element_type=F32) * ATT_SCALE
            s_p = lax.dot_general(q, k_p, nt, preferred_element_type=F32) * ATT_SCALE
            if s == 0:
                s_p = s_p + first_bias
            s_c = jnp.where(cur_ok, s_c, NEG_INF)
            s_p = jnp.where(prev_ok, s_p, NEG_INF)
            m = jnp.maximum(jnp.max(s_c, axis=-1, keepdims=True), jnp.max(s_p, axis=-1, keepdims=True))
            p_c = jnp.exp(s_c - m)
            p_p = jnp.exp(s_p - m)
            l = jnp.sum(p_c, axis=-1, keepdims=True) + jnp.sum(p_p, axis=-1, keepdims=True)
            o = (jnp.dot(p_c.astype(BF16), v_c, preferred_element_type=F32)
                 + jnp.dot(p_p.astype(BF16), v_p, preferred_element_type=F32))
            o_ref[rows(base, r), sl] = o / l
            lse_tile = jnp.where(lane == n, m + jnp.log(l), lse_tile)
        lse_ref[rows(base, r), :] = lse_tile

    for s in range(nsub):
        if dil == 1:
            one(0, s)
        else:
            def body(r, carry, s=s):
                one(r, s)
                return carry
            lax.fori_loop(0, dil, body, 0)


def _attn_prompt_group(za, batch, seq, gi, dil, *, nsub, hw):
    step = ATT_SPAN * dil
    tc = nsub * step
    assert seq % tc == 0 and ATT_HEADS % hw == 0
    nh = ATT_HEADS // hw
    w = hw * ATT_HEAD_DIM
    z3 = za.reshape(batch, seq, NA)
    qc, kc, vc = [(col + gi * ATT_GROUP_W) // w for col in (A_QA, A_KA, A_VA)]

    def cur(col):
        return pl.BlockSpec((None, tc, w), lambda b, c, h: (b, c, col + h))

    def prev(col):
        return pl.BlockSpec((None, step, w), lambda b, c, h: (b, jnp.maximum(c * nsub - 1, 0), col + h))

    kern = functools.partial(_attn_prompt_kernel, dil=dil, nsub=nsub)
    o, lse = pl.pallas_call(
        kern,
        grid=(batch, seq // tc, nh),
        in_specs=[cur(qc), prev(kc), cur(kc), prev(vc), cur(vc)],
        out_specs=[pl.BlockSpec((None, tc, w), lambda b, c, h: (b, c, h)),
                   pl.BlockSpec((None, tc, LANES), lambda b, c, h: (b, c, h))],
        out_shape=[jax.ShapeDtypeStruct((batch, seq, ATT_GROUP_W), F32),
                   jax.ShapeDtypeStruct((batch, seq, nh * LANES), F32)],
        compiler_params=_cparams(("parallel", "parallel", "parallel")),
        name=f"attn_prompt_g{gi}",
    )(z3, z3, z3, z3, z3)
    return o.reshape(batch * seq, ATT_GROUP_W), lse.reshape(batch * seq, nh * LANES)


def _attn_sample_kernel(z_ref, k1, v1, k2, v2, k3, v3, o1, o2, o3, l1, l2, l3):
    bb = z_ref.shape[0]
    bufs = ((k1, v1, o1, l1), (k2, v2, o2, l2), (k3, v3, o3, l3))
    lane = lax.broadcasted_iota(jnp.int32, (1, LANES), 1)

    def heads(b, col):
        return jnp.concatenate([z_ref[b, :, col + n * ATT_HEAD_DIM:col + (n + 1) * ATT_HEAD_DIM]
                                for n in range(ATT_HEADS)], axis=0)

    def body(b, carry):
        for gi, (k_ref, v_ref, o_ref, l_ref) in enumerate(bufs):
            q = heads(b, A_QA + gi * ATT_GROUP_W)
            k_new = heads(b, A_KA + gi * ATT_GROUP_W)
            v_new = heads(b, A_VA + gi * ATT_GROUP_W)
            kb = k_ref[b]
            vb = v_ref[b]
            s = jnp.sum(kb * q[None], axis=-1, keepdims=True) * ATT_SCALE
            s_new = jnp.sum(k_new * q, axis=-1, keepdims=True) * ATT_SCALE
            m = jnp.maximum(jnp.max(s, axis=0), s_new)
            p = jnp.exp(s - m[None])
            p_new = jnp.exp(s_new - m)
            l = jnp.sum(p, axis=0) + p_new
            o = (jnp.sum(p * vb, axis=0) + p_new * v_new) / l
            lse = m + jnp.log(l)
            lse_row = jnp.zeros((1, LANES), F32)
            for n in range(ATT_HEADS):
                o_ref[b, :, n * ATT_HEAD_DIM:(n + 1) * ATT_HEAD_DIM] = o[n:n + 1, :]
                lse_row = jnp.where(lane == n, lse[n:n + 1, :], lse_row)
            l_ref[b] = lse_row
        return carry

    lax.fori_loop(0, bb, body, 0)


def _attn_sample(za_s, caches, *, bb):
    bd = za_s.shape[0]
    assert bd % bb == 0
    ins, in_specs = [za_s.reshape(bd, 1, NA)], [pl.BlockSpec((bb, 1, NA), lambda i: (i, 0, 0))]
    for (k_buf, v_buf), (win, dil) in zip(caches, DIL_PATTERNS):
        assert k_buf.shape[1:] == (ATT_SPAN * dil, ATT_HEADS, ATT_HEAD_DIM)
        for buf in (k_buf, v_buf):
            ins.append(buf.reshape(bd, ATT_SPAN, dil, ATT_HEADS, ATT_HEAD_DIM))
            in_specs.append(pl.BlockSpec((bb, ATT_SPAN, None, ATT_HEADS, ATT_HEAD_DIM),
                                         lambda i: (i, 0, 0, 0, 0)))
    outs = pl.pallas_call(
        _attn_sample_kernel,
        grid=(bd // bb,),
        in_specs=in_specs,
        out_specs=[pl.BlockSpec((bb, 1, ATT_GROUP_W), lambda i: (i, 0, 0))] * 3
                  + [pl.BlockSpec((bb, 1, LANES), lambda i: (i, 0, 0))] * 3,
        out_shape=[jax.ShapeDtypeStruct((bd, 1, ATT_GROUP_W), F32)] * 3
                  + [jax.ShapeDtypeStruct((bd, 1, LANES), F32)] * 3,
        compiler_params=_cparams(("parallel",)),
        name="attn_sample",
    )(*ins)
    return [o[:, 0, :] for o in outs[:3]], [l[:, 0, :] for l in outs[3:]]


def _mlstm_prompt_kernel(q_ref, k_ref, v_ref, om_ref, g_ref, bias_ref, h_ref, c_out, n_out, m_out,
                         c_s, n_s, m_s):
    hd = pl.program_id(1)
    ci = pl.program_id(2)
    L = q_ref.shape[0]

    @pl.when(ci == 0)
    def _():
        c_s[...] = jnp.zeros_like(c_s)
        n_s[...] = jnp.zeros_like(n_s)
        m_s[...] = jnp.zeros_like(m_s)

    gates = g_ref[...] + bias_ref[...]
    lane = lax.broadcasted_iota(jnp.int32, (L, LANES), 1)
    li_col = jnp.sum(jnp.where(lane == hd, gates, 0.0), axis=1, keepdims=True)
    lf_col = _log_sigmoid(jnp.sum(jnp.where(lane == hd + M_HEADS, gates, 0.0), axis=1, keepdims=True))
    gates_t = gates.T
    sub = lax.broadcasted_iota(jnp.int32, (LANES, L), 0)
    li_row = jnp.sum(jnp.where(sub == hd, gates_t, 0.0), axis=0, keepdims=True)
    lf_row = _log_sigmoid(jnp.sum(jnp.where(sub == hd + M_HEADS, gates_t, 0.0), axis=0, keepdims=True))

    ti = lax.broadcasted_iota(jnp.int32, (L, L), 0)
    si = lax.broadcasted_iota(jnp.int32, (L, L), 1)
    causal = si <= ti
    b_col = jnp.sum(jnp.where(causal, lf_row, 0.0), axis=1, keepdims=True)
    b_row = jnp.sum(jnp.where(ti <= si, lf_col, 0.0), axis=0, keepdims=True)
    b_end = jnp.sum(lf_row, axis=1, keepdims=True)

    m_prev = m_s[...]
    dmat = jnp.where(causal, b_col - b_row + li_row, NEG_INF)
    inter = b_col + m_prev
    mt = jnp.maximum(inter, jnp.max(dmat, axis=1, keepdims=True))

    qb = q_ref[...]
    q = qb.astype(F32)
    k = k_ref[...].astype(F32) * M_K_SCALE
    vb = v_ref[...]
    qk = lax.dot_general(qb, k.astype(BF16), (((1,), (1,)), ((), ())), preferred_element_type=F32)
    a = jnp.exp(dmat - mt) * qk
    w_inter = jnp.exp(inter - mt)
    num = (jnp.dot(a.astype(BF16), vb, preferred_element_type=F32)
           + w_inter * jnp.dot(qb, c_s[...].astype(BF16), preferred_element_type=F32))
    den = jnp.sum(a, axis=1, keepdims=True) + w_inter * jnp.sum(q * n_s[...], axis=1, keepdims=True)
    h = num / jnp.maximum(jnp.abs(den), jnp.exp(-mt))
    h_ref[...] = (jax.nn.sigmoid(om_ref[...].astype(F32)) * h).astype(h_ref.dtype)

    g_col = b_end - b_col + li_col
    g_row = b_end - b_row + li_row
    m_new = jnp.maximum(b_end + m_prev, jnp.max(g_row, axis=1, keepdims=True))
    decay = jnp.exp(b_end + m_prev - m_new)
    kw = jnp.exp(g_col - m_new) * k
    c_s[...] = decay * c_s[...] + jnp.dot(kw.T.astype(BF16), vb, preferred_element_type=F32)
    n_s[...] = decay * n_s[...] + jnp.sum(kw, axis=0, keepdims=True)
    m_s[...] = m_new

    @pl.when(ci == pl.num_programs(2) - 1)
    def _():
        c_out[...] = c_s[...]
        n_out[...] = n_s[...]
        m_out[...] = jnp.broadcast_to(m_s[...], m_out.shape)


def _mlstm_prompt(zif, zb, gate_bias, batch, seq, *, chunk):
    assert seq % chunk == 0
    nc = seq // chunk
    zif3 = zif.reshape(batch, seq, LANES)
    zb3 = zb.reshape(batch, seq, NB)
    qk_blk = lambda col: pl.BlockSpec((None, chunk, M_QK_DIM), lambda b, h, c: (b, c, col // M_QK_DIM + h))
    v_blk = lambda col: pl.BlockSpec((None, chunk, M_V_DIM), lambda b, h, c: (b, c, col // M_V_DIM + h))
    hm, c1, n1, m1 = pl.pallas_call(
        _mlstm_prompt_kernel,
        grid=(batch, M_HEADS, nc),
        in_specs=[qk_blk(B_QM), qk_blk(B_KM), v_blk(B_VM), v_blk(B_OM),
                  pl.BlockSpec((None, chunk, LANES), lambda b, h, c: (b, c, 0)),
                  pl.BlockSpec((1, LANES), lambda b, h, c: (0, 0))],
        out_specs=[pl.BlockSpec((None, chunk, M_V_DIM), lambda b, h, c: (b, c, h)),
                   pl.BlockSpec((None, None, M_QK_DIM, M_V_DIM), lambda b, h, c: (b, h, 0, 0)),
                   pl.BlockSpec((None, None, 1, M_QK_DIM), lambda b, h, c: (b, h, 0, 0)),
                   pl.BlockSpec((None, None, 1, LANES), lambda b, h, c: (b, h, 0, 0))],
        out_shape=[jax.ShapeDtypeStruct((batch, seq, M_V_WIDTH), BF16),
                   jax.ShapeDtypeStruct((batch, M_HEADS, M_QK_DIM, M_V_DIM), F32),
                   jax.ShapeDtypeStruct((batch, M_HEADS, 1, M_QK_DIM), F32),
                   jax.ShapeDtypeStruct((batch, M_HEADS, 1, LANES), F32)],
        scratch_shapes=[pltpu.VMEM((M_QK_DIM, M_V_DIM), F32), pltpu.VMEM((1, M_QK_DIM), F32),
                        pltpu.VMEM((1, 1), F32)],
        compiler_params=_cparams(("parallel", "parallel", "arbitrary")),
        name="mlstm_prompt",
    )(zb3, zb3, zb3, zb3, zif3, gate_bias)
    return hm.reshape(batch * seq, M_V_WIDTH), c1, n1[:, :, 0, :], m1[:, :, 0, 0]


def _row_to_col(row):
    n = row.shape[1]
    return jnp.broadcast_to(row, (LANES, n)).T[:, 0:1]


def _mlstm_sample_kernel(zif_ref, zb_ref, bias_ref, c_ref, n_ref, m_ref, h_ref, c_out, n_out, m_out):
    gates = zif_ref[...] + bias_ref[...]
    lane = lax.broadcasted_iota(jnp.int32, (1, LANES), 1)
    m_row = jnp.zeros((1, LANES), F32)
    for h in range(M_HEADS):
        q = zb_ref[:, B_QM + h * M_QK_DIM:B_QM + (h + 1) * M_QK_DIM].astype(F32)
        k = zb_ref[:, B_KM + h * M_QK_DIM:B_KM + (h + 1) * M_QK_DIM].astype(F32) * M_K_SCALE
        v = zb_ref[:, B_VM + h * M_V_DIM:B_VM + (h + 1) * M_V_DIM].astype(F32)
        om = zb_ref[:, B_OM + h * M_V_DIM:B_OM + (h + 1) * M_V_DIM].astype(F32)
        li = gates[:, h:h + 1]
        lf = _log_sigmoid(gates[:, M_HEADS + h:M_HEADS + h + 1])
        m0 = m_ref[:, h:h + 1]
        c0 = c_ref[h]
        n0 = n_ref[h:h + 1, :]
        inter = lf + m0
        mt = jnp.maximum(inter, li)
        a = jnp.exp(li - mt) * jnp.sum(q * k, axis=1, keepdims=True)
        w_inter = jnp.exp(inter - mt)
        q_c = jnp.sum(_row_to_col(q) * c0, axis=0, keepdims=True)
        num = a * v + w_inter * q_c
        den = a + w_inter * jnp.sum(q * n0, axis=1, keepdims=True)
        hv = num / jnp.maximum(jnp.abs(den), jnp.exp(-mt))
        h_ref[:, h * M_V_DIM:(h + 1) * M_V_DIM] = (jax.nn.sigmoid(om) * hv).astype(h_ref.dtype)
        m_new = jnp.maximum(inter, li)
        decay = jnp.exp(inter - m_new)
        ws = jnp.exp(li - m_new)
        c_out[h] = decay * c0 + _row_to_col(ws * k) * v
        n_out[h:h + 1, :] = decay * n0 + ws * k
        m_row = jnp.where(lane == h, m_new, m_row)
    m_out[...] = m_row


def _mlstm_sample(zif_s, zb_s, gate_bias, c0, n0, m0):
    bd = zif_s.shape[0]
    hm, c1, n1, m1 = pl.pallas_call(
        _mlstm_sample_kernel,
        grid=(bd,),
        in_specs=[pl.BlockSpec((None, 1, LANES), lambda b: (b, 0, 0)),
                  pl.BlockSpec((None, 1, NB), lambda b: (b, 0, 0)),
                  pl.BlockSpec((1, LANES), lambda b: (0, 0)),
                  pl.BlockSpec((None, M_HEADS, M_QK_DIM, M_V_DIM), lambda b: (b, 0, 0, 0)),
                  pl.BlockSpec((None, M_HEADS, M_QK_DIM), lambda b: (b, 0, 0)),
                  pl.BlockSpec((None, 1, M_HEADS), lambda b: (b, 0, 0))],
        out_specs=[pl.BlockSpec((None, 1, M_V_WIDTH), lambda b: (b, 0, 0)),
                   pl.BlockSpec((None, M_HEADS, M_QK_DIM, M_V_DIM), lambda b: (b, 0, 0, 0)),
                   pl.BlockSpec((None, M_HEADS, M_QK_DIM), lambda b: (b, 0, 0)),
                   pl.BlockSpec((None, 1, LANES), lambda b: (b, 0, 0))],
        out_shape=[jax.ShapeDtypeStruct((bd, 1, M_V_WIDTH), BF16),
                   jax.ShapeDtypeStruct((bd, M_HEADS, M_QK_DIM, M_V_DIM), F32),
                   jax.ShapeDtypeStruct((bd, M_HEADS, M_QK_DIM), F32),
                   jax.ShapeDtypeStruct((bd, 1, LANES), F32)],
        compiler_params=_cparams(("parallel",)),
        name="mlstm_sample",
    )(zif_s.reshape(bd, 1, LANES), zb_s.reshape(bd, 1, NB), gate_bias, c0, n0, m0.reshape(bd, 1, M_HEADS))
    return hm.reshape(bd, M_V_WIDTH), c1, n1, m1[:, 0, :M_HEADS]


def _gate_cols(zt_ref, start, width):
    lo = (start // LANES) * LANES
    hi = -(-(start + width) // LANES) * LANES
    wide = zt_ref[:, lo:hi].astype(F32)
    return wide[:, start - lo:start - lo + width]


def _merge_kernel(o1, o2, o3, l1, l2, l3, hm_ref, zt_ref, x_ref, wpa_ref, wpm_ref, wo_ref, g2_ref,
                  x1_ref, h2_ref, *, hws):
    os_ = (o1, o2, o3)
    parts = []
    for n in range(ATT_HEADS):
        sl = slice(n * ATT_HEAD_DIM, (n + 1) * ATT_HEAD_DIM)
        ls = []
        for l_ref, hw in zip((l1, l2, l3), hws):
            ln = (n // hw) * LANES + n % hw
            ls.append(l_ref[:, ln:ln + 1])
        mx = jnp.maximum(jnp.maximum(ls[0], ls[1]), ls[2])
        es = [jnp.exp(l - mx) for l in ls]
        tot = es[0] + es[1] + es[2]
        parts.append((es[0] / tot) * o1[:, sl] + (es[1] / tot) * o2[:, sl] + (es[2] / tot) * o3[:, sl])
    att = jnp.concatenate(parts, axis=1).astype(BF16)
    pa = jnp.dot(att, wpa_ref[...], preferred_element_type=F32)
    pm = jnp.dot(hm_ref[...], wpm_ref[...], preferred_element_type=F32)
    d = x_ref.shape[1]
    merged = (jax.nn.sigmoid(_gate_cols(zt_ref, T_GA, d)) * pa + jax.nn.sigmoid(_gate_cols(zt_ref, T_GB, d)) * pm)
    x1 = x_ref[...] + jnp.dot(merged.astype(BF16), wo_ref[...], preferred_element_type=F32)
    x1_ref[...] = x1
    ms = jnp.mean(x1 * x1, axis=-1, keepdims=True)
    h2_ref[...] = (x1 * lax.rsqrt(ms + RMS_EPS) * g2_ref[...]).astype(BF16)


def _merge(o_list, lse_list, hm, zt, x, wpa, wpm, wo, g2, *, tm):
    m, d = x.shape
    assert m % tm == 0 and zt.shape[1] >= T_END
    row_blk = lambda w: pl.BlockSpec((tm, w), lambda i: (i, 0))
    const = lambda shape: pl.BlockSpec(shape, lambda i: (0, 0), pipeline_mode=pl.Buffered(1))
    hws = tuple(ATT_HEADS * LANES // l.shape[1] for l in lse_list)
    return pl.pallas_call(
        functools.partial(_merge_kernel, hws=hws),
        grid=(m // tm,),
        in_specs=[row_blk(ATT_GROUP_W)] * 3 + [row_blk(l.shape[1]) for l in lse_list]
                 + [row_blk(M_V_WIDTH), row_blk(zt.shape[1]), row_blk(d),
                    const(wpa.shape), const(wpm.shape), const(wo.shape), const((1, d))],
        out_specs=[row_blk(d), row_blk(d)],
        out_shape=[jax.ShapeDtypeStruct((m, d), F32), jax.ShapeDtypeStruct((m, d), BF16)],
        compiler_params=_cparams(("parallel",)),
        name="merge",
    )(*o_list, *lse_list, hm, zt, x, wpa, wpm, wo, g2)


def _ffn_body(h2_ref, wu_ref, wg_ref, cw_ref, cb_ref, wd_ref, x1_ref, y_ref, g_prev2, g_prev1):
    j = pl.program_id(1)
    h2 = h2_ref[...]
    u = jnp.dot(h2, wu_ref[...], preferred_element_type=F32)
    g = jnp.dot(h2, wg_ref[...], preferred_element_type=F32)
    gconv = cb_ref[...] + ((g_prev2(g) * cw_ref[0:1, :] + g_prev1(g) * cw_ref[1:2, :]) + g * cw_ref[2:3, :])
    act = (jax.nn.gelu(gconv) * u).astype(BF16)
    down = jnp.dot(act, wd_ref[...], preferred_element_type=F32)

    @pl.when(j == 0)
    def _():
        y_ref[...] = x1_ref[...] + down

    @pl.when(j > 0)
    def _():
        y_ref[...] += down

    return g


def _ffn_prompt_kernel(h2_ref, wu_ref, wg_ref, cw_ref, cb_ref, wd_ref, x1_ref, y_ref, tail_ref, prev_s,
                       *, tiles_per_seq):
    i = pl.program_id(0)
    j = pl.program_id(1)
    tm = h2_ref.shape[0]
    prev = jnp.where(i % tiles_per_seq == 0, 0.0, prev_s[j])
    p2, p1 = prev[6:7, :], prev[7:8, :]

    def g_prev1(g):
        r = lax.broadcasted_iota(jnp.int32, g.shape, 0)
        return jnp.where(r == 0, p1, pltpu.roll(g, 1, axis=0))

    def g_prev2(g):
        r = lax.broadcasted_iota(jnp.int32, g.shape, 0)
        return jnp.where(r == 0, p2, jnp.where(r == 1, p1, pltpu.roll(g, 2, axis=0)))

    g = _ffn_body(h2_ref, wu_ref, wg_ref, cw_ref, cb_ref, wd_ref, x1_ref, y_ref, g_prev2, g_prev1)
    prev_s[j] = g[tm - 8:tm, :]
    tail_ref[...] = g[tm - 8:tm, :]


def _ffn_sample_kernel(h2_ref, wu_ref, wg_ref, cw_ref, cb_ref, wd_ref, x1_ref, b2_ref, b1_ref, y_ref, g_ref):
    g = _ffn_body(h2_ref, wu_ref, wg_ref, cw_ref, cb_ref, wd_ref, x1_ref, y_ref,
                  lambda g: b2_ref[...], lambda g: b1_ref[...])
    g_ref[...] = g


def _ffn_specs(tm, tf, d, nf):
    return [pl.BlockSpec((tm, d), lambda i, j: (i, 0)),
            pl.BlockSpec((d, tf), lambda i, j: (0, j)),
            pl.BlockSpec((d, tf), lambda i, j: (0, nf + j)),
            pl.BlockSpec((CONV_W, tf), lambda i, j: (0, j)),
            pl.BlockSpec((1, tf), lambda i, j: (0, j)),
            pl.BlockSpec((tf, d), lambda i, j: (j, 0)),
            pl.BlockSpec((tm, d), lambda i, j: (i, 0))]


def _ffn_prompt(h2, x1, w_in, conv_w, conv_b, w_down, seq, *, tm, tf):
    m, d = x1.shape
    assert m % tm == 0 and seq % tm == 0 and D_FF % tf == 0 and tm % 8 == 0
    nf = D_FF // tf
    kern = functools.partial(_ffn_prompt_kernel, tiles_per_seq=seq // tm)
    y, tails = pl.pallas_call(
        kern,
        grid=(m // tm, nf),
        in_specs=_ffn_specs(tm, tf, d, nf),
        out_specs=[pl.BlockSpec((tm, d), lambda i, j: (i, 0)),
                   pl.BlockSpec((None, 8, tf), lambda i, j: (i, 0, j))],
        out_shape=[jax.ShapeDtypeStruct((m, d), F32), jax.ShapeDtypeStruct((m // tm, 8, D_FF), F32)],
        scratch_shapes=[pltpu.VMEM((nf, 8, tf), F32)],
        compiler_params=_cparams(("arbitrary", "arbitrary")),
        name="ffn_prompt",
    )(h2, w_in, w_in, conv_w, conv_b, w_down, x1)
    return y, tails


def _ffn_sample(h2, x1, w_in, conv_w, conv_b, w_down, conv_buf, *, tf):
    m, d = x1.shape
    nf = D_FF // tf
    buf2d = conv_buf.reshape(m, (CONV_W - 1) * D_FF)
    return pl.pallas_call(
        _ffn_sample_kernel,
        grid=(1, nf),
        in_specs=_ffn_specs(m, tf, d, nf) + [pl.BlockSpec((m, tf), lambda i, j: (0, j)),
                                             pl.BlockSpec((m, tf), lambda i, j: (0, nf + j))],
        out_specs=[pl.BlockSpec((m, d), lambda i, j: (i, 0)), pl.BlockSpec((m, tf), lambda i, j: (0, j))],
        out_shape=[jax.ShapeDtypeStruct((m, d), F32), jax.ShapeDtypeStruct((m, D_FF), F32)],
        compiler_params=_cparams(("arbitrary", "arbitrary")),
        name="ffn_sample",
    )(h2, w_in, w_in, conv_w, conv_b, w_down, x1, buf2d, buf2d)


IN_PROJ_TN = 512


def _w_in_last(w_in):
    rem = IN_COLS % IN_PROJ_TN
    return jnp.pad(w_in[:, IN_COLS - rem:], ((0, 0), (0, LANES - rem)))


def _qk_gain_row(q_norm, k_norm):
    reps = N_GROUPS * ATT_HEADS
    return jnp.concatenate([jnp.tile(q_norm, reps), jnp.tile(k_norm, reps), jnp.zeros((NA - A_VA,), F32)])[None, :]


ATTN_PROMPT_TILING = ((4, 4), (1, 1), (1, 1))


def _layer(x_prompt, x_sample, caches, norm_mix, w_in, q_norm, k_norm, b_igate, b_fgate, w_proj_att,
           w_proj_mlstm, w_out, norm_ffn, w_ffn_in, conv_w, conv_b, w_ffn_down):
    batch, seq, d = x_prompt.shape
    bd = x_sample.shape[0]
    assert x_sample.shape[1] == 1 and d == D_MODEL
    (ck1, cv1, ck2, cv2, ck3, cv3, st_c, st_n, st_m, st_conv) = caches

    w_last = _w_in_last(w_in)
    qk_gain = _qk_gain_row(q_norm, k_norm)
    g1 = norm_mix[None, :]
    g2 = norm_ffn[None, :]
    gate_bias = jnp.concatenate([b_igate, b_fgate, jnp.zeros((LANES - 2 * M_HEADS,), F32)])[None, :]
    wpa = w_proj_att.astype(BF16)
    wpm = w_proj_mlstm.astype(BF16)
    wo = w_out.astype(BF16)
    w_ff = w_ffn_in.astype(BF16)
    w_dn = w_ffn_down.astype(BF16)
    cb = conv_b[None, :]

    xp = x_prompt.reshape(batch * seq, d)
    xs = x_sample.reshape(bd, d)

    h_p = _rmsnorm(xp, g1, tm=512)
    za_p, zb_p, zif_p, zt_p = _in_proj(h_p, w_in, w_last, qk_gain, tm=2048, tn=IN_PROJ_TN, z_dtype=BF16)
    o_p, lse_p = zip(*[_attn_prompt_group(za_p, batch, seq, gi, dil, nsub=nsub, hw=hw)
                       for gi, ((_, dil), (nsub, hw)) in enumerate(zip(DIL_PATTERNS, ATTN_PROMPT_TILING))])
    hm_p, p_c, p_n, p_m = _mlstm_prompt(zif_p, zb_p, gate_bias, batch, seq, chunk=256)
    x1_p, h2_p = _merge(o_p, lse_p, hm_p, zt_p, xp, wpa, wpm, wo, g2, tm=256)
    y_p, tails = _ffn_prompt(h2_p, x1_p, w_ff, conv_w, cb, w_dn, seq, tm=512, tf=512)

    za_p3 = za_p.reshape(batch, seq, NA)
    p_kv = []
    for gi, (win, _) in enumerate(DIL_PATTERNS):
        keep = min(win, seq)
        for col in (A_KA, A_VA):
            lo = col + gi * ATT_GROUP_W
            p_kv.append(za_p3[:, seq - keep:, lo:lo + ATT_GROUP_W].reshape(batch, keep, ATT_HEADS, ATT_HEAD_DIM))
    tiles_per_seq = seq // 512
    p_conv = tails.reshape(batch, tiles_per_seq, 8, D_FF)[:, -1, 8 - (CONV_W - 1):, :]

    h_s = _rmsnorm(xs, g1, tm=bd)
    za_s, zb_s, zif_s, zt_s = _in_proj(h_s, w_in, w_last, qk_gain, tm=bd, tn=IN_PROJ_TN, z_dtype=F32)
    o_s, lse_s = _attn_sample(za_s, [(ck1, cv1), (ck2, cv2), (ck3, cv3)], bb=4)
    hm_s, s_c, s_n, s_m = _mlstm_sample(zif_s, zb_s, gate_bias, st_c, st_n, st_m)
    x1_s, h2_s = _merge(o_s, lse_s, hm_s, zt_s, xs, wpa, wpm, wo, g2, tm=bd)
    y_s, g_s = _ffn_sample(h2_s, x1_s, w_ff, conv_w, cb, w_dn, st_conv, tf=512)

    s_kv = []
    for gi in range(N_GROUPS):
        for col in (A_KA, A_VA):
            lo = col + gi * ATT_GROUP_W
            s_kv.append(za_s[:, lo:lo + ATT_GROUP_W].reshape(bd, 1, ATT_HEADS, ATT_HEAD_DIM))
    s_conv = jnp.stack([st_conv[:, 1, :], g_s], axis=1)

    p_state = p_kv + [p_c, p_n, p_m, p_conv]
    s_state = s_kv + [s_c, s_n, s_m, s_conv]
    return y_p.reshape(batch, seq, d), y_s.reshape(bd, 1, d), p_state, s_state


def kernel(x_prompt, x_sample, cache_k_w128, cache_v_w128, cache_k_w512, cache_v_w512, cache_k_w2048,
           cache_v_w2048, state_mlstm_C, state_mlstm_n, state_mlstm_m, state_ffn_conv, norm_mix, w_in, q_norm,
           k_norm, b_igate, b_fgate, w_proj_att, w_proj_mlstm, w_out, norm_ffn, w_ffn_in, conv_w, conv_b,
           w_ffn_down):
    assert norm_mix.shape[0] == 1
    caches = [c[0] for c in (cache_k_w128, cache_v_w128, cache_k_w512, cache_v_w512, cache_k_w2048,
                             cache_v_w2048, state_mlstm_C, state_mlstm_n, state_mlstm_m, state_ffn_conv)]
    weights = [w[0] for w in (norm_mix, w_in, q_norm, k_norm, b_igate, b_fgate, w_proj_att, w_proj_mlstm,
                              w_out, norm_ffn, w_ffn_in, conv_w, conv_b, w_ffn_down)]
    y_p, y_s, p_state, s_state = _layer(x_prompt, x_sample, caches, *weights)
    return (y_p, y_s, *[a[None] for a in p_state], *[a[None] for a in s_state])
```

```python
import functools

import jax
import jax.numpy as jnp
from jax import lax
from jax.experimental import pallas as pl
from jax.experimental.pallas import tpu as pltpu

F32 = jnp.float32
BF16 = jnp.bfloat16

RMS_EPS = 1e-6
NEG_INF = -1e30
LANES = 128
VMEM_LIMIT = 56 * 1024 * 1024

D_MODEL = 2048
DIL_PATTERNS = ((128, 1), (512, 4), (2048, 16))
N_GROUPS = 3
ATT_HEADS = 4
ATT_HEAD_DIM = 128
ATT_SPAN = 128
ATT_SCALE = ATT_HEAD_DIM ** -0.5
ATT_GROUP_W = ATT_HEADS * ATT_HEAD_DIM
ATT_WIDTH = N_GROUPS * ATT_GROUP_W
M_HEADS = 4
M_QK_DIM = D_MODEL // (2 * M_HEADS)
M_V_DIM = D_MODEL // M_HEADS
M_QK_WIDTH = M_HEADS * M_QK_DIM
M_V_WIDTH = M_HEADS * M_V_DIM
M_K_SCALE = M_QK_DIM ** -0.5
D_FF = ((8 * D_MODEL // 3 + 255) // 256) * 256
CONV_W = 3
SPLIT_SIZES = (ATT_WIDTH, ATT_WIDTH, ATT_WIDTH, M_QK_WIDTH, M_QK_WIDTH, M_V_WIDTH, M_V_WIDTH,
               M_HEADS, M_HEADS, D_MODEL, D_MODEL)

IN_COLS = sum(SPLIT_SIZES)
A_QA = 0
A_KA = A_QA + ATT_WIDTH
A_VA = A_KA + ATT_WIDTH
NA = A_VA + ATT_WIDTH
B_QM = 0
B_KM = B_QM + M_QK_WIDTH
B_VM = B_KM + M_QK_WIDTH
B_OM = B_VM + M_V_WIDTH
NB = B_OM + M_V_WIDTH
N_IF = 2 * M_HEADS
T_GA = 0
T_GB = T_GA + D_MODEL
NT = T_GB + D_MODEL
assert NA + NB + N_IF + NT == IN_COLS


def _cparams(sem):
    return pltpu.CompilerParams(dimension_semantics=sem, vmem_limit_bytes=VMEM_LIMIT)


def _log_sigmoid(x):
    return jnp.minimum(x, 0.0) - jnp.log(1.0 + jnp.exp(-jnp.abs(x)))


def _rmsnorm_kernel(x_ref, g_ref, h_ref):
    x = x_ref[...]
    ms = jnp.mean(x * x, axis=-1, keepdims=True)
    h_ref[...] = (x * lax.rsqrt(ms + RMS_EPS) * g_ref[...]).astype(h_ref.dtype)


def _rmsnorm(x, gain, *, tm):
    m, d = x.shape
    assert m % tm == 0
    return pl.pallas_call(
        _rmsnorm_kernel,
        grid=(m // tm,),
        in_specs=[pl.BlockSpec((tm, d), lambda i: (i, 0)), pl.BlockSpec((1, d), lambda i: (0, 0))],
        out_specs=pl.BlockSpec((tm, d), lambda i: (i, 0)),
        out_shape=jax.ShapeDtypeStruct((m, d), BF16),
        compiler_params=_cparams(("parallel",)),
        name="rmsnorm",
    )(x, gain)


def _in_proj_kernel(h_ref, w_ref, wn_ref, wif_ref, qkg_ref, za_ref, zb_ref, zt_ref, zif_ref, *, n_norm, na, nb, nt):
    j = pl.program_id(1)

    def z_of(w):
        return lax.dot_general(h_ref[...], w.astype(BF16), (((1,), (1,)), ((), ())), preferred_element_type=F32)

    @pl.when(j < na)
    def _():
        za_ref[...] = z_of(w_ref[...])

    @pl.when(j < n_norm)
    def _():
        for c in range(za_ref.shape[1] // LANES):
            sl = slice(c * LANES, (c + 1) * LANES)
            zc = za_ref[:, sl]
            ms = jnp.mean(zc * zc, axis=-1, keepdims=True)
            za_ref[:, sl] = zc * lax.rsqrt(ms + RMS_EPS) * qkg_ref[:, sl]

    @pl.when((j >= na) & (j < na + nb))
    def _():
        zb_ref[...] = z_of(w_ref[...]).astype(zb_ref.dtype)

    @pl.when((j >= na + nb) & (j < na + nb + nt))
    def _():
        w = jnp.concatenate([w_ref[N_IF:, :], wn_ref[...]], axis=0)
        zt_ref[...] = z_of(w).astype(zt_ref.dtype)

    @pl.when(j == na + nb + nt)
    def _():
        w = jnp.concatenate([wif_ref[...], jnp.zeros((LANES - N_IF, wif_ref.shape[1]), F32)], axis=0)
        zif_ref[...] = z_of(w)


def _in_proj(h, wt, qk_gain, *, tm, tn, z_dtype):
    m, d = h.shape
    assert wt.shape == (IN_COLS, d) and m % tm == 0 and N_IF == 8
    assert NA % tn == 0 and NB % tn == 0 and NT % tn == 0 and A_VA % tn == 0 and tn % N_IF == 0
    na, nb, nt = NA // tn, NB // tn, NT // tn
    nw = na + nb + nt
    g_if = (NA + NB) // N_IF
    g_tn = tn // N_IF
    kern = functools.partial(_in_proj_kernel, n_norm=A_VA // tn, na=na, nb=nb, nt=nt)
    return pl.pallas_call(
        kern,
        grid=(m // tm, nw + 1),
        in_specs=[
            pl.BlockSpec((tm, d), lambda i, j: (i, 0)),
            pl.BlockSpec((tn, d), lambda i, j: (jnp.minimum(j, nw - 1), 0)),
            pl.BlockSpec((N_IF, d), lambda i, j: (g_if + g_tn * (jnp.clip(j, na + nb, nw - 1) - (na + nb) + 1), 0)),
            pl.BlockSpec((N_IF, d), lambda i, j: (g_if, 0)),
            pl.BlockSpec((1, tn), lambda i, j: (0, jnp.minimum(j, na - 1))),
        ],
        out_specs=[pl.BlockSpec((tm, tn), lambda i, j: (i, jnp.minimum(j, na - 1))),
                   pl.BlockSpec((tm, tn), lambda i, j: (i, jnp.clip(j - na, 0, nb - 1))),
                   pl.BlockSpec((tm, tn), lambda i, j: (i, jnp.clip(j - na - nb, 0, nt - 1))),
                   pl.BlockSpec((tm, LANES), lambda i, j: (i, 0))],
        out_shape=[jax.ShapeDtypeStruct((m, NA), F32), jax.ShapeDtypeStruct((m, NB), z_dtype),
                   jax.ShapeDtypeStruct((m, NT), z_dtype), jax.ShapeDtypeStruct((m, LANES), F32)],
        compiler_params=_cparams(("parallel", "arbitrary")),
        name="in_proj",
    )(h, wt, wt, wt, qk_gain)


def _attn_prompt_kernel(q_ref, kp_ref, kc_ref, vp_ref, vc_ref, o_ref, lse_ref, *, dil, nsub):
    c = pl.program_id(1)
    blk = ATT_SPAN
    step = blk * dil
    hw = q_ref.shape[1] // ATT_HEAD_DIM
    qi = lax.broadcasted_iota(jnp.int32, (blk, blk), 0)
    kj = lax.broadcasted_iota(jnp.int32, (blk, blk), 1)
    cur_ok = kj <= qi
    prev_ok = kj >= qi
    first_bias = jnp.where(c > 0, 0.0, NEG_INF)
    lane = lax.broadcasted_iota(jnp.int32, (blk, LANES), 1)
    nt = (((1,), (1,)), ((), ()))

    def rows(base, r):
        return pl.ds(base + r, blk) if dil == 1 else pl.ds(base + r, blk, stride=dil)

    def one(r, s):
        base = s * step
        lse_tile = jnp.zeros((blk, LANES), F32)
        for n in range(hw):
            sl = slice(n * ATT_HEAD_DIM, (n + 1) * ATT_HEAD_DIM)
            q = q_ref[rows(base, r), sl].astype(BF16)
            k_c = kc_ref[rows(base, r), sl].astype(BF16)
            v_c = vc_ref[rows(base, r), sl].astype(BF16)
            if s == 0:
                k_p = kp_ref[rows(0, r), sl].astype(BF16)
                v_p = vp_ref[rows(0, r), sl].astype(BF16)
            else:
                k_p = kc_ref[rows(base - step, r), sl].astype(BF16)
                v_p = vc_ref[rows(base - step, r), sl].astype(BF16)
            s_c = lax.dot_general(q, k_c, nt, preferred_element_type=F32) * ATT_SCALE
            s_p = lax.dot_general(q, k_p, nt, preferred_element_type=F32) * ATT_SCALE
            if s == 0:
                s_p = s_p + first_bias
            s_c = jnp.where(cur_ok, s_c, NEG_INF)
            s_p = jnp.where(prev_ok, s_p, NEG_INF)
            m = jnp.maximum(jnp.max(s_c, axis=-1, keepdims=True), jnp.max(s_p, axis=-1, keepdims=True))
            p_c = jnp.exp(s_c - m)
            p_p = jnp.exp(s_p - m)
            l = jnp.sum(p_c, axis=-1, keepdims=True) + jnp.sum(p_p, axis=-1, keepdims=True)
            o = (jnp.dot(p_c.astype(BF16), v_c, preferred_element_type=F32)
                 + jnp.dot(p_p.astype(BF16), v_p, preferred_element_type=F32))
            o_ref[rows(base, r), sl] = o / l
            lse_tile = jnp.where(lane == n, m + jnp.log(l), lse_tile)
        lse_ref[rows(base, r), :] = lse_tile

    for s in range(nsub):
        if dil == 1:
            one(0, s)
        else:
            def body(r, carry, s=s):
                one(r, s)
                return carry
            lax.fori_loop(0, dil, body, 0)


def _attn_prompt_group(za, batch, seq, gi, dil, *, nsub, hw):
    step = ATT_SPAN * dil
    tc = nsub * step
    assert seq % tc == 0 and ATT_HEADS % hw == 0
    nh = ATT_HEADS // hw
    w = hw * ATT_HEAD_DIM
    z3 = za.reshape(batch, seq, NA)
    qc, kc, vc = [(col + gi * ATT_GROUP_W) // w for col in (A_QA, A_KA, A_VA)]

    def cur(col):
        return pl.BlockSpec((None, tc, w), lambda b, c, h: (b, c, col + h))

    def prev(col):
        return pl.BlockSpec((None, step, w), lambda b, c, h: (b, jnp.maximum(c * nsub - 1, 0), col + h))

    kern = functools.partial(_attn_prompt_kernel, dil=dil, nsub=nsub)
    o, lse = pl.pallas_call(
        kern,
        grid=(batch, seq // tc, nh),
        in_specs=[cur(qc), prev(kc), cur(kc), prev(vc), cur(vc)],
        out_specs=[pl.BlockSpec((None, tc, w), lambda b, c, h: (b, c, h)),
                   pl.BlockSpec((None, tc, LANES), lambda b, c, h: (b, c, h))],
        out_shape=[jax.ShapeDtypeStruct((batch, seq, ATT_GROUP_W), F32),
                   jax.ShapeDtypeStruct((batch, seq, nh * LANES), F32)],
        compiler_params=_cparams(("parallel", "parallel", "parallel")),
        name=f"attn_prompt_g{gi}",
    )(z3, z3, z3, z3, z3)
    return o.reshape(batch * seq, ATT_GROUP_W), lse.reshape(batch * seq, nh * LANES)


def _attn_sample_kernel(z_ref, k1, v1, k2, v2, k3, v3, o1, o2, o3, l1, l2, l3):
    bb = z_ref.shape[0]
    bufs = ((k1, v1, o1, l1), (k2, v2, o2, l2), (k3, v3, o3, l3))
    lane = lax.broadcasted_iota(jnp.int32, (1, LANES), 1)

    def heads(b, col):
        return jnp.concatenate([z_ref[b, :, col + n * ATT_HEAD_DIM:col + (n + 1) * ATT_HEAD_DIM]
                                for n in range(ATT_HEADS)], axis=0)

    def body(b, carry):
        for gi, (k_ref, v_ref, o_ref, l_ref) in enumerate(bufs):
            q = heads(b, A_QA + gi * ATT_GROUP_W)
            k_new = heads(b, A_KA + gi * ATT_GROUP_W)
            v_new = heads(b, A_VA + gi * ATT_GROUP_W)
            kb = k_ref[b]
            vb = v_ref[b]
            s = jnp.sum(kb * q[None], axis=-1, keepdims=True) * ATT_SCALE
            s_new = jnp.sum(k_new * q, axis=-1, keepdims=True) * ATT_SCALE
            m = jnp.maximum(jnp.max(s, axis=0), s_new)
            p = jnp.exp(s - m[None])
            p_new = jnp.exp(s_new - m)
            l = jnp.sum(p, axis=0) + p_new
            o = (jnp.sum(p * vb, axis=0) + p_new * v_new) / l
            lse = m + jnp.log(l)
            lse_row = jnp.zeros((1, LANES), F32)
            for n in range(ATT_HEADS):
                o_ref[b, :, n * ATT_HEAD_DIM:(n + 1) * ATT_HEAD_DIM] = o[n:n + 1, :]
                lse_row = jnp.where(lane == n, lse[n:n + 1, :], lse_row)
            l_ref[b] = lse_row
        return carry

    lax.fori_loop(0, bb, body, 0)


def _attn_sample(za_s, caches, *, bb):
    bd = za_s.shape[0]
    assert bd % bb == 0
    ins, in_specs = [za_s.reshape(bd, 1, NA)], [pl.BlockSpec((bb, 1, NA), lambda i: (i, 0, 0))]
    for (k_buf, v_buf), (win, dil) in zip(caches, DIL_PATTERNS):
        assert k_buf.shape[1:] == (ATT_SPAN * dil, ATT_HEADS, ATT_HEAD_DIM)
        for buf in (k_buf, v_buf):
            ins.append(buf.reshape(bd, ATT_SPAN, dil, ATT_HEADS, ATT_HEAD_DIM))
            in_specs.append(pl.BlockSpec((bb, ATT_SPAN, None, ATT_HEADS, ATT_HEAD_DIM),
                                         lambda i: (i, 0, 0, 0, 0)))
    outs = pl.pallas_call(
        _attn_sample_kernel,
        grid=(bd // bb,),
        in_specs=in_specs,
        out_specs=[pl.BlockSpec((bb, 1, ATT_GROUP_W), lambda i: (i, 0, 0))] * 3
                  + [pl.BlockSpec((bb, 1, LANES), lambda i: (i, 0, 0))] * 3,
        out_shape=[jax.ShapeDtypeStruct((bd, 1, ATT_GROUP_W), F32)] * 3
                  + [jax.ShapeDtypeStruct((bd, 1, LANES), F32)] * 3,
        compiler_params=_cparams(("parallel",)),
        name="attn_sample",
    )(*ins)
    return [o[:, 0, :] for o in outs[:3]], [l[:, 0, :] for l in outs[3:]]


def _mlstm_prompt_kernel(q_ref, k_ref, v_ref, om_ref, g_ref, bias_ref, h_ref, c_out, n_out, m_out,
                         c_s, n_s, m_s):
    hd = pl.program_id(1)
    ci = pl.program_id(2)
    L = q_ref.shape[0]

    @pl.when(ci == 0)
    def _():
        c_s[...] = jnp.zeros_like(c_s)
        n_s[...] = jnp.zeros_like(n_s)
        m_s[...] = jnp.zeros_like(m_s)

    gates = g_ref[...] + bias_ref[...]
    lane = lax.broadcasted_iota(jnp.int32, (L, LANES), 1)
    li_col = jnp.sum(jnp.where(lane == hd, gates, 0.0), axis=1, keepdims=True)
    lf_col = _log_sigmoid(jnp.sum(jnp.where(lane == hd + M_HEADS, gates, 0.0), axis=1, keepdims=True))
    gates_t = gates.T
    sub = lax.broadcasted_iota(jnp.int32, (LANES, L), 0)
    li_row = jnp.sum(jnp.where(sub == hd, gates_t, 0.0), axis=0, keepdims=True)
    lf_row = _log_sigmoid(jnp.sum(jnp.where(sub == hd + M_HEADS, gates_t, 0.0), axis=0, keepdims=True))

    ti = lax.broadcasted_iota(jnp.int32, (L, L), 0)
    si = lax.broadcasted_iota(jnp.int32, (L, L), 1)
    causal = si <= ti
    b_col = jnp.sum(jnp.where(causal, lf_row, 0.0), axis=1, keepdims=True)
    b_row = jnp.sum(jnp.where(ti <= si, lf_col, 0.0), axis=0, keepdims=True)
    b_end = jnp.sum(lf_row, axis=1, keepdims=True)

    m_prev = m_s[...]
    dmat = jnp.where(causal, b_col - b_row + li_row, NEG_INF)
    inter = b_col + m_prev
    mt = jnp.maximum(inter, jnp.max(dmat, axis=1, keepdims=True))

    qb = q_ref[...]
    q = qb.astype(F32)
    k = k_ref[...].astype(F32) * M_K_SCALE
    vb = v_ref[...]
    qk = lax.dot_general(qb, k.astype(BF16), (((1,), (1,)), ((), ())), preferred_element_type=F32)
    a = jnp.exp(dmat - mt) * qk
    w_inter = jnp.exp(inter - mt)
    num = (jnp.dot(a.astype(BF16), vb, preferred_element_type=F32)
           + w_inter * jnp.dot(qb, c_s[...].astype(BF16), preferred_element_type=F32))
    den = jnp.sum(a, axis=1, keepdims=True) + w_inter * jnp.sum(q * n_s[...], axis=1, keepdims=True)
    h = num / jnp.maximum(jnp.abs(den), jnp.exp(-mt))
    h_ref[...] = (jax.nn.sigmoid(om_ref[...].astype(F32)) * h).astype(h_ref.dtype)

    g_col = b_end - b_col + li_col
    g_row = b_end - b_row + li_row
    m_new = jnp.maximum(b_end + m_prev, jnp.max(g_row, axis=1, keepdims=True))
    decay = jnp.exp(b_end + m_prev - m_new)
    kw = jnp.exp(g_col - m_new) * k
    c_s[...] = decay * c_s[...] + jnp.dot(kw.T.astype(BF16), vb, preferred_element_type=F32)
    n_s[...] = decay * n_s[...] + jnp.sum(kw, axis=0, keepdims=True)
    m_s[...] = m_new

    @pl.when(ci == pl.num_programs(2) - 1)
    def _():
        c_out[...] = c_s[...]
        n_out[...] = n_s[...]
        m_out[...] = jnp.broadcast_to(m_s[...], m_out.shape)


def _mlstm_prompt(zif, zb, gate_bias, batch, seq, *, chunk):
    assert seq % chunk == 0
    nc = seq // chunk
    zif3 = zif.reshape(batch, seq, LANES)
    zb3 = zb.reshape(batch, seq, NB)
    qk_blk = lambda col: pl.BlockSpec((None, chunk, M_QK_DIM), lambda b, h, c: (b, c, col // M_QK_DIM + h))
    v_blk = lambda col: pl.BlockSpec((None, chunk, M_V_DIM), lambda b, h, c: (b, c, col // M_V_DIM + h))
    hm, c1, n1, m1 = pl.pallas_call(
        _mlstm_prompt_kernel,
        grid=(batch, M_HEADS, nc),
        in_specs=[qk_blk(B_QM), qk_blk(B_KM), v_blk(B_VM), v_blk(B_OM),
                  pl.BlockSpec((None, chunk, LANES), lambda b, h, c: (b, c, 0)),
                  pl.BlockSpec((1, LANES), lambda b, h, c: (0, 0))],
        out_specs=[pl.BlockSpec((None, chunk, M_V_DIM), lambda b, h, c: (b, c, h)),
                   pl.BlockSpec((None, None, M_QK_DIM, M_V_DIM), lambda b, h, c: (b, h, 0, 0)),
                   pl.BlockSpec((None, None, 1, M_QK_DIM), lambda b, h, c: (b, h, 0, 0)),
                   pl.BlockSpec((None, None, 1, LANES), lambda b, h, c: (b, h, 0, 0))],
        out_shape=[jax.ShapeDtypeStruct((batch, seq, M_V_WIDTH), BF16),
                   jax.ShapeDtypeStruct((batch, M_HEADS, M_QK_DIM, M_V_DIM), F32),
                   jax.ShapeDtypeStruct((batch, M_HEADS, 1, M_QK_DIM), F32),
                   jax.ShapeDtypeStruct((batch, M_HEADS, 1, LANES), F32)],
        scratch_shapes=[pltpu.VMEM((M_QK_DIM, M_V_DIM), F32), pltpu.VMEM((1, M_QK_DIM), F32),
                        pltpu.VMEM((1, 1), F32)],
        compiler_params=_cparams(("parallel", "parallel", "arbitrary")),
        name="mlstm_prompt",
    )(zb3, zb3, zb3, zb3, zif3, gate_bias)
    return hm.reshape(batch * seq, M_V_WIDTH), c1, n1[:, :, 0, :], m1[:, :, 0, 0]


def _row_to_col(row):
    n = row.shape[1]
    return jnp.broadcast_to(row, (LANES, n)).T[:, 0:1]


def _mlstm_sample_kernel(zif_ref, zb_ref, bias_ref, c_ref, n_ref, m_ref, h_ref, c_out, n_out, m_out):
    gates = zif_ref[...] + bias_ref[...]
    lane = lax.broadcasted_iota(jnp.int32, (1, LANES), 1)
    m_row = jnp.zeros((1, LANES), F32)
    for h in range(M_HEADS):
        q = zb_ref[:, B_QM + h * M_QK_DIM:B_QM + (h + 1) * M_QK_DIM].astype(F32)
        k = zb_ref[:, B_KM + h * M_QK_DIM:B_KM + (h + 1) * M_QK_DIM].astype(F32) * M_K_SCALE
        v = zb_ref[:, B_VM + h * M_V_DIM:B_VM + (h + 1) * M_V_DIM].astype(F32)
        om = zb_ref[:, B_OM + h * M_V_DIM:B_OM + (h + 1) * M_V_DIM].astype(F32)
        li = gates[:, h:h + 1]
        lf = _log_sigmoid(gates[:, M_HEADS + h:M_HEADS + h + 1])
        m0 = m_ref[:, h:h + 1]
        c0 = c_ref[h]
        n0 = n_ref[h:h + 1, :]
        inter = lf + m0
        mt = jnp.maximum(inter, li)
        a = jnp.exp(li - mt) * jnp.sum(q * k, axis=1, keepdims=True)
        w_inter = jnp.exp(inter - mt)
        q_c = jnp.sum(_row_to_col(q) * c0, axis=0, keepdims=True)
        num = a * v + w_inter * q_c
        den = a + w_inter * jnp.sum(q * n0, axis=1, keepdims=True)
        hv = num / jnp.maximum(jnp.abs(den), jnp.exp(-mt))
        h_ref[:, h * M_V_DIM:(h + 1) * M_V_DIM] = (jax.nn.sigmoid(om) * hv).astype(h_ref.dtype)
        m_new = jnp.maximum(inter, li)
        decay = jnp.exp(inter - m_new)
        ws = jnp.exp(li - m_new)
        c_out[h] = decay * c0 + _row_to_col(ws * k) * v
        n_out[h:h + 1, :] = decay * n0 + ws * k
        m_row = jnp.where(lane == h, m_new, m_row)
    m_out[...] = m_row


def _mlstm_sample(zif_s, zb_s, gate_bias, c0, n0, m0):
    bd = zif_s.shape[0]
    hm, c1, n1, m1 = pl.pallas_call(
        _mlstm_sample_kernel,
        grid=(bd,),
        in_specs=[pl.BlockSpec((None, 1, LANES), lambda b: (b, 0, 0)),
                  pl.BlockSpec((None, 1, NB), lambda b: (b, 0, 0)),
                  pl.BlockSpec((1, LANES), lambda b: (0, 0)),
                  pl.BlockSpec((None, M_HEADS, M_QK_DIM, M_V_DIM), lambda b: (b, 0, 0, 0)),
                  pl.BlockSpec((None, M_HEADS, M_QK_DIM), lambda b: (b, 0, 0)),
                  pl.BlockSpec((None, 1, M_HEADS), lambda b: (b, 0, 0))],
        out_specs=[pl.BlockSpec((None, 1, M_V_WIDTH), lambda b: (b, 0, 0)),
                   pl.BlockSpec((None, M_HEADS, M_QK_DIM, M_V_DIM), lambda b: (b, 0, 0, 0)),
                   pl.BlockSpec((None, M_HEADS, M_QK_DIM), lambda b: (b, 0, 0)),
                   pl.BlockSpec((None, 1, LANES), lambda b: (b, 0, 0))],
        out_shape=[jax.ShapeDtypeStruct((bd, 1, M_V_WIDTH), BF16),
                   jax.ShapeDtypeStruct((bd, M_HEADS, M_QK_DIM, M_V_DIM), F32),
                   jax.ShapeDtypeStruct((bd, M_HEADS, M_QK_DIM), F32),
                   jax.ShapeDtypeStruct((bd, 1, LANES), F32)],
        compiler_params=_cparams(("parallel",)),
        name="mlstm_sample",
    )(zif_s.reshape(bd, 1, LANES), zb_s.reshape(bd, 1, NB), gate_bias, c0, n0, m0.reshape(bd, 1, M_HEADS))
    return hm.reshape(bd, M_V_WIDTH), c1, n1, m1[:, 0, :M_HEADS]


def _merge_kernel(o1, o2, o3, l1, l2, l3, hm_ref, ga_ref, gb_ref, x_ref, wpa_ref, wpm_ref, wo_ref, g2_ref,
                  x1_ref, h2_ref, *, hws):
    os_ = (o1, o2, o3)
    parts = []
    for n in range(ATT_HEADS):
        sl = slice(n * ATT_HEAD_DIM, (n + 1) * ATT_HEAD_DIM)
        ls = []
        for l_ref, hw in zip((l1, l2, l3), hws):
            ln = (n // hw) * LANES + n % hw
            ls.append(l_ref[:, ln:ln + 1])
        mx = jnp.maximum(jnp.maximum(ls[0], ls[1]), ls[2])
        es = [jnp.exp(l - mx) for l in ls]
        tot = es[0] + es[1] + es[2]
        parts.append((es[0] / tot) * o1[:, sl] + (es[1] / tot) * o2[:, sl] + (es[2] / tot) * o3[:, sl])
    att = jnp.concatenate(parts, axis=1).astype(BF16)
    pa = jnp.dot(att, wpa_ref[...], preferred_element_type=F32)
    pm = jnp.dot(hm_ref[...], wpm_ref[...], preferred_element_type=F32)
    merged = (jax.nn.sigmoid(ga_ref[...].astype(F32)) * pa + jax.nn.sigmoid(gb_ref[...].astype(F32)) * pm)
    x1 = x_ref[...] + jnp.dot(merged.astype(BF16), wo_ref[...], preferred_element_type=F32)
    x1_ref[...] = x1
    ms = jnp.mean(x1 * x1, axis=-1, keepdims=True)
    h2_ref[...] = (x1 * lax.rsqrt(ms + RMS_EPS) * g2_ref[...]).astype(BF16)


def _merge(o_list, lse_list, hm, zt, x, wpa, wpm, wo, g2, *, tm):
    m, d = x.shape
    assert m % tm == 0 and zt.shape[1] == NT
    row_blk = lambda w: pl.BlockSpec((tm, w), lambda i: (i, 0))
    const = lambda shape: pl.BlockSpec(shape, lambda i: (0, 0), pipeline_mode=pl.Buffered(1))
    hws = tuple(ATT_HEADS * LANES // l.shape[1] for l in lse_list)
    return pl.pallas_call(
        functools.partial(_merge_kernel, hws=hws),
        grid=(m // tm,),
        in_specs=[row_blk(ATT_GROUP_W)] * 3 + [row_blk(l.shape[1]) for l in lse_list]
                 + [row_blk(M_V_WIDTH),
                    pl.BlockSpec((tm, d), lambda i: (i, T_GA // d)),
                    pl.BlockSpec((tm, d), lambda i: (i, T_GB // d)),
                    row_blk(d),
                    const(wpa.shape), const(wpm.shape), const(wo.shape), const((1, d))],
        out_specs=[row_blk(d), row_blk(d)],
        out_shape=[jax.ShapeDtypeStruct((m, d), F32), jax.ShapeDtypeStruct((m, d), BF16)],
        compiler_params=_cparams(("parallel",)),
        name="merge",
    )(*o_list, *lse_list, hm, zt, zt, x, wpa, wpm, wo, g2)


def _ffn_body(h2_ref, wu_ref, wg_ref, cw_ref, cb_ref, wd_ref, x1_ref, y_ref, g_prev2, g_prev1):
    j = pl.program_id(1)
    h2 = h2_ref[...]
    u = jnp.dot(h2, wu_ref[...], preferred_element_type=F32)
    g = jnp.dot(h2, wg_ref[...], preferred_element_type=F32)
    gconv = cb_ref[...] + ((g_prev2(g) * cw_ref[0:1, :] + g_prev1(g) * cw_ref[1:2, :]) + g * cw_ref[2:3, :])
    act = (jax.nn.gelu(gconv) * u).astype(BF16)
    down = jnp.dot(act, wd_ref[...], preferred_element_type=F32)

    @pl.when(j == 0)
    def _():
        y_ref[...] = x1_ref[...] + down

    @pl.when(j > 0)
    def _():
        y_ref[...] += down

    return g


def _ffn_prompt_kernel(h2_ref, wu_ref, wg_ref, cw_ref, cb_ref, wd_ref, x1_ref, y_ref, tail_ref, prev_s,
                       *, tiles_per_seq):
    i = pl.program_id(0)
    j = pl.program_id(1)
    tm = h2_ref.shape[0]
    prev = jnp.where(i % tiles_per_seq == 0, 0.0, prev_s[j])
    p2, p1 = prev[6:7, :], prev[7:8, :]

    def g_prev1(g):
        r = lax.broadcasted_iota(jnp.int32, g.shape, 0)
        return jnp.where(r == 0, p1, pltpu.roll(g, 1, axis=0))

    def g_prev2(g):
        r = lax.broadcasted_iota(jnp.int32, g.shape, 0)
        return jnp.where(r == 0, p2, jnp.where(r == 1, p1, pltpu.roll(g, 2, axis=0)))

    g = _ffn_body(h2_ref, wu_ref, wg_ref, cw_ref, cb_ref, wd_ref, x1_ref, y_ref, g_prev2, g_prev1)
    prev_s[j] = g[tm - 8:tm, :]
    tail_ref[...] = g[tm - 8:tm, :]


def _ffn_sample_kernel(h2_ref, wu_ref, wg_ref, cw_ref, cb_ref, wd_ref, x1_ref, b2_ref, b1_ref, y_ref, g_ref):
    g = _ffn_body(h2_ref, wu_ref, wg_ref, cw_ref, cb_ref, wd_ref, x1_ref, y_ref,
                  lambda g: b2_ref[...], lambda g: b1_ref[...])
    g_ref[...] = g


def _ffn_specs(tm, tf, d, nf):
    return [pl.BlockSpec((tm, d), lambda i, j: (i, 0)),
            pl.BlockSpec((d, tf), lambda i, j: (0, j)),
            pl.BlockSpec((d, tf), lambda i, j: (0, nf + j)),
            pl.BlockSpec((CONV_W, tf), lambda i, j: (0, j)),
            pl.BlockSpec((1, tf), lambda i, j: (0, j)),
            pl.BlockSpec((tf, d), lambda i, j: (j, 0)),
            pl.BlockSpec((tm, d), lambda i, j: (i, 0))]


def _ffn_prompt(h2, x1, w_in, conv_w, conv_b, w_down, seq, *, tm, tf):
    m, d = x1.shape
    assert m % tm == 0 and seq % tm == 0 and D_FF % tf == 0 and tm % 8 == 0
    nf = D_FF // tf
    kern = functools.partial(_ffn_prompt_kernel, tiles_per_seq=seq // tm)
    y, tails = pl.pallas_call(
        kern,
        grid=(m // tm, nf),
        in_specs=_ffn_specs(tm, tf, d, nf),
        out_specs=[pl.BlockSpec((tm, d), lambda i, j: (i, 0)),
                   pl.BlockSpec((None, 8, tf), lambda i, j: (i, 0, j))],
        out_shape=[jax.ShapeDtypeStruct((m, d), F32), jax.ShapeDtypeStruct((m // tm, 8, D_FF), F32)],
        scratch_shapes=[pltpu.VMEM((nf, 8, tf), F32)],
        compiler_params=_cparams(("arbitrary", "arbitrary")),
        name="ffn_prompt",
    )(h2, w_in, w_in, conv_w, conv_b, w_down, x1)
    return y, tails


def _ffn_sample(h2, x1, w_in, conv_w, conv_b, w_down, conv_buf, *, tf):
    m, d = x1.shape
    nf = D_FF // tf
    buf2d = conv_buf.reshape(m, (CONV_W - 1) * D_FF)
    return pl.pallas_call(
        _ffn_sample_kernel,
        grid=(1, nf),
        in_specs=_ffn_specs(m, tf, d, nf) + [pl.BlockSpec((m, tf), lambda i, j: (0, j)),
                                             pl.BlockSpec((m, tf), lambda i, j: (0, nf + j))],
        out_specs=[pl.BlockSpec((m, d), lambda i, j: (i, 0)), pl.BlockSpec((m, tf), lambda i, j: (0, j))],
        out_shape=[jax.ShapeDtypeStruct((m, d), F32), jax.ShapeDtypeStruct((m, D_FF), F32)],
        compiler_params=_cparams(("arbitrary", "arbitrary")),
        name="ffn_sample",
    )(h2, w_in, w_in, conv_w, conv_b, w_down, x1, buf2d, buf2d)


IN_PROJ_TN = 512


def _qk_gain_row(q_norm, k_norm):
    reps = N_GROUPS * ATT_HEADS
    return jnp.concatenate([jnp.tile(q_norm, reps), jnp.tile(k_norm, reps), jnp.zeros((NA - A_VA,), F32)])[None, :]


ATTN_PROMPT_TILING = ((4, 4), (1, 1), (1, 1))


def _layer(x_prompt, x_sample, caches, norm_mix, w_in, q_norm, k_norm, b_igate, b_fgate, w_proj_att,
           w_proj_mlstm, w_out, norm_ffn, w_ffn_in, conv_w, conv_b, w_ffn_down):
    batch, seq, d = x_prompt.shape
    bd = x_sample.shape[0]
    assert x_sample.shape[1] == 1 and d == D_MODEL
    (ck1, cv1, ck2, cv2, ck3, cv3, st_c, st_n, st_m, st_conv) = caches

    wt_in = jnp.swapaxes(w_in, 0, 1)
    qk_gain = _qk_gain_row(q_norm, k_norm)
    g1 = norm_mix[None, :]
    g2 = norm_ffn[None, :]
    gate_bias = jnp.concatenate([b_igate, b_fgate, jnp.zeros((LANES - 2 * M_HEADS,), F32)])[None, :]
    wpa = w_proj_att.astype(BF16)
    wpm = w_proj_mlstm.astype(BF16)
    wo = w_out.astype(BF16)
    w_ff = w_ffn_in.astype(BF16)
    w_dn = w_ffn_down.astype(BF16)
    cb = conv_b[None, :]

    xp = x_prompt.reshape(batch * seq, d)
    xs = x_sample.reshape(bd, d)

    h_p = _rmsnorm(xp, g1, tm=512)
    za_p, zb_p, zt_p, zif_p = _in_proj(h_p, wt_in, qk_gain, tm=2048, tn=IN_PROJ_TN, z_dtype=BF16)
    o_p, lse_p = zip(*[_attn_prompt_group(za_p, batch, seq, gi, dil, nsub=nsub, hw=hw)
                       for gi, ((_, dil), (nsub, hw)) in enumerate(zip(DIL_PATTERNS, ATTN_PROMPT_TILING))])
    hm_p, p_c, p_n, p_m = _mlstm_prompt(zif_p, zb_p, gate_bias, batch, seq, chunk=256)
    x1_p, h2_p = _merge(o_p, lse_p, hm_p, zt_p, xp, wpa, wpm, wo, g2, tm=256)
    y_p, tails = _ffn_prompt(h2_p, x1_p, w_ff, conv_w, cb, w_dn, seq, tm=512, tf=512)

    za_p3 = za_p.reshape(batch, seq, NA)
    p_kv = []
    for gi, (win, _) in enumerate(DIL_PATTERNS):
        keep = min(win, seq)
        for col in (A_KA, A_VA):
            lo = col + gi * ATT_GROUP_W
            p_kv.append(za_p3[:, seq - keep:, lo:lo + ATT_GROUP_W].reshape(batch, keep, ATT_HEADS, ATT_HEAD_DIM))
    tiles_per_seq = seq // 512
    p_conv = tails.reshape(batch, tiles_per_seq, 8, D_FF)[:, -1, 8 - (CONV_W - 1):, :]

    h_s = _rmsnorm(xs, g1, tm=bd)
    za_s, zb_s, zt_s, zif_s = _in_proj(h_s, wt_in, qk_gain, tm=bd, tn=IN_PROJ_TN, z_dtype=F32)
    o_s, lse_s = _attn_sample(za_s, [(ck1, cv1), (ck2, cv2), (ck3, cv3)], bb=4)
    hm_s, s_c, s_n, s_m = _mlstm_sample(zif_s, zb_s, gate_bias, st_c, st_n, st_m)
    x1_s, h2_s = _merge(o_s, lse_s, hm_s, zt_s, xs, wpa, wpm, wo, g2, tm=bd)
    y_s, g_s = _ffn_sample(h2_s, x1_s, w_ff, conv_w, cb, w_dn, st_conv, tf=512)

    s_kv = []
    for gi in range(N_GROUPS):
        for col in (A_KA, A_VA):
            lo = col + gi * ATT_GROUP_W
            s_kv.append(za_s[:, lo:lo + ATT_GROUP_W].reshape(bd, 1, ATT_HEADS, ATT_HEAD_DIM))
    s_conv = jnp.stack([st_conv[:, 1, :], g_s], axis=1)

    p_state = p_kv + [p_c, p_n, p_m, p_conv]
    s_state = s_kv + [s_c, s_n, s_m, s_conv]
    return y_p.reshape(batch, seq, d), y_s.reshape(bd, 1, d), p_state, s_state


def kernel(x_prompt, x_sample, cache_k_w128, cache_v_w128, cache_k_w512, cache_v_w512, cache_k_w2048,
           cache_v_w2048, state_mlstm_C, state_mlstm_n, state_mlstm_m, state_ffn_conv, norm_mix, w_in, q_norm,
           k_norm, b_igate, b_fgate, w_proj_att, w_proj_mlstm, w_out, norm_ffn, w_ffn_in, conv_w, conv_b,
           w_ffn_down):
    assert norm_mix.shape[0] == 1
    caches = [c[0] for c in (cache_k_w128, cache_v_w128, cache_k_w512, cache_v_w512, cache_k_w2048,
                             cache_v_w2048, state_mlstm_C, state_mlstm_n, state_mlstm_m, state_ffn_conv)]
    weights = [w[0] for w in (norm_mix, w_in, q_norm, k_norm, b_igate, b_fgate, w_proj_att, w_proj_mlstm,
                              w_out, norm_ffn, w_ffn_in, conv_w, conv_b, w_ffn_down)]
    y_p, y_s, p_state, s_state = _layer(x_prompt, x_sample, caches, *weights)
    return (y_p, y_s, *[a[None] for a in p_state], *[a[None] for a in s_state])
```

```python
import functools

import jax
import jax.numpy as jnp
from jax import lax
from jax.experimental import pallas as pl
from jax.experimental.pallas import tpu as pltpu

F32 = jnp.float32
BF16 = jnp.bfloat16

RMS_EPS = 1e-6
NEG_INF = -1e30
LANES = 128
VMEM_LIMIT = 56 * 1024 * 1024

D_MODEL = 2048
DIL_PATTERNS = ((128, 1), (512, 4), (2048, 16))
N_GROUPS = 3
ATT_HEADS = 4
ATT_HEAD_DIM = 128
ATT_SPAN = 128
ATT_SCALE = ATT_HEAD_DIM ** -0.5
ATT_GROUP_W = ATT_HEADS * ATT_HEAD_DIM
ATT_WIDTH = N_GROUPS * ATT_GROUP_W
M_HEADS = 4
M_QK_DIM = D_MODEL // (2 * M_HEADS)
M_V_DIM = D_MODEL // M_HEADS
M_QK_WIDTH = M_HEADS * M_QK_DIM
M_V_WIDTH = M_HEADS * M_V_DIM
M_K_SCALE = M_QK_DIM ** -0.5
D_FF = ((8 * D_MODEL // 3 + 255) // 256) * 256
CONV_W = 3
SPLIT_SIZES = (ATT_WIDTH, ATT_WIDTH, ATT_WIDTH, M_QK_WIDTH, M_QK_WIDTH, M_V_WIDTH, M_V_WIDTH,
               M_HEADS, M_HEADS, D_MODEL, D_MODEL)

IN_COLS = sum(SPLIT_SIZES)
A_QA = 0
A_KA = A_QA + ATT_WIDTH
A_VA = A_KA + ATT_WIDTH
NA = A_VA + ATT_WIDTH
B_QM = 0
B_KM = B_QM + M_QK_WIDTH
B_VM = B_KM + M_QK_WIDTH
B_OM = B_VM + M_V_WIDTH
NB = B_OM + M_V_WIDTH
N_IF = 2 * M_HEADS
T_GA = 0
T_GB = T_GA + D_MODEL
NT = T_GB + D_MODEL
assert NA + NB + N_IF + NT == IN_COLS


def _cparams(sem):
    return pltpu.CompilerParams(dimension_semantics=sem, vmem_limit_bytes=VMEM_LIMIT)


def _log_sigmoid(x):
    return jnp.minimum(x, 0.0) - jnp.log(1.0 + jnp.exp(-jnp.abs(x)))


def _rmsnorm_kernel(x_ref, g_ref, h_ref):
    x = x_ref[...]
    ms = jnp.mean(x * x, axis=-1, keepdims=True)
    h_ref[...] = (x * lax.rsqrt(ms + RMS_EPS) * g_ref[...]).astype(h_ref.dtype)


def _rmsnorm(x, gain, *, tm):
    m, d = x.shape
    assert m % tm == 0
    return pl.pallas_call(
        _rmsnorm_kernel,
        grid=(m // tm,),
        in_specs=[pl.BlockSpec((tm, d), lambda i: (i, 0)), pl.BlockSpec((1, d), lambda i: (0, 0))],
        out_specs=pl.BlockSpec((tm, d), lambda i: (i, 0)),
        out_shape=jax.ShapeDtypeStruct((m, d), BF16),
        compiler_params=_cparams(("parallel",)),
        name="rmsnorm",
    )(x, gain)


def _in_proj_kernel(h_ref, w_ref, wn_ref, wif_ref, qkg_ref, za_ref, zb_ref, zt_ref, zif_ref, *, n_norm, na, nb, nt):
    j = pl.program_id(1)

    def z_of(w):
        return lax.dot_general(h_ref[...], w.astype(BF16), (((1,), (1,)), ((), ())), preferred_element_type=F32)

    @pl.when(j < na)
    def _():
        za_ref[...] = z_of(w_ref[...])

    @pl.when(j < n_norm)
    def _():
        for c in range(za_ref.shape[1] // LANES):
            sl = slice(c * LANES, (c + 1) * LANES)
            zc = za_ref[:, sl]
            ms = jnp.mean(zc * zc, axis=-1, keepdims=True)
            za_ref[:, sl] = zc * lax.rsqrt(ms + RMS_EPS) * qkg_ref[:, sl]

    @pl.when((j >= na) & (j < na + nb))
    def _():
        zb_ref[...] = z_of(w_ref[...]).astype(zb_ref.dtype)

    @pl.when((j >= na + nb) & (j < na + nb + nt))
    def _():
        w = jnp.concatenate([w_ref[N_IF:, :], wn_ref[...]], axis=0)
        zt_ref[...] = z_of(w).astype(zt_ref.dtype)

    @pl.when(j == na + nb + nt)
    def _():
        w = jnp.concatenate([wif_ref[...], jnp.zeros((LANES - N_IF, wif_ref.shape[1]), F32)], axis=0)
        zif_ref[...] = z_of(w)


def _in_proj(h, wt, qk_gain, *, tm, tn, z_dtype):
    m, d = h.shape
    assert wt.shape == (IN_COLS, d) and m % tm == 0 and N_IF == 8
    assert NA % tn == 0 and NB % tn == 0 and NT % tn == 0 and A_VA % tn == 0 and tn % N_IF == 0
    na, nb, nt = NA // tn, NB // tn, NT // tn
    nw = na + nb + nt
    g_if = (NA + NB) // N_IF
    g_tn = tn // N_IF
    kern = functools.partial(_in_proj_kernel, n_norm=A_VA // tn, na=na, nb=nb, nt=nt)
    return pl.pallas_call(
        kern,
        grid=(m // tm, nw + 1),
        in_specs=[
            pl.BlockSpec((tm, d), lambda i, j: (i, 0)),
            pl.BlockSpec((tn, d), lambda i, j: (jnp.minimum(j, nw - 1), 0)),
            pl.BlockSpec((N_IF, d), lambda i, j: (g_if + g_tn * (jnp.clip(j, na + nb, nw - 1) - (na + nb) + 1), 0)),
            pl.BlockSpec((N_IF, d), lambda i, j: (g_if, 0)),
            pl.BlockSpec((1, tn), lambda i, j: (0, jnp.minimum(j, na - 1))),
        ],
        out_specs=[pl.BlockSpec((tm, tn), lambda i, j: (i, jnp.minimum(j, na - 1))),
                   pl.BlockSpec((tm, tn), lambda i, j: (i, jnp.clip(j - na, 0, nb - 1))),
                   pl.BlockSpec((tm, tn), lambda i, j: (i, jnp.clip(j - na - nb, 0, nt - 1))),
                   pl.BlockSpec((tm, LANES), lambda i, j: (i, 0))],
        out_shape=[jax.ShapeDtypeStruct((m, NA), F32), jax.ShapeDtypeStruct((m, NB), z_dtype),
                   jax.ShapeDtypeStruct((m, NT), z_dtype), jax.ShapeDtypeStruct((m, LANES), F32)],
        compiler_params=_cparams(("parallel", "arbitrary")),
        name="in_proj",
    )(h, wt, wt, wt, qk_gain)


def _attn_prompt_kernel(q_ref, kp_ref, kc_ref, vp_ref, vc_ref, o_ref, lse_ref, *, dil, nsub):
    c = pl.program_id(1)
    blk = ATT_SPAN
    step = blk * dil
    hw = q_ref.shape[1] // ATT_HEAD_DIM
    qi = lax.broadcasted_iota(jnp.int32, (blk, blk), 0)
    kj = lax.broadcasted_iota(jnp.int32, (blk, blk), 1)
    cur_ok = kj <= qi
    prev_ok = kj >= qi
    first_bias = jnp.where(c > 0, 0.0, NEG_INF)
    lane = lax.broadcasted_iota(jnp.int32, (blk, LANES), 1)
    nt = (((1,), (1,)), ((), ()))

    def rows(base, r):
        return pl.ds(base + r, blk) if dil == 1 else pl.ds(base + r, blk, stride=dil)

    def one(r, s):
        base = s * step
        lse_tile = jnp.zeros((blk, LANES), F32)
        for n in range(hw):
            sl = slice(n * ATT_HEAD_DIM, (n + 1) * ATT_HEAD_DIM)
            q = q_ref[rows(base, r), sl].astype(BF16)
            k_c = kc_ref[rows(base, r), sl].astype(BF16)
            v_c = vc_ref[rows(base, r), sl].astype(BF16)
            if s == 0:
                k_p = kp_ref[rows(0, r), sl].astype(BF16)
                v_p = vp_ref[rows(0, r), sl].astype(BF16)
            else:
                k_p = kc_ref[rows(base - step, r), sl].astype(BF16)
                v_p = vc_ref[rows(base - step, r), sl].astype(BF16)
            s_c = lax.dot_general(q, k_c, nt, preferred_element_type=F32) * ATT_SCALE
            s_p = lax.dot_general(q, k_p, nt, preferred_element_type=F32) * ATT_SCALE
            if s == 0:
                s_p = s_p + first_bias
            s_c = jnp.where(cur_ok, s_c, NEG_INF)
            s_p = jnp.where(prev_ok, s_p, NEG_INF)
            m = jnp.maximum(jnp.max(s_c, axis=-1, keepdims=True), jnp.max(s_p, axis=-1, keepdims=True))
            p_c = jnp.exp(s_c - m)
            p_p = jnp.exp(s_p - m)
            l = jnp.sum(p_c, axis=-1, keepdims=True) + jnp.sum(p_p, axis=-1, keepdims=True)
            o = (jnp.dot(p_c.astype(BF16), v_c, preferred_element_type=F32)
                 + jnp.dot(p_p.astype(BF16), v_p, preferred_element_type=F32))
            o_ref[rows(base, r), sl] = o / l
            lse_tile = jnp.where(lane == n, m + jnp.log(l), lse_tile)
        lse_ref[rows(base, r), :] = lse_tile

    for s in range(nsub):
        if dil == 1:
            one(0, s)
        else:
            def body(r, carry, s=s):
                one(r, s)
                return carry
            lax.fori_loop(0, dil, body, 0)


def _attn_prompt_group(za, batch, seq, gi, dil, *, nsub, hw):
    step = ATT_SPAN * dil
    tc = nsub * step
    assert seq % tc == 0 and ATT_HEADS % hw == 0
    nh = ATT_HEADS // hw
    w = hw * ATT_HEAD_DIM
    z3 = za.reshape(batch, seq, NA)
    qc, kc, vc = [(col + gi * ATT_GROUP_W) // w for col in (A_QA, A_KA, A_VA)]

    def cur(col):
        return pl.BlockSpec((None, tc, w), lambda b, c, h: (b, c, col + h))

    def prev(col):
        return pl.BlockSpec((None, step, w), lambda b, c, h: (b, jnp.maximum(c * nsub - 1, 0), col + h))

    kern = functools.partial(_attn_prompt_kernel, dil=dil, nsub=nsub)
    o, lse = pl.pallas_call(
        kern,
        grid=(batch, seq // tc, nh),
        in_specs=[cur(qc), prev(kc), cur(kc), prev(vc), cur(vc)],
        out_specs=[pl.BlockSpec((None, tc, w), lambda b, c, h: (b, c, h)),
                   pl.BlockSpec((None, tc, LANES), lambda b, c, h: (b, c, h))],
        out_shape=[jax.ShapeDtypeStruct((batch, seq, ATT_GROUP_W), F32),
                   jax.ShapeDtypeStruct((batch, seq, nh * LANES), F32)],
        compiler_params=_cparams(("parallel", "parallel", "parallel")),
        name=f"attn_prompt_g{gi}",
    )(z3, z3, z3, z3, z3)
    return o.reshape(batch * seq, ATT_GROUP_W), lse.reshape(batch * seq, nh * LANES)


def _attn_sample_kernel(z_ref, k1, v1, k2, v2, k3, v3, o1, o2, o3, l1, l2, l3):
    bb = z_ref.shape[0]
    bufs = ((k1, v1, o1, l1), (k2, v2, o2, l2), (k3, v3, o3, l3))
    lane = lax.broadcasted_iota(jnp.int32, (1, LANES), 1)

    def heads(b, col):
        return jnp.concatenate([z_ref[b, :, col + n * ATT_HEAD_DIM:col + (n + 1) * ATT_HEAD_DIM]
                                for n in range(ATT_HEADS)], axis=0)

    def body(b, carry):
        for gi, (k_ref, v_ref, o_ref, l_ref) in enumerate(bufs):
            q = heads(b, A_QA + gi * ATT_GROUP_W)
            k_new = heads(b, A_KA + gi * ATT_GROUP_W)
            v_new = heads(b, A_VA + gi * ATT_GROUP_W)
            kb = k_ref[b]
            vb = v_ref[b]
            s = jnp.sum(kb * q[None], axis=-1, keepdims=True) * ATT_SCALE
            s_new = jnp.sum(k_new * q, axis=-1, keepdims=True) * ATT_SCALE
            m = jnp.maximum(jnp.max(s, axis=0), s_new)
            p = jnp.exp(s - m[None])
            p_new = jnp.exp(s_new - m)
            l = jnp.sum(p, axis=0) + p_new
            o = (jnp.sum(p * vb, axis=0) + p_new * v_new) / l
            lse = m + jnp.log(l)
            lse_row = jnp.zeros((1, LANES), F32)
            for n in range(ATT_HEADS):
                o_ref[b, :, n * ATT_HEAD_DIM:(n + 1) * ATT_HEAD_DIM] = o[n:n + 1, :]
                lse_row = jnp.where(lane == n, lse[n:n + 1, :], lse_row)
            l_ref[b] = lse_row
        return carry

    lax.fori_loop(0, bb, body, 0)


def _attn_sample(za_s, caches, *, bb):
    bd = za_s.shape[0]
    assert bd % bb == 0
    ins, in_specs = [za_s.reshape(bd, 1, NA)], [pl.BlockSpec((bb, 1, NA), lambda i: (i, 0, 0))]
    for (k_buf, v_buf), (win, dil) in zip(caches, DIL_PATTERNS):
        assert k_buf.shape[1:] == (ATT_SPAN * dil, ATT_HEADS, ATT_HEAD_DIM)
        for buf in (k_buf, v_buf):
            ins.append(buf.reshape(bd, ATT_SPAN, dil, ATT_HEADS, ATT_HEAD_DIM))
            in_specs.append(pl.BlockSpec((bb, ATT_SPAN, None, ATT_HEADS, ATT_HEAD_DIM),
                                         lambda i: (i, 0, 0, 0, 0)))
    outs = pl.pallas_call(
        _attn_sample_kernel,
        grid=(bd // bb,),
        in_specs=in_specs,
        out_specs=[pl.BlockSpec((bb, 1, ATT_GROUP_W), lambda i: (i, 0, 0))] * 3
                  + [pl.BlockSpec((bb, 1, LANES), lambda i: (i, 0, 0))] * 3,
        out_shape=[jax.ShapeDtypeStruct((bd, 1, ATT_GROUP_W), F32)] * 3
                  + [jax.ShapeDtypeStruct((bd, 1, LANES), F32)] * 3,
        compiler_params=_cparams(("parallel",)),
        name="attn_sample",
    )(*ins)
    return [o[:, 0, :] for o in outs[:3]], [l[:, 0, :] for l in outs[3:]]


def _mlstm_prompt_kernel(q_ref, k_ref, v_ref, om_ref, g_ref, bias_ref, h_ref, c_out, n_out, m_out,
                         c_s, n_s, m_s):
    hd = pl.program_id(1)
    ci = pl.program_id(2)
    L = q_ref.shape[0]

    @pl.when(ci == 0)
    def _():
        c_s[...] = jnp.zeros_like(c_s)
        n_s[...] = jnp.zeros_like(n_s)
        m_s[...] = jnp.zeros_like(m_s)

    gates = g_ref[...] + bias_ref[...]
    lane = lax.broadcasted_iota(jnp.int32, (L, LANES), 1)
    li_col = jnp.sum(jnp.where(lane == hd, gates, 0.0), axis=1, keepdims=True)
    lf_col = _log_sigmoid(jnp.sum(jnp.where(lane == hd + M_HEADS, gates, 0.0), axis=1, keepdims=True))
    gates_t = gates.T
    sub = lax.broadcasted_iota(jnp.int32, (LANES, L), 0)
    li_row = jnp.sum(jnp.where(sub == hd, gates_t, 0.0), axis=0, keepdims=True)
    lf_row = _log_sigmoid(jnp.sum(jnp.where(sub == hd + M_HEADS, gates_t, 0.0), axis=0, keepdims=True))

    ti = lax.broadcasted_iota(jnp.int32, (L, L), 0)
    si = lax.broadcasted_iota(jnp.int32, (L, L), 1)
    causal = si <= ti
    b_col = jnp.sum(jnp.where(causal, lf_row, 0.0), axis=1, keepdims=True)
    b_row = jnp.sum(jnp.where(ti <= si, lf_col, 0.0), axis=0, keepdims=True)
    b_end = jnp.sum(lf_row, axis=1, keepdims=True)

    m_prev = m_s[...]
    dmat = jnp.where(causal, b_col - b_row + li_row, NEG_INF)
    inter = b_col + m_prev
    mt = jnp.maximum(inter, jnp.max(dmat, axis=1, keepdims=True))

    qb = q_ref[...]
    q = qb.astype(F32)
    k = k_ref[...].astype(F32) * M_K_SCALE
    vb = v_ref[...]
    qk = lax.dot_general(qb, k.astype(BF16), (((1,), (1,)), ((), ())), preferred_element_type=F32)
    a = jnp.exp(dmat - mt) * qk
    w_inter = jnp.exp(inter - mt)
    num = (jnp.dot(a.astype(BF16), vb, preferred_element_type=F32)
           + w_inter * jnp.dot(qb, c_s[...].astype(BF16), preferred_element_type=F32))
    den = jnp.sum(a, axis=1, keepdims=True) + w_inter * jnp.sum(q * n_s[...], axis=1, keepdims=True)
    h = num / jnp.maximum(jnp.abs(den), jnp.exp(-mt))
    h_ref[...] = (jax.nn.sigmoid(om_ref[...].astype(F32)) * h).astype(h_ref.dtype)

    g_col = b_end - b_col + li_col
    g_row = b_end - b_row + li_row
    m_new = jnp.maximum(b_end + m_prev, jnp.max(g_row, axis=1, keepdims=True))
    decay = jnp.exp(b_end + m_prev - m_new)
    kw = jnp.exp(g_col - m_new) * k
    c_s[...] = decay * c_s[...] + jnp.dot(kw.T.astype(BF16), vb, preferred_element_type=F32)
    n_s[...] = decay * n_s[...] + jnp.sum(kw, axis=0, keepdims=True)
    m_s[...] = m_new

    @pl.when(ci == pl.num_programs(2) - 1)
    def _():
        c_out[...] = c_s[...]
        n_out[...] = n_s[...]
        m_out[...] = jnp.broadcast_to(m_s[...], m_out.shape)


def _mlstm_prompt(zif, zb, gate_bias, batch, seq, *, chunk):
    assert seq % chunk == 0
    nc = seq // chunk
    zif3 = zif.reshape(batch, seq, LANES)
    zb3 = zb.reshape(batch, seq, NB)
    qk_blk = lambda col: pl.BlockSpec((None, chunk, M_QK_DIM), lambda b, h, c: (b, c, col // M_QK_DIM + h))
    v_blk = lambda col: pl.BlockSpec((None, chunk, M_V_DIM), lambda b, h, c: (b, c, col // M_V_DIM + h))
    hm, c1, n1, m1 = pl.pallas_call(
        _mlstm_prompt_kernel,
        grid=(batch, M_HEADS, nc),
        in_specs=[qk_blk(B_QM), qk_blk(B_KM), v_blk(B_VM), v_blk(B_OM),
                  pl.BlockSpec((None, chunk, LANES), lambda b, h, c: (b, c, 0)),
                  pl.BlockSpec((1, LANES), lambda b, h, c: (0, 0))],
        out_specs=[pl.BlockSpec((None, chunk, M_V_DIM), lambda b, h, c: (b, c, h)),
                   pl.BlockSpec((None, None, M_QK_DIM, M_V_DIM), lambda b, h, c: (b, h, 0, 0)),
                   pl.BlockSpec((None, None, 1, M_QK_DIM), lambda b, h, c: (b, h, 0, 0)),
                   pl.BlockSpec((None, None, 1, LANES), lambda b, h, c: (b, h, 0, 0))],
        out_shape=[jax.ShapeDtypeStruct((batch, seq, M_V_WIDTH), BF16),
                   jax.ShapeDtypeStruct((batch, M_HEADS, M_QK_DIM, M_V_DIM), F32),
                   jax.ShapeDtypeStruct((batch, M_HEADS, 1, M_QK_DIM), F32),
                   jax.ShapeDtypeStruct((batch, M_HEADS, 1, LANES), F32)],
        scratch_shapes=[pltpu.VMEM((M_QK_DIM, M_V_DIM), F32), pltpu.VMEM((1, M_QK_DIM), F32),
                        pltpu.VMEM((1, 1), F32)],
        compiler_params=_cparams(("parallel", "parallel", "arbitrary")),
        name="mlstm_prompt",
    )(zb3, zb3, zb3, zb3, zif3, gate_bias)
    return hm.reshape(batch * seq, M_V_WIDTH), c1, n1[:, :, 0, :], m1[:, :, 0, 0]


def _row_to_col(row):
    n = row.shape[1]
    return jnp.broadcast_to(row, (LANES, n)).T[:, 0:1]


def _mlstm_sample_kernel(zif_ref, zb_ref, bias_ref, c_ref, n_ref, m_ref, h_ref, c_out, n_out, m_out):
    gates = zif_ref[...] + bias_ref[...]
    lane = lax.broadcasted_iota(jnp.int32, (1, LANES), 1)
    m_row = jnp.zeros((1, LANES), F32)
    for h in range(M_HEADS):
        q = zb_ref[:, B_QM + h * M_QK_DIM:B_QM + (h + 1) * M_QK_DIM].astype(F32)
        k = zb_ref[:, B_KM + h * M_QK_DIM:B_KM + (h + 1) * M_QK_DIM].astype(F32) * M_K_SCALE
        v = zb_ref[:, B_VM + h * M_V_DIM:B_VM + (h + 1) * M_V_DIM].astype(F32)
        om = zb_ref[:, B_OM + h * M_V_DIM:B_OM + (h + 1) * M_V_DIM].astype(F32)
        li = gates[:, h:h + 1]
        lf = _log_sigmoid(gates[:, M_HEADS + h:M_HEADS + h + 1])
        m0 = m_ref[:, h:h + 1]
        c0 = c_ref[h]
        n0 = n_ref[h:h + 1, :]
        inter = lf + m0
        mt = jnp.maximum(inter, li)
        a = jnp.exp(li - mt) * jnp.sum(q * k, axis=1, keepdims=True)
        w_inter = jnp.exp(inter - mt)
        q_c = jnp.sum(_row_to_col(q) * c0, axis=0, keepdims=True)
        num = a * v + w_inter * q_c
        den = a + w_inter * jnp.sum(q * n0, axis=1, keepdims=True)
        hv = num / jnp.maximum(jnp.abs(den), jnp.exp(-mt))
        h_ref[:, h * M_V_DIM:(h + 1) * M_V_DIM] = (jax.nn.sigmoid(om) * hv).astype(h_ref.dtype)
        m_new = jnp.maximum(inter, li)
        decay = jnp.exp(inter - m_new)
        ws = jnp.exp(li - m_new)
        c_out[h] = decay * c0 + _row_to_col(ws * k) * v
        n_out[h:h + 1, :] = decay * n0 + ws * k
        m_row = jnp.where(lane == h, m_new, m_row)
    m_out[...] = m_row


def _mlstm_sample(zif_s, zb_s, gate_bias, c0, n0, m0):
    bd = zif_s.shape[0]
    hm, c1, n1, m1 = pl.pallas_call(
        _mlstm_sample_kernel,
        grid=(bd,),
        in_specs=[pl.BlockSpec((None, 1, LANES), lambda b: (b, 0, 0)),
                  pl.BlockSpec((None, 1, NB), lambda b: (b, 0, 0)),
                  pl.BlockSpec((1, LANES), lambda b: (0, 0)),
                  pl.BlockSpec((None, M_HEADS, M_QK_DIM, M_V_DIM), lambda b: (b, 0, 0, 0)),
                  pl.BlockSpec((None, M_HEADS, M_QK_DIM), lambda b: (b, 0, 0)),
                  pl.BlockSpec((None, 1, M_HEADS), lambda b: (b, 0, 0))],
        out_specs=[pl.BlockSpec((None, 1, M_V_WIDTH), lambda b: (b, 0, 0)),
                   pl.BlockSpec((None, M_HEADS, M_QK_DIM, M_V_DIM), lambda b: (b, 0, 0, 0)),
                   pl.BlockSpec((None, M_HEADS, M_QK_DIM), lambda b: (b, 0, 0)),
                   pl.BlockSpec((None, 1, LANES), lambda b: (b, 0, 0))],
        out_shape=[jax.ShapeDtypeStruct((bd, 1, M_V_WIDTH), BF16),
                   jax.ShapeDtypeStruct((bd, M_HEADS, M_QK_DIM, M_V_DIM), F32),
                   jax.ShapeDtypeStruct((bd, M_HEADS, M_QK_DIM), F32),
                   jax.ShapeDtypeStruct((bd, 1, LANES), F32)],
        compiler_params=_cparams(("parallel",)),
        name="mlstm_sample",
    )(zif_s.reshape(bd, 1, LANES), zb_s.reshape(bd, 1, NB), gate_bias, c0, n0, m0.reshape(bd, 1, M_HEADS))
    return hm.reshape(bd, M_V_WIDTH), c1, n1, m1[:, 0, :M_HEADS]


def _merge_kernel(o1, o2, o3, l1, l2, l3, hm_ref, ga_ref, gb_ref, x_ref, wpa_ref, wpm_ref, wo_ref, g2_ref,
                  x1_ref, h2_ref, *, hws):
    os_ = (o1, o2, o3)
    parts = []
    for n in range(ATT_HEADS):
        sl = slice(n * ATT_HEAD_DIM, (n + 1) * ATT_HEAD_DIM)
        ls = []
        for l_ref, hw in zip((l1, l2, l3), hws):
            ln = (n // hw) * LANES + n % hw
            ls.append(l_ref[:, ln:ln + 1])
        mx = jnp.maximum(jnp.maximum(ls[0], ls[1]), ls[2])
        es = [jnp.exp(l - mx) for l in ls]
        tot = es[0] + es[1] + es[2]
        parts.append((es[0] / tot) * o1[:, sl] + (es[1] / tot) * o2[:, sl] + (es[2] / tot) * o3[:, sl])
    att = jnp.concatenate(parts, axis=1).astype(BF16)
    pa = jnp.dot(att, wpa_ref[...], preferred_element_type=F32)
    pm = jnp.dot(hm_ref[...], wpm_ref[...], preferred_element_type=F32)
    merged = (jax.nn.sigmoid(ga_ref[...].astype(F32)) * pa + jax.nn.sigmoid(gb_ref[...].astype(F32)) * pm)
    x1 = x_ref[...] + jnp.dot(merged.astype(BF16), wo_ref[...], preferred_element_type=F32)
    x1_ref[...] = x1
    ms = jnp.mean(x1 * x1, axis=-1, keepdims=True)
    h2_ref[...] = (x1 * lax.rsqrt(ms + RMS_EPS) * g2_ref[...]).astype(BF16)


def _merge(o_list, lse_list, hm, zt, x, wpa, wpm, wo, g2, *, tm):
    m, d = x.shape
    assert m % tm == 0 and zt.shape[1] == NT
    row_blk = lambda w: pl.BlockSpec((tm, w), lambda i: (i, 0))
    const = lambda shape: pl.BlockSpec(shape, lambda i: (0, 0), pipeline_mode=pl.Buffered(1))
    hws = tuple(ATT_HEADS * LANES // l.shape[1] for l in lse_list)
    return pl.pallas_call(
        functools.partial(_merge_kernel, hws=hws),
        grid=(m // tm,),
        in_specs=[row_blk(ATT_GROUP_W)] * 3 + [row_blk(l.shape[1]) for l in lse_list]
                 + [row_blk(M_V_WIDTH),
                    pl.BlockSpec((tm, d), lambda i: (i, T_GA // d)),
                    pl.BlockSpec((tm, d), lambda i: (i, T_GB // d)),
                    row_blk(d),
                    const(wpa.shape), const(wpm.shape), const(wo.shape), const((1, d))],
        out_specs=[row_blk(d), row_blk(d)],
        out_shape=[jax.ShapeDtypeStruct((m, d), F32), jax.ShapeDtypeStruct((m, d), BF16)],
        compiler_params=_cparams(("parallel",)),
        name="merge",
    )(*o_list, *lse_list, hm, zt, zt, x, wpa, wpm, wo, g2)


def _ffn_body(h2_ref, wu_ref, wg_ref, cw_ref, cb_ref, wd_ref, x1_ref, y_ref, g_prev2, g_prev1):
    j = pl.program_id(1)
    h2 = h2_ref[...]
    u = jnp.dot(h2, wu_ref[...], preferred_element_type=F32)
    g = jnp.dot(h2, wg_ref[...], preferred_element_type=F32)
    gconv = cb_ref[...] + ((g_prev2(g) * cw_ref[0:1, :] + g_prev1(g) * cw_ref[1:2, :]) + g * cw_ref[2:3, :])
    act = (jax.nn.gelu(gconv) * u).astype(BF16)
    down = jnp.dot(act, wd_ref[...], preferred_element_type=F32)

    @pl.when(j == 0)
    def _():
        y_ref[...] = x1_ref[...] + down

    @pl.when(j > 0)
    def _():
        y_ref[...] += down

    return g


def _ffn_prompt_kernel(h2_ref, wu_ref, wg_ref, cw_ref, cb_ref, wd_ref, x1_ref, y_ref, tail_ref, prev_s,
                       u0_s, g0_s, u1_s, g1_s, *, tiles_per_seq, nf):
    i = pl.program_id(0)
    j = pl.program_id(1)
    tm = h2_ref.shape[0]
    slots = ((u0_s, g0_s), (u1_s, g1_s))

    def up(u_s, g_s):
        h2 = h2_ref[...]
        u_s[...] = jnp.dot(h2, wu_ref[...], preferred_element_type=F32)
        g_s[...] = jnp.dot(h2, wg_ref[...], preferred_element_type=F32)

    def down(u_s, g_s):
        jt = j - 1
        prev = jnp.where(i % tiles_per_seq == 0, 0.0, prev_s[jt])
        p2, p1 = prev[6:7, :], prev[7:8, :]
        g = g_s[...]
        r = lax.broadcasted_iota(jnp.int32, g.shape, 0)
        g_m1 = jnp.where(r == 0, p1, pltpu.roll(g, 1, axis=0))
        g_m2 = jnp.where(r == 0, p2, jnp.where(r == 1, p1, pltpu.roll(g, 2, axis=0)))
        gconv = cb_ref[...] + ((g_m2 * cw_ref[0:1, :] + g_m1 * cw_ref[1:2, :]) + g * cw_ref[2:3, :])
        act = (jax.nn.gelu(gconv) * u_s[...]).astype(BF16)
        y_ref[...] += jnp.dot(act, wd_ref[...], preferred_element_type=F32)
        prev_s[jt] = g[tm - 8:tm, :]
        tail_ref[...] = g[tm - 8:tm, :]

    @pl.when(j == 0)
    def _():
        y_ref[...] = x1_ref[...]
        up(*slots[0])

    for par in range(2):
        @pl.when((j > 0) & (j < nf) & (j % 2 == par))
        def _(par=par):
            up(*slots[par])
            down(*slots[1 - par])

    @pl.when(j == nf)
    def _():
        down(*slots[(nf - 1) % 2])


def _ffn_sample_kernel(h2_ref, wu_ref, wg_ref, cw_ref, cb_ref, wd_ref, x1_ref, b2_ref, b1_ref, y_ref, g_ref):
    g = _ffn_body(h2_ref, wu_ref, wg_ref, cw_ref, cb_ref, wd_ref, x1_ref, y_ref,
                  lambda g: b2_ref[...], lambda g: b1_ref[...])
    g_ref[...] = g


def _ffn_specs(tm, tf, d, nf):
    return [pl.BlockSpec((tm, d), lambda i, j: (i, 0)),
            pl.BlockSpec((d, tf), lambda i, j: (0, j)),
            pl.BlockSpec((d, tf), lambda i, j: (0, nf + j)),
            pl.BlockSpec((CONV_W, tf), lambda i, j: (0, j)),
            pl.BlockSpec((1, tf), lambda i, j: (0, j)),
            pl.BlockSpec((tf, d), lambda i, j: (j, 0)),
            pl.BlockSpec((tm, d), lambda i, j: (i, 0))]


def _ffn_prompt(h2, x1, w_in, conv_w, conv_b, w_down, seq, *, tm, tf):
    m, d = x1.shape
    assert m % tm == 0 and seq % tm == 0 and D_FF % tf == 0 and tm % 8 == 0
    nf = D_FF // tf
    kern = functools.partial(_ffn_prompt_kernel, tiles_per_seq=seq // tm, nf=nf)
    up_t = lambda j: jnp.minimum(j, nf - 1)
    dn_t = lambda j: jnp.maximum(j - 1, 0)
    y, tails = pl.pallas_call(
        kern,
        grid=(m // tm, nf + 1),
        in_specs=[pl.BlockSpec((tm, d), lambda i, j: (i, 0)),
                  pl.BlockSpec((d, tf), lambda i, j: (0, up_t(j))),
                  pl.BlockSpec((d, tf), lambda i, j: (0, nf + up_t(j))),
                  pl.BlockSpec((CONV_W, tf), lambda i, j: (0, dn_t(j))),
                  pl.BlockSpec((1, tf), lambda i, j: (0, dn_t(j))),
                  pl.BlockSpec((tf, d), lambda i, j: (dn_t(j), 0)),
                  pl.BlockSpec((tm, d), lambda i, j: (i, 0))],
        out_specs=[pl.BlockSpec((tm, d), lambda i, j: (i, 0)),
                   pl.BlockSpec((None, 8, tf), lambda i, j: (i, 0, dn_t(j)))],
        out_shape=[jax.ShapeDtypeStruct((m, d), F32), jax.ShapeDtypeStruct((m // tm, 8, D_FF), F32)],
        scratch_shapes=[pltpu.VMEM((nf, 8, tf), F32)] + [pltpu.VMEM((tm, tf), F32)] * 4,
        compiler_params=_cparams(("arbitrary", "arbitrary")),
        name="ffn_prompt",
    )(h2, w_in, w_in, conv_w, conv_b, w_down, x1)
    return y, tails


def _ffn_sample(h2, x1, w_in, conv_w, conv_b, w_down, conv_buf, *, tf):
    m, d = x1.shape
    nf = D_FF // tf
    buf2d = conv_buf.reshape(m, (CONV_W - 1) * D_FF)
    return pl.pallas_call(
        _ffn_sample_kernel,
        grid=(1, nf),
        in_specs=_ffn_specs(m, tf, d, nf) + [pl.BlockSpec((m, tf), lambda i, j: (0, j)),
                                             pl.BlockSpec((m, tf), lambda i, j: (0, nf + j))],
        out_specs=[pl.BlockSpec((m, d), lambda i, j: (i, 0)), pl.BlockSpec((m, tf), lambda i, j: (0, j))],
        out_shape=[jax.ShapeDtypeStruct((m, d), F32), jax.ShapeDtypeStruct((m, D_FF), F32)],
        compiler_params=_cparams(("arbitrary", "arbitrary")),
        name="ffn_sample",
    )(h2, w_in, w_in, conv_w, conv_b, w_down, x1, buf2d, buf2d)


IN_PROJ_TN = 512


def _qk_gain_row(q_norm, k_norm):
    reps = N_GROUPS * ATT_HEADS
    return jnp.concatenate([jnp.tile(q_norm, reps), jnp.tile(k_norm, reps), jnp.zeros((NA - A_VA,), F32)])[None, :]


ATTN_PROMPT_TILING = ((4, 4), (1, 1), (1, 1))


def _layer(x_prompt, x_sample, caches, norm_mix, w_in, q_norm, k_norm, b_igate, b_fgate, w_proj_att,
           w_proj_mlstm, w_out, norm_ffn, w_ffn_in, conv_w, conv_b, w_ffn_down):
    batch, seq, d = x_prompt.shape
    bd = x_sample.shape[0]
    assert x_sample.shape[1] == 1 and d == D_MODEL
    (ck1, cv1, ck2, cv2, ck3, cv3, st_c, st_n, st_m, st_conv) = caches

    wt_in = jnp.swapaxes(w_in, 0, 1)
    qk_gain = _qk_gain_row(q_norm, k_norm)
    g1 = norm_mix[None, :]
    g2 = norm_ffn[None, :]
    gate_bias = jnp.concatenate([b_igate, b_fgate, jnp.zeros((LANES - 2 * M_HEADS,), F32)])[None, :]
    wpa = w_proj_att.astype(BF16)
    wpm = w_proj_mlstm.astype(BF16)
    wo = w_out.astype(BF16)
    w_ff = w_ffn_in.astype(BF16)
    w_dn = w_ffn_down.astype(BF16)
    cb = conv_b[None, :]

    xp = x_prompt.reshape(batch * seq, d)
    xs = x_sample.reshape(bd, d)

    h_p = _rmsnorm(xp, g1, tm=512)
    za_p, zb_p, zt_p, zif_p = _in_proj(h_p, wt_in, qk_gain, tm=2048, tn=IN_PROJ_TN, z_dtype=BF16)
    o_p, lse_p = zip(*[_attn_prompt_group(za_p, batch, seq, gi, dil, nsub=nsub, hw=hw)
                       for gi, ((_, dil), (nsub, hw)) in enumerate(zip(DIL_PATTERNS, ATTN_PROMPT_TILING))])
    hm_p, p_c, p_n, p_m = _mlstm_prompt(zif_p, zb_p, gate_bias, batch, seq, chunk=256)
    x1_p, h2_p = _merge(o_p, lse_p, hm_p, zt_p, xp, wpa, wpm, wo, g2, tm=256)
    y_p, tails = _ffn_prompt(h2_p, x1_p, w_ff, conv_w, cb, w_dn, seq, tm=512, tf=512)

    za_p3 = za_p.reshape(batch, seq, NA)
    p_kv = []
    for gi, (win, _) in enumerate(DIL_PATTERNS):
        keep = min(win, seq)
        for col in (A_KA, A_VA):
            lo = col + gi * ATT_GROUP_W
            p_kv.append(za_p3[:, seq - keep:, lo:lo + ATT_GROUP_W].reshape(batch, keep, ATT_HEADS, ATT_HEAD_DIM))
    tiles_per_seq = seq // 512
    p_conv = tails.reshape(batch, tiles_per_seq, 8, D_FF)[:, -1, 8 - (CONV_W - 1):, :]

    h_s = _rmsnorm(xs, g1, tm=bd)
    za_s, zb_s, zt_s, zif_s = _in_proj(h_s, wt_in, qk_gain, tm=bd, tn=IN_PROJ_TN, z_dtype=F32)
    o_s, lse_s = _attn_sample(za_s, [(ck1, cv1), (ck2, cv2), (ck3, cv3)], bb=4)
    hm_s, s_c, s_n, s_m = _mlstm_sample(zif_s, zb_s, gate_bias, st_c, st_n, st_m)
    x1_s, h2_s = _merge(o_s, lse_s, hm_s, zt_s, xs, wpa, wpm, wo, g2, tm=bd)
    y_s, g_s = _ffn_sample(h2_s, x1_s, w_ff, conv_w, cb, w_dn, st_conv, tf=512)

    s_kv = []
    for gi in range(N_GROUPS):
        for col in (A_KA, A_VA):
            lo = col + gi * ATT_GROUP_W
            s_kv.append(za_s[:, lo:lo + ATT_GROUP_W].reshape(bd, 1, ATT_HEADS, ATT_HEAD_DIM))
    s_conv = jnp.stack([st_conv[:, 1, :], g_s], axis=1)

    p_state = p_kv + [p_c, p_n, p_m, p_conv]
    s_state = s_kv + [s_c, s_n, s_m, s_conv]
    return y_p.reshape(batch, seq, d), y_s.reshape(bd, 1, d), p_state, s_state


def kernel(x_prompt, x_sample, cache_k_w128, cache_v_w128, cache_k_w512, cache_v_w512, cache_k_w2048,
           cache_v_w2048, state_mlstm_C, state_mlstm_n, state_mlstm_m, state_ffn_conv, norm_mix, w_in, q_norm,
           k_norm, b_igate, b_fgate, w_proj_att, w_proj_mlstm, w_out, norm_ffn, w_ffn_in, conv_w, conv_b,
           w_ffn_down):
    assert norm_mix.shape[0] == 1
    caches = [c[0] for c in (cache_k_w128, cache_v_w128, cache_k_w512, cache_v_w512, cache_k_w2048,
                             cache_v_w2048, state_mlstm_C, state_mlstm_n, state_mlstm_m, state_ffn_conv)]
    weights = [w[0] for w in (norm_mix, w_in, q_norm, k_norm, b_igate, b_fgate, w_proj_att, w_proj_mlstm,
                              w_out, norm_ffn, w_ffn_in, conv_w, conv_b, w_ffn_down)]
    y_p, y_s, p_state, s_state = _layer(x_prompt, x_sample, caches, *weights)
    return (y_p, y_s, *[a[None] for a in p_state], *[a[None] for a in s_state])
```

```python
import functools

import jax
import jax.numpy as jnp
from jax import lax
from jax.experimental import pallas as pl
from jax.experimental.pallas import tpu as pltpu

F32 = jnp.float32
BF16 = jnp.bfloat16

RMS_EPS = 1e-6
NEG_INF = -1e30
LANES = 128
VMEM_LIMIT = 56 * 1024 * 1024

D_MODEL = 2048
DIL_PATTERNS = ((128, 1), (512, 4), (2048, 16))
N_GROUPS = 3
ATT_HEADS = 4
ATT_HEAD_DIM = 128
ATT_SPAN = 128
ATT_SCALE = ATT_HEAD_DIM ** -0.5
ATT_GROUP_W = ATT_HEADS * ATT_HEAD_DIM
ATT_WIDTH = N_GROUPS * ATT_GROUP_W
M_HEADS = 4
M_QK_DIM = D_MODEL // (2 * M_HEADS)
M_V_DIM = D_MODEL // M_HEADS
M_QK_WIDTH = M_HEADS * M_QK_DIM
M_V_WIDTH = M_HEADS * M_V_DIM
M_K_SCALE = M_QK_DIM ** -0.5
D_FF = ((8 * D_MODEL // 3 + 255) // 256) * 256
CONV_W = 3
SPLIT_SIZES = (ATT_WIDTH, ATT_WIDTH, ATT_WIDTH, M_QK_WIDTH, M_QK_WIDTH, M_V_WIDTH, M_V_WIDTH,
               M_HEADS, M_HEADS, D_MODEL, D_MODEL)

IN_COLS = sum(SPLIT_SIZES)
A_QA = 0
A_KA = A_QA + ATT_WIDTH
A_VA = A_KA + ATT_WIDTH
NA = A_VA + ATT_WIDTH
B_QM = 0
B_KM = B_QM + M_QK_WIDTH
B_VM = B_KM + M_QK_WIDTH
B_OM = B_VM + M_V_WIDTH
NB = B_OM + M_V_WIDTH
N_IF = 2 * M_HEADS
T_GA = 0
T_GB = T_GA + D_MODEL
NT = T_GB + D_MODEL
assert NA + NB + N_IF + NT == IN_COLS


def _cparams(sem):
    return pltpu.CompilerParams(dimension_semantics=sem, vmem_limit_bytes=VMEM_LIMIT)


def _log_sigmoid(x):
    return jnp.minimum(x, 0.0) - jnp.log(1.0 + jnp.exp(-jnp.abs(x)))


def _rmsnorm_kernel(x_ref, g_ref, h_ref):
    x = x_ref[...]
    ms = jnp.mean(x * x, axis=-1, keepdims=True)
    h_ref[...] = (x * lax.rsqrt(ms + RMS_EPS) * g_ref[...]).astype(h_ref.dtype)


def _rmsnorm(x, gain, *, tm):
    m, d = x.shape
    assert m % tm == 0
    return pl.pallas_call(
        _rmsnorm_kernel,
        grid=(m // tm,),
        in_specs=[pl.BlockSpec((tm, d), lambda i: (i, 0)), pl.BlockSpec((1, d), lambda i: (0, 0))],
        out_specs=pl.BlockSpec((tm, d), lambda i: (i, 0)),
        out_shape=jax.ShapeDtypeStruct((m, d), BF16),
        compiler_params=_cparams(("parallel",)),
        name="rmsnorm",
    )(x, gain)


def _in_proj_kernel(h_ref, w_ref, wn_ref, wif_ref, qkg_ref, za_ref, zb_ref, zt_ref, zif_ref, *, n_norm, na, nb, nt):
    j = pl.program_id(1)

    def z_of(w):
        return lax.dot_general(h_ref[...], w.astype(BF16), (((1,), (1,)), ((), ())), preferred_element_type=F32)

    @pl.when(j < na)
    def _():
        za_ref[...] = z_of(w_ref[...])

    @pl.when(j < n_norm)
    def _():
        for c in range(za_ref.shape[1] // LANES):
            sl = slice(c * LANES, (c + 1) * LANES)
            zc = za_ref[:, sl]
            ms = jnp.mean(zc * zc, axis=-1, keepdims=True)
            za_ref[:, sl] = zc * lax.rsqrt(ms + RMS_EPS) * qkg_ref[:, sl]

    @pl.when((j >= na) & (j < na + nb))
    def _():
        zb_ref[...] = z_of(w_ref[...]).astype(zb_ref.dtype)

    @pl.when((j >= na + nb) & (j < na + nb + nt))
    def _():
        w = jnp.concatenate([w_ref[N_IF:, :], wn_ref[...]], axis=0)
        zt_ref[...] = z_of(w).astype(zt_ref.dtype)

    @pl.when(j == na + nb + nt)
    def _():
        w = jnp.concatenate([wif_ref[...], jnp.zeros((LANES - N_IF, wif_ref.shape[1]), F32)], axis=0)
        zif_ref[...] = z_of(w)


def _in_proj(h, wt, qk_gain, *, tm, tn, z_dtype):
    m, d = h.shape
    assert wt.shape == (IN_COLS, d) and m % tm == 0 and N_IF == 8
    assert NA % tn == 0 and NB % tn == 0 and NT % tn == 0 and A_VA % tn == 0 and tn % N_IF == 0
    na, nb, nt = NA // tn, NB // tn, NT // tn
    nw = na + nb + nt
    g_if = (NA + NB) // N_IF
    g_tn = tn // N_IF
    kern = functools.partial(_in_proj_kernel, n_norm=A_VA // tn, na=na, nb=nb, nt=nt)
    return pl.pallas_call(
        kern,
        grid=(m // tm, nw + 1),
        in_specs=[
            pl.BlockSpec((tm, d), lambda i, j: (i, 0)),
            pl.BlockSpec((tn, d), lambda i, j: (jnp.minimum(j, nw - 1), 0)),
            pl.BlockSpec((N_IF, d), lambda i, j: (g_if + g_tn * (jnp.clip(j, na + nb, nw - 1) - (na + nb) + 1), 0)),
            pl.BlockSpec((N_IF, d), lambda i, j: (g_if, 0)),
            pl.BlockSpec((1, tn), lambda i, j: (0, jnp.minimum(j, na - 1))),
        ],
        out_specs=[pl.BlockSpec((tm, tn), lambda i, j: (i, jnp.minimum(j, na - 1))),
                   pl.BlockSpec((tm, tn), lambda i, j: (i, jnp.clip(j - na, 0, nb - 1))),
                   pl.BlockSpec((tm, tn), lambda i, j: (i, jnp.clip(j - na - nb, 0, nt - 1))),
                   pl.BlockSpec((tm, LANES), lambda i, j: (i, 0))],
        out_shape=[jax.ShapeDtypeStruct((m, NA), F32), jax.ShapeDtypeStruct((m, NB), z_dtype),
                   jax.ShapeDtypeStruct((m, NT), z_dtype), jax.ShapeDtypeStruct((m, LANES), F32)],
        compiler_params=_cparams(("parallel", "arbitrary")),
        name="in_proj",
    )(h, wt, wt, wt, qk_gain)


def _attn_prompt_kernel(*refs, dil, nsub, hw, rb):
    q_refs, kp_refs, kc_refs, vp_refs, vc_refs = [refs[t * hw:(t + 1) * hw] for t in range(5)]
    o_ref, lse_ref = refs[5 * hw:5 * hw + 2]
    o_refs = refs[5 * hw + 2:]
    c = pl.program_id(1)
    blk = ATT_SPAN
    step = blk * dil
    qi = lax.broadcasted_iota(jnp.int32, (blk, 2 * blk), 0)
    kj = lax.broadcasted_iota(jnp.int32, (blk, 2 * blk), 1)
    band = (kj >= qi) & (kj <= qi + blk)
    first_bias = jnp.where((kj < blk) & (c == 0), NEG_INF, 0.0)
    lane = lax.broadcasted_iota(jnp.int32, (blk, LANES), 1)

    def rows(base, r):
        return pl.ds(base + r, blk) if dil == 1 else pl.ds(base + r, blk, stride=dil)

    def group(r0, s):
        base = s * step
        ids = [(r0 + r, n) for r in range(rb) for n in range(hw)]

        def stacked(cur_refs, prev_refs):
            parts = []
            for r, n in ids:
                cur = cur_refs[n][rows(base, r), :]
                if prev_refs is None:
                    parts.append(cur)
                else:
                    prv = prev_refs[n][rows(0, r), :] if s == 0 else cur_refs[n][rows(base - step, r), :]
                    parts.append(jnp.concatenate([prv, cur], axis=0))
            return jnp.stack(parts).astype(BF16)

        q = stacked(q_refs, None)
        k = stacked(kc_refs, kp_refs)
        v = stacked(vc_refs, vp_refs)
        sc = jnp.einsum('bqe,bke->bqk', q, k, preferred_element_type=F32) * ATT_SCALE
        if s == 0:
            sc = sc + first_bias
        sc = jnp.where(band, sc, NEG_INF)
        m = jnp.max(sc, axis=-1, keepdims=True)
        p = jnp.exp(sc - m)
        l = jnp.sum(p, axis=-1, keepdims=True)
        o = jnp.einsum('bqk,bke->bqe', p.astype(BF16), v, preferred_element_type=F32) / l
        lse = m + jnp.log(l)
        for r in range(rb):
            lse_tile = jnp.zeros((blk, LANES), F32)
            for n in range(hw):
                b = r * hw + n
                o_refs[n][rows(base, r0 + r), :] = o[b]
                lse_tile = jnp.where(lane == n, lse[b], lse_tile)
            lse_ref[rows(base, r0 + r), :] = lse_tile

    for s in range(nsub):
        if dil == rb:
            group(0, s)
        else:
            def body(it, carry, s=s):
                group(it * rb, s)
                return carry
            lax.fori_loop(0, dil // rb, body, 0)

    for n in range(hw):
        o_ref[:, n * ATT_HEAD_DIM:(n + 1) * ATT_HEAD_DIM] = o_refs[n][...]


def _attn_prompt_group(za, batch, seq, gi, dil, *, nsub, hw, rb):
    step = ATT_SPAN * dil
    tc = nsub * step
    assert seq % tc == 0 and ATT_HEADS % hw == 0 and dil % rb == 0
    nh = ATT_HEADS // hw
    e = ATT_HEAD_DIM
    z3 = za.reshape(batch, seq, NA)
    qc, kc, vc = [(col + gi * ATT_GROUP_W) // e for col in (A_QA, A_KA, A_VA)]

    def cur(col):
        return [pl.BlockSpec((None, tc, e), lambda b, c, h, n=n: (b, c, col + h * hw + n)) for n in range(hw)]

    def prev(col):
        return [pl.BlockSpec((None, step, e), lambda b, c, h, n=n: (b, jnp.maximum(c * nsub - 1, 0), col + h * hw + n))
                for n in range(hw)]

    kern = functools.partial(_attn_prompt_kernel, dil=dil, nsub=nsub, hw=hw, rb=rb)
    o, lse = pl.pallas_call(
        kern,
        grid=(batch, seq // tc, nh),
        in_specs=cur(qc) + prev(kc) + cur(kc) + prev(vc) + cur(vc),
        out_specs=[pl.BlockSpec((None, tc, hw * e), lambda b, c, h: (b, c, h)),
                   pl.BlockSpec((None, tc, LANES), lambda b, c, h: (b, c, h))],
        out_shape=[jax.ShapeDtypeStruct((batch, seq, ATT_GROUP_W), F32),
                   jax.ShapeDtypeStruct((batch, seq, nh * LANES), F32)],
        scratch_shapes=[pltpu.VMEM((tc, e), F32)] * hw,
        compiler_params=_cparams(("parallel", "parallel", "parallel")),
        name=f"attn_prompt_g{gi}",
    )(*([z3] * (5 * hw)))
    return o.reshape(batch * seq, ATT_GROUP_W), lse.reshape(batch * seq, nh * LANES)


def _attn_sample_kernel(z_ref, k1, v1, k2, v2, k3, v3, o1, o2, o3, l1, l2, l3):
    bb = z_ref.shape[0]
    bufs = ((k1, v1, o1, l1), (k2, v2, o2, l2), (k3, v3, o3, l3))
    lane = lax.broadcasted_iota(jnp.int32, (1, LANES), 1)

    def heads(b, col):
        return jnp.concatenate([z_ref[b, :, col + n * ATT_HEAD_DIM:col + (n + 1) * ATT_HEAD_DIM]
                                for n in range(ATT_HEADS)], axis=0)

    def body(b, carry):
        for gi, (k_ref, v_ref, o_ref, l_ref) in enumerate(bufs):
            q = heads(b, A_QA + gi * ATT_GROUP_W)
            k_new = heads(b, A_KA + gi * ATT_GROUP_W)
            v_new = heads(b, A_VA + gi * ATT_GROUP_W)
            kb = k_ref[b]
            vb = v_ref[b]
            s = jnp.sum(kb * q[None], axis=-1, keepdims=True) * ATT_SCALE
            s_new = jnp.sum(k_new * q, axis=-1, keepdims=True) * ATT_SCALE
            m = jnp.maximum(jnp.max(s, axis=0), s_new)
            p = jnp.exp(s - m[None])
            p_new = jnp.exp(s_new - m)
            l = jnp.sum(p, axis=0) + p_new
            o = (jnp.sum(p * vb, axis=0) + p_new * v_new) / l
            lse = m + jnp.log(l)
            lse_row = jnp.zeros((1, LANES), F32)
            for n in range(ATT_HEADS):
                o_ref[b, :, n * ATT_HEAD_DIM:(n + 1) * ATT_HEAD_DIM] = o[n:n + 1, :]
                lse_row = jnp.where(lane == n, lse[n:n + 1, :], lse_row)
            l_ref[b] = lse_row
        return carry

    lax.fori_loop(0, bb, body, 0)


def _attn_sample(za_s, caches, *, bb):
    bd = za_s.shape[0]
    assert bd % bb == 0
    ins, in_specs = [za_s.reshape(bd, 1, NA)], [pl.BlockSpec((bb, 1, NA), lambda i: (i, 0, 0))]
    for (k_buf, v_buf), (win, dil) in zip(caches, DIL_PATTERNS):
        assert k_buf.shape[1:] == (ATT_SPAN * dil, ATT_HEADS, ATT_HEAD_DIM)
        for buf in (k_buf, v_buf):
            ins.append(buf.reshape(bd, ATT_SPAN, dil, ATT_HEADS, ATT_HEAD_DIM))
            in_specs.append(pl.BlockSpec((bb, ATT_SPAN, None, ATT_HEADS, ATT_HEAD_DIM),
                                         lambda i: (i, 0, 0, 0, 0)))
    outs = pl.pallas_call(
        _attn_sample_kernel,
        grid=(bd // bb,),
        in_specs=in_specs,
        out_specs=[pl.BlockSpec((bb, 1, ATT_GROUP_W), lambda i: (i, 0, 0))] * 3
                  + [pl.BlockSpec((bb, 1, LANES), lambda i: (i, 0, 0))] * 3,
        out_shape=[jax.ShapeDtypeStruct((bd, 1, ATT_GROUP_W), F32)] * 3
                  + [jax.ShapeDtypeStruct((bd, 1, LANES), F32)] * 3,
        compiler_params=_cparams(("parallel",)),
        name="attn_sample",
    )(*ins)
    return [o[:, 0, :] for o in outs[:3]], [l[:, 0, :] for l in outs[3:]]


def _mlstm_prompt_kernel(q_ref, k_ref, v_ref, om_ref, g_ref, bias_ref, h_ref, c_out, n_out, m_out,
                         c_s, n_s, m_s):
    hd = pl.program_id(1)
    ci = pl.program_id(2)
    L = q_ref.shape[0]

    @pl.when(ci == 0)
    def _():
        c_s[...] = jnp.zeros_like(c_s)
        n_s[...] = jnp.zeros_like(n_s)
        m_s[...] = jnp.zeros_like(m_s)

    gates = g_ref[...] + bias_ref[...]
    lane = lax.broadcasted_iota(jnp.int32, (L, LANES), 1)
    li_col = jnp.sum(jnp.where(lane == hd, gates, 0.0), axis=1, keepdims=True)
    lf_col = _log_sigmoid(jnp.sum(jnp.where(lane == hd + M_HEADS, gates, 0.0), axis=1, keepdims=True))
    gates_t = gates.T
    sub = lax.broadcasted_iota(jnp.int32, (LANES, L), 0)
    li_row = jnp.sum(jnp.where(sub == hd, gates_t, 0.0), axis=0, keepdims=True)
    lf_row = _log_sigmoid(jnp.sum(jnp.where(sub == hd + M_HEADS, gates_t, 0.0), axis=0, keepdims=True))

    ti = lax.broadcasted_iota(jnp.int32, (L, L), 0)
    si = lax.broadcasted_iota(jnp.int32, (L, L), 1)
    causal = si <= ti
    b_col = jnp.sum(jnp.where(causal, lf_row, 0.0), axis=1, keepdims=True)
    b_row = jnp.sum(jnp.where(ti <= si, lf_col, 0.0), axis=0, keepdims=True)
    b_end = jnp.sum(lf_row, axis=1, keepdims=True)

    m_prev = m_s[...]
    dmat = jnp.where(causal, b_col - b_row + li_row, NEG_INF)
    inter = b_col + m_prev
    mt = jnp.maximum(inter, jnp.max(dmat, axis=1, keepdims=True))

    qb = q_ref[...]
    q = qb.astype(F32)
    k = k_ref[...].astype(F32) * M_K_SCALE
    vb = v_ref[...]
    qk = lax.dot_general(qb, k.astype(BF16), (((1,), (1,)), ((), ())), preferred_element_type=F32)
    a = jnp.exp(dmat - mt) * qk
    w_inter = jnp.exp(inter - mt)
    num = (jnp.dot(a.astype(BF16), vb, preferred_element_type=F32)
           + w_inter * jnp.dot(qb, c_s[...].astype(BF16), preferred_element_type=F32))
    den = jnp.sum(a, axis=1, keepdims=True) + w_inter * jnp.sum(q * n_s[...], axis=1, keepdims=True)
    h = num / jnp.maximum(jnp.abs(den), jnp.exp(-mt))
    h_ref[...] = (jax.nn.sigmoid(om_ref[...].astype(F32)) * h).astype(h_ref.dtype)

    g_col = b_end - b_col + li_col
    g_row = b_end - b_row + li_row
    m_new = jnp.maximum(b_end + m_prev, jnp.max(g_row, axis=1, keepdims=True))
    decay = jnp.exp(b_end + m_prev - m_new)
    kw = jnp.exp(g_col - m_new) * k
    c_s[...] = decay * c_s[...] + jnp.dot(kw.T.astype(BF16), vb, preferred_element_type=F32)
    n_s[...] = decay * n_s[...] + jnp.sum(kw, axis=0, keepdims=True)
    m_s[...] = m_new

    @pl.when(ci == pl.num_programs(2) - 1)
    def _():
        c_out[...] = c_s[...]
        n_out[...] = n_s[...]
        m_out[...] = jnp.broadcast_to(m_s[...], m_out.shape)


def _mlstm_prompt(zif, zb, gate_bias, batch, seq, *, chunk):
    assert seq % chunk == 0
    nc = seq // chunk
    zif3 = zif.reshape(batch, seq, LANES)
    zb3 = zb.reshape(batch, seq, NB)
    qk_blk = lambda col: pl.BlockSpec((None, chunk, M_QK_DIM), lambda b, h, c: (b, c, col // M_QK_DIM + h))
    v_blk = lambda col: pl.BlockSpec((None, chunk, M_V_DIM), lambda b, h, c: (b, c, col // M_V_DIM + h))
    hm, c1, n1, m1 = pl.pallas_call(
        _mlstm_prompt_kernel,
        grid=(batch, M_HEADS, nc),
        in_specs=[qk_blk(B_QM), qk_blk(B_KM), v_blk(B_VM), v_blk(B_OM),
                  pl.BlockSpec((None, chunk, LANES), lambda b, h, c: (b, c, 0)),
                  pl.BlockSpec((1, LANES), lambda b, h, c: (0, 0))],
        out_specs=[pl.BlockSpec((None, chunk, M_V_DIM), lambda b, h, c: (b, c, h)),
                   pl.BlockSpec((None, None, M_QK_DIM, M_V_DIM), lambda b, h, c: (b, h, 0, 0)),
                   pl.BlockSpec((None, None, 1, M_QK_DIM), lambda b, h, c: (b, h, 0, 0)),
                   pl.BlockSpec((None, None, 1, LANES), lambda b, h, c: (b, h, 0, 0))],
        out_shape=[jax.ShapeDtypeStruct((batch, seq, M_V_WIDTH), BF16),
                   jax.ShapeDtypeStruct((batch, M_HEADS, M_QK_DIM, M_V_DIM), F32),
                   jax.ShapeDtypeStruct((batch, M_HEADS, 1, M_QK_DIM), F32),
                   jax.ShapeDtypeStruct((batch, M_HEADS, 1, LANES), F32)],
        scratch_shapes=[pltpu.VMEM((M_QK_DIM, M_V_DIM), F32), pltpu.VMEM((1, M_QK_DIM), F32),
                        pltpu.VMEM((1, 1), F32)],
        compiler_params=_cparams(("parallel", "parallel", "arbitrary")),
        name="mlstm_prompt",
    )(zb3, zb3, zb3, zb3, zif3, gate_bias)
    return hm.reshape(batch * seq, M_V_WIDTH), c1, n1[:, :, 0, :], m1[:, :, 0, 0]


def _row_to_col(row):
    n = row.shape[1]
    return jnp.broadcast_to(row, (LANES, n)).T[:, 0:1]


def _mlstm_sample_kernel(zif_ref, zb_ref, bias_ref, c_ref, n_ref, m_ref, h_ref, c_out, n_out, m_out):
    gates = zif_ref[...] + bias_ref[...]
    lane = lax.broadcasted_iota(jnp.int32, (1, LANES), 1)
    m_row = jnp.zeros((1, LANES), F32)
    for h in range(M_HEADS):
        q = zb_ref[:, B_QM + h * M_QK_DIM:B_QM + (h + 1) * M_QK_DIM].astype(F32)
        k = zb_ref[:, B_KM + h * M_QK_DIM:B_KM + (h + 1) * M_QK_DIM].astype(F32) * M_K_SCALE
        v = zb_ref[:, B_VM + h * M_V_DIM:B_VM + (h + 1) * M_V_DIM].astype(F32)
        om = zb_ref[:, B_OM + h * M_V_DIM:B_OM + (h + 1) * M_V_DIM].astype(F32)
        li = gates[:, h:h + 1]
        lf = _log_sigmoid(gates[:, M_HEADS + h:M_HEADS + h + 1])
        m0 = m_ref[:, h:h + 1]
        c0 = c_ref[h]
        n0 = n_ref[h:h + 1, :]
        inter = lf + m0
        mt = jnp.maximum(inter, li)
        a = jnp.exp(li - mt) * jnp.sum(q * k, axis=1, keepdims=True)
        w_inter = jnp.exp(inter - mt)
        q_c = jnp.sum(_row_to_col(q) * c0, axis=0, keepdims=True)
        num = a * v + w_inter * q_c
        den = a + w_inter * jnp.sum(q * n0, axis=1, keepdims=True)
        hv = num / jnp.maximum(jnp.abs(den), jnp.exp(-mt))
        h_ref[:, h * M_V_DIM:(h + 1) * M_V_DIM] = (jax.nn.sigmoid(om) * hv).astype(h_ref.dtype)
        m_new = jnp.maximum(inter, li)
        decay = jnp.exp(inter - m_new)
        ws = jnp.exp(li - m_new)
        c_out[h] = decay * c0 + _row_to_col(ws * k) * v
        n_out[h:h + 1, :] = decay * n0 + ws * k
        m_row = jnp.where(lane == h, m_new, m_row)
    m_out[...] = m_row


def _mlstm_sample(zif_s, zb_s, gate_bias, c0, n0, m0):
    bd = zif_s.shape[0]
    hm, c1, n1, m1 = pl.pallas_call(
        _mlstm_sample_kernel,
        grid=(bd,),
        in_specs=[pl.BlockSpec((None, 1, LANES), lambda b: (b, 0, 0)),
                  pl.BlockSpec((None, 1, NB), lambda b: (b, 0, 0)),
                  pl.BlockSpec((1, LANES), lambda b: (0, 0)),
                  pl.BlockSpec((None, M_HEADS, M_QK_DIM, M_V_DIM), lambda b: (b, 0, 0, 0)),
                  pl.BlockSpec((None, M_HEADS, M_QK_DIM), lambda b: (b, 0, 0)),
                  pl.BlockSpec((None, 1, M_HEADS), lambda b: (b, 0, 0))],
        out_specs=[pl.BlockSpec((None, 1, M_V_WIDTH), lambda b: (b, 0, 0)),
                   pl.BlockSpec((None, M_HEADS, M_QK_DIM, M_V_DIM), lambda b: (b, 0, 0, 0)),
                   pl.BlockSpec((None, M_HEADS, M_QK_DIM), lambda b: (b, 0, 0)),
                   pl.BlockSpec((None, 1, LANES), lambda b: (b, 0, 0))],
        out_shape=[jax.ShapeDtypeStruct((bd, 1, M_V_WIDTH), BF16),
                   jax.ShapeDtypeStruct((bd, M_HEADS, M_QK_DIM, M_V_DIM), F32),
                   jax.ShapeDtypeStruct((bd, M_HEADS, M_QK_DIM), F32),
                   jax.ShapeDtypeStruct((bd, 1, LANES), F32)],
        compiler_params=_cparams(("parallel",)),
        name="mlstm_sample",
    )(zif_s.reshape(bd, 1, LANES), zb_s.reshape(bd, 1, NB), gate_bias, c0, n0, m0.reshape(bd, 1, M_HEADS))
    return hm.reshape(bd, M_V_WIDTH), c1, n1, m1[:, 0, :M_HEADS]


def _merge_kernel(o1, o2, o3, l1, l2, l3, hm_ref, ga_ref, gb_ref, x_ref, wpa_ref, wpm_ref, wo_ref, g2_ref,
                  x1_ref, h2_ref, *, hws):
    os_ = (o1, o2, o3)
    parts = []
    for n in range(ATT_HEADS):
        sl = slice(n * ATT_HEAD_DIM, (n + 1) * ATT_HEAD_DIM)
        ls = []
        for l_ref, hw in zip((l1, l2, l3), hws):
            ln = (n // hw) * LANES + n % hw
            ls.append(l_ref[:, ln:ln + 1])
        mx = jnp.maximum(jnp.maximum(ls[0], ls[1]), ls[2])
        es = [jnp.exp(l - mx) for l in ls]
        tot = es[0] + es[1] + es[2]
        parts.append((es[0] / tot) * o1[:, sl] + (es[1] / tot) * o2[:, sl] + (es[2] / tot) * o3[:, sl])
    att = jnp.concatenate(parts, axis=1).astype(BF16)
    pa = jnp.dot(att, wpa_ref[...], preferred_element_type=F32)
    pm = jnp.dot(hm_ref[...], wpm_ref[...], preferred_element_type=F32)
    merged = (jax.nn.sigmoid(ga_ref[...].astype(F32)) * pa + jax.nn.sigmoid(gb_ref[...].astype(F32)) * pm)
    x1 = x_ref[...] + jnp.dot(merged.astype(BF16), wo_ref[...], preferred_element_type=F32)
    x1_ref[...] = x1
    ms = jnp.mean(x1 * x1, axis=-1, keepdims=True)
    h2_ref[...] = (x1 * lax.rsqrt(ms + RMS_EPS) * g2_ref[...]).astype(BF16)


def _merge(o_list, lse_list, hm, zt, x, wpa, wpm, wo, g2, *, tm):
    m, d = x.shape
    assert m % tm == 0 and zt.shape[1] == NT
    row_blk = lambda w: pl.BlockSpec((tm, w), lambda i: (i, 0))
    const = lambda shape: pl.BlockSpec(shape, lambda i: (0, 0), pipeline_mode=pl.Buffered(1))
    hws = tuple(ATT_HEADS * LANES // l.shape[1] for l in lse_list)
    return pl.pallas_call(
        functools.partial(_merge_kernel, hws=hws),
        grid=(m // tm,),
        in_specs=[row_blk(ATT_GROUP_W)] * 3 + [row_blk(l.shape[1]) for l in lse_list]
                 + [row_blk(M_V_WIDTH),
                    pl.BlockSpec((tm, d), lambda i: (i, T_GA // d)),
                    pl.BlockSpec((tm, d), lambda i: (i, T_GB // d)),
                    row_blk(d),
                    const(wpa.shape), const(wpm.shape), const(wo.shape), const((1, d))],
        out_specs=[row_blk(d), row_blk(d)],
        out_shape=[jax.ShapeDtypeStruct((m, d), F32), jax.ShapeDtypeStruct((m, d), BF16)],
        compiler_params=_cparams(("parallel",)),
        name="merge",
    )(*o_list, *lse_list, hm, zt, zt, x, wpa, wpm, wo, g2)


def _ffn_body(h2_ref, wu_ref, wg_ref, cw_ref, cb_ref, wd_ref, x1_ref, y_ref, g_prev2, g_prev1):
    j = pl.program_id(1)
    h2 = h2_ref[...]
    u = jnp.dot(h2, wu_ref[...], preferred_element_type=F32)
    g = jnp.dot(h2, wg_ref[...], preferred_element_type=F32)
    gconv = cb_ref[...] + ((g_prev2(g) * cw_ref[0:1, :] + g_prev1(g) * cw_ref[1:2, :]) + g * cw_ref[2:3, :])
    act = (jax.nn.gelu(gconv) * u).astype(BF16)
    down = jnp.dot(act, wd_ref[...], preferred_element_type=F32)

    @pl.when(j == 0)
    def _():
        y_ref[...] = x1_ref[...] + down

    @pl.when(j > 0)
    def _():
        y_ref[...] += down

    return g


def _ffn_prompt_kernel(h2_ref, wu_ref, wg_ref, cw_ref, cb_ref, wd_ref, x1_ref, y_ref, tail_ref, prev_s,
                       u0_s, g0_s, u1_s, g1_s, *, tiles_per_seq, nf):
    i = pl.program_id(0)
    j = pl.program_id(1)
    tm = h2_ref.shape[0]
    slots = ((u0_s, g0_s), (u1_s, g1_s))

    def up(u_s, g_s):
        h2 = h2_ref[...]
        u_s[...] = jnp.dot(h2, wu_ref[...], preferred_element_type=F32)
        g_s[...] = jnp.dot(h2, wg_ref[...], preferred_element_type=F32)

    def down(u_s, g_s):
        jt = j - 1
        prev = jnp.where(i % tiles_per_seq == 0, 0.0, prev_s[jt])
        p2, p1 = prev[6:7, :], prev[7:8, :]
        g = g_s[...]
        r = lax.broadcasted_iota(jnp.int32, g.shape, 0)
        g_m1 = jnp.where(r == 0, p1, pltpu.roll(g, 1, axis=0))
        g_m2 = jnp.where(r == 0, p2, jnp.where(r == 1, p1, pltpu.roll(g, 2, axis=0)))
        gconv = cb_ref[...] + ((g_m2 * cw_ref[0:1, :] + g_m1 * cw_ref[1:2, :]) + g * cw_ref[2:3, :])
        act = (jax.nn.gelu(gconv) * u_s[...]).astype(BF16)
        y_ref[...] += jnp.dot(act, wd_ref[...], preferred_element_type=F32)
        prev_s[jt] = g[tm - 8:tm, :]
        tail_ref[...] = g[tm - 8:tm, :]

    @pl.when(j == 0)
    def _():
        y_ref[...] = x1_ref[...]
        up(*slots[0])

    for par in range(2):
        @pl.when((j > 0) & (j < nf) & (j % 2 == par))
        def _(par=par):
            up(*slots[par])
            down(*slots[1 - par])

    @pl.when(j == nf)
    def _():
        down(*slots[(nf - 1) % 2])


def _ffn_sample_kernel(h2_ref, wu_ref, wg_ref, cw_ref, cb_ref, wd_ref, x1_ref, b2_ref, b1_ref, y_ref, g_ref):
    g = _ffn_body(h2_ref, wu_ref, wg_ref, cw_ref, cb_ref, wd_ref, x1_ref, y_ref,
                  lambda g: b2_ref[...], lambda g: b1_ref[...])
    g_ref[...] = g


def _ffn_specs(tm, tf, d, nf):
    return [pl.BlockSpec((tm, d), lambda i, j: (i, 0)),
            pl.BlockSpec((d, tf), lambda i, j: (0, j)),
            pl.BlockSpec((d, tf), lambda i, j: (0, nf + j)),
            pl.BlockSpec((CONV_W, tf), lambda i, j: (0, j)),
            pl.BlockSpec((1, tf), lambda i, j: (0, j)),
            pl.BlockSpec((tf, d), lambda i, j: (j, 0)),
            pl.BlockSpec((tm, d), lambda i, j: (i, 0))]


def _ffn_prompt(h2, x1, w_in, conv_w, conv_b, w_down, seq, *, tm, tf):
    m, d = x1.shape
    assert m % tm == 0 and seq % tm == 0 and D_FF % tf == 0 and tm % 8 == 0
    nf = D_FF // tf
    kern = functools.partial(_ffn_prompt_kernel, tiles_per_seq=seq // tm, nf=nf)
    up_t = lambda j: jnp.minimum(j, nf - 1)
    dn_t = lambda j: jnp.maximum(j - 1, 0)
    y, tails = pl.pallas_call(
        kern,
        grid=(m // tm, nf + 1),
        in_specs=[pl.BlockSpec((tm, d), lambda i, j: (i, 0)),
                  pl.BlockSpec((d, tf), lambda i, j: (0, up_t(j))),
                  pl.BlockSpec((d, tf), lambda i, j: (0, nf + up_t(j))),
                  pl.BlockSpec((CONV_W, tf), lambda i, j: (0, dn_t(j))),
                  pl.BlockSpec((1, tf), lambda i, j: (0, dn_t(j))),
                  pl.BlockSpec((tf, d), lambda i, j: (dn_t(j), 0)),
                  pl.BlockSpec((tm, d), lambda i, j: (i, 0))],
        out_specs=[pl.BlockSpec((tm, d), lambda i, j: (i, 0)),
                   pl.BlockSpec((None, 8, tf), lambda i, j: (i, 0, dn_t(j)))],
        out_shape=[jax.ShapeDtypeStruct((m, d), F32), jax.ShapeDtypeStruct((m // tm, 8, D_FF), F32)],
        scratch_shapes=[pltpu.VMEM((nf, 8, tf), F32)] + [pltpu.VMEM((tm, tf), F32)] * 4,
        compiler_params=_cparams(("arbitrary", "arbitrary")),
        name="ffn_prompt",
    )(h2, w_in, w_in, conv_w, conv_b, w_down, x1)
    return y, tails


def _ffn_sample(h2, x1, w_in, conv_w, conv_b, w_down, conv_buf, *, tf):
    m, d = x1.shape
    nf = D_FF // tf
    buf2d = conv_buf.reshape(m, (CONV_W - 1) * D_FF)
    return pl.pallas_call(
        _ffn_sample_kernel,
        grid=(1, nf),
        in_specs=_ffn_specs(m, tf, d, nf) + [pl.BlockSpec((m, tf), lambda i, j: (0, j)),
                                             pl.BlockSpec((m, tf), lambda i, j: (0, nf + j))],
        out_specs=[pl.BlockSpec((m, d), lambda i, j: (i, 0)), pl.BlockSpec((m, tf), lambda i, j: (0, j))],
        out_shape=[jax.ShapeDtypeStruct((m, d), F32), jax.ShapeDtypeStruct((m, D_FF), F32)],
        compiler_params=_cparams(("arbitrary", "arbitrary")),
        name="ffn_sample",
    )(h2, w_in, w_in, conv_w, conv_b, w_down, x1, buf2d, buf2d)


IN_PROJ_TN = 512


def _qk_gain_row(q_norm, k_norm):
    reps = N_GROUPS * ATT_HEADS
    return jnp.concatenate([jnp.tile(q_norm, reps), jnp.tile(k_norm, reps), jnp.zeros((NA - A_VA,), F32)])[None, :]


ATTN_PROMPT_TILING = ((4, 4, 1), (1, 4, 1), (1, 2, 2))


def _layer(x_prompt, x_sample, caches, norm_mix, w_in, q_norm, k_norm, b_igate, b_fgate, w_proj_att,
           w_proj_mlstm, w_out, norm_ffn, w_ffn_in, conv_w, conv_b, w_ffn_down):
    batch, seq, d = x_prompt.shape
    bd = x_sample.shape[0]
    assert x_sample.shape[1] == 1 and d == D_MODEL
    (ck1, cv1, ck2, cv2, ck3, cv3, st_c, st_n, st_m, st_conv) = caches

    wt_in = jnp.swapaxes(w_in, 0, 1)
    qk_gain = _qk_gain_row(q_norm, k_norm)
    g1 = norm_mix[None, :]
    g2 = norm_ffn[None, :]
    gate_bias = jnp.concatenate([b_igate, b_fgate, jnp.zeros((LANES - 2 * M_HEADS,), F32)])[None, :]
    wpa = w_proj_att.astype(BF16)
    wpm = w_proj_mlstm.astype(BF16)
    wo = w_out.astype(BF16)
    w_ff = w_ffn_in.astype(BF16)
    w_dn = w_ffn_down.astype(BF16)
    cb = conv_b[None, :]

    xp = x_prompt.reshape(batch * seq, d)
    xs = x_sample.reshape(bd, d)

    h_p = _rmsnorm(xp, g1, tm=512)
    za_p, zb_p, zt_p, zif_p = _in_proj(h_p, wt_in, qk_gain, tm=2048, tn=IN_PROJ_TN, z_dtype=BF16)
    o_p, lse_p = zip(*[_attn_prompt_group(za_p, batch, seq, gi, dil, nsub=nsub, hw=hw, rb=rb)
                       for gi, ((_, dil), (nsub, hw, rb)) in enumerate(zip(DIL_PATTERNS, ATTN_PROMPT_TILING))])
    hm_p, p_c, p_n, p_m = _mlstm_prompt(zif_p, zb_p, gate_bias, batch, seq, chunk=256)
    x1_p, h2_p = _merge(o_p, lse_p, hm_p, zt_p, xp, wpa, wpm, wo, g2, tm=256)
    y_p, tails = _ffn_prompt(h2_p, x1_p, w_ff, conv_w, cb, w_dn, seq, tm=512, tf=512)

    za_p3 = za_p.reshape(batch, seq, NA)
    p_kv = []
    for gi, (win, _) in enumerate(DIL_PATTERNS):
        keep = min(win, seq)
        for col in (A_KA, A_VA):
            lo = col + gi * ATT_GROUP_W
            p_kv.append(za_p3[:, seq - keep:, lo:lo + ATT_GROUP_W].reshape(batch, keep, ATT_HEADS, ATT_HEAD_DIM))
    tiles_per_seq = seq // 512
    p_conv = tails.reshape(batch, tiles_per_seq, 8, D_FF)[:, -1, 8 - (CONV_W - 1):, :]

    h_s = _rmsnorm(xs, g1, tm=bd)
    za_s, zb_s, zt_s, zif_s = _in_proj(h_s, wt_in, qk_gain, tm=bd, tn=IN_PROJ_TN, z_dtype=F32)
    o_s, lse_s = _attn_sample(za_s, [(ck1, cv1), (ck2, cv2), (ck3, cv3)], bb=4)
    hm_s, s_c, s_n, s_m = _mlstm_sample(zif_s, zb_s, gate_bias, st_c, st_n, st_m)
    x1_s, h2_s = _merge(o_s, lse_s, hm_s, zt_s, xs, wpa, wpm, wo, g2, tm=bd)
    y_s, g_s = _ffn_sample(h2_s, x1_s, w_ff, conv_w, cb, w_dn, st_conv, tf=512)

    s_kv = []
    for gi in range(N_GROUPS):
        for col in (A_KA, A_VA):
            lo = col + gi * ATT_GROUP_W
            s_kv.append(za_s[:, lo:lo + ATT_GROUP_W].reshape(bd, 1, ATT_HEADS, ATT_HEAD_DIM))
    s_conv = jnp.stack([st_conv[:, 1, :], g_s], axis=1)

    p_state = p_kv + [p_c, p_n, p_m, p_conv]
    s_state = s_kv + [s_c, s_n, s_m, s_conv]
    return y_p.reshape(batch, seq, d), y_s.reshape(bd, 1, d), p_state, s_state


def kernel(x_prompt, x_sample, cache_k_w128, cache_v_w128, cache_k_w512, cache_v_w512, cache_k_w2048,
           cache_v_w2048, state_mlstm_C, state_mlstm_n, state_mlstm_m, state_ffn_conv, norm_mix, w_in, q_norm,
           k_norm, b_igate, b_fgate, w_proj_att, w_proj_mlstm, w_out, norm_ffn, w_ffn_in, conv_w, conv_b,
           w_ffn_down):
    assert norm_mix.shape[0] == 1
    caches = [c[0] for c in (cache_k_w128, cache_v_w128, cache_k_w512, cache_v_w512, cache_k_w2048,
                             cache_v_w2048, state_mlstm_C, state_mlstm_n, state_mlstm_m, state_ffn_conv)]
    weights = [w[0] for w in (norm_mix, w_in, q_norm, k_norm, b_igate, b_fgate, w_proj_att, w_proj_mlstm,
                              w_out, norm_ffn, w_ffn_in, conv_w, conv_b, w_ffn_down)]
    y_p, y_s, p_state, s_state = _layer(x_prompt, x_sample, caches, *weights)
    return (y_p, y_s, *[a[None] for a in p_state], *[a[None] for a in s_state])
```

```python
import functools

import jax
import jax.numpy as jnp
from jax import lax
from jax.experimental import pallas as pl
from jax.experimental.pallas import tpu as pltpu

F32 = jnp.float32
BF16 = jnp.bfloat16

RMS_EPS = 1e-6
NEG_INF = -1e30
LANES = 128
VMEM_LIMIT = 56 * 1024 * 1024

D_MODEL = 2048
DIL_PATTERNS = ((128, 1), (512, 4), (2048, 16))
N_GROUPS = 3
ATT_HEADS = 4
ATT_HEAD_DIM = 128
ATT_SPAN = 128
ATT_SCALE = ATT_HEAD_DIM ** -0.5
ATT_GROUP_W = ATT_HEADS * ATT_HEAD_DIM
ATT_WIDTH = N_GROUPS * ATT_GROUP_W
M_HEADS = 4
M_QK_DIM = D_MODEL // (2 * M_HEADS)
M_V_DIM = D_MODEL // M_HEADS
M_QK_WIDTH = M_HEADS * M_QK_DIM
M_V_WIDTH = M_HEADS * M_V_DIM
M_K_SCALE = M_QK_DIM ** -0.5
D_FF = ((8 * D_MODEL // 3 + 255) // 256) * 256
CONV_W = 3
SPLIT_SIZES = (ATT_WIDTH, ATT_WIDTH, ATT_WIDTH, M_QK_WIDTH, M_QK_WIDTH, M_V_WIDTH, M_V_WIDTH,
               M_HEADS, M_HEADS, D_MODEL, D_MODEL)

IN_COLS = sum(SPLIT_SIZES)
A_QA = 0
A_KA = A_QA + ATT_WIDTH
A_VA = A_KA + ATT_WIDTH
NA = A_VA + ATT_WIDTH
B_QM = 0
B_KM = B_QM + M_QK_WIDTH
B_VM = B_KM + M_QK_WIDTH
B_OM = B_VM + M_V_WIDTH
NB = B_OM + M_V_WIDTH
N_IF = 2 * M_HEADS
T_GA = 0
T_GB = T_GA + D_MODEL
NT = T_GB + D_MODEL
assert NA + NB + N_IF + NT == IN_COLS


def _cparams(sem):
    return pltpu.CompilerParams(dimension_semantics=sem, vmem_limit_bytes=VMEM_LIMIT)


def _log_sigmoid(x):
    return jnp.minimum(x, 0.0) - jnp.log(1.0 + jnp.exp(-jnp.abs(x)))


def _rmsnorm_kernel(x_ref, g_ref, h_ref):
    x = x_ref[...]
    ms = jnp.mean(x * x, axis=-1, keepdims=True)
    h_ref[...] = (x * lax.rsqrt(ms + RMS_EPS) * g_ref[...]).astype(h_ref.dtype)


def _rmsnorm(x, gain, *, tm):
    m, d = x.shape
    assert m % tm == 0
    return pl.pallas_call(
        _rmsnorm_kernel,
        grid=(m // tm,),
        in_specs=[pl.BlockSpec((tm, d), lambda i: (i, 0)), pl.BlockSpec((1, d), lambda i: (0, 0))],
        out_specs=pl.BlockSpec((tm, d), lambda i: (i, 0)),
        out_shape=jax.ShapeDtypeStruct((m, d), BF16),
        compiler_params=_cparams(("parallel",)),
        name="rmsnorm",
    )(x, gain)


def _in_proj_kernel(h_ref, w_ref, wn_ref, wif_ref, qkg_ref, za_ref, zb_ref, zt_ref, zif_ref, *, n_norm, na, nb, nt):
    j = pl.program_id(1)

    def z_of(w):
        return lax.dot_general(h_ref[...], w.astype(BF16), (((1,), (1,)), ((), ())), preferred_element_type=F32)

    @pl.when(j < na)
    def _():
        za_ref[...] = z_of(w_ref[...])

    @pl.when(j < n_norm)
    def _():
        for c in range(za_ref.shape[1] // LANES):
            sl = slice(c * LANES, (c + 1) * LANES)
            zc = za_ref[:, sl]
            ms = jnp.mean(zc * zc, axis=-1, keepdims=True)
            za_ref[:, sl] = zc * lax.rsqrt(ms + RMS_EPS) * qkg_ref[:, sl]

    @pl.when((j >= na) & (j < na + nb))
    def _():
        zb_ref[...] = z_of(w_ref[...]).astype(zb_ref.dtype)

    @pl.when((j >= na + nb) & (j < na + nb + nt))
    def _():
        w = jnp.concatenate([w_ref[N_IF:, :], wn_ref[...]], axis=0)
        zt_ref[...] = z_of(w).astype(zt_ref.dtype)

    @pl.when(j == na + nb + nt)
    def _():
        w = jnp.concatenate([wif_ref[...], jnp.zeros((LANES - N_IF, wif_ref.shape[1]), F32)], axis=0)
        zif_ref[...] = z_of(w)


def _in_proj(h, wt, qk_gain, *, tm, tn, z_dtype):
    m, d = h.shape
    assert wt.shape == (IN_COLS, d) and m % tm == 0 and N_IF == 8
    assert NA % tn == 0 and NB % tn == 0 and NT % tn == 0 and A_VA % tn == 0 and tn % N_IF == 0
    na, nb, nt = NA // tn, NB // tn, NT // tn
    nw = na + nb + nt
    g_if = (NA + NB) // N_IF
    g_tn = tn // N_IF
    kern = functools.partial(_in_proj_kernel, n_norm=A_VA // tn, na=na, nb=nb, nt=nt)
    return pl.pallas_call(
        kern,
        grid=(m // tm, nw + 1),
        in_specs=[
            pl.BlockSpec((tm, d), lambda i, j: (i, 0)),
            pl.BlockSpec((tn, d), lambda i, j: (jnp.minimum(j, nw - 1), 0)),
            pl.BlockSpec((N_IF, d), lambda i, j: (g_if + g_tn * (jnp.clip(j, na + nb, nw - 1) - (na + nb) + 1), 0)),
            pl.BlockSpec((N_IF, d), lambda i, j: (g_if, 0)),
            pl.BlockSpec((1, tn), lambda i, j: (0, jnp.minimum(j, na - 1))),
        ],
        out_specs=[pl.BlockSpec((tm, tn), lambda i, j: (i, jnp.minimum(j, na - 1))),
                   pl.BlockSpec((tm, tn), lambda i, j: (i, jnp.clip(j - na, 0, nb - 1))),
                   pl.BlockSpec((tm, tn), lambda i, j: (i, jnp.clip(j - na - nb, 0, nt - 1))),
                   pl.BlockSpec((tm, LANES), lambda i, j: (i, 0))],
        out_shape=[jax.ShapeDtypeStruct((m, NA), F32), jax.ShapeDtypeStruct((m, NB), z_dtype),
                   jax.ShapeDtypeStruct((m, NT), z_dtype), jax.ShapeDtypeStruct((m, LANES), F32)],
        compiler_params=_cparams(("parallel", "arbitrary")),
        name="in_proj",
    )(h, wt, wt, wt, qk_gain)


def _attn_prompt_kernel(*refs, dil, nsub, hw, rb):
    q_refs, kp_refs, kc_refs, vp_refs, vc_refs = [refs[t * hw:(t + 1) * hw] for t in range(5)]
    o_ref, lse_ref = refs[5 * hw:5 * hw + 2]
    o_refs = refs[5 * hw + 2:]
    c = pl.program_id(1)
    blk = ATT_SPAN
    step = blk * dil
    qi = lax.broadcasted_iota(jnp.int32, (blk, 2 * blk), 0)
    kj = lax.broadcasted_iota(jnp.int32, (blk, 2 * blk), 1)
    band = (kj >= qi) & (kj <= qi + blk)
    first_bias = jnp.where((kj < blk) & (c == 0), NEG_INF, 0.0)
    lane = lax.broadcasted_iota(jnp.int32, (blk, LANES), 1)

    def rows(base, r):
        return pl.ds(base + r, blk) if dil == 1 else pl.ds(base + r, blk, stride=dil)

    def group(r0, s):
        base = s * step
        ids = [(r0 + r, n) for r in range(rb) for n in range(hw)]

        def stacked(cur_refs, prev_refs):
            parts = []
            for r, n in ids:
                cur = cur_refs[n][rows(base, r), :]
                if prev_refs is None:
                    parts.append(cur)
                else:
                    prv = prev_refs[n][rows(0, r), :] if s == 0 else cur_refs[n][rows(base - step, r), :]
                    parts.append(jnp.concatenate([prv, cur], axis=0))
            return jnp.stack(parts).astype(BF16)

        q = stacked(q_refs, None)
        k = stacked(kc_refs, kp_refs)
        v = stacked(vc_refs, vp_refs)
        sc = jnp.einsum('bqe,bke->bqk', q, k, preferred_element_type=F32) * ATT_SCALE
        if s == 0:
            sc = sc + first_bias
        sc = jnp.where(band, sc, NEG_INF)
        m = jnp.max(sc, axis=-1, keepdims=True)
        p = jnp.exp(sc - m)
        l = jnp.sum(p, axis=-1, keepdims=True)
        o = jnp.einsum('bqk,bke->bqe', p.astype(BF16), v, preferred_element_type=F32) / l
        lse = m + jnp.log(l)
        for r in range(rb):
            lse_tile = jnp.zeros((blk, LANES), F32)
            for n in range(hw):
                b = r * hw + n
                o_refs[n][rows(base, r0 + r), :] = o[b]
                lse_tile = jnp.where(lane == n, lse[b], lse_tile)
            lse_ref[rows(base, r0 + r), :] = lse_tile

    for s in range(nsub):
        if dil == rb:
            group(0, s)
        else:
            def body(it, carry, s=s):
                group(it * rb, s)
                return carry
            lax.fori_loop(0, dil // rb, body, 0)

    for n in range(hw):
        o_ref[:, n * ATT_HEAD_DIM:(n + 1) * ATT_HEAD_DIM] = o_refs[n][...]


def _attn_prompt_group(za, batch, seq, gi, dil, *, nsub, hw, rb):
    step = ATT_SPAN * dil
    tc = nsub * step
    assert seq % tc == 0 and ATT_HEADS % hw == 0 and dil % rb == 0
    nh = ATT_HEADS // hw
    e = ATT_HEAD_DIM
    z3 = za.reshape(batch, seq, NA)
    qc, kc, vc = [(col + gi * ATT_GROUP_W) // e for col in (A_QA, A_KA, A_VA)]

    def cur(col):
        return [pl.BlockSpec((None, tc, e), lambda b, c, h, n=n: (b, c, col + h * hw + n)) for n in range(hw)]

    def prev(col):
        return [pl.BlockSpec((None, step, e), lambda b, c, h, n=n: (b, jnp.maximum(c * nsub - 1, 0), col + h * hw + n))
                for n in range(hw)]

    kern = functools.partial(_attn_prompt_kernel, dil=dil, nsub=nsub, hw=hw, rb=rb)
    o, lse = pl.pallas_call(
        kern,
        grid=(batch, seq // tc, nh),
        in_specs=cur(qc) + prev(kc) + cur(kc) + prev(vc) + cur(vc),
        out_specs=[pl.BlockSpec((None, tc, hw * e), lambda b, c, h: (b, c, h)),
                   pl.BlockSpec((None, tc, LANES), lambda b, c, h: (b, c, h))],
        out_shape=[jax.ShapeDtypeStruct((batch, seq, ATT_GROUP_W), F32),
                   jax.ShapeDtypeStruct((batch, seq, nh * LANES), F32)],
        scratch_shapes=[pltpu.VMEM((tc, e), F32)] * hw,
        compiler_params=_cparams(("parallel", "parallel", "parallel")),
        name=f"attn_prompt_g{gi}",
    )(*([z3] * (5 * hw)))
    return o.reshape(batch * seq, ATT_GROUP_W), lse.reshape(batch * seq, nh * LANES)


def _attn_sample_kernel(z_ref, k1, v1, k2, v2, k3, v3, o1, o2, o3, l1, l2, l3):
    bb = z_ref.shape[0]
    bufs = ((k1, v1, o1, l1), (k2, v2, o2, l2), (k3, v3, o3, l3))
    lane = lax.broadcasted_iota(jnp.int32, (1, LANES), 1)

    def heads(b, col):
        return jnp.concatenate([z_ref[b, :, col + n * ATT_HEAD_DIM:col + (n + 1) * ATT_HEAD_DIM]
                                for n in range(ATT_HEADS)], axis=0)

    def body(b, carry):
        for gi, (k_ref, v_ref, o_ref, l_ref) in enumerate(bufs):
            q = heads(b, A_QA + gi * ATT_GROUP_W)
            k_new = heads(b, A_KA + gi * ATT_GROUP_W)
            v_new = heads(b, A_VA + gi * ATT_GROUP_W)
            kb = k_ref[b]
            vb = v_ref[b]
            s = jnp.sum(kb * q[None], axis=-1, keepdims=True) * ATT_SCALE
            s_new = jnp.sum(k_new * q, axis=-1, keepdims=True) * ATT_SCALE
            m = jnp.maximum(jnp.max(s, axis=0), s_new)
            p = jnp.exp(s - m[None])
            p_new = jnp.exp(s_new - m)
            l = jnp.sum(p, axis=0) + p_new
            o = (jnp.sum(p * vb, axis=0) + p_new * v_new) / l
            lse = m + jnp.log(l)
            lse_row = jnp.zeros((1, LANES), F32)
            for n in range(ATT_HEADS):
                o_ref[b, :, n * ATT_HEAD_DIM:(n + 1) * ATT_HEAD_DIM] = o[n:n + 1, :]
                lse_row = jnp.where(lane == n, lse[n:n + 1, :], lse_row)
            l_ref[b] = lse_row
        return carry

    lax.fori_loop(0, bb, body, 0)


def _attn_sample(za_s, caches, *, bb):
    bd = za_s.shape[0]
    assert bd % bb == 0
    ins, in_specs = [za_s.reshape(bd, 1, NA)], [pl.BlockSpec((bb, 1, NA), lambda i: (i, 0, 0))]
    for (k_buf, v_buf), (win, dil) in zip(caches, DIL_PATTERNS):
        assert k_buf.shape[1:] == (ATT_SPAN * dil, ATT_HEADS, ATT_HEAD_DIM)
        for buf in (k_buf, v_buf):
            ins.append(buf.reshape(bd, ATT_SPAN, dil, ATT_HEADS, ATT_HEAD_DIM))
            in_specs.append(pl.BlockSpec((bb, ATT_SPAN, None, ATT_HEADS, ATT_HEAD_DIM),
                                         lambda i: (i, 0, 0, 0, 0)))
    outs = pl.pallas_call(
        _attn_sample_kernel,
        grid=(bd // bb,),
        in_specs=in_specs,
        out_specs=[pl.BlockSpec((bb, 1, ATT_GROUP_W), lambda i: (i, 0, 0))] * 3
                  + [pl.BlockSpec((bb, 1, LANES), lambda i: (i, 0, 0))] * 3,
        out_shape=[jax.ShapeDtypeStruct((bd, 1, ATT_GROUP_W), F32)] * 3
                  + [jax.ShapeDtypeStruct((bd, 1, LANES), F32)] * 3,
        compiler_params=_cparams(("parallel",)),
        name="attn_sample",
    )(*ins)
    return [o[:, 0, :] for o in outs[:3]], [l[:, 0, :] for l in outs[3:]]


def _mlstm_prompt_kernel(q_ref, k_ref, v_ref, om_ref, g_ref, bias_ref, h_ref, c_out, n_out, m_out,
                         c_s, n_s, m_s):
    hd = pl.program_id(1)
    ci = pl.program_id(2)
    L = q_ref.shape[0]

    @pl.when(ci == 0)
    def _():
        c_s[...] = jnp.zeros_like(c_s)
        n_s[...] = jnp.zeros_like(n_s)
        m_s[...] = jnp.zeros_like(m_s)

    gates = g_ref[...] + bias_ref[...]
    lane = lax.broadcasted_iota(jnp.int32, (L, LANES), 1)
    li_col = jnp.sum(jnp.where(lane == hd, gates, 0.0), axis=1, keepdims=True)
    lf_col = _log_sigmoid(jnp.sum(jnp.where(lane == hd + M_HEADS, gates, 0.0), axis=1, keepdims=True))
    gates_t = gates.T
    sub = lax.broadcasted_iota(jnp.int32, (LANES, L), 0)
    li_row = jnp.sum(jnp.where(sub == hd, gates_t, 0.0), axis=0, keepdims=True)
    lf_row = _log_sigmoid(jnp.sum(jnp.where(sub == hd + M_HEADS, gates_t, 0.0), axis=0, keepdims=True))

    ti = lax.broadcasted_iota(jnp.int32, (L, L), 0)
    si = lax.broadcasted_iota(jnp.int32, (L, L), 1)
    causal = si <= ti
    b_col = jnp.sum(jnp.where(causal, lf_row, 0.0), axis=1, keepdims=True)
    b_row = jnp.sum(jnp.where(ti <= si, lf_col, 0.0), axis=0, keepdims=True)
    b_end = jnp.sum(lf_row, axis=1, keepdims=True)

    m_prev = m_s[...]
    dmat = jnp.where(causal, b_col - b_row + li_row, NEG_INF)
    inter = b_col + m_prev
    mt = jnp.maximum(inter, jnp.max(dmat, axis=1, keepdims=True))

    qb = q_ref[...]
    q = qb.astype(F32)
    k = k_ref[...].astype(F32) * M_K_SCALE
    vb = v_ref[...]
    qk = lax.dot_general(qb, k.astype(BF16), (((1,), (1,)), ((), ())), preferred_element_type=F32)
    a = jnp.exp(dmat - mt) * qk
    w_inter = jnp.exp(inter - mt)
    num = (jnp.dot(a.astype(BF16), vb, preferred_element_type=F32)
           + w_inter * jnp.dot(qb, c_s[...].astype(BF16), preferred_element_type=F32))
    den = jnp.sum(a, axis=1, keepdims=True) + w_inter * jnp.sum(q * n_s[...], axis=1, keepdims=True)
    h = num / jnp.maximum(jnp.abs(den), jnp.exp(-mt))
    h_ref[...] = (jax.nn.sigmoid(om_ref[...].astype(F32)) * h).astype(h_ref.dtype)

    g_col = b_end - b_col + li_col
    g_row = b_end - b_row + li_row
    m_new = jnp.maximum(b_end + m_prev, jnp.max(g_row, axis=1, keepdims=True))
    decay = jnp.exp(b_end + m_prev - m_new)
    kw = jnp.exp(g_col - m_new) * k
    c_s[...] = decay * c_s[...] + jnp.dot(kw.T.astype(BF16), vb, preferred_element_type=F32)
    n_s[...] = decay * n_s[...] + jnp.sum(kw, axis=0, keepdims=True)
    m_s[...] = m_new

    @pl.when(ci == pl.num_programs(2) - 1)
    def _():
        c_out[...] = c_s[...]
        n_out[...] = n_s[...]
        m_out[...] = jnp.broadcast_to(m_s[...], m_out.shape)


def _mlstm_prompt(zif, zb, gate_bias, batch, seq, *, chunk):
    assert seq % chunk == 0
    nc = seq // chunk
    zif3 = zif.reshape(batch, seq, LANES)
    zb3 = zb.reshape(batch, seq, NB)
    qk_blk = lambda col: pl.BlockSpec((None, chunk, M_QK_DIM), lambda b, h, c: (b, c, col // M_QK_DIM + h))
    v_blk = lambda col: pl.BlockSpec((None, chunk, M_V_DIM), lambda b, h, c: (b, c, col // M_V_DIM + h))
    hm, c1, n1, m1 = pl.pallas_call(
        _mlstm_prompt_kernel,
        grid=(batch, M_HEADS, nc),
        in_specs=[qk_blk(B_QM), qk_blk(B_KM), v_blk(B_VM), v_blk(B_OM),
                  pl.BlockSpec((None, chunk, LANES), lambda b, h, c: (b, c, 0)),
                  pl.BlockSpec((1, LANES), lambda b, h, c: (0, 0))],
        out_specs=[pl.BlockSpec((None, chunk, M_V_DIM), lambda b, h, c: (b, c, h)),
                   pl.BlockSpec((None, None, M_QK_DIM, M_V_DIM), lambda b, h, c: (b, h, 0, 0)),
                   pl.BlockSpec((None, None, 1, M_QK_DIM), lambda b, h, c: (b, h, 0, 0)),
                   pl.BlockSpec((None, None, 1, LANES), lambda b, h, c: (b, h, 0, 0))],
        out_shape=[jax.ShapeDtypeStruct((batch, seq, M_V_WIDTH), BF16),
                   jax.ShapeDtypeStruct((batch, M_HEADS, M_QK_DIM, M_V_DIM), F32),
                   jax.ShapeDtypeStruct((batch, M_HEADS, 1, M_QK_DIM), F32),
                   jax.ShapeDtypeStruct((batch, M_HEADS, 1, LANES), F32)],
        scratch_shapes=[pltpu.VMEM((M_QK_DIM, M_V_DIM), F32), pltpu.VMEM((1, M_QK_DIM), F32),
                        pltpu.VMEM((1, 1), F32)],
        compiler_params=_cparams(("parallel", "parallel", "arbitrary")),
        name="mlstm_prompt",
    )(zb3, zb3, zb3, zb3, zif3, gate_bias)
    return hm.reshape(batch * seq, M_V_WIDTH), c1, n1[:, :, 0, :], m1[:, :, 0, 0]


def _row_to_col(row):
    n = row.shape[1]
    return jnp.broadcast_to(row, (LANES, n)).T[:, 0:1]


def _mlstm_sample_kernel(zif_ref, zb_ref, bias_ref, c_ref, n_ref, m_ref, h_ref, c_out, n_out, m_out):
    for b in range(zif_ref.shape[0]):
        _mlstm_sample_one(zif_ref.at[b], zb_ref.at[b], bias_ref, c_ref.at[b], n_ref.at[b], m_ref.at[b],
                          h_ref.at[b], c_out.at[b], n_out.at[b], m_out.at[b])


def _mlstm_sample_one(zif_ref, zb_ref, bias_ref, c_ref, n_ref, m_ref, h_ref, c_out, n_out, m_out):
    gates = zif_ref[...] + bias_ref[...]
    lane = lax.broadcasted_iota(jnp.int32, (1, LANES), 1)
    m_row = jnp.zeros((1, LANES), F32)
    for h in range(M_HEADS):
        q = zb_ref[:, B_QM + h * M_QK_DIM:B_QM + (h + 1) * M_QK_DIM].astype(F32)
        k = zb_ref[:, B_KM + h * M_QK_DIM:B_KM + (h + 1) * M_QK_DIM].astype(F32) * M_K_SCALE
        v = zb_ref[:, B_VM + h * M_V_DIM:B_VM + (h + 1) * M_V_DIM].astype(F32)
        om = zb_ref[:, B_OM + h * M_V_DIM:B_OM + (h + 1) * M_V_DIM].astype(F32)
        li = gates[:, h:h + 1]
        lf = _log_sigmoid(gates[:, M_HEADS + h:M_HEADS + h + 1])
        m0 = m_ref[:, h:h + 1]
        c0 = c_ref[h]
        n0 = n_ref[h:h + 1, :]
        inter = lf + m0
        mt = jnp.maximum(inter, li)
        a = jnp.exp(li - mt) * jnp.sum(q * k, axis=1, keepdims=True)
        w_inter = jnp.exp(inter - mt)
        q_c = jnp.sum(_row_to_col(q) * c0, axis=0, keepdims=True)
        num = a * v + w_inter * q_c
        den = a + w_inter * jnp.sum(q * n0, axis=1, keepdims=True)
        hv = num / jnp.maximum(jnp.abs(den), jnp.exp(-mt))
        h_ref[:, h * M_V_DIM:(h + 1) * M_V_DIM] = (jax.nn.sigmoid(om) * hv).astype(h_ref.dtype)
        m_new = jnp.maximum(inter, li)
        decay = jnp.exp(inter - m_new)
        ws = jnp.exp(li - m_new)
        c_out[h] = decay * c0 + _row_to_col(ws * k) * v
        n_out[h:h + 1, :] = decay * n0 + ws * k
        m_row = jnp.where(lane == h, m_new, m_row)
    m_out[...] = m_row


def _mlstm_sample(zif_s, zb_s, gate_bias, c0, n0, m0, *, bb):
    bd = zif_s.shape[0]
    assert bd % bb == 0
    hm, c1, n1, m1 = pl.pallas_call(
        _mlstm_sample_kernel,
        grid=(bd // bb,),
        in_specs=[pl.BlockSpec((bb, 1, LANES), lambda b: (b, 0, 0)),
                  pl.BlockSpec((bb, 1, NB), lambda b: (b, 0, 0)),
                  pl.BlockSpec((1, LANES), lambda b: (0, 0)),
                  pl.BlockSpec((bb, M_HEADS, M_QK_DIM, M_V_DIM), lambda b: (b, 0, 0, 0)),
                  pl.BlockSpec((bb, M_HEADS, M_QK_DIM), lambda b: (b, 0, 0)),
                  pl.BlockSpec((bb, 1, M_HEADS), lambda b: (b, 0, 0))],
        out_specs=[pl.BlockSpec((bb, 1, M_V_WIDTH), lambda b: (b, 0, 0)),
                   pl.BlockSpec((bb, M_HEADS, M_QK_DIM, M_V_DIM), lambda b: (b, 0, 0, 0)),
                   pl.BlockSpec((bb, M_HEADS, M_QK_DIM), lambda b: (b, 0, 0)),
                   pl.BlockSpec((bb, 1, LANES), lambda b: (b, 0, 0))],
        out_shape=[jax.ShapeDtypeStruct((bd, 1, M_V_WIDTH), BF16),
                   jax.ShapeDtypeStruct((bd, M_HEADS, M_QK_DIM, M_V_DIM), F32),
                   jax.ShapeDtypeStruct((bd, M_HEADS, M_QK_DIM), F32),
                   jax.ShapeDtypeStruct((bd, 1, LANES), F32)],
        compiler_params=_cparams(("parallel",)),
        name="mlstm_sample",
    )(zif_s.reshape(bd, 1, LANES), zb_s.reshape(bd, 1, NB), gate_bias, c0, n0, m0.reshape(bd, 1, M_HEADS))
    return hm.reshape(bd, M_V_WIDTH), c1, n1, m1[:, 0, :M_HEADS]


def _merge_kernel(o1, o2, o3, l1, l2, l3, hm_ref, ga_ref, gb_ref, x_ref, wpa_ref, wpm_ref, wo_ref, g2_ref,
                  x1_ref, h2_ref, *, hws):
    os_ = (o1, o2, o3)
    parts = []
    for n in range(ATT_HEADS):
        sl = slice(n * ATT_HEAD_DIM, (n + 1) * ATT_HEAD_DIM)
        ls = []
        for l_ref, hw in zip((l1, l2, l3), hws):
            ln = (n // hw) * LANES + n % hw
            ls.append(l_ref[:, ln:ln + 1])
        mx = jnp.maximum(jnp.maximum(ls[0], ls[1]), ls[2])
        es = [jnp.exp(l - mx) for l in ls]
        tot = es[0] + es[1] + es[2]
        parts.append((es[0] / tot) * o1[:, sl] + (es[1] / tot) * o2[:, sl] + (es[2] / tot) * o3[:, sl])
    att = jnp.concatenate(parts, axis=1).astype(BF16)
    pa = jnp.dot(att, wpa_ref[...], preferred_element_type=F32)
    pm = jnp.dot(hm_ref[...], wpm_ref[...], preferred_element_type=F32)
    merged = (jax.nn.sigmoid(ga_ref[...].astype(F32)) * pa + jax.nn.sigmoid(gb_ref[...].astype(F32)) * pm)
    x1 = x_ref[...] + jnp.dot(merged.astype(BF16), wo_ref[...], preferred_element_type=F32)
    x1_ref[...] = x1
    ms = jnp.mean(x1 * x1, axis=-1, keepdims=True)
    h2_ref[...] = (x1 * lax.rsqrt(ms + RMS_EPS) * g2_ref[...]).astype(BF16)


def _merge(o_list, lse_list, hm, zt, x, wpa, wpm, wo, g2, *, tm):
    m, d = x.shape
    assert m % tm == 0 and zt.shape[1] == NT
    row_blk = lambda w: pl.BlockSpec((tm, w), lambda i: (i, 0))
    const = lambda shape: pl.BlockSpec(shape, lambda i: (0, 0), pipeline_mode=pl.Buffered(1))
    hws = tuple(ATT_HEADS * LANES // l.shape[1] for l in lse_list)
    return pl.pallas_call(
        functools.partial(_merge_kernel, hws=hws),
        grid=(m // tm,),
        in_specs=[row_blk(ATT_GROUP_W)] * 3 + [row_blk(l.shape[1]) for l in lse_list]
                 + [row_blk(M_V_WIDTH),
                    pl.BlockSpec((tm, d), lambda i: (i, T_GA // d)),
                    pl.BlockSpec((tm, d), lambda i: (i, T_GB // d)),
                    row_blk(d),
                    const(wpa.shape), const(wpm.shape), const(wo.shape), const((1, d))],
        out_specs=[row_blk(d), row_blk(d)],
        out_shape=[jax.ShapeDtypeStruct((m, d), F32), jax.ShapeDtypeStruct((m, d), BF16)],
        compiler_params=_cparams(("parallel",)),
        name="merge",
    )(*o_list, *lse_list, hm, zt, zt, x, wpa, wpm, wo, g2)


def _ffn_body(h2_ref, wu_ref, wg_ref, cw_ref, cb_ref, wd_ref, x1_ref, y_ref, g_prev2, g_prev1):
    j = pl.program_id(1)
    h2 = h2_ref[...]
    u = jnp.dot(h2, wu_ref[...], preferred_element_type=F32)
    g = jnp.dot(h2, wg_ref[...], preferred_element_type=F32)
    gconv = cb_ref[...] + ((g_prev2(g) * cw_ref[0:1, :] + g_prev1(g) * cw_ref[1:2, :]) + g * cw_ref[2:3, :])
    act = (jax.nn.gelu(gconv) * u).astype(BF16)
    down = jnp.dot(act, wd_ref[...], preferred_element_type=F32)

    @pl.when(j == 0)
    def _():
        y_ref[...] = x1_ref[...] + down

    @pl.when(j > 0)
    def _():
        y_ref[...] += down

    return g


def _ffn_prompt_kernel(h2_ref, wu_ref, wg_ref, cw_ref, cb_ref, wd_ref, x1_ref, y_ref, tail_ref, prev_s,
                       u0_s, g0_s, u1_s, g1_s, *, tiles_per_seq, nf):
    i = pl.program_id(0)
    j = pl.program_id(1)
    tm = h2_ref.shape[0]
    slots = ((u0_s, g0_s), (u1_s, g1_s))

    def up(u_s, g_s):
        h2 = h2_ref[...]
        u_s[...] = jnp.dot(h2, wu_ref[...], preferred_element_type=F32)
        g_s[...] = jnp.dot(h2, wg_ref[...], preferred_element_type=F32)

    def down(u_s, g_s):
        jt = j - 1
        prev = jnp.where(i % tiles_per_seq == 0, 0.0, prev_s[jt])
        p2, p1 = prev[6:7, :], prev[7:8, :]
        g = g_s[...]
        r = lax.broadcasted_iota(jnp.int32, g.shape, 0)
        g_m1 = jnp.where(r == 0, p1, pltpu.roll(g, 1, axis=0))
        g_m2 = jnp.where(r == 0, p2, jnp.where(r == 1, p1, pltpu.roll(g, 2, axis=0)))
        gconv = cb_ref[...] + ((g_m2 * cw_ref[0:1, :] + g_m1 * cw_ref[1:2, :]) + g * cw_ref[2:3, :])
        act = (jax.nn.gelu(gconv) * u_s[...]).astype(BF16)
        y_ref[...] += jnp.dot(act, wd_ref[...], preferred_element_type=F32)
        prev_s[jt] = g[tm - 8:tm, :]
        tail_ref[...] = g[tm - 8:tm, :]

    @pl.when(j == 0)
    def _():
        y_ref[...] = x1_ref[...]
        up(*slots[0])

    for par in range(2):
        @pl.when((j > 0) & (j < nf) & (j % 2 == par))
        def _(par=par):
            up(*slots[par])
            down(*slots[1 - par])

    @pl.when(j == nf)
    def _():
        down(*slots[(nf - 1) % 2])


def _ffn_sample_kernel(h2_ref, wu_ref, wg_ref, cw_ref, cb_ref, wd_ref, x1_ref, b2_ref, b1_ref, y_ref, g_ref):
    g = _ffn_body(h2_ref, wu_ref, wg_ref, cw_ref, cb_ref, wd_ref, x1_ref, y_ref,
                  lambda g: b2_ref[...], lambda g: b1_ref[...])
    g_ref[...] = g


def _ffn_specs(tm, tf, d, nf):
    return [pl.BlockSpec((tm, d), lambda i, j: (i, 0)),
            pl.BlockSpec((None, d, tf), lambda i, j: (j, 0, 0)),
            pl.BlockSpec((None, d, tf), lambda i, j: (nf + j, 0, 0)),
            pl.BlockSpec((CONV_W, tf), lambda i, j: (0, j)),
            pl.BlockSpec((1, tf), lambda i, j: (0, j)),
            pl.BlockSpec((tf, d), lambda i, j: (j, 0)),
            pl.BlockSpec((tm, d), lambda i, j: (i, 0))]


def _ffn_prompt(h2, x1, w_in, conv_w, conv_b, w_down, seq, *, tm, tf):
    m, d = x1.shape
    assert m % tm == 0 and seq % tm == 0 and D_FF % tf == 0 and tm % 8 == 0
    nf = D_FF // tf
    kern = functools.partial(_ffn_prompt_kernel, tiles_per_seq=seq // tm, nf=nf)
    up_t = lambda j: jnp.minimum(j, nf - 1)
    dn_t = lambda j: jnp.maximum(j - 1, 0)
    y, tails = pl.pallas_call(
        kern,
        grid=(m // tm, nf + 1),
        in_specs=[pl.BlockSpec((tm, d), lambda i, j: (i, 0)),
                  pl.BlockSpec((None, d, tf), lambda i, j: (up_t(j), 0, 0)),
                  pl.BlockSpec((None, d, tf), lambda i, j: (nf + up_t(j), 0, 0)),
                  pl.BlockSpec((CONV_W, tf), lambda i, j: (0, dn_t(j))),
                  pl.BlockSpec((1, tf), lambda i, j: (0, dn_t(j))),
                  pl.BlockSpec((tf, d), lambda i, j: (dn_t(j), 0)),
                  pl.BlockSpec((tm, d), lambda i, j: (i, 0))],
        out_specs=[pl.BlockSpec((tm, d), lambda i, j: (i, 0)),
                   pl.BlockSpec((None, 8, tf), lambda i, j: (i, 0, dn_t(j)))],
        out_shape=[jax.ShapeDtypeStruct((m, d), F32), jax.ShapeDtypeStruct((m // tm, 8, D_FF), F32)],
        scratch_shapes=[pltpu.VMEM((nf, 8, tf), F32)] + [pltpu.VMEM((tm, tf), F32)] * 4,
        compiler_params=_cparams(("arbitrary", "arbitrary")),
        name="ffn_prompt",
    )(h2, w_in, w_in, conv_w, conv_b, w_down, x1)
    return y, tails


def _ffn_sample(h2, x1, w_in, conv_w, conv_b, w_down, conv_buf, *, tf):
    m, d = x1.shape
    nf = D_FF // tf
    buf2d = conv_buf.reshape(m, (CONV_W - 1) * D_FF)
    return pl.pallas_call(
        _ffn_sample_kernel,
        grid=(1, nf),
        in_specs=_ffn_specs(m, tf, d, nf) + [pl.BlockSpec((m, tf), lambda i, j: (0, j)),
                                             pl.BlockSpec((m, tf), lambda i, j: (0, nf + j))],
        out_specs=[pl.BlockSpec((m, d), lambda i, j: (i, 0)), pl.BlockSpec((m, tf), lambda i, j: (0, j))],
        out_shape=[jax.ShapeDtypeStruct((m, d), F32), jax.ShapeDtypeStruct((m, D_FF), F32)],
        compiler_params=_cparams(("arbitrary", "arbitrary")),
        name="ffn_sample",
    )(h2, w_in, w_in, conv_w, conv_b, w_down, x1, buf2d, buf2d)


IN_PROJ_TN = 512
FFN_TF = 512


def _qk_gain_row(q_norm, k_norm):
    reps = N_GROUPS * ATT_HEADS
    return jnp.concatenate([jnp.tile(q_norm, reps), jnp.tile(k_norm, reps), jnp.zeros((NA - A_VA,), F32)])[None, :]


ATTN_PROMPT_TILING = ((4, 4, 1), (1, 4, 1), (1, 2, 2))


def _layer(x_prompt, x_sample, caches, norm_mix, w_in, q_norm, k_norm, b_igate, b_fgate, w_proj_att,
           w_proj_mlstm, w_out, norm_ffn, w_ffn_in, conv_w, conv_b, w_ffn_down):
    batch, seq, d = x_prompt.shape
    bd = x_sample.shape[0]
    assert x_sample.shape[1] == 1 and d == D_MODEL
    (ck1, cv1, ck2, cv2, ck3, cv3, st_c, st_n, st_m, st_conv) = caches

    wt_in = jnp.swapaxes(w_in, 0, 1)
    qk_gain = _qk_gain_row(q_norm, k_norm)
    g1 = norm_mix[None, :]
    g2 = norm_ffn[None, :]
    gate_bias = jnp.concatenate([b_igate, b_fgate, jnp.zeros((LANES - 2 * M_HEADS,), F32)])[None, :]
    wpa = w_proj_att.astype(BF16)
    wpm = w_proj_mlstm.astype(BF16)
    wo = w_out.astype(BF16)
    w_ff = w_ffn_in.astype(BF16).reshape(d, 2 * D_FF // FFN_TF, FFN_TF).swapaxes(0, 1)
    w_dn = w_ffn_down.astype(BF16)
    cb = conv_b[None, :]

    xp = x_prompt.reshape(batch * seq, d)
    xs = x_sample.reshape(bd, d)

    h_p = _rmsnorm(xp, g1, tm=512)
    za_p, zb_p, zt_p, zif_p = _in_proj(h_p, wt_in, qk_gain, tm=2048, tn=IN_PROJ_TN, z_dtype=BF16)
    o_p, lse_p = zip(*[_attn_prompt_group(za_p, batch, seq, gi, dil, nsub=nsub, hw=hw, rb=rb)
                       for gi, ((_, dil), (nsub, hw, rb)) in enumerate(zip(DIL_PATTERNS, ATTN_PROMPT_TILING))])
    hm_p, p_c, p_n, p_m = _mlstm_prompt(zif_p, zb_p, gate_bias, batch, seq, chunk=256)
    x1_p, h2_p = _merge(o_p, lse_p, hm_p, zt_p, xp, wpa, wpm, wo, g2, tm=256)
    y_p, tails = _ffn_prompt(h2_p, x1_p, w_ff, conv_w, cb, w_dn, seq, tm=512, tf=FFN_TF)

    za_p3 = za_p.reshape(batch, seq, NA)
    p_kv = []
    for gi, (win, _) in enumerate(DIL_PATTERNS):
        keep = min(win, seq)
        for col in (A_KA, A_VA):
            lo = col + gi * ATT_GROUP_W
            p_kv.append(za_p3[:, seq - keep:, lo:lo + ATT_GROUP_W].reshape(batch, keep, ATT_HEADS, ATT_HEAD_DIM))
    tiles_per_seq = seq // 512
    p_conv = tails.reshape(batch, tiles_per_seq, 8, D_FF)[:, -1, 8 - (CONV_W - 1):, :]

    h_s = _rmsnorm(xs, g1, tm=bd)
    za_s, zb_s, zt_s, zif_s = _in_proj(h_s, wt_in, qk_gain, tm=bd, tn=IN_PROJ_TN, z_dtype=F32)
    o_s, lse_s = _attn_sample(za_s, [(ck1, cv1), (ck2, cv2), (ck3, cv3)], bb=4)
    hm_s, s_c, s_n, s_m = _mlstm_sample(zif_s, zb_s, gate_bias, st_c, st_n, st_m, bb=2)
    x1_s, h2_s = _merge(o_s, lse_s, hm_s, zt_s, xs, wpa, wpm, wo, g2, tm=bd)
    y_s, g_s = _ffn_sample(h2_s, x1_s, w_ff, conv_w, cb, w_dn, st_conv, tf=FFN_TF)

    s_kv = []
    for gi in range(N_GROUPS):
        for col in (A_KA, A_VA):
            lo = col + gi * ATT_GROUP_W
            s_kv.append(za_s[:, lo:lo + ATT_GROUP_W].reshape(bd, 1, ATT_HEADS, ATT_HEAD_DIM))
    s_conv = jnp.stack([st_conv[:, 1, :], g_s], axis=1)

    p_state = p_kv + [p_c, p_n, p_m, p_conv]
    s_state = s_kv + [s_c, s_n, s_m, s_conv]
    return y_p.reshape(batch, seq, d), y_s.reshape(bd, 1, d), p_state, s_state


def kernel(x_prompt, x_sample, cache_k_w128, cache_v_w128, cache_k_w512, cache_v_w512, cache_k_w2048,
           cache_v_w2048, state_mlstm_C, state_mlstm_n, state_mlstm_m, state_ffn_conv, norm_mix, w_in, q_norm,
           k_norm, b_igate, b_fgate, w_proj_att, w_proj_mlstm, w_out, norm_ffn, w_ffn_in, conv_w, conv_b,
           w_ffn_down):
    assert norm_mix.shape[0] == 1
    caches = [c[0] for c in (cache_k_w128, cache_v_w128, cache_k_w512, cache_v_w512, cache_k_w2048,
                             cache_v_w2048, state_mlstm_C, state_mlstm_n, state_mlstm_m, state_ffn_conv)]
    weights = [w[0] for w in (norm_mix, w_in, q_norm, k_norm, b_igate, b_fgate, w_proj_att, w_proj_mlstm,
                              w_out, norm_ffn, w_ffn_in, conv_w, conv_b, w_ffn_down)]
    y_p, y_s, p_state, s_state = _layer(x_prompt, x_sample, caches, *weights)
    return (y_p, y_s, *[a[None] for a in p_state], *[a[None] for a in s_state])
```

```python
import functools

import jax
import jax.numpy as jnp
from jax import lax
from jax.experimental import pallas as pl
from jax.experimental.pallas import tpu as pltpu

F32 = jnp.float32
BF16 = jnp.bfloat16

RMS_EPS = 1e-6
NEG_INF = -1e30
LANES = 128
VMEM_LIMIT = 56 * 1024 * 1024

D_MODEL = 2048
DIL_PATTERNS = ((128, 1), (512, 4), (2048, 16))
N_GROUPS = 3
ATT_HEADS = 4
ATT_HEAD_DIM = 128
ATT_SPAN = 128
ATT_SCALE = ATT_HEAD_DIM ** -0.5
ATT_GROUP_W = ATT_HEADS * ATT_HEAD_DIM
ATT_WIDTH = N_GROUPS * ATT_GROUP_W
M_HEADS = 4
M_QK_DIM = D_MODEL // (2 * M_HEADS)
M_V_DIM = D_MODEL // M_HEADS
M_QK_WIDTH = M_HEADS * M_QK_DIM
M_V_WIDTH = M_HEADS * M_V_DIM
M_K_SCALE = M_QK_DIM ** -0.5
D_FF = ((8 * D_MODEL // 3 + 255) // 256) * 256
CONV_W = 3
SPLIT_SIZES = (ATT_WIDTH, ATT_WIDTH, ATT_WIDTH, M_QK_WIDTH, M_QK_WIDTH, M_V_WIDTH, M_V_WIDTH,
               M_HEADS, M_HEADS, D_MODEL, D_MODEL)

IN_COLS = sum(SPLIT_SIZES)
A_QA = 0
A_KA = A_QA + ATT_WIDTH
A_VA = A_KA + ATT_WIDTH
NA = A_VA + ATT_WIDTH
B_QM = 0
B_KM = B_QM + M_QK_WIDTH
B_VM = B_KM + M_QK_WIDTH
B_OM = B_VM + M_V_WIDTH
NB = B_OM + M_V_WIDTH
N_IF = 2 * M_HEADS
T_GA = 0
T_GB = T_GA + D_MODEL
NT = T_GB + D_MODEL
assert NA + NB + N_IF + NT == IN_COLS


def _cparams(sem):
    return pltpu.CompilerParams(dimension_semantics=sem, vmem_limit_bytes=VMEM_LIMIT)


def _log_sigmoid(x):
    return jnp.minimum(x, 0.0) - jnp.log(1.0 + jnp.exp(-jnp.abs(x)))


def _rmsnorm_kernel(x_ref, xs_ref, g_ref, h_ref, *, nsub):
    j = pl.program_id(1)
    ts = x_ref.shape[0]

    def norm(x):
        ms = jnp.mean(x * x, axis=-1, keepdims=True)
        return (x * lax.rsqrt(ms + RMS_EPS) * g_ref[...]).astype(h_ref.dtype)

    @pl.when(j < nsub)
    def _():
        h_ref[pl.ds(pl.multiple_of(j * ts, ts), ts), :] = norm(x_ref[...])

    @pl.when(j == nsub)
    def _():
        h_ref[nsub * ts:, :] = norm(xs_ref[...])


def _rmsnorm(x, xs, gain, *, tm, ts):
    m, d = x.shape
    ms = xs.shape[0]
    assert m % tm == 0 and tm % ts == 0
    nsub = tm // ts
    last = m // ts - 1
    return pl.pallas_call(
        functools.partial(_rmsnorm_kernel, nsub=nsub),
        grid=(m // tm, nsub + 1),
        in_specs=[pl.BlockSpec((ts, d), lambda i, j: (jnp.minimum(i * nsub + j, last), 0)),
                  pl.BlockSpec((ms, d), lambda i, j: (0, 0)),
                  pl.BlockSpec((1, d), lambda i, j: (0, 0))],
        out_specs=pl.BlockSpec((None, tm + ms, d), lambda i, j: (i, 0, 0)),
        out_shape=jax.ShapeDtypeStruct((m // tm, tm + ms, d), BF16),
        compiler_params=_cparams(("parallel", "arbitrary")),
        name="rmsnorm",
    )(x, xs, gain)


def _in_proj_kernel(h_ref, w_ref, wn_ref, wif_ref, qkg_ref, za_ref, zb_ref, zt_ref, zif_ref,
                    sa_ref, sb_ref, st_ref, sif_ref, *, n_norm, na, nb, nt):
    j = pl.program_id(1)
    tp = za_ref.shape[0]

    def z_of(w):
        return lax.dot_general(h_ref[...], w.astype(BF16), (((1,), (1,)), ((), ())), preferred_element_type=F32)

    def put(z, p_ref, s_ref):
        p_ref[...] = z[:tp].astype(p_ref.dtype)
        s_ref[...] = z[tp:].astype(s_ref.dtype)

    @pl.when(j < n_norm)
    def _():
        z = z_of(w_ref[...])
        for c in range(za_ref.shape[1] // LANES):
            sl = slice(c * LANES, (c + 1) * LANES)
            zc = z[:, sl]
            ms = jnp.mean(zc * zc, axis=-1, keepdims=True)
            zn = zc * lax.rsqrt(ms + RMS_EPS) * qkg_ref[:, sl]
            za_ref[:, sl] = zn[:tp]
            sa_ref[:, sl] = zn[tp:]

    @pl.when((j >= n_norm) & (j < na))
    def _():
        put(z_of(w_ref[...]), za_ref, sa_ref)

    @pl.when((j >= na) & (j < na + nb))
    def _():
        put(z_of(w_ref[...]), zb_ref, sb_ref)

    @pl.when((j >= na + nb) & (j < na + nb + nt))
    def _():
        put(z_of(jnp.concatenate([w_ref[N_IF:, :], wn_ref[...]], axis=0)), zt_ref, st_ref)

    @pl.when(j == na + nb + nt)
    def _():
        w = jnp.concatenate([wif_ref[...], jnp.zeros((LANES - N_IF, wif_ref.shape[1]), F32)], axis=0)
        put(z_of(w), zif_ref, sif_ref)


def _in_proj(h, wt, qk_gain, *, tn, ms):
    ni, th, d = h.shape
    tm = th - ms
    m = ni * tm
    assert wt.shape == (IN_COLS, d) and N_IF == 8
    assert NA % tn == 0 and NB % tn == 0 and NT % tn == 0 and A_VA % tn == 0 and tn % N_IF == 0
    na, nb, nt = NA // tn, NB // tn, NT // tn
    nw = na + nb + nt
    g_if = (NA + NB) // N_IF
    g_tn = tn // N_IF
    kern = functools.partial(_in_proj_kernel, n_norm=A_VA // tn, na=na, nb=nb, nt=nt)
    col_a = lambda j: jnp.minimum(j, na - 1)
    col_b = lambda j: jnp.clip(j - na, 0, nb - 1)
    col_t = lambda j: jnp.clip(j - na - nb, 0, nt - 1)
    out_specs, out_shape = [], []
    for rows, dts in ((tm, (F32, BF16, BF16, F32)), (ms, (F32,) * 4)):
        out_specs += [pl.BlockSpec((None, rows, tn), lambda i, j: (i, 0, col_a(j))),
                      pl.BlockSpec((None, rows, tn), lambda i, j: (i, 0, col_b(j))),
                      pl.BlockSpec((None, rows, tn), lambda i, j: (i, 0, col_t(j))),
                      pl.BlockSpec((None, rows, LANES), lambda i, j: (i, 0, 0))]
        out_shape += [jax.ShapeDtypeStruct((ni, rows, w), dt) for w, dt in zip((NA, NB, NT, LANES), dts)]
    outs = pl.pallas_call(
        kern,
        grid=(ni, nw + 1),
        in_specs=[
            pl.BlockSpec((None, th, d), lambda i, j: (i, 0, 0)),
            pl.BlockSpec((tn, d), lambda i, j: (jnp.minimum(j, nw - 1), 0)),
            pl.BlockSpec((N_IF, d), lambda i, j: (g_if + g_tn * (jnp.clip(j, na + nb, nw - 1) - (na + nb) + 1), 0)),
            pl.BlockSpec((N_IF, d), lambda i, j: (g_if, 0)),
            pl.BlockSpec((1, tn), lambda i, j: (0, col_a(j))),
        ],
        out_specs=out_specs,
        out_shape=out_shape,
        compiler_params=_cparams(("parallel", "arbitrary")),
        name="in_proj",
    )(h, wt, wt, wt, qk_gain)
    return [o.reshape(m, o.shape[2]) for o in outs[:4]], [o[0] for o in outs[4:]]


def _attn_prompt_kernel(*refs, dil, nsub, hw, rb):
    q_refs, kp_refs, kc_refs, vp_refs, vc_refs = [refs[t * hw:(t + 1) * hw] for t in range(5)]
    o_ref, lse_ref = refs[5 * hw:5 * hw + 2]
    o_refs = refs[5 * hw + 2:]
    c = pl.program_id(1)
    blk = ATT_SPAN
    step = blk * dil
    qi = lax.broadcasted_iota(jnp.int32, (blk, 2 * blk), 0)
    kj = lax.broadcasted_iota(jnp.int32, (blk, 2 * blk), 1)
    band = (kj >= qi) & (kj <= qi + blk)
    first_bias = jnp.where((kj < blk) & (c == 0), NEG_INF, 0.0)
    lane = lax.broadcasted_iota(jnp.int32, (blk, LANES), 1)

    def rows(base, r):
        return pl.ds(base + r, blk) if dil == 1 else pl.ds(base + r, blk, stride=dil)

    def group(r0, s):
        base = s * step
        ids = [(r0 + r, n) for r in range(rb) for n in range(hw)]

        def stacked(cur_refs, prev_refs):
            parts = []
            for r, n in ids:
                cur = cur_refs[n][rows(base, r), :]
                if prev_refs is None:
                    parts.append(cur)
                else:
                    prv = prev_refs[n][rows(0, r), :] if s == 0 else cur_refs[n][rows(base - step, r), :]
                    parts.append(jnp.concatenate([prv, cur], axis=0))
            return jnp.stack(parts).astype(BF16)

        q = stacked(q_refs, None)
        k = stacked(kc_refs, kp_refs)
        v = stacked(vc_refs, vp_refs)
        sc = jnp.einsum('bqe,bke->bqk', q, k, preferred_element_type=F32) * ATT_SCALE
        if s == 0:
            sc = sc + first_bias
        sc = jnp.where(band, sc, NEG_INF)
        m = jnp.max(sc, axis=-1, keepdims=True)
        p = jnp.exp(sc - m)
        l = jnp.sum(p, axis=-1, keepdims=True)
        o = jnp.einsum('bqk,bke->bqe', p.astype(BF16), v, preferred_element_type=F32) / l
        lse = m + jnp.log(l)
        for r in range(rb):
            lse_tile = jnp.zeros((blk, LANES), F32)
            for n in range(hw):
                b = r * hw + n
                o_refs[n][rows(base, r0 + r), :] = o[b]
                lse_tile = jnp.where(lane == n, lse[b], lse_tile)
            lse_ref[rows(base, r0 + r), :] = lse_tile

    for s in range(nsub):
        if dil == rb:
            group(0, s)
        else:
            def body(it, carry, s=s):
                group(it * rb, s)
                return carry
            lax.fori_loop(0, dil // rb, body, 0)

    for n in range(hw):
        o_ref[:, n * ATT_HEAD_DIM:(n + 1) * ATT_HEAD_DIM] = o_refs[n][...]


def _attn_prompt_group(za, batch, seq, gi, dil, *, nsub, hw, rb):
    step = ATT_SPAN * dil
    tc = nsub * step
    assert seq % tc == 0 and ATT_HEADS % hw == 0 and dil % rb == 0
    nh = ATT_HEADS // hw
    e = ATT_HEAD_DIM
    z3 = za.reshape(batch, seq, NA)
    qc, kc, vc = [(col + gi * ATT_GROUP_W) // e for col in (A_QA, A_KA, A_VA)]

    def cur(col):
        return [pl.BlockSpec((None, tc, e), lambda b, c, h, n=n: (b, c, col + h * hw + n)) for n in range(hw)]

    def prev(col):
        return [pl.BlockSpec((None, step, e), lambda b, c, h, n=n: (b, jnp.maximum(c * nsub - 1, 0), col + h * hw + n))
                for n in range(hw)]

    kern = functools.partial(_attn_prompt_kernel, dil=dil, nsub=nsub, hw=hw, rb=rb)
    o, lse = pl.pallas_call(
        kern,
        grid=(batch, seq // tc, nh),
        in_specs=cur(qc) + prev(kc) + cur(kc) + prev(vc) + cur(vc),
        out_specs=[pl.BlockSpec((None, tc, hw * e), lambda b, c, h: (b, c, h)),
                   pl.BlockSpec((None, tc, LANES), lambda b, c, h: (b, c, h))],
        out_shape=[jax.ShapeDtypeStruct((batch, seq, ATT_GROUP_W), F32),
                   jax.ShapeDtypeStruct((batch, seq, nh * LANES), F32)],
        scratch_shapes=[pltpu.VMEM((tc, e), F32)] * hw,
        compiler_params=_cparams(("parallel", "parallel", "parallel")),
        name=f"attn_prompt_g{gi}",
    )(*([z3] * (5 * hw)))
    return o.reshape(batch * seq, ATT_GROUP_W), lse.reshape(batch * seq, nh * LANES)


def _attn_sample_kernel(z_ref, k1, v1, k2, v2, k3, v3, o1, o2, o3, l1, l2, l3):
    bb = z_ref.shape[0]
    bufs = ((k1, v1, o1, l1), (k2, v2, o2, l2), (k3, v3, o3, l3))
    lane = lax.broadcasted_iota(jnp.int32, (1, LANES), 1)

    def heads(b, col):
        return jnp.concatenate([z_ref[b, :, col + n * ATT_HEAD_DIM:col + (n + 1) * ATT_HEAD_DIM]
                                for n in range(ATT_HEADS)], axis=0)

    def body(b, carry):
        for gi, (k_ref, v_ref, o_ref, l_ref) in enumerate(bufs):
            q = heads(b, A_QA + gi * ATT_GROUP_W)
            k_new = heads(b, A_KA + gi * ATT_GROUP_W)
            v_new = heads(b, A_VA + gi * ATT_GROUP_W)
            kb = k_ref[b]
            vb = v_ref[b]
            s = jnp.sum(kb * q[None], axis=-1, keepdims=True) * ATT_SCALE
            s_new = jnp.sum(k_new * q, axis=-1, keepdims=True) * ATT_SCALE
            m = jnp.maximum(jnp.max(s, axis=0), s_new)
            p = jnp.exp(s - m[None])
            p_new = jnp.exp(s_new - m)
            l = jnp.sum(p, axis=0) + p_new
            o = (jnp.sum(p * vb, axis=0) + p_new * v_new) / l
            lse = m + jnp.log(l)
            lse_row = jnp.zeros((1, LANES), F32)
            for n in range(ATT_HEADS):
                o_ref[b, :, n * ATT_HEAD_DIM:(n + 1) * ATT_HEAD_DIM] = o[n:n + 1, :]
                lse_row = jnp.where(lane == n, lse[n:n + 1, :], lse_row)
            l_ref[b] = lse_row
        return carry

    lax.fori_loop(0, bb, body, 0)


def _attn_sample(za_s, caches, *, bb):
    bd = za_s.shape[0]
    assert bd % bb == 0
    ins, in_specs = [za_s.reshape(bd, 1, NA)], [pl.BlockSpec((bb, 1, NA), lambda i: (i, 0, 0))]
    for (k_buf, v_buf), (win, dil) in zip(caches, DIL_PATTERNS):
        assert k_buf.shape[1:] == (ATT_SPAN * dil, ATT_HEADS, ATT_HEAD_DIM)
        for buf in (k_buf, v_buf):
            ins.append(buf.reshape(bd, ATT_SPAN, dil, ATT_HEADS, ATT_HEAD_DIM))
            in_specs.append(pl.BlockSpec((bb, ATT_SPAN, None, ATT_HEADS, ATT_HEAD_DIM),
                                         lambda i: (i, 0, 0, 0, 0)))
    outs = pl.pallas_call(
        _attn_sample_kernel,
        grid=(bd // bb,),
        in_specs=in_specs,
        out_specs=[pl.BlockSpec((bb, 1, ATT_GROUP_W), lambda i: (i, 0, 0))] * 3
                  + [pl.BlockSpec((bb, 1, LANES), lambda i: (i, 0, 0))] * 3,
        out_shape=[jax.ShapeDtypeStruct((bd, 1, ATT_GROUP_W), F32)] * 3
                  + [jax.ShapeDtypeStruct((bd, 1, LANES), F32)] * 3,
        compiler_params=_cparams(("parallel",)),
        name="attn_sample",
    )(*ins)
    return [o[:, 0, :] for o in outs[:3]], [l[:, 0, :] for l in outs[3:]]


def _mlstm_prompt_kernel(q_ref, k_ref, v_ref, om_ref, g_ref, bias_ref, h_ref, c_out, n_out, m_out,
                         c_s, n_s, m_s):
    hd = pl.program_id(1)
    ci = pl.program_id(2)
    L = q_ref.shape[0]

    @pl.when(ci == 0)
    def _():
        c_s[...] = jnp.zeros_like(c_s)
        n_s[...] = jnp.zeros_like(n_s)
        m_s[...] = jnp.zeros_like(m_s)

    gates = g_ref[...] + bias_ref[...]
    lane = lax.broadcasted_iota(jnp.int32, (L, LANES), 1)
    li_col = jnp.sum(jnp.where(lane == hd, gates, 0.0), axis=1, keepdims=True)
    lf_col = _log_sigmoid(jnp.sum(jnp.where(lane == hd + M_HEADS, gates, 0.0), axis=1, keepdims=True))
    gates_t = gates.T
    sub = lax.broadcasted_iota(jnp.int32, (LANES, L), 0)
    li_row = jnp.sum(jnp.where(sub == hd, gates_t, 0.0), axis=0, keepdims=True)
    lf_row = _log_sigmoid(jnp.sum(jnp.where(sub == hd + M_HEADS, gates_t, 0.0), axis=0, keepdims=True))

    ti = lax.broadcasted_iota(jnp.int32, (L, L), 0)
    si = lax.broadcasted_iota(jnp.int32, (L, L), 1)
    causal = si <= ti
    b_col = jnp.sum(jnp.where(causal, lf_row, 0.0), axis=1, keepdims=True)
    b_row = jnp.sum(jnp.where(ti <= si, lf_col, 0.0), axis=0, keepdims=True)
    b_end = jnp.sum(lf_row, axis=1, keepdims=True)

    m_prev = m_s[...]
    dmat = jnp.where(causal, b_col - b_row + li_row, NEG_INF)
    inter = b_col + m_prev
    mt = jnp.maximum(inter, jnp.max(dmat, axis=1, keepdims=True))

    qb = q_ref[...]
    q = qb.astype(F32)
    k = k_ref[...].astype(F32) * M_K_SCALE
    vb = v_ref[...]
    qk = lax.dot_general(qb, k.astype(BF16), (((1,), (1,)), ((), ())), preferred_element_type=F32)
    a = jnp.exp(dmat - mt) * qk
    w_inter = jnp.exp(inter - mt)
    num = (jnp.dot(a.astype(BF16), vb, preferred_element_type=F32)
           + w_inter * jnp.dot(qb, c_s[...].astype(BF16), preferred_element_type=F32))
    den = jnp.sum(a, axis=1, keepdims=True) + w_inter * jnp.sum(q * n_s[...], axis=1, keepdims=True)
    h = num / jnp.maximum(jnp.abs(den), jnp.exp(-mt))
    h_ref[...] = (jax.nn.sigmoid(om_ref[...].astype(F32)) * h).astype(h_ref.dtype)

    g_col = b_end - b_col + li_col
    g_row = b_end - b_row + li_row
    m_new = jnp.maximum(b_end + m_prev, jnp.max(g_row, axis=1, keepdims=True))
    decay = jnp.exp(b_end + m_prev - m_new)
    kw = jnp.exp(g_col - m_new) * k
    c_s[...] = decay * c_s[...] + jnp.dot(kw.T.astype(BF16), vb, preferred_element_type=F32)
    n_s[...] = decay * n_s[...] + jnp.sum(kw, axis=0, keepdims=True)
    m_s[...] = m_new

    @pl.when(ci == pl.num_programs(2) - 1)
    def _():
        c_out[...] = c_s[...]
        n_out[...] = n_s[...]
        m_out[...] = jnp.broadcast_to(m_s[...], m_out.shape)


def _mlstm_prompt(zif, zb, gate_bias, batch, seq, *, chunk):
    assert seq % chunk == 0
    nc = seq // chunk
    zif3 = zif.reshape(batch, seq, LANES)
    zb3 = zb.reshape(batch, seq, NB)
    qk_blk = lambda col: pl.BlockSpec((None, chunk, M_QK_DIM), lambda b, h, c: (b, c, col // M_QK_DIM + h))
    v_blk = lambda col: pl.BlockSpec((None, chunk, M_V_DIM), lambda b, h, c: (b, c, col // M_V_DIM + h))
    hm, c1, n1, m1 = pl.pallas_call(
        _mlstm_prompt_kernel,
        grid=(batch, M_HEADS, nc),
        in_specs=[qk_blk(B_QM), qk_blk(B_KM), v_blk(B_VM), v_blk(B_OM),
                  pl.BlockSpec((None, chunk, LANES), lambda b, h, c: (b, c, 0)),
                  pl.BlockSpec((1, LANES), lambda b, h, c: (0, 0))],
        out_specs=[pl.BlockSpec((None, chunk, M_V_DIM), lambda b, h, c: (b, c, h)),
                   pl.BlockSpec((None, None, M_QK_DIM, M_V_DIM), lambda b, h, c: (b, h, 0, 0)),
                   pl.BlockSpec((None, None, 1, M_QK_DIM), lambda b, h, c: (b, h, 0, 0)),
                   pl.BlockSpec((None, None, 1, LANES), lambda b, h, c: (b, h, 0, 0))],
        out_shape=[jax.ShapeDtypeStruct((batch, seq, M_V_WIDTH), BF16),
                   jax.ShapeDtypeStruct((batch, M_HEADS, M_QK_DIM, M_V_DIM), F32),
                   jax.ShapeDtypeStruct((batch, M_HEADS, 1, M_QK_DIM), F32),
                   jax.ShapeDtypeStruct((batch, M_HEADS, 1, LANES), F32)],
        scratch_shapes=[pltpu.VMEM((M_QK_DIM, M_V_DIM), F32), pltpu.VMEM((1, M_QK_DIM), F32),
                        pltpu.VMEM((1, 1), F32)],
        compiler_params=_cparams(("parallel", "parallel", "arbitrary")),
        name="mlstm_prompt",
    )(zb3, zb3, zb3, zb3, zif3, gate_bias)
    return hm.reshape(batch * seq, M_V_WIDTH), c1, n1[:, :, 0, :], m1[:, :, 0, 0]


def _row_to_col(row):
    n = row.shape[1]
    return jnp.broadcast_to(row, (LANES, n)).T[:, 0:1]


def _mlstm_sample_kernel(zif_ref, zb_ref, bias_ref, c_ref, n_ref, m_ref, h_ref, c_out, n_out, m_out):
    for b in range(zif_ref.shape[0]):
        _mlstm_sample_one(zif_ref.at[b], zb_ref.at[b], bias_ref, c_ref.at[b], n_ref.at[b], m_ref.at[b],
                          h_ref.at[b], c_out.at[b], n_out.at[b], m_out.at[b])


def _mlstm_sample_one(zif_ref, zb_ref, bias_ref, c_ref, n_ref, m_ref, h_ref, c_out, n_out, m_out):
    gates = zif_ref[...] + bias_ref[...]
    lane = lax.broadcasted_iota(jnp.int32, (1, LANES), 1)
    m_row = jnp.zeros((1, LANES), F32)
    for h in range(M_HEADS):
        q = zb_ref[:, B_QM + h * M_QK_DIM:B_QM + (h + 1) * M_QK_DIM].astype(F32)
        k = zb_ref[:, B_KM + h * M_QK_DIM:B_KM + (h + 1) * M_QK_DIM].astype(F32) * M_K_SCALE
        v = zb_ref[:, B_VM + h * M_V_DIM:B_VM + (h + 1) * M_V_DIM].astype(F32)
        om = zb_ref[:, B_OM + h * M_V_DIM:B_OM + (h + 1) * M_V_DIM].astype(F32)
        li = gates[:, h:h + 1]
        lf = _log_sigmoid(gates[:, M_HEADS + h:M_HEADS + h + 1])
        m0 = m_ref[:, h:h + 1]
        c0 = c_ref[h]
        n0 = n_ref[h:h + 1, :]
        inter = lf + m0
        mt = jnp.maximum(inter, li)
        a = jnp.exp(li - mt) * jnp.sum(q * k, axis=1, keepdims=True)
        w_inter = jnp.exp(inter - mt)
        q_c = jnp.sum(_row_to_col(q) * c0, axis=0, keepdims=True)
        num = a * v + w_inter * q_c
        den = a + w_inter * jnp.sum(q * n0, axis=1, keepdims=True)
        hv = num / jnp.maximum(jnp.abs(den), jnp.exp(-mt))
        h_ref[:, h * M_V_DIM:(h + 1) * M_V_DIM] = (jax.nn.sigmoid(om) * hv).astype(h_ref.dtype)
        m_new = jnp.maximum(inter, li)
        decay = jnp.exp(inter - m_new)
        ws = jnp.exp(li - m_new)
        c_out[h] = decay * c0 + _row_to_col(ws * k) * v
        n_out[h:h + 1, :] = decay * n0 + ws * k
        m_row = jnp.where(lane == h, m_new, m_row)
    m_out[...] = m_row


def _mlstm_sample(zif_s, zb_s, gate_bias, c0, n0, m0, *, bb):
    bd = zif_s.shape[0]
    assert bd % bb == 0
    hm, c1, n1, m1 = pl.pallas_call(
        _mlstm_sample_kernel,
        grid=(bd // bb,),
        in_specs=[pl.BlockSpec((bb, 1, LANES), lambda b: (b, 0, 0)),
                  pl.BlockSpec((bb, 1, NB), lambda b: (b, 0, 0)),
                  pl.BlockSpec((1, LANES), lambda b: (0, 0)),
                  pl.BlockSpec((bb, M_HEADS, M_QK_DIM, M_V_DIM), lambda b: (b, 0, 0, 0)),
                  pl.BlockSpec((bb, M_HEADS, M_QK_DIM), lambda b: (b, 0, 0)),
                  pl.BlockSpec((bb, 1, M_HEADS), lambda b: (b, 0, 0))],
        out_specs=[pl.BlockSpec((bb, 1, M_V_WIDTH), lambda b: (b, 0, 0)),
                   pl.BlockSpec((bb, M_HEADS, M_QK_DIM, M_V_DIM), lambda b: (b, 0, 0, 0)),
                   pl.BlockSpec((bb, M_HEADS, M_QK_DIM), lambda b: (b, 0, 0)),
                   pl.BlockSpec((bb, 1, LANES), lambda b: (b, 0, 0))],
        out_shape=[jax.ShapeDtypeStruct((bd, 1, M_V_WIDTH), BF16),
                   jax.ShapeDtypeStruct((bd, M_HEADS, M_QK_DIM, M_V_DIM), F32),
                   jax.ShapeDtypeStruct((bd, M_HEADS, M_QK_DIM), F32),
                   jax.ShapeDtypeStruct((bd, 1, LANES), F32)],
        compiler_params=_cparams(("parallel",)),
        name="mlstm_sample",
    )(zif_s.reshape(bd, 1, LANES), zb_s.reshape(bd, 1, NB), gate_bias, c0, n0, m0.reshape(bd, 1, M_HEADS))
    return hm.reshape(bd, M_V_WIDTH), c1, n1, m1[:, 0, :M_HEADS]


def _merge_kernel(o1, o2, o3, l1, l2, l3, hm_ref, ga_ref, gb_ref, x_ref, wpa_ref, wpm_ref, wo_ref, g2_ref,
                  x1_ref, h2_ref, *, hws):
    os_ = (o1, o2, o3)
    parts = []
    for n in range(ATT_HEADS):
        sl = slice(n * ATT_HEAD_DIM, (n + 1) * ATT_HEAD_DIM)
        ls = []
        for l_ref, hw in zip((l1, l2, l3), hws):
            ln = (n // hw) * LANES + n % hw
            ls.append(l_ref[:, ln:ln + 1])
        mx = jnp.maximum(jnp.maximum(ls[0], ls[1]), ls[2])
        es = [jnp.exp(l - mx) for l in ls]
        tot = es[0] + es[1] + es[2]
        parts.append((es[0] / tot) * o1[:, sl] + (es[1] / tot) * o2[:, sl] + (es[2] / tot) * o3[:, sl])
    att = jnp.concatenate(parts, axis=1).astype(BF16)
    pa = jnp.dot(att, wpa_ref[...], preferred_element_type=F32)
    pm = jnp.dot(hm_ref[...], wpm_ref[...], preferred_element_type=F32)
    merged = (jax.nn.sigmoid(ga_ref[...].astype(F32)) * pa + jax.nn.sigmoid(gb_ref[...].astype(F32)) * pm)
    x1 = x_ref[...] + jnp.dot(merged.astype(BF16), wo_ref[...], preferred_element_type=F32)
    x1_ref[...] = x1
    ms = jnp.mean(x1 * x1, axis=-1, keepdims=True)
    h2_ref[...] = (x1 * lax.rsqrt(ms + RMS_EPS) * g2_ref[...]).astype(BF16)


def _merge(o_list, lse_list, hm, zt, x, wpa, wpm, wo, g2, *, tm):
    m, d = x.shape
    assert m % tm == 0 and zt.shape[1] == NT
    row_blk = lambda w: pl.BlockSpec((tm, w), lambda i: (i, 0))
    const = lambda shape: pl.BlockSpec(shape, lambda i: (0, 0), pipeline_mode=pl.Buffered(1))
    hws = tuple(ATT_HEADS * LANES // l.shape[1] for l in lse_list)
    return pl.pallas_call(
        functools.partial(_merge_kernel, hws=hws),
        grid=(m // tm,),
        in_specs=[row_blk(ATT_GROUP_W)] * 3 + [row_blk(l.shape[1]) for l in lse_list]
                 + [row_blk(M_V_WIDTH),
                    pl.BlockSpec((tm, d), lambda i: (i, T_GA // d)),
                    pl.BlockSpec((tm, d), lambda i: (i, T_GB // d)),
                    row_blk(d),
                    const(wpa.shape), const(wpm.shape), const(wo.shape), const((1, d))],
        out_specs=[row_blk(d), row_blk(d)],
        out_shape=[jax.ShapeDtypeStruct((m, d), F32), jax.ShapeDtypeStruct((m, d), BF16)],
        compiler_params=_cparams(("parallel",)),
        name="merge",
    )(*o_list, *lse_list, hm, zt, zt, x, wpa, wpm, wo, g2)


def _ffn_body(h2_ref, wu_ref, wg_ref, cw_ref, cb_ref, wd_ref, x1_ref, y_ref, g_prev2, g_prev1):
    j = pl.program_id(1)
    h2 = h2_ref[...]
    u = jnp.dot(h2, wu_ref[...], preferred_element_type=F32)
    g = jnp.dot(h2, wg_ref[...], preferred_element_type=F32)
    gconv = cb_ref[...] + ((g_prev2(g) * cw_ref[0:1, :] + g_prev1(g) * cw_ref[1:2, :]) + g * cw_ref[2:3, :])
    act = (jax.nn.gelu(gconv) * u).astype(BF16)
    down = jnp.dot(act, wd_ref[...], preferred_element_type=F32)

    @pl.when(j == 0)
    def _():
        y_ref[...] = x1_ref[...] + down

    @pl.when(j > 0)
    def _():
        y_ref[...] += down

    return g


def _ffn_prompt_kernel(h2_ref, wu_ref, wg_ref, cw_ref, cb_ref, wd_ref, x1_ref, y_ref, tail_ref, prev_s,
                       u0_s, g0_s, u1_s, g1_s, *, tiles_per_seq, nf):
    i = pl.program_id(0)
    j = pl.program_id(1)
    tm = h2_ref.shape[0]
    slots = ((u0_s, g0_s), (u1_s, g1_s))

    def up(u_s, g_s):
        h2 = h2_ref[...]
        u_s[...] = jnp.dot(h2, wu_ref[...], preferred_element_type=F32)
        g_s[...] = jnp.dot(h2, wg_ref[...], preferred_element_type=F32)

    def down(u_s, g_s):
        jt = j - 1
        prev = jnp.where(i % tiles_per_seq == 0, 0.0, prev_s[jt])
        p2, p1 = prev[6:7, :], prev[7:8, :]
        g = g_s[...]
        r = lax.broadcasted_iota(jnp.int32, g.shape, 0)
        g_m1 = jnp.where(r == 0, p1, pltpu.roll(g, 1, axis=0))
        g_m2 = jnp.where(r == 0, p2, jnp.where(r == 1, p1, pltpu.roll(g, 2, axis=0)))
        gconv = cb_ref[...] + ((g_m2 * cw_ref[0:1, :] + g_m1 * cw_ref[1:2, :]) + g * cw_ref[2:3, :])
        act = (jax.nn.gelu(gconv) * u_s[...]).astype(BF16)
        y_ref[...] += jnp.dot(act, wd_ref[...], preferred_element_type=F32)
        prev_s[jt] = g[tm - 8:tm, :]
        tail_ref[...] = g[tm - 8:tm, :]

    @pl.when(j == 0)
    def _():
        y_ref[...] = x1_ref[...]
        up(*slots[0])

    for par in range(2):
        @pl.when((j > 0) & (j < nf) & (j % 2 == par))
        def _(par=par):
            up(*slots[par])
            down(*slots[1 - par])

    @pl.when(j == nf)
    def _():
        down(*slots[(nf - 1) % 2])


def _ffn_sample_kernel(h2_ref, wu_ref, wg_ref, cw_ref, cb_ref, wd_ref, x1_ref, b2_ref, b1_ref, y_ref, g_ref):
    g = _ffn_body(h2_ref, wu_ref, wg_ref, cw_ref, cb_ref, wd_ref, x1_ref, y_ref,
                  lambda g: b2_ref[...], lambda g: b1_ref[...])
    g_ref[...] = g


def _ffn_specs(tm, tf, d, nf):
    return [pl.BlockSpec((tm, d), lambda i, j: (i, 0)),
            pl.BlockSpec((d, tf), lambda i, j: (0, j)),
            pl.BlockSpec((d, tf), lambda i, j: (0, nf + j)),
            pl.BlockSpec((CONV_W, tf), lambda i, j: (0, j)),
            pl.BlockSpec((1, tf), lambda i, j: (0, j)),
            pl.BlockSpec((tf, d), lambda i, j: (j, 0)),
            pl.BlockSpec((tm, d), lambda i, j: (i, 0))]


def _ffn_prompt(h2, x1, w_in, conv_w, conv_b, w_down, seq, *, tm, tf):
    m, d = x1.shape
    assert m % tm == 0 and seq % tm == 0 and D_FF % tf == 0 and tm % 8 == 0
    nf = D_FF // tf
    kern = functools.partial(_ffn_prompt_kernel, tiles_per_seq=seq // tm, nf=nf)
    up_t = lambda j: jnp.minimum(j, nf - 1)
    dn_t = lambda j: jnp.maximum(j - 1, 0)
    y, tails = pl.pallas_call(
        kern,
        grid=(m // tm, nf + 1),
        in_specs=[pl.BlockSpec((tm, d), lambda i, j: (i, 0)),
                  pl.BlockSpec((d, tf), lambda i, j: (0, up_t(j))),
                  pl.BlockSpec((d, tf), lambda i, j: (0, nf + up_t(j))),
                  pl.BlockSpec((CONV_W, tf), lambda i, j: (0, dn_t(j))),
                  pl.BlockSpec((1, tf), lambda i, j: (0, dn_t(j))),
                  pl.BlockSpec((tf, d), lambda i, j: (dn_t(j), 0)),
                  pl.BlockSpec((tm, d), lambda i, j: (i, 0))],
        out_specs=[pl.BlockSpec((tm, d), lambda i, j: (i, 0)),
                   pl.BlockSpec((None, 8, tf), lambda i, j: (i, 0, dn_t(j)))],
        out_shape=[jax.ShapeDtypeStruct((m, d), F32), jax.ShapeDtypeStruct((m // tm, 8, D_FF), F32)],
        scratch_shapes=[pltpu.VMEM((nf, 8, tf), F32)] + [pltpu.VMEM((tm, tf), F32)] * 4,
        compiler_params=_cparams(("arbitrary", "arbitrary")),
        name="ffn_prompt",
    )(h2, w_in, w_in, conv_w, conv_b, w_down, x1)
    return y, tails


def _ffn_sample(h2, x1, w_in, conv_w, conv_b, w_down, conv_buf, *, tf):
    m, d = x1.shape
    nf = D_FF // tf
    buf2d = conv_buf.reshape(m, (CONV_W - 1) * D_FF)
    return pl.pallas_call(
        _ffn_sample_kernel,
        grid=(1, nf),
        in_specs=_ffn_specs(m, tf, d, nf) + [pl.BlockSpec((m, tf), lambda i, j: (0, j)),
                                             pl.BlockSpec((m, tf), lambda i, j: (0, nf + j))],
        out_specs=[pl.BlockSpec((m, d), lambda i, j: (i, 0)), pl.BlockSpec((m, tf), lambda i, j: (0, j))],
        out_shape=[jax.ShapeDtypeStruct((m, d), F32), jax.ShapeDtypeStruct((m, D_FF), F32)],
        compiler_params=_cparams(("arbitrary", "arbitrary")),
        name="ffn_sample",
    )(h2, w_in, w_in, conv_w, conv_b, w_down, x1, buf2d, buf2d)


IN_PROJ_TN = 512
FFN_TF = 512


def _qk_gain_row(q_norm, k_norm):
    reps = N_GROUPS * ATT_HEADS
    return jnp.concatenate([jnp.tile(q_norm, reps), jnp.tile(k_norm, reps), jnp.zeros((NA - A_VA,), F32)])[None, :]


ATTN_PROMPT_TILING = ((4, 4, 1), (1, 4, 1), (1, 2, 2))


def _layer(x_prompt, x_sample, caches, norm_mix, w_in, q_norm, k_norm, b_igate, b_fgate, w_proj_att,
           w_proj_mlstm, w_out, norm_ffn, w_ffn_in, conv_w, conv_b, w_ffn_down):
    batch, seq, d = x_prompt.shape
    bd = x_sample.shape[0]
    assert x_sample.shape[1] == 1 and d == D_MODEL
    (ck1, cv1, ck2, cv2, ck3, cv3, st_c, st_n, st_m, st_conv) = caches

    wt_in = jnp.swapaxes(w_in, 0, 1)
    qk_gain = _qk_gain_row(q_norm, k_norm)
    g1 = norm_mix[None, :]
    g2 = norm_ffn[None, :]
    gate_bias = jnp.concatenate([b_igate, b_fgate, jnp.zeros((LANES - 2 * M_HEADS,), F32)])[None, :]
    wpa = w_proj_att.astype(BF16)
    wpm = w_proj_mlstm.astype(BF16)
    wo = w_out.astype(BF16)
    w_ff = w_ffn_in.astype(BF16)
    w_dn = w_ffn_down.astype(BF16)
    cb = conv_b[None, :]

    xp = x_prompt.reshape(batch * seq, d)
    xs = x_sample.reshape(bd, d)

    h_all = _rmsnorm(xp, xs, g1, tm=2048, ts=512)
    (za_p, zb_p, zt_p, zif_p), (za_s, zb_s, zt_s, zif_s) = _in_proj(h_all, wt_in, qk_gain, tn=IN_PROJ_TN, ms=bd)
    o_p, lse_p = zip(*[_attn_prompt_group(za_p, batch, seq, gi, dil, nsub=nsub, hw=hw, rb=rb)
                       for gi, ((_, dil), (nsub, hw, rb)) in enumerate(zip(DIL_PATTERNS, ATTN_PROMPT_TILING))])
    hm_p, p_c, p_n, p_m = _mlstm_prompt(zif_p, zb_p, gate_bias, batch, seq, chunk=256)
    x1_p, h2_p = _merge(o_p, lse_p, hm_p, zt_p, xp, wpa, wpm, wo, g2, tm=256)
    y_p, tails = _ffn_prompt(h2_p, x1_p, w_ff, conv_w, cb, w_dn, seq, tm=512, tf=FFN_TF)

    za_p3 = za_p.reshape(batch, seq, NA)
    p_kv = []
    for gi, (win, _) in enumerate(DIL_PATTERNS):
        keep = min(win, seq)
        for col in (A_KA, A_VA):
            lo = col + gi * ATT_GROUP_W
            p_kv.append(za_p3[:, seq - keep:, lo:lo + ATT_GROUP_W].reshape(batch, keep, ATT_HEADS, ATT_HEAD_DIM))
    tiles_per_seq = seq // 512
    p_conv = tails.reshape(batch, tiles_per_seq, 8, D_FF)[:, -1, 8 - (CONV_W - 1):, :]

    o_s, lse_s = _attn_sample(za_s, [(ck1, cv1), (ck2, cv2), (ck3, cv3)], bb=4)
    hm_s, s_c, s_n, s_m = _mlstm_sample(zif_s, zb_s, gate_bias, st_c, st_n, st_m, bb=2)
    x1_s, h2_s = _merge(o_s, lse_s, hm_s, zt_s, xs, wpa, wpm, wo, g2, tm=bd)
    y_s, g_s = _ffn_sample(h2_s, x1_s, w_ff, conv_w, cb, w_dn, st_conv, tf=FFN_TF)

    s_kv = []
    for gi in range(N_GROUPS):
        for col in (A_KA, A_VA):
            lo = col + gi * ATT_GROUP_W
            s_kv.append(za_s[:, lo:lo + ATT_GROUP_W].reshape(bd, 1, ATT_HEADS, ATT_HEAD_DIM))
    s_conv = jnp.stack([st_conv[:, 1, :], g_s], axis=1)

    p_state = p_kv + [p_c, p_n, p_m, p_conv]
    s_state = s_kv + [s_c, s_n, s_m, s_conv]
    return y_p.reshape(batch, seq, d), y_s.reshape(bd, 1, d), p_state, s_state


def kernel(x_prompt, x_sample, cache_k_w128, cache_v_w128, cache_k_w512, cache_v_w512, cache_k_w2048,
           cache_v_w2048, state_mlstm_C, state_mlstm_n, state_mlstm_m, state_ffn_conv, norm_mix, w_in, q_norm,
           k_norm, b_igate, b_fgate, w_proj_att, w_proj_mlstm, w_out, norm_ffn, w_ffn_in, conv_w, conv_b,
           w_ffn_down):
    assert norm_mix.shape[0] == 1
    caches = [c[0] for c in (cache_k_w128, cache_v_w128, cache_k_w512, cache_v_w512, cache_k_w2048,
                             cache_v_w2048, state_mlstm_C, state_mlstm_n, state_mlstm_m, state_ffn_conv)]
    weights = [w[0] for w in (norm_mix, w_in, q_norm, k_norm, b_igate, b_fgate, w_proj_att, w_proj_mlstm,
                              w_out, norm_ffn, w_ffn_in, conv_w, conv_b, w_ffn_down)]
    y_p, y_s, p_state, s_state = _layer(x_prompt, x_sample, caches, *weights)
    return (y_p, y_s, *[a[None] for a in p_state], *[a[None] for a in s_state])
```

```python
import functools

import jax
import jax.numpy as jnp
from jax import lax
from jax.experimental import pallas as pl
from jax.experimental.pallas import tpu as pltpu

F32 = jnp.float32
BF16 = jnp.bfloat16

RMS_EPS = 1e-6
NEG_INF = -1e30
LANES = 128
VMEM_LIMIT = 56 * 1024 * 1024

D_MODEL = 2048
DIL_PATTERNS = ((128, 1), (512, 4), (2048, 16))
N_GROUPS = 3
ATT_HEADS = 4
ATT_HEAD_DIM = 128
ATT_SPAN = 128
ATT_SCALE = ATT_HEAD_DIM ** -0.5
ATT_GROUP_W = ATT_HEADS * ATT_HEAD_DIM
ATT_WIDTH = N_GROUPS * ATT_GROUP_W
M_HEADS = 4
M_QK_DIM = D_MODEL // (2 * M_HEADS)
M_V_DIM = D_MODEL // M_HEADS
M_QK_WIDTH = M_HEADS * M_QK_DIM
M_V_WIDTH = M_HEADS * M_V_DIM
M_K_SCALE = M_QK_DIM ** -0.5
D_FF = ((8 * D_MODEL // 3 + 255) // 256) * 256
CONV_W = 3
SPLIT_SIZES = (ATT_WIDTH, ATT_WIDTH, ATT_WIDTH, M_QK_WIDTH, M_QK_WIDTH, M_V_WIDTH, M_V_WIDTH,
               M_HEADS, M_HEADS, D_MODEL, D_MODEL)

IN_COLS = sum(SPLIT_SIZES)
A_QA = 0
A_KA = A_QA + ATT_WIDTH
A_VA = A_KA + ATT_WIDTH
NA = A_VA + ATT_WIDTH
B_QM = 0
B_KM = B_QM + M_QK_WIDTH
B_VM = B_KM + M_QK_WIDTH
B_OM = B_VM + M_V_WIDTH
NB = B_OM + M_V_WIDTH
N_IF = 2 * M_HEADS
T_GA = 0
T_GB = T_GA + D_MODEL
NT = T_GB + D_MODEL
assert NA + NB + N_IF + NT == IN_COLS


def _cparams(sem):
    return pltpu.CompilerParams(dimension_semantics=sem, vmem_limit_bytes=VMEM_LIMIT)


def _log_sigmoid(x):
    return jnp.minimum(x, 0.0) - jnp.log(1.0 + jnp.exp(-jnp.abs(x)))


def _rmsnorm_kernel(x_ref, xs_ref, g_ref, h_ref, *, nsub):
    j = pl.program_id(1)
    ts = x_ref.shape[0]

    def norm(x):
        ms = jnp.mean(x * x, axis=-1, keepdims=True)
        return (x * lax.rsqrt(ms + RMS_EPS) * g_ref[...]).astype(h_ref.dtype)

    @pl.when(j < nsub)
    def _():
        h_ref[pl.ds(pl.multiple_of(j * ts, ts), ts), :] = norm(x_ref[...])

    @pl.when(j == nsub)
    def _():
        h_ref[nsub * ts:, :] = norm(xs_ref[...])


def _rmsnorm(x, xs, gain, *, tm, ts):
    m, d = x.shape
    ms = xs.shape[0]
    assert m % tm == 0 and tm % ts == 0
    nsub = tm // ts
    last = m // ts - 1
    return pl.pallas_call(
        functools.partial(_rmsnorm_kernel, nsub=nsub),
        grid=(m // tm, nsub + 1),
        in_specs=[pl.BlockSpec((ts, d), lambda i, j: (jnp.minimum(i * nsub + j, last), 0)),
                  pl.BlockSpec((ms, d), lambda i, j: (0, 0)),
                  pl.BlockSpec((1, d), lambda i, j: (0, 0))],
        out_specs=pl.BlockSpec((None, tm + ms, d), lambda i, j: (i, 0, 0)),
        out_shape=jax.ShapeDtypeStruct((m // tm, tm + ms, d), BF16),
        compiler_params=_cparams(("parallel", "arbitrary")),
        name="rmsnorm",
    )(x, xs, gain)


def _in_proj_kernel(h_ref, w_ref, wn_ref, wif_ref, qkg_ref, za_ref, zb_ref, zt_ref, zif_ref,
                    sa_ref, sb_ref, st_ref, sif_ref, *, n_norm, na, nb, nt):
    j = pl.program_id(1)
    tp = za_ref.shape[0]

    def z_of(w):
        return lax.dot_general(h_ref[...], w.astype(BF16), (((1,), (1,)), ((), ())), preferred_element_type=F32)

    @pl.when(j < n_norm)
    def _():
        z = z_of(w_ref[...])
        for c in range(za_ref.shape[1] // LANES):
            sl = slice(c * LANES, (c + 1) * LANES)
            zc = z[:, sl]
            ms = jnp.mean(zc * zc, axis=-1, keepdims=True)
            zn = zc * lax.rsqrt(ms + RMS_EPS) * qkg_ref[:, sl]
            za_ref[:, sl] = zn[:tp]
            sa_ref[j, :, sl] = zn[tp:]

    @pl.when((j >= n_norm) & (j < na))
    def _():
        z = z_of(w_ref[...])
        za_ref[...] = z[:tp]
        sa_ref[j] = z[tp:]

    @pl.when((j >= na) & (j < na + nb))
    def _():
        z = z_of(w_ref[...])
        zb_ref[...] = z[:tp].astype(zb_ref.dtype)
        sb_ref[j - na] = z[tp:]

    @pl.when((j >= na + nb) & (j < na + nb + nt))
    def _():
        z = z_of(jnp.concatenate([w_ref[N_IF:, :], wn_ref[...]], axis=0))
        zt_ref[...] = z[:tp].astype(zt_ref.dtype)
        st_ref[j - na - nb] = z[tp:]

    @pl.when(j == na + nb + nt)
    def _():
        z = z_of(jnp.concatenate([wif_ref[...], jnp.zeros((LANES - N_IF, wif_ref.shape[1]), F32)], axis=0))
        zif_ref[...] = z[:tp]
        sif_ref[...] = z[tp:]


def _in_proj(h, wt, qk_gain, *, tn, ms):
    ni, th, d = h.shape
    tm = th - ms
    m = ni * tm
    assert wt.shape == (IN_COLS, d) and N_IF == 8
    assert NA % tn == 0 and NB % tn == 0 and NT % tn == 0 and A_VA % tn == 0 and tn % N_IF == 0
    na, nb, nt = NA // tn, NB // tn, NT // tn
    nw = na + nb + nt
    g_if = (NA + NB) // N_IF
    g_tn = tn // N_IF
    kern = functools.partial(_in_proj_kernel, n_norm=A_VA // tn, na=na, nb=nb, nt=nt)
    col_a = lambda j: jnp.minimum(j, na - 1)
    col_b = lambda j: jnp.clip(j - na, 0, nb - 1)
    col_t = lambda j: jnp.clip(j - na - nb, 0, nt - 1)
    out_specs = [pl.BlockSpec((None, tm, tn), lambda i, j: (i, 0, col_a(j))),
                 pl.BlockSpec((None, tm, tn), lambda i, j: (i, 0, col_b(j))),
                 pl.BlockSpec((None, tm, tn), lambda i, j: (i, 0, col_t(j))),
                 pl.BlockSpec((None, tm, LANES), lambda i, j: (i, 0, 0))]
    out_shape = [jax.ShapeDtypeStruct((ni, tm, w), dt)
                 for w, dt in zip((NA, NB, NT, LANES), (F32, BF16, BF16, F32))]
    for n_tiles in (na, nb, nt):
        out_specs.append(pl.BlockSpec((None, n_tiles, ms, tn), lambda i, j: (i, 0, 0, 0)))
        out_shape.append(jax.ShapeDtypeStruct((ni, n_tiles, ms, tn), F32))
    out_specs.append(pl.BlockSpec((None, ms, LANES), lambda i, j: (i, 0, 0)))
    out_shape.append(jax.ShapeDtypeStruct((ni, ms, LANES), F32))
    outs = pl.pallas_call(
        kern,
        grid=(ni, nw + 1),
        in_specs=[
            pl.BlockSpec((None, th, d), lambda i, j: (i, 0, 0)),
            pl.BlockSpec((tn, d), lambda i, j: (jnp.minimum(j, nw - 1), 0)),
            pl.BlockSpec((N_IF, d), lambda i, j: (g_if + g_tn * (jnp.clip(j, na + nb, nw - 1) - (na + nb) + 1), 0)),
            pl.BlockSpec((N_IF, d), lambda i, j: (g_if, 0)),
            pl.BlockSpec((1, tn), lambda i, j: (0, col_a(j))),
        ],
        out_specs=out_specs,
        out_shape=out_shape,
        compiler_params=_cparams(("parallel", "arbitrary")),
        name="in_proj",
    )(h, wt, wt, wt, qk_gain)
    sample = [o[0].swapaxes(0, 1).reshape(ms, -1) for o in outs[4:7]] + [outs[7][0]]
    return [o.reshape(m, o.shape[2]) for o in outs[:4]], sample


def _attn_prompt_kernel(*refs, dil, nsub, hw, rb):
    q_refs, kp_refs, kc_refs, vp_refs, vc_refs = [refs[t * hw:(t + 1) * hw] for t in range(5)]
    o_ref, lse_ref = refs[5 * hw:5 * hw + 2]
    o_refs = refs[5 * hw + 2:]
    c = pl.program_id(1)
    blk = ATT_SPAN
    step = blk * dil
    qi = lax.broadcasted_iota(jnp.int32, (blk, 2 * blk), 0)
    kj = lax.broadcasted_iota(jnp.int32, (blk, 2 * blk), 1)
    band = (kj >= qi) & (kj <= qi + blk)
    first_bias = jnp.where((kj < blk) & (c == 0), NEG_INF, 0.0)
    lane = lax.broadcasted_iota(jnp.int32, (blk, LANES), 1)

    def rows(base, r):
        return pl.ds(base + r, blk) if dil == 1 else pl.ds(base + r, blk, stride=dil)

    def group(r0, s):
        base = s * step
        ids = [(r0 + r, n) for r in range(rb) for n in range(hw)]

        def stacked(cur_refs, prev_refs):
            parts = []
            for r, n in ids:
                cur = cur_refs[n][rows(base, r), :]
                if prev_refs is None:
                    parts.append(cur)
                else:
                    prv = prev_refs[n][rows(0, r), :] if s == 0 else cur_refs[n][rows(base - step, r), :]
                    parts.append(jnp.concatenate([prv, cur], axis=0))
            return jnp.stack(parts).astype(BF16)

        q = stacked(q_refs, None)
        k = stacked(kc_refs, kp_refs)
        v = stacked(vc_refs, vp_refs)
        sc = jnp.einsum('bqe,bke->bqk', q, k, preferred_element_type=F32) * ATT_SCALE
        if s == 0:
            sc = sc + first_bias
        sc = jnp.where(band, sc, NEG_INF)
        m = jnp.max(sc, axis=-1, keepdims=True)
        p = jnp.exp(sc - m)
        l = jnp.sum(p, axis=-1, keepdims=True)
        o = jnp.einsum('bqk,bke->bqe', p.astype(BF16), v, preferred_element_type=F32) / l
        lse = m + jnp.log(l)
        for r in range(rb):
            lse_tile = jnp.zeros((blk, LANES), F32)
            for n in range(hw):
                b = r * hw + n
                o_refs[n][rows(base, r0 + r), :] = o[b]
                lse_tile = jnp.where(lane == n, lse[b], lse_tile)
            lse_ref[rows(base, r0 + r), :] = lse_tile

    for s in range(nsub):
        if dil == rb:
            group(0, s)
        else:
            def body(it, carry, s=s):
                group(it * rb, s)
                return carry
            lax.fori_loop(0, dil // rb, body, 0)

    for n in range(hw):
        o_ref[:, n * ATT_HEAD_DIM:(n + 1) * ATT_HEAD_DIM] = o_refs[n][...]


def _attn_prompt_group(za, batch, seq, gi, dil, *, nsub, hw, rb):
    step = ATT_SPAN * dil
    tc = nsub * step
    assert seq % tc == 0 and ATT_HEADS % hw == 0 and dil % rb == 0
    nh = ATT_HEADS // hw
    e = ATT_HEAD_DIM
    z3 = za.reshape(batch, seq, NA)
    qc, kc, vc = [(col + gi * ATT_GROUP_W) // e for col in (A_QA, A_KA, A_VA)]

    def cur(col):
        return [pl.BlockSpec((None, tc, e), lambda b, c, h, n=n: (b, c, col + h * hw + n)) for n in range(hw)]

    def prev(col):
        return [pl.BlockSpec((None, step, e), lambda b, c, h, n=n: (b, jnp.maximum(c * nsub - 1, 0), col + h * hw + n))
                for n in range(hw)]

    kern = functools.partial(_attn_prompt_kernel, dil=dil, nsub=nsub, hw=hw, rb=rb)
    o, lse = pl.pallas_call(
        kern,
        grid=(batch, seq // tc, nh),
        in_specs=cur(qc) + prev(kc) + cur(kc) + prev(vc) + cur(vc),
        out_specs=[pl.BlockSpec((None, tc, hw * e), lambda b, c, h: (b, c, h)),
                   pl.BlockSpec((None, tc, LANES), lambda b, c, h: (b, c, h))],
        out_shape=[jax.ShapeDtypeStruct((batch, seq, ATT_GROUP_W), F32),
                   jax.ShapeDtypeStruct((batch, seq, nh * LANES), F32)],
        scratch_shapes=[pltpu.VMEM((tc, e), F32)] * hw,
        compiler_params=_cparams(("parallel", "parallel", "parallel")),
        name=f"attn_prompt_g{gi}",
    )(*([z3] * (5 * hw)))
    return o.reshape(batch * seq, ATT_GROUP_W), lse.reshape(batch * seq, nh * LANES)


def _attn_sample_kernel(z_ref, k1, v1, k2, v2, k3, v3, o1, o2, o3, l1, l2, l3):
    bb = z_ref.shape[0]
    bufs = ((k1, v1, o1, l1), (k2, v2, o2, l2), (k3, v3, o3, l3))
    lane = lax.broadcasted_iota(jnp.int32, (1, LANES), 1)

    def heads(b, col):
        return jnp.concatenate([z_ref[b, :, col + n * ATT_HEAD_DIM:col + (n + 1) * ATT_HEAD_DIM]
                                for n in range(ATT_HEADS)], axis=0)

    def body(b, carry):
        for gi, (k_ref, v_ref, o_ref, l_ref) in enumerate(bufs):
            q = heads(b, A_QA + gi * ATT_GROUP_W)
            k_new = heads(b, A_KA + gi * ATT_GROUP_W)
            v_new = heads(b, A_VA + gi * ATT_GROUP_W)
            kb = k_ref[b]
            vb = v_ref[b]
            s = jnp.sum(kb * q[None], axis=-1, keepdims=True) * ATT_SCALE
            s_new = jnp.sum(k_new * q, axis=-1, keepdims=True) * ATT_SCALE
            m = jnp.maximum(jnp.max(s, axis=0), s_new)
            p = jnp.exp(s - m[None])
            p_new = jnp.exp(s_new - m)
            l = jnp.sum(p, axis=0) + p_new
            o = (jnp.sum(p * vb, axis=0) + p_new * v_new) / l
            lse = m + jnp.log(l)
            lse_row = jnp.zeros((1, LANES), F32)
            for n in range(ATT_HEADS):
                o_ref[b, :, n * ATT_HEAD_DIM:(n + 1) * ATT_HEAD_DIM] = o[n:n + 1, :]
                lse_row = jnp.where(lane == n, lse[n:n + 1, :], lse_row)
            l_ref[b] = lse_row
        return carry

    lax.fori_loop(0, bb, body, 0)


def _attn_sample(za_s, caches, *, bb):
    bd = za_s.shape[0]
    assert bd % bb == 0
    ins, in_specs = [za_s.reshape(bd, 1, NA)], [pl.BlockSpec((bb, 1, NA), lambda i: (i, 0, 0))]
    for (k_buf, v_buf), (win, dil) in zip(caches, DIL_PATTERNS):
        assert k_buf.shape[1:] == (ATT_SPAN * dil, ATT_HEADS, ATT_HEAD_DIM)
        for buf in (k_buf, v_buf):
            ins.append(buf.reshape(bd, ATT_SPAN, dil, ATT_HEADS, ATT_HEAD_DIM))
            in_specs.append(pl.BlockSpec((bb, ATT_SPAN, None, ATT_HEADS, ATT_HEAD_DIM),
                                         lambda i: (i, 0, 0, 0, 0)))
    outs = pl.pallas_call(
        _attn_sample_kernel,
        grid=(bd // bb,),
        in_specs=in_specs,
        out_specs=[pl.BlockSpec((bb, 1, ATT_GROUP_W), lambda i: (i, 0, 0))] * 3
                  + [pl.BlockSpec((bb, 1, LANES), lambda i: (i, 0, 0))] * 3,
        out_shape=[jax.ShapeDtypeStruct((bd, 1, ATT_GROUP_W), F32)] * 3
                  + [jax.ShapeDtypeStruct((bd, 1, LANES), F32)] * 3,
        compiler_params=_cparams(("parallel",)),
        name="attn_sample",
    )(*ins)
    return [o[:, 0, :] for o in outs[:3]], [l[:, 0, :] for l in outs[3:]]


def _mlstm_prompt_kernel(q_ref, k_ref, v_ref, om_ref, g_ref, bias_ref, h_ref, c_out, n_out, m_out,
                         c_s, n_s, m_s):
    hd = pl.program_id(1)
    ci = pl.program_id(2)
    L = q_ref.shape[0]

    @pl.when(ci == 0)
    def _():
        c_s[...] = jnp.zeros_like(c_s)
        n_s[...] = jnp.zeros_like(n_s)
        m_s[...] = jnp.zeros_like(m_s)

    gates = g_ref[...] + bias_ref[...]
    lane = lax.broadcasted_iota(jnp.int32, (L, LANES), 1)
    li_col = jnp.sum(jnp.where(lane == hd, gates, 0.0), axis=1, keepdims=True)
    lf_col = _log_sigmoid(jnp.sum(jnp.where(lane == hd + M_HEADS, gates, 0.0), axis=1, keepdims=True))
    gates_t = gates.T
    sub = lax.broadcasted_iota(jnp.int32, (LANES, L), 0)
    li_row = jnp.sum(jnp.where(sub == hd, gates_t, 0.0), axis=0, keepdims=True)
    lf_row = _log_sigmoid(jnp.sum(jnp.where(sub == hd + M_HEADS, gates_t, 0.0), axis=0, keepdims=True))

    ti = lax.broadcasted_iota(jnp.int32, (L, L), 0)
    si = lax.broadcasted_iota(jnp.int32, (L, L), 1)
    causal = si <= ti
    b_col = jnp.sum(jnp.where(causal, lf_row, 0.0), axis=1, keepdims=True)
    b_row = jnp.sum(jnp.where(ti <= si, lf_col, 0.0), axis=0, keepdims=True)
    b_end = jnp.sum(lf_row, axis=1, keepdims=True)

    m_prev = m_s[...]
    dmat = jnp.where(causal, b_col - b_row + li_row, NEG_INF)
    inter = b_col + m_prev
    mt = jnp.maximum(inter, jnp.max(dmat, axis=1, keepdims=True))

    qb = q_ref[...]
    q = qb.astype(F32)
    k = k_ref[...].astype(F32) * M_K_SCALE
    vb = v_ref[...]
    qk = lax.dot_general(qb, k.astype(BF16), (((1,), (1,)), ((), ())), preferred_element_type=F32)
    a = jnp.exp(dmat - mt) * qk
    w_inter = jnp.exp(inter - mt)
    num = (jnp.dot(a.astype(BF16), vb, preferred_element_type=F32)
           + w_inter * jnp.dot(qb, c_s[...].astype(BF16), preferred_element_type=F32))
    den = jnp.sum(a, axis=1, keepdims=True) + w_inter * jnp.sum(q * n_s[...], axis=1, keepdims=True)
    h = num / jnp.maximum(jnp.abs(den), jnp.exp(-mt))
    h_ref[...] = (jax.nn.sigmoid(om_ref[...].astype(F32)) * h).astype(h_ref.dtype)

    g_col = b_end - b_col + li_col
    g_row = b_end - b_row + li_row
    m_new = jnp.maximum(b_end + m_prev, jnp.max(g_row, axis=1, keepdims=True))
    decay = jnp.exp(b_end + m_prev - m_new)
    kw = jnp.exp(g_col - m_new) * k
    c_s[...] = decay * c_s[...] + jnp.dot(kw.T.astype(BF16), vb, preferred_element_type=F32)
    n_s[...] = decay * n_s[...] + jnp.sum(kw, axis=0, keepdims=True)
    m_s[...] = m_new

    @pl.when(ci == pl.num_programs(2) - 1)
    def _():
        c_out[...] = c_s[...]
        n_out[...] = n_s[...]
        m_out[...] = jnp.broadcast_to(m_s[...], m_out.shape)


def _mlstm_prompt(zif, zb, gate_bias, batch, seq, *, chunk):
    assert seq % chunk == 0
    nc = seq // chunk
    zif3 = zif.reshape(batch, seq, LANES)
    zb3 = zb.reshape(batch, seq, NB)
    qk_blk = lambda col: pl.BlockSpec((None, chunk, M_QK_DIM), lambda b, h, c: (b, c, col // M_QK_DIM + h))
    v_blk = lambda col: pl.BlockSpec((None, chunk, M_V_DIM), lambda b, h, c: (b, c, col // M_V_DIM + h))
    hm, c1, n1, m1 = pl.pallas_call(
        _mlstm_prompt_kernel,
        grid=(batch, M_HEADS, nc),
        in_specs=[qk_blk(B_QM), qk_blk(B_KM), v_blk(B_VM), v_blk(B_OM),
                  pl.BlockSpec((None, chunk, LANES), lambda b, h, c: (b, c, 0)),
                  pl.BlockSpec((1, LANES), lambda b, h, c: (0, 0))],
        out_specs=[pl.BlockSpec((None, chunk, M_V_DIM), lambda b, h, c: (b, c, h)),
                   pl.BlockSpec((None, None, M_QK_DIM, M_V_DIM), lambda b, h, c: (b, h, 0, 0)),
                   pl.BlockSpec((None, None, 1, M_QK_DIM), lambda b, h, c: (b, h, 0, 0)),
                   pl.BlockSpec((None, None, 1, LANES), lambda b, h, c: (b, h, 0, 0))],
        out_shape=[jax.ShapeDtypeStruct((batch, seq, M_V_WIDTH), BF16),
                   jax.ShapeDtypeStruct((batch, M_HEADS, M_QK_DIM, M_V_DIM), F32),
                   jax.ShapeDtypeStruct((batch, M_HEADS, 1, M_QK_DIM), F32),
                   jax.ShapeDtypeStruct((batch, M_HEADS, 1, LANES), F32)],
        scratch_shapes=[pltpu.VMEM((M_QK_DIM, M_V_DIM), F32), pltpu.VMEM((1, M_QK_DIM), F32),
                        pltpu.VMEM((1, 1), F32)],
        compiler_params=_cparams(("parallel", "parallel", "arbitrary")),
        name="mlstm_prompt",
    )(zb3, zb3, zb3, zb3, zif3, gate_bias)
    return hm.reshape(batch * seq, M_V_WIDTH), c1, n1[:, :, 0, :], m1[:, :, 0, 0]


def _row_to_col(row):
    n = row.shape[1]
    return jnp.broadcast_to(row, (LANES, n)).T[:, 0:1]


def _mlstm_sample_kernel(zif_ref, zb_ref, bias_ref, c_ref, n_ref, m_ref, h_ref, c_out, n_out, m_out):
    for b in range(zif_ref.shape[0]):
        _mlstm_sample_one(zif_ref.at[b], zb_ref.at[b], bias_ref, c_ref.at[b], n_ref.at[b], m_ref.at[b],
                          h_ref.at[b], c_out.at[b], n_out.at[b], m_out.at[b])


def _mlstm_sample_one(zif_ref, zb_ref, bias_ref, c_ref, n_ref, m_ref, h_ref, c_out, n_out, m_out):
    gates = zif_ref[...] + bias_ref[...]
    lane = lax.broadcasted_iota(jnp.int32, (1, LANES), 1)
    m_row = jnp.zeros((1, LANES), F32)
    for h in range(M_HEADS):
        q = zb_ref[:, B_QM + h * M_QK_DIM:B_QM + (h + 1) * M_QK_DIM].astype(F32)
        k = zb_ref[:, B_KM + h * M_QK_DIM:B_KM + (h + 1) * M_QK_DIM].astype(F32) * M_K_SCALE
        v = zb_ref[:, B_VM + h * M_V_DIM:B_VM + (h + 1) * M_V_DIM].astype(F32)
        om = zb_ref[:, B_OM + h * M_V_DIM:B_OM + (h + 1) * M_V_DIM].astype(F32)
        li = gates[:, h:h + 1]
        lf = _log_sigmoid(gates[:, M_HEADS + h:M_HEADS + h + 1])
        m0 = m_ref[:, h:h + 1]
        c0 = c_ref[h]
        n0 = n_ref[h:h + 1, :]
        inter = lf + m0
        mt = jnp.maximum(inter, li)
        a = jnp.exp(li - mt) * jnp.sum(q * k, axis=1, keepdims=True)
        w_inter = jnp.exp(inter - mt)
        q_c = jnp.sum(_row_to_col(q) * c0, axis=0, keepdims=True)
        num = a * v + w_inter * q_c
        den = a + w_inter * jnp.sum(q * n0, axis=1, keepdims=True)
        hv = num / jnp.maximum(jnp.abs(den), jnp.exp(-mt))
        h_ref[:, h * M_V_DIM:(h + 1) * M_V_DIM] = (jax.nn.sigmoid(om) * hv).astype(h_ref.dtype)
        m_new = jnp.maximum(inter, li)
        decay = jnp.exp(inter - m_new)
        ws = jnp.exp(li - m_new)
        c_out[h] = decay * c0 + _row_to_col(ws * k) * v
        n_out[h:h + 1, :] = decay * n0 + ws * k
        m_row = jnp.where(lane == h, m_new, m_row)
    m_out[...] = m_row


def _mlstm_sample(zif_s, zb_s, gate_bias, c0, n0, m0, *, bb):
    bd = zif_s.shape[0]
    assert bd % bb == 0
    hm, c1, n1, m1 = pl.pallas_call(
        _mlstm_sample_kernel,
        grid=(bd // bb,),
        in_specs=[pl.BlockSpec((bb, 1, LANES), lambda b: (b, 0, 0)),
                  pl.BlockSpec((bb, 1, NB), lambda b: (b, 0, 0)),
                  pl.BlockSpec((1, LANES), lambda b: (0, 0)),
                  pl.BlockSpec((bb, M_HEADS, M_QK_DIM, M_V_DIM), lambda b: (b, 0, 0, 0)),
                  pl.BlockSpec((bb, M_HEADS, M_QK_DIM), lambda b: (b, 0, 0)),
                  pl.BlockSpec((bb, 1, M_HEADS), lambda b: (b, 0, 0))],
        out_specs=[pl.BlockSpec((bb, 1, M_V_WIDTH), lambda b: (b, 0, 0)),
                   pl.BlockSpec((bb, M_HEADS, M_QK_DIM, M_V_DIM), lambda b: (b, 0, 0, 0)),
                   pl.BlockSpec((bb, M_HEADS, M_QK_DIM), lambda b: (b, 0, 0)),
                   pl.BlockSpec((bb, 1, LANES), lambda b: (b, 0, 0))],
        out_shape=[jax.ShapeDtypeStruct((bd, 1, M_V_WIDTH), BF16),
                   jax.ShapeDtypeStruct((bd, M_HEADS, M_QK_DIM, M_V_DIM), F32),
                   jax.ShapeDtypeStruct((bd, M_HEADS, M_QK_DIM), F32),
                   jax.ShapeDtypeStruct((bd, 1, LANES), F32)],
        compiler_params=_cparams(("parallel",)),
        name="mlstm_sample",
    )(zif_s.reshape(bd, 1, LANES), zb_s.reshape(bd, 1, NB), gate_bias, c0, n0, m0.reshape(bd, 1, M_HEADS))
    return hm.reshape(bd, M_V_WIDTH), c1, n1, m1[:, 0, :M_HEADS]


def _merge_kernel(o1, o2, o3, l1, l2, l3, hm_ref, ga_ref, gb_ref, x_ref, wpa_ref, wpm_ref, wo_ref, g2_ref,
                  x1_ref, h2_ref, *, hws):
    os_ = (o1, o2, o3)
    parts = []
    for n in range(ATT_HEADS):
        sl = slice(n * ATT_HEAD_DIM, (n + 1) * ATT_HEAD_DIM)
        ls = []
        for l_ref, hw in zip((l1, l2, l3), hws):
            ln = (n // hw) * LANES + n % hw
            ls.append(l_ref[:, ln:ln + 1])
        mx = jnp.maximum(jnp.maximum(ls[0], ls[1]), ls[2])
        es = [jnp.exp(l - mx) for l in ls]
        tot = es[0] + es[1] + es[2]
        parts.append((es[0] / tot) * o1[:, sl] + (es[1] / tot) * o2[:, sl] + (es[2] / tot) * o3[:, sl])
    att = jnp.concatenate(parts, axis=1).astype(BF16)
    pa = jnp.dot(att, wpa_ref[...], preferred_element_type=F32)
    pm = jnp.dot(hm_ref[...], wpm_ref[...], preferred_element_type=F32)
    merged = (jax.nn.sigmoid(ga_ref[...].astype(F32)) * pa + jax.nn.sigmoid(gb_ref[...].astype(F32)) * pm)
    x1 = x_ref[...] + jnp.dot(merged.astype(BF16), wo_ref[...], preferred_element_type=F32)
    x1_ref[...] = x1
    ms = jnp.mean(x1 * x1, axis=-1, keepdims=True)
    h2_ref[...] = (x1 * lax.rsqrt(ms + RMS_EPS) * g2_ref[...]).astype(BF16)


def _merge(o_list, lse_list, hm, zt, x, wpa, wpm, wo, g2, *, tm):
    m, d = x.shape
    assert m % tm == 0 and zt.shape[1] == NT
    row_blk = lambda w: pl.BlockSpec((tm, w), lambda i: (i, 0))
    const = lambda shape: pl.BlockSpec(shape, lambda i: (0, 0), pipeline_mode=pl.Buffered(1))
    hws = tuple(ATT_HEADS * LANES // l.shape[1] for l in lse_list)
    return pl.pallas_call(
        functools.partial(_merge_kernel, hws=hws),
        grid=(m // tm,),
        in_specs=[row_blk(ATT_GROUP_W)] * 3 + [row_blk(l.shape[1]) for l in lse_list]
                 + [row_blk(M_V_WIDTH),
                    pl.BlockSpec((tm, d), lambda i: (i, T_GA // d)),
                    pl.BlockSpec((tm, d), lambda i: (i, T_GB // d)),
                    row_blk(d),
                    const(wpa.shape), const(wpm.shape), const(wo.shape), const((1, d))],
        out_specs=[row_blk(d), row_blk(d)],
        out_shape=[jax.ShapeDtypeStruct((m, d), F32), jax.ShapeDtypeStruct((m, d), BF16)],
        compiler_params=_cparams(("parallel",)),
        name="merge",
    )(*o_list, *lse_list, hm, zt, zt, x, wpa, wpm, wo, g2)


def _ffn_body(h2_ref, wu_ref, wg_ref, cw_ref, cb_ref, wd_ref, x1_ref, y_ref, g_prev2, g_prev1):
    j = pl.program_id(1)
    h2 = h2_ref[...]
    u = jnp.dot(h2, wu_ref[...], preferred_element_type=F32)
    g = jnp.dot(h2, wg_ref[...], preferred_element_type=F32)
    gconv = cb_ref[...] + ((g_prev2(g) * cw_ref[0:1, :] + g_prev1(g) * cw_ref[1:2, :]) + g * cw_ref[2:3, :])
    act = (jax.nn.gelu(gconv) * u).astype(BF16)
    down = jnp.dot(act, wd_ref[...], preferred_element_type=F32)

    @pl.when(j == 0)
    def _():
        y_ref[...] = x1_ref[...] + down

    @pl.when(j > 0)
    def _():
        y_ref[...] += down

    return g


def _ffn_prompt_kernel(h2_ref, wu_ref, wg_ref, cw_ref, cb_ref, wd_ref, x1_ref, y_ref, tail_ref, prev_s,
                       u0_s, g0_s, u1_s, g1_s, *, tiles_per_seq, nf):
    i = pl.program_id(0)
    j = pl.program_id(1)
    tm = h2_ref.shape[0]
    slots = ((u0_s, g0_s), (u1_s, g1_s))

    def up(u_s, g_s):
        h2 = h2_ref[...]
        u_s[...] = jnp.dot(h2, wu_ref[...], preferred_element_type=F32)
        g_s[...] = jnp.dot(h2, wg_ref[...], preferred_element_type=F32)

    def down(u_s, g_s):
        jt = j - 1
        prev = jnp.where(i % tiles_per_seq == 0, 0.0, prev_s[jt])
        p2, p1 = prev[6:7, :], prev[7:8, :]
        g = g_s[...]
        r = lax.broadcasted_iota(jnp.int32, g.shape, 0)
        g_m1 = jnp.where(r == 0, p1, pltpu.roll(g, 1, axis=0))
        g_m2 = jnp.where(r == 0, p2, jnp.where(r == 1, p1, pltpu.roll(g, 2, axis=0)))
        cw = cw_ref[jt]
        gconv = cb_ref[jt] + ((g_m2 * cw[0:1, :] + g_m1 * cw[1:2, :]) + g * cw[2:3, :])
        act = (jax.nn.gelu(gconv) * u_s[...]).astype(BF16)
        y_ref[...] += jnp.dot(act, wd_ref[...], preferred_element_type=F32)
        prev_s[jt] = g[tm - 8:tm, :]

    @pl.when(j == 0)
    def _():
        y_ref[...] = x1_ref[...]
        up(*slots[0])

    for par in range(2):
        @pl.when((j > 0) & (j < nf) & (j % 2 == par))
        def _(par=par):
            up(*slots[par])
            down(*slots[1 - par])

    @pl.when(j == nf)
    def _():
        down(*slots[(nf - 1) % 2])
        tail_ref[...] = prev_s[...]


def _ffn_sample_kernel(h2_ref, wu_ref, wg_ref, cw_ref, cb_ref, wd_ref, x1_ref, b2_ref, b1_ref, y_ref, g_ref):
    g = _ffn_body(h2_ref, wu_ref, wg_ref, cw_ref, cb_ref, wd_ref, x1_ref, y_ref,
                  lambda g: b2_ref[...], lambda g: b1_ref[...])
    g_ref[...] = g


def _ffn_specs(tm, tf, d, nf):
    return [pl.BlockSpec((tm, d), lambda i, j: (i, 0)),
            pl.BlockSpec((d, tf), lambda i, j: (0, j)),
            pl.BlockSpec((d, tf), lambda i, j: (0, nf + j)),
            pl.BlockSpec((CONV_W, tf), lambda i, j: (0, j)),
            pl.BlockSpec((1, tf), lambda i, j: (0, j)),
            pl.BlockSpec((tf, d), lambda i, j: (j, 0)),
            pl.BlockSpec((tm, d), lambda i, j: (i, 0))]


def _ffn_prompt(h2, x1, w_in, conv_w, conv_b, w_down, seq, *, tm, tf):
    m, d = x1.shape
    assert m % tm == 0 and seq % tm == 0 and D_FF % tf == 0 and tm % 8 == 0
    nf = D_FF // tf
    kern = functools.partial(_ffn_prompt_kernel, tiles_per_seq=seq // tm, nf=nf)
    up_t = lambda j: jnp.minimum(j, nf - 1)
    dn_t = lambda j: jnp.maximum(j - 1, 0)
    y, tails = pl.pallas_call(
        kern,
        grid=(m // tm, nf + 1),
        in_specs=[pl.BlockSpec((tm, d), lambda i, j: (i, 0)),
                  pl.BlockSpec((d, tf), lambda i, j: (0, up_t(j))),
                  pl.BlockSpec((d, tf), lambda i, j: (0, nf + up_t(j))),
                  pl.BlockSpec((nf, CONV_W, tf), lambda i, j: (0, 0, 0)),
                  pl.BlockSpec((nf, 1, tf), lambda i, j: (0, 0, 0)),
                  pl.BlockSpec((tf, d), lambda i, j: (dn_t(j), 0)),
                  pl.BlockSpec((tm, d), lambda i, j: (i, 0))],
        out_specs=[pl.BlockSpec((tm, d), lambda i, j: (i, 0)),
                   pl.BlockSpec((None, nf, 8, tf), lambda i, j: (i, 0, 0, 0))],
        out_shape=[jax.ShapeDtypeStruct((m, d), F32), jax.ShapeDtypeStruct((m // tm, nf, 8, tf), F32)],
        scratch_shapes=[pltpu.VMEM((nf, 8, tf), F32)] + [pltpu.VMEM((tm, tf), F32)] * 4,
        compiler_params=_cparams(("arbitrary", "arbitrary")),
        name="ffn_prompt",
    )(h2, w_in, w_in, conv_w.reshape(CONV_W, nf, tf).swapaxes(0, 1), conv_b.reshape(nf, 1, tf), w_down, x1)
    return y, tails.swapaxes(1, 2).reshape(m // tm, 8, D_FF)


def _ffn_sample(h2, x1, w_in, conv_w, conv_b, w_down, conv_buf, *, tf):
    m, d = x1.shape
    nf = D_FF // tf
    buf2d = conv_buf.reshape(m, (CONV_W - 1) * D_FF)
    return pl.pallas_call(
        _ffn_sample_kernel,
        grid=(1, nf),
        in_specs=_ffn_specs(m, tf, d, nf) + [pl.BlockSpec((m, tf), lambda i, j: (0, j)),
                                             pl.BlockSpec((m, tf), lambda i, j: (0, nf + j))],
        out_specs=[pl.BlockSpec((m, d), lambda i, j: (i, 0)), pl.BlockSpec((m, tf), lambda i, j: (0, j))],
        out_shape=[jax.ShapeDtypeStruct((m, d), F32), jax.ShapeDtypeStruct((m, D_FF), F32)],
        compiler_params=_cparams(("arbitrary", "arbitrary")),
        name="ffn_sample",
    )(h2, w_in, w_in, conv_w, conv_b, w_down, x1, buf2d, buf2d)


IN_PROJ_TN = 512
FFN_TF = 512


def _qk_gain_row(q_norm, k_norm):
    reps = N_GROUPS * ATT_HEADS
    return jnp.concatenate([jnp.tile(q_norm, reps), jnp.tile(k_norm, reps), jnp.zeros((NA - A_VA,), F32)])[None, :]


ATTN_PROMPT_TILING = ((4, 4, 1), (1, 4, 1), (1, 2, 2))


def _layer(x_prompt, x_sample, caches, norm_mix, w_in, q_norm, k_norm, b_igate, b_fgate, w_proj_att,
           w_proj_mlstm, w_out, norm_ffn, w_ffn_in, conv_w, conv_b, w_ffn_down):
    batch, seq, d = x_prompt.shape
    bd = x_sample.shape[0]
    assert x_sample.shape[1] == 1 and d == D_MODEL
    (ck1, cv1, ck2, cv2, ck3, cv3, st_c, st_n, st_m, st_conv) = caches

    wt_in = jnp.swapaxes(w_in, 0, 1)
    qk_gain = _qk_gain_row(q_norm, k_norm)
    g1 = norm_mix[None, :]
    g2 = norm_ffn[None, :]
    gate_bias = jnp.concatenate([b_igate, b_fgate, jnp.zeros((LANES - 2 * M_HEADS,), F32)])[None, :]
    wpa = w_proj_att.astype(BF16)
    wpm = w_proj_mlstm.astype(BF16)
    wo = w_out.astype(BF16)
    w_ff = w_ffn_in.astype(BF16)
    w_dn = w_ffn_down.astype(BF16)
    cb = conv_b[None, :]

    xp = x_prompt.reshape(batch * seq, d)
    xs = x_sample.reshape(bd, d)

    h_all = _rmsnorm(xp, xs, g1, tm=2048, ts=512)
    (za_p, zb_p, zt_p, zif_p), (za_s, zb_s, zt_s, zif_s) = _in_proj(h_all, wt_in, qk_gain, tn=IN_PROJ_TN, ms=bd)
    o_p, lse_p = zip(*[_attn_prompt_group(za_p, batch, seq, gi, dil, nsub=nsub, hw=hw, rb=rb)
                       for gi, ((_, dil), (nsub, hw, rb)) in enumerate(zip(DIL_PATTERNS, ATTN_PROMPT_TILING))])
    hm_p, p_c, p_n, p_m = _mlstm_prompt(zif_p, zb_p, gate_bias, batch, seq, chunk=256)
    x1_p, h2_p = _merge(o_p, lse_p, hm_p, zt_p, xp, wpa, wpm, wo, g2, tm=256)
    y_p, tails = _ffn_prompt(h2_p, x1_p, w_ff, conv_w, cb, w_dn, seq, tm=512, tf=FFN_TF)

    za_p3 = za_p.reshape(batch, seq, NA)
    p_kv = []
    for gi, (win, _) in enumerate(DIL_PATTERNS):
        keep = min(win, seq)
        for col in (A_KA, A_VA):
            lo = col + gi * ATT_GROUP_W
            p_kv.append(za_p3[:, seq - keep:, lo:lo + ATT_GROUP_W].reshape(batch, keep, ATT_HEADS, ATT_HEAD_DIM))
    tiles_per_seq = seq // 512
    p_conv = tails.reshape(batch, tiles_per_seq, 8, D_FF)[:, -1, 8 - (CONV_W - 1):, :]

    o_s, lse_s = _attn_sample(za_s, [(ck1, cv1), (ck2, cv2), (ck3, cv3)], bb=4)
    hm_s, s_c, s_n, s_m = _mlstm_sample(zif_s, zb_s, gate_bias, st_c, st_n, st_m, bb=2)
    x1_s, h2_s = _merge(o_s, lse_s, hm_s, zt_s, xs, wpa, wpm, wo, g2, tm=bd)
    y_s, g_s = _ffn_sample(h2_s, x1_s, w_ff, conv_w, cb, w_dn, st_conv, tf=FFN_TF)

    s_kv = []
    for gi in range(N_GROUPS):
        for col in (A_KA, A_VA):
            lo = col + gi * ATT_GROUP_W
            s_kv.append(za_s[:, lo:lo + ATT_GROUP_W].reshape(bd, 1, ATT_HEADS, ATT_HEAD_DIM))
    s_conv = jnp.stack([st_conv[:, 1, :], g_s], axis=1)

    p_state = p_kv + [p_c, p_n, p_m, p_conv]
    s_state = s_kv + [s_c, s_n, s_m, s_conv]
    return y_p.reshape(batch, seq, d), y_s.reshape(bd, 1, d), p_state, s_state


def kernel(x_prompt, x_sample, cache_k_w128, cache_v_w128, cache_k_w512, cache_v_w512, cache_k_w2048,
           cache_v_w2048, state_mlstm_C, state_mlstm_n, state_mlstm_m, state_ffn_conv, norm_mix, w_in, q_norm,
           k_norm, b_igate, b_fgate, w_proj_att, w_proj_mlstm, w_out, norm_ffn, w_ffn_in, conv_w, conv_b,
           w_ffn_down):
    assert norm_mix.shape[0] == 1
    caches = [c[0] for c in (cache_k_w128, cache_v_w128, cache_k_w512, cache_v_w512, cache_k_w2048,
                             cache_v_w2048, state_mlstm_C, state_mlstm_n, state_mlstm_m, state_ffn_conv)]
    weights = [w[0] for w in (norm_mix, w_in, q_norm, k_norm, b_igate, b_fgate, w_proj_att, w_proj_mlstm,
                              w_out, norm_ffn, w_ffn_in, conv_w, conv_b, w_ffn_down)]
    y_p, y_s, p_state, s_state = _layer(x_prompt, x_sample, caches, *weights)
    return (y_p, y_s, *[a[None] for a in p_state], *[a[None] for a in s_state])
```

```python
import functools

import jax
import jax.numpy as jnp
from jax import lax
from jax.experimental import pallas as pl
from jax.experimental.pallas import tpu as pltpu

F32 = jnp.float32
BF16 = jnp.bfloat16

RMS_EPS = 1e-6
NEG_INF = -1e30
LANES = 128
VMEM_LIMIT = 56 * 1024 * 1024

D_MODEL = 2048
DIL_PATTERNS = ((128, 1), (512, 4), (2048, 16))
N_GROUPS = 3
ATT_HEADS = 4
ATT_HEAD_DIM = 128
ATT_SPAN = 128
ATT_SCALE = ATT_HEAD_DIM ** -0.5
ATT_GROUP_W = ATT_HEADS * ATT_HEAD_DIM
ATT_WIDTH = N_GROUPS * ATT_GROUP_W
M_HEADS = 4
M_QK_DIM = D_MODEL // (2 * M_HEADS)
M_V_DIM = D_MODEL // M_HEADS
M_QK_WIDTH = M_HEADS * M_QK_DIM
M_V_WIDTH = M_HEADS * M_V_DIM
M_K_SCALE = M_QK_DIM ** -0.5
D_FF = ((8 * D_MODEL // 3 + 255) // 256) * 256
CONV_W = 3
SPLIT_SIZES = (ATT_WIDTH, ATT_WIDTH, ATT_WIDTH, M_QK_WIDTH, M_QK_WIDTH, M_V_WIDTH, M_V_WIDTH,
               M_HEADS, M_HEADS, D_MODEL, D_MODEL)

IN_COLS = sum(SPLIT_SIZES)
A_QA = 0
A_KA = A_QA + ATT_WIDTH
A_VA = A_KA + ATT_WIDTH
NA = A_VA + ATT_WIDTH
B_QM = 0
B_KM = B_QM + M_QK_WIDTH
B_VM = B_KM + M_QK_WIDTH
B_OM = B_VM + M_V_WIDTH
NB = B_OM + M_V_WIDTH
N_IF = 2 * M_HEADS
T_GA = 0
T_GB = T_GA + D_MODEL
NT = T_GB + D_MODEL
assert NA + NB + N_IF + NT == IN_COLS


def _cparams(sem):
    return pltpu.CompilerParams(dimension_semantics=sem, vmem_limit_bytes=VMEM_LIMIT)


def _log_sigmoid(x):
    return jnp.minimum(x, 0.0) - jnp.log(1.0 + jnp.exp(-jnp.abs(x)))


def _rmsnorm_kernel(x_ref, xs_ref, g_ref, h_ref, *, nsub):
    j = pl.program_id(1)
    ts = x_ref.shape[0]

    def norm(x):
        ms = jnp.mean(x * x, axis=-1, keepdims=True)
        return (x * lax.rsqrt(ms + RMS_EPS) * g_ref[...]).astype(h_ref.dtype)

    @pl.when(j < nsub)
    def _():
        h_ref[pl.ds(pl.multiple_of(j * ts, ts), ts), :] = norm(x_ref[...])

    @pl.when(j == nsub)
    def _():
        h_ref[nsub * ts:, :] = norm(xs_ref[...])


def _rmsnorm(x, xs, gain, *, tm, ts):
    m, d = x.shape
    ms = xs.shape[0]
    assert m % tm == 0 and tm % ts == 0
    nsub = tm // ts
    last = m // ts - 1
    return pl.pallas_call(
        functools.partial(_rmsnorm_kernel, nsub=nsub),
        grid=(m // tm, nsub + 1),
        in_specs=[pl.BlockSpec((ts, d), lambda i, j: (jnp.minimum(i * nsub + j, last), 0)),
                  pl.BlockSpec((ms, d), lambda i, j: (0, 0)),
                  pl.BlockSpec((1, d), lambda i, j: (0, 0))],
        out_specs=pl.BlockSpec((None, tm + ms, d), lambda i, j: (i, 0, 0)),
        out_shape=jax.ShapeDtypeStruct((m // tm, tm + ms, d), BF16),
        compiler_params=_cparams(("parallel", "arbitrary")),
        name="rmsnorm",
    )(x, xs, gain)


def _in_proj_kernel(h_ref, w_ref, wn_ref, wif_ref, qkg_ref, za_ref, zb_ref, zt_ref, zif_ref,
                    sa_ref, sb_ref, st_ref, sif_ref, *, n_norm, na, nb, nt):
    j = pl.program_id(1)
    tp = za_ref.shape[0]

    def z_of(w):
        return lax.dot_general(h_ref[...], w.astype(BF16), (((1,), (1,)), ((), ())), preferred_element_type=F32)

    @pl.when(j < n_norm)
    def _():
        z = z_of(w_ref[...])
        for c in range(za_ref.shape[1] // LANES):
            sl = slice(c * LANES, (c + 1) * LANES)
            zc = z[:, sl]
            ms = jnp.mean(zc * zc, axis=-1, keepdims=True)
            zn = zc * lax.rsqrt(ms + RMS_EPS) * qkg_ref[:, sl]
            za_ref[:, sl] = zn[:tp]
            sa_ref[j, :, sl] = zn[tp:]

    @pl.when((j >= n_norm) & (j < na))
    def _():
        z = z_of(w_ref[...])
        za_ref[...] = z[:tp]
        sa_ref[j] = z[tp:]

    @pl.when((j >= na) & (j < na + nb))
    def _():
        z = z_of(w_ref[...])
        zb_ref[...] = z[:tp].astype(zb_ref.dtype)
        sb_ref[j - na] = z[tp:]

    @pl.when((j >= na + nb) & (j < na + nb + nt))
    def _():
        z = z_of(jnp.concatenate([w_ref[N_IF:, :], wn_ref[...]], axis=0))
        zt_ref[...] = z[:tp].astype(zt_ref.dtype)
        st_ref[j - na - nb] = z[tp:]

    @pl.when(j == na + nb + nt)
    def _():
        z = z_of(jnp.concatenate([wif_ref[...], jnp.zeros((LANES - N_IF, wif_ref.shape[1]), F32)], axis=0))
        zif_ref[...] = z[:tp]
        sif_ref[...] = z[tp:]


def _in_proj(h, wt, qk_gain, *, tn, ms):
    ni, th, d = h.shape
    tm = th - ms
    m = ni * tm
    assert wt.shape == (IN_COLS, d) and N_IF == 8
    assert NA % tn == 0 and NB % tn == 0 and NT % tn == 0 and A_VA % tn == 0 and tn % N_IF == 0
    na, nb, nt = NA // tn, NB // tn, NT // tn
    nw = na + nb + nt
    g_if = (NA + NB) // N_IF
    g_tn = tn // N_IF
    kern = functools.partial(_in_proj_kernel, n_norm=A_VA // tn, na=na, nb=nb, nt=nt)
    col_a = lambda j: jnp.minimum(j, na - 1)
    col_b = lambda j: jnp.clip(j - na, 0, nb - 1)
    col_t = lambda j: jnp.clip(j - na - nb, 0, nt - 1)
    out_specs = [pl.BlockSpec((None, tm, tn), lambda i, j: (i, 0, col_a(j))),
                 pl.BlockSpec((None, tm, tn), lambda i, j: (i, 0, col_b(j))),
                 pl.BlockSpec((None, tm, tn), lambda i, j: (i, 0, col_t(j))),
                 pl.BlockSpec((None, tm, LANES), lambda i, j: (i, 0, 0))]
    out_shape = [jax.ShapeDtypeStruct((ni, tm, w), dt)
                 for w, dt in zip((NA, NB, NT, LANES), (F32, BF16, BF16, F32))]
    for n_tiles in (na, nb, nt):
        out_specs.append(pl.BlockSpec((None, n_tiles, ms, tn), lambda i, j: (i, 0, 0, 0)))
        out_shape.append(jax.ShapeDtypeStruct((ni, n_tiles, ms, tn), F32))
    out_specs.append(pl.BlockSpec((None, ms, LANES), lambda i, j: (i, 0, 0)))
    out_shape.append(jax.ShapeDtypeStruct((ni, ms, LANES), F32))
    outs = pl.pallas_call(
        kern,
        grid=(ni, nw + 1),
        in_specs=[
            pl.BlockSpec((None, th, d), lambda i, j: (i, 0, 0)),
            pl.BlockSpec((tn, d), lambda i, j: (jnp.minimum(j, nw - 1), 0)),
            pl.BlockSpec((N_IF, d), lambda i, j: (g_if + g_tn * (jnp.clip(j, na + nb, nw - 1) - (na + nb) + 1), 0)),
            pl.BlockSpec((N_IF, d), lambda i, j: (g_if, 0)),
            pl.BlockSpec((1, tn), lambda i, j: (0, col_a(j))),
        ],
        out_specs=out_specs,
        out_shape=out_shape,
        compiler_params=_cparams(("parallel", "arbitrary")),
        name="in_proj",
    )(h, wt, wt, wt, qk_gain)
    sample = [o[0].swapaxes(0, 1).reshape(ms, -1) for o in outs[4:7]] + [outs[7][0]]
    return [o.reshape(m, o.shape[2]) for o in outs[:4]], sample


def _attn_prompt_kernel(*refs, dil, nsub, hw, rb):
    q_refs, kp_refs, kc_refs, vp_refs, vc_refs = [refs[t * hw:(t + 1) * hw] for t in range(5)]
    o_ref, lse_ref = refs[5 * hw:5 * hw + 2]
    o_refs = refs[5 * hw + 2:]
    c = pl.program_id(1)
    blk = ATT_SPAN
    step = blk * dil
    qi = lax.broadcasted_iota(jnp.int32, (blk, 2 * blk), 0)
    kj = lax.broadcasted_iota(jnp.int32, (blk, 2 * blk), 1)
    band = (kj >= qi) & (kj <= qi + blk)
    first_bias = jnp.where((kj < blk) & (c == 0), NEG_INF, 0.0)
    lane = lax.broadcasted_iota(jnp.int32, (blk, LANES), 1)

    def rows(base, r):
        return pl.ds(base + r, blk) if dil == 1 else pl.ds(base + r, blk, stride=dil)

    def group(r0, s):
        base = s * step
        ids = [(r0 + r, n) for r in range(rb) for n in range(hw)]

        def stacked(cur_refs, prev_refs):
            parts = []
            for r, n in ids:
                cur = cur_refs[n][rows(base, r), :]
                if prev_refs is None:
                    parts.append(cur)
                else:
                    prv = prev_refs[n][rows(0, r), :] if s == 0 else cur_refs[n][rows(base - step, r), :]
                    parts.append(jnp.concatenate([prv, cur], axis=0))
            return jnp.stack(parts).astype(BF16)

        q = stacked(q_refs, None)
        k = stacked(kc_refs, kp_refs)
        v = stacked(vc_refs, vp_refs)
        sc = jnp.einsum('bqe,bke->bqk', q, k, preferred_element_type=F32) * ATT_SCALE
        if s == 0:
            sc = sc + first_bias
        sc = jnp.where(band, sc, NEG_INF)
        m = jnp.max(sc, axis=-1, keepdims=True)
        p = jnp.exp(sc - m)
        l = jnp.sum(p, axis=-1, keepdims=True)
        o = jnp.einsum('bqk,bke->bqe', p.astype(BF16), v, preferred_element_type=F32) / l
        lse = m + jnp.log(l)
        for r in range(rb):
            lse_tile = jnp.zeros((blk, LANES), F32)
            for n in range(hw):
                b = r * hw + n
                o_refs[n][rows(base, r0 + r), :] = o[b]
                lse_tile = jnp.where(lane == n, lse[b], lse_tile)
            lse_ref[rows(base, r0 + r), :] = lse_tile

    for s in range(nsub):
        if dil == rb:
            group(0, s)
        else:
            def body(it, carry, s=s):
                group(it * rb, s)
                return carry
            lax.fori_loop(0, dil // rb, body, 0)

    for n in range(hw):
        o_ref[:, n * ATT_HEAD_DIM:(n + 1) * ATT_HEAD_DIM] = o_refs[n][...]


def _attn_prompt_group(za, batch, seq, gi, dil, *, nsub, hw, rb):
    step = ATT_SPAN * dil
    tc = nsub * step
    assert seq % tc == 0 and ATT_HEADS % hw == 0 and dil % rb == 0
    nh = ATT_HEADS // hw
    e = ATT_HEAD_DIM
    z3 = za.reshape(batch, seq, NA)
    qc, kc, vc = [(col + gi * ATT_GROUP_W) // e for col in (A_QA, A_KA, A_VA)]

    def cur(col):
        return [pl.BlockSpec((None, tc, e), lambda b, c, h, n=n: (b, c, col + h * hw + n)) for n in range(hw)]

    def prev(col):
        return [pl.BlockSpec((None, step, e), lambda b, c, h, n=n: (b, jnp.maximum(c * nsub - 1, 0), col + h * hw + n))
                for n in range(hw)]

    kern = functools.partial(_attn_prompt_kernel, dil=dil, nsub=nsub, hw=hw, rb=rb)
    o, lse = pl.pallas_call(
        kern,
        grid=(batch, seq // tc, nh),
        in_specs=cur(qc) + prev(kc) + cur(kc) + prev(vc) + cur(vc),
        out_specs=[pl.BlockSpec((None, tc, hw * e), lambda b, c, h: (b, c, h)),
                   pl.BlockSpec((None, tc, LANES), lambda b, c, h: (b, c, h))],
        out_shape=[jax.ShapeDtypeStruct((batch, seq, ATT_GROUP_W), F32),
                   jax.ShapeDtypeStruct((batch, seq, nh * LANES), F32)],
        scratch_shapes=[pltpu.VMEM((tc, e), F32)] * hw,
        compiler_params=_cparams(("parallel", "parallel", "parallel")),
        name=f"attn_prompt_g{gi}",
    )(*([z3] * (5 * hw)))
    return o.reshape(batch * seq, ATT_GROUP_W), lse.reshape(batch * seq, nh * LANES)


def _attn_sample_kernel(z_ref, k1, v1, k2, v2, k3, v3, o1, o2, o3, l1, l2, l3):
    bb = z_ref.shape[0]
    bufs = ((k1, v1, o1, l1), (k2, v2, o2, l2), (k3, v3, o3, l3))
    lane = lax.broadcasted_iota(jnp.int32, (1, LANES), 1)

    def heads(b, col):
        return jnp.concatenate([z_ref[b, :, col + n * ATT_HEAD_DIM:col + (n + 1) * ATT_HEAD_DIM]
                                for n in range(ATT_HEADS)], axis=0)

    def body(b, carry):
        for gi, (k_ref, v_ref, o_ref, l_ref) in enumerate(bufs):
            q = heads(b, A_QA + gi * ATT_GROUP_W)
            k_new = heads(b, A_KA + gi * ATT_GROUP_W)
            v_new = heads(b, A_VA + gi * ATT_GROUP_W)
            kb = k_ref[b].reshape(ATT_SPAN // 2, 2 * ATT_HEADS, ATT_HEAD_DIM)
            vb = v_ref[b].reshape(ATT_SPAN // 2, 2 * ATT_HEADS, ATT_HEAD_DIM)
            q2 = jnp.concatenate([q, q], axis=0)
            s = jnp.sum(kb * q2[None], axis=-1, keepdims=True) * ATT_SCALE
            s_new = jnp.sum(k_new * q, axis=-1, keepdims=True) * ATT_SCALE
            m2 = jnp.max(s, axis=0)
            m = jnp.maximum(jnp.maximum(m2[:ATT_HEADS], m2[ATT_HEADS:]), s_new)
            p = jnp.exp(s - jnp.concatenate([m, m], axis=0)[None])
            p_new = jnp.exp(s_new - m)
            l2 = jnp.sum(p, axis=0)
            l = l2[:ATT_HEADS] + l2[ATT_HEADS:] + p_new
            o2 = jnp.sum(p * vb, axis=0)
            o = (o2[:ATT_HEADS] + o2[ATT_HEADS:] + p_new * v_new) / l
            lse = m + jnp.log(l)
            lse_row = jnp.zeros((1, LANES), F32)
            for n in range(ATT_HEADS):
                o_ref[b, :, n * ATT_HEAD_DIM:(n + 1) * ATT_HEAD_DIM] = o[n:n + 1, :]
                lse_row = jnp.where(lane == n, lse[n:n + 1, :], lse_row)
            l_ref[b] = lse_row
        return carry

    lax.fori_loop(0, bb, body, 0)


def _attn_sample(za_s, caches, *, bb):
    bd = za_s.shape[0]
    assert bd % bb == 0
    ins, in_specs = [za_s.reshape(bd, 1, NA)], [pl.BlockSpec((bb, 1, NA), lambda i: (i, 0, 0))]
    for (k_buf, v_buf), (win, dil) in zip(caches, DIL_PATTERNS):
        assert k_buf.shape[1:] == (ATT_SPAN * dil, ATT_HEADS, ATT_HEAD_DIM)
        for buf in (k_buf, v_buf):
            ins.append(buf.reshape(bd, ATT_SPAN, dil, ATT_HEADS, ATT_HEAD_DIM))
            in_specs.append(pl.BlockSpec((bb, ATT_SPAN, None, ATT_HEADS, ATT_HEAD_DIM),
                                         lambda i: (i, 0, 0, 0, 0)))
    outs = pl.pallas_call(
        _attn_sample_kernel,
        grid=(bd // bb,),
        in_specs=in_specs,
        out_specs=[pl.BlockSpec((bb, 1, ATT_GROUP_W), lambda i: (i, 0, 0))] * 3
                  + [pl.BlockSpec((bb, 1, LANES), lambda i: (i, 0, 0))] * 3,
        out_shape=[jax.ShapeDtypeStruct((bd, 1, ATT_GROUP_W), F32)] * 3
                  + [jax.ShapeDtypeStruct((bd, 1, LANES), F32)] * 3,
        compiler_params=_cparams(("parallel",)),
        name="attn_sample",
    )(*ins)
    return [o[:, 0, :] for o in outs[:3]], [l[:, 0, :] for l in outs[3:]]


def _mlstm_prompt_kernel(q_ref, k_ref, v_ref, om_ref, g_ref, bias_ref, h_ref, c_out, n_out, m_out,
                         c_s, n_s, m_s):
    hd = pl.program_id(1)
    ci = pl.program_id(2)
    L = q_ref.shape[0]

    @pl.when(ci == 0)
    def _():
        c_s[...] = jnp.zeros_like(c_s)
        n_s[...] = jnp.zeros_like(n_s)
        m_s[...] = jnp.zeros_like(m_s)

    gates = g_ref[...] + bias_ref[...]
    lane = lax.broadcasted_iota(jnp.int32, (L, LANES), 1)
    li_col = jnp.sum(jnp.where(lane == hd, gates, 0.0), axis=1, keepdims=True)
    lf_col = _log_sigmoid(jnp.sum(jnp.where(lane == hd + M_HEADS, gates, 0.0), axis=1, keepdims=True))
    gates_t = gates.T
    sub = lax.broadcasted_iota(jnp.int32, (LANES, L), 0)
    li_row = jnp.sum(jnp.where(sub == hd, gates_t, 0.0), axis=0, keepdims=True)
    lf_row = _log_sigmoid(jnp.sum(jnp.where(sub == hd + M_HEADS, gates_t, 0.0), axis=0, keepdims=True))

    ti = lax.broadcasted_iota(jnp.int32, (L, L), 0)
    si = lax.broadcasted_iota(jnp.int32, (L, L), 1)
    causal = si <= ti
    b_col = jnp.sum(jnp.where(causal, lf_row, 0.0), axis=1, keepdims=True)
    b_row = jnp.sum(jnp.where(ti <= si, lf_col, 0.0), axis=0, keepdims=True)
    b_end = jnp.sum(lf_row, axis=1, keepdims=True)

    m_prev = m_s[...]
    dmat = jnp.where(causal, b_col - b_row + li_row, NEG_INF)
    inter = b_col + m_prev
    mt = jnp.maximum(inter, jnp.max(dmat, axis=1, keepdims=True))

    qb = q_ref[...]
    q = qb.astype(F32)
    k = k_ref[...].astype(F32) * M_K_SCALE
    vb = v_ref[...]
    qk = lax.dot_general(qb, k.astype(BF16), (((1,), (1,)), ((), ())), preferred_element_type=F32)
    a = jnp.exp(dmat - mt) * qk
    w_inter = jnp.exp(inter - mt)
    num = (jnp.dot(a.astype(BF16), vb, preferred_element_type=F32)
           + w_inter * jnp.dot(qb, c_s[...].astype(BF16), preferred_element_type=F32))
    den = jnp.sum(a, axis=1, keepdims=True) + w_inter * jnp.sum(q * n_s[...], axis=1, keepdims=True)
    h = num / jnp.maximum(jnp.abs(den), jnp.exp(-mt))
    h_ref[...] = (jax.nn.sigmoid(om_ref[...].astype(F32)) * h).astype(h_ref.dtype)

    g_col = b_end - b_col + li_col
    g_row = b_end - b_row + li_row
    m_new = jnp.maximum(b_end + m_prev, jnp.max(g_row, axis=1, keepdims=True))
    decay = jnp.exp(b_end + m_prev - m_new)
    kw = jnp.exp(g_col - m_new) * k
    c_s[...] = decay * c_s[...] + jnp.dot(kw.T.astype(BF16), vb, preferred_element_type=F32)
    n_s[...] = decay * n_s[...] + jnp.sum(kw, axis=0, keepdims=True)
    m_s[...] = m_new

    @pl.when(ci == pl.num_programs(2) - 1)
    def _():
        c_out[...] = c_s[...]
        n_out[...] = n_s[...]
        m_out[...] = jnp.broadcast_to(m_s[...], m_out.shape)


def _mlstm_prompt(zif, zb, gate_bias, batch, seq, *, chunk):
    assert seq % chunk == 0
    nc = seq // chunk
    zif3 = zif.reshape(batch, seq, LANES)
    zb3 = zb.reshape(batch, seq, NB)
    qk_blk = lambda col: pl.BlockSpec((None, chunk, M_QK_DIM), lambda b, h, c: (b, c, col // M_QK_DIM + h))
    v_blk = lambda col: pl.BlockSpec((None, chunk, M_V_DIM), lambda b, h, c: (b, c, col // M_V_DIM + h))
    hm, c1, n1, m1 = pl.pallas_call(
        _mlstm_prompt_kernel,
        grid=(batch, M_HEADS, nc),
        in_specs=[qk_blk(B_QM), qk_blk(B_KM), v_blk(B_VM), v_blk(B_OM),
                  pl.BlockSpec((None, chunk, LANES), lambda b, h, c: (b, c, 0)),
                  pl.BlockSpec((1, LANES), lambda b, h, c: (0, 0))],
        out_specs=[pl.BlockSpec((None, chunk, M_V_DIM), lambda b, h, c: (b, c, h)),
                   pl.BlockSpec((None, None, M_QK_DIM, M_V_DIM), lambda b, h, c: (b, h, 0, 0)),
                   pl.BlockSpec((None, None, 1, M_QK_DIM), lambda b, h, c: (b, h, 0, 0)),
                   pl.BlockSpec((None, None, 1, LANES), lambda b, h, c: (b, h, 0, 0))],
        out_shape=[jax.ShapeDtypeStruct((batch, seq, M_V_WIDTH), BF16),
                   jax.ShapeDtypeStruct((batch, M_HEADS, M_QK_DIM, M_V_DIM), F32),
                   jax.ShapeDtypeStruct((batch, M_HEADS, 1, M_QK_DIM), F32),
                   jax.ShapeDtypeStruct((batch, M_HEADS, 1, LANES), F32)],
        scratch_shapes=[pltpu.VMEM((M_QK_DIM, M_V_DIM), F32), pltpu.VMEM((1, M_QK_DIM), F32),
                        pltpu.VMEM((1, 1), F32)],
        compiler_params=_cparams(("parallel", "parallel", "arbitrary")),
        name="mlstm_prompt",
    )(zb3, zb3, zb3, zb3, zif3, gate_bias)
    return hm.reshape(batch * seq, M_V_WIDTH), c1, n1[:, :, 0, :], m1[:, :, 0, 0]


def _row_to_col(row):
    n = row.shape[1]
    return jnp.broadcast_to(row, (LANES, n)).T[:, 0:1]


def _mlstm_sample_kernel(zif_ref, zb_ref, bias_ref, c_ref, n_ref, m_ref, h_ref, c_out, n_out, m_out):
    for b in range(zif_ref.shape[0]):
        _mlstm_sample_one(zif_ref.at[b], zb_ref.at[b], bias_ref, c_ref.at[b], n_ref.at[b], m_ref.at[b],
                          h_ref.at[b], c_out.at[b], n_out.at[b], m_out.at[b])


def _mlstm_sample_one(zif_ref, zb_ref, bias_ref, c_ref, n_ref, m_ref, h_ref, c_out, n_out, m_out):
    gates = zif_ref[...] + bias_ref[...]
    lane = lax.broadcasted_iota(jnp.int32, (1, LANES), 1)
    m_row = jnp.zeros((1, LANES), F32)
    for h in range(M_HEADS):
        q = zb_ref[:, B_QM + h * M_QK_DIM:B_QM + (h + 1) * M_QK_DIM].astype(F32)
        k = zb_ref[:, B_KM + h * M_QK_DIM:B_KM + (h + 1) * M_QK_DIM].astype(F32) * M_K_SCALE
        v = zb_ref[:, B_VM + h * M_V_DIM:B_VM + (h + 1) * M_V_DIM].astype(F32)
        om = zb_ref[:, B_OM + h * M_V_DIM:B_OM + (h + 1) * M_V_DIM].astype(F32)
        li = gates[:, h:h + 1]
        lf = _log_sigmoid(gates[:, M_HEADS + h:M_HEADS + h + 1])
        m0 = m_ref[:, h:h + 1]
        c0 = c_ref[h]
        n0 = n_ref[h:h + 1, :]
        inter = lf + m0
        mt = jnp.maximum(inter, li)
        a = jnp.exp(li - mt) * jnp.sum(q * k, axis=1, keepdims=True)
        w_inter = jnp.exp(inter - mt)
        q_c = jnp.sum(_row_to_col(q) * c0, axis=0, keepdims=True)
        num = a * v + w_inter * q_c
        den = a + w_inter * jnp.sum(q * n0, axis=1, keepdims=True)
        hv = num / jnp.maximum(jnp.abs(den), jnp.exp(-mt))
        h_ref[:, h * M_V_DIM:(h + 1) * M_V_DIM] = (jax.nn.sigmoid(om) * hv).astype(h_ref.dtype)
        m_new = jnp.maximum(inter, li)
        decay = jnp.exp(inter - m_new)
        ws = jnp.exp(li - m_new)
        c_out[h] = decay * c0 + _row_to_col(ws * k) * v
        n_out[h:h + 1, :] = decay * n0 + ws * k
        m_row = jnp.where(lane == h, m_new, m_row)
    m_out[...] = m_row


def _mlstm_sample(zif_s, zb_s, gate_bias, c0, n0, m0, *, bb):
    bd = zif_s.shape[0]
    assert bd % bb == 0
    hm, c1, n1, m1 = pl.pallas_call(
        _mlstm_sample_kernel,
        grid=(bd // bb,),
        in_specs=[pl.BlockSpec((bb, 1, LANES), lambda b: (b, 0, 0)),
                  pl.BlockSpec((bb, 1, NB), lambda b: (b, 0, 0)),
                  pl.BlockSpec((1, LANES), lambda b: (0, 0)),
                  pl.BlockSpec((bb, M_HEADS, M_QK_DIM, M_V_DIM), lambda b: (b, 0, 0, 0)),
                  pl.BlockSpec((bb, M_HEADS, M_QK_DIM), lambda b: (b, 0, 0)),
                  pl.BlockSpec((bb, 1, M_HEADS), lambda b: (b, 0, 0))],
        out_specs=[pl.BlockSpec((bb, 1, M_V_WIDTH), lambda b: (b, 0, 0)),
                   pl.BlockSpec((bb, M_HEADS, M_QK_DIM, M_V_DIM), lambda b: (b, 0, 0, 0)),
                   pl.BlockSpec((bb, M_HEADS, M_QK_DIM), lambda b: (b, 0, 0)),
                   pl.BlockSpec((bb, 1, LANES), lambda b: (b, 0, 0))],
        out_shape=[jax.ShapeDtypeStruct((bd, 1, M_V_WIDTH), BF16),
                   jax.ShapeDtypeStruct((bd, M_HEADS, M_QK_DIM, M_V_DIM), F32),
                   jax.ShapeDtypeStruct((bd, M_HEADS, M_QK_DIM), F32),
                   jax.ShapeDtypeStruct((bd, 1, LANES), F32)],
        compiler_params=_cparams(("parallel",)),
        name="mlstm_sample",
    )(zif_s.reshape(bd, 1, LANES), zb_s.reshape(bd, 1, NB), gate_bias, c0, n0, m0.reshape(bd, 1, M_HEADS))
    return hm.reshape(bd, M_V_WIDTH), c1, n1, m1[:, 0, :M_HEADS]


def _merge_kernel(o1, o2, o3, l1, l2, l3, hm_ref, ga_ref, gb_ref, x_ref, wpa_ref, wpm_ref, wo_ref, g2_ref,
                  x1_ref, h2_ref, *, hws):
    os_ = (o1, o2, o3)
    parts = []
    for n in range(ATT_HEADS):
        sl = slice(n * ATT_HEAD_DIM, (n + 1) * ATT_HEAD_DIM)
        ls = []
        for l_ref, hw in zip((l1, l2, l3), hws):
            ln = (n // hw) * LANES + n % hw
            ls.append(l_ref[:, ln:ln + 1])
        mx = jnp.maximum(jnp.maximum(ls[0], ls[1]), ls[2])
        es = [jnp.exp(l - mx) for l in ls]
        tot = es[0] + es[1] + es[2]
        parts.append((es[0] / tot) * o1[:, sl] + (es[1] / tot) * o2[:, sl] + (es[2] / tot) * o3[:, sl])
    att = jnp.concatenate(parts, axis=1).astype(BF16)
    pa = jnp.dot(att, wpa_ref[...], preferred_element_type=F32)
    pm = jnp.dot(hm_ref[...], wpm_ref[...], preferred_element_type=F32)
    merged = (jax.nn.sigmoid(ga_ref[...].astype(F32)) * pa + jax.nn.sigmoid(gb_ref[...].astype(F32)) * pm)
    x1 = x_ref[...] + jnp.dot(merged.astype(BF16), wo_ref[...], preferred_element_type=F32)
    x1_ref[...] = x1
    ms = jnp.mean(x1 * x1, axis=-1, keepdims=True)
    h2_ref[...] = (x1 * lax.rsqrt(ms + RMS_EPS) * g2_ref[...]).astype(BF16)


def _merge(o_list, lse_list, hm, zt, x, wpa, wpm, wo, g2, *, tm):
    m, d = x.shape
    assert m % tm == 0 and zt.shape[1] == NT
    row_blk = lambda w: pl.BlockSpec((tm, w), lambda i: (i, 0))
    const = lambda shape: pl.BlockSpec(shape, lambda i: (0, 0), pipeline_mode=pl.Buffered(1))
    hws = tuple(ATT_HEADS * LANES // l.shape[1] for l in lse_list)
    return pl.pallas_call(
        functools.partial(_merge_kernel, hws=hws),
        grid=(m // tm,),
        in_specs=[row_blk(ATT_GROUP_W)] * 3 + [row_blk(l.shape[1]) for l in lse_list]
                 + [row_blk(M_V_WIDTH),
                    pl.BlockSpec((tm, d), lambda i: (i, T_GA // d)),
                    pl.BlockSpec((tm, d), lambda i: (i, T_GB // d)),
                    row_blk(d),
                    const(wpa.shape), const(wpm.shape), const(wo.shape), const((1, d))],
        out_specs=[row_blk(d), row_blk(d)],
        out_shape=[jax.ShapeDtypeStruct((m, d), F32), jax.ShapeDtypeStruct((m, d), BF16)],
        compiler_params=_cparams(("parallel",)),
        name="merge",
    )(*o_list, *lse_list, hm, zt, zt, x, wpa, wpm, wo, g2)


def _ffn_body(h2_ref, wu_ref, wg_ref, cw_ref, cb_ref, wd_ref, x1_ref, y_ref, g_prev2, g_prev1):
    j = pl.program_id(1)
    h2 = h2_ref[...]
    u = jnp.dot(h2, wu_ref[...], preferred_element_type=F32)
    g = jnp.dot(h2, wg_ref[...], preferred_element_type=F32)
    gconv = cb_ref[...] + ((g_prev2(g) * cw_ref[0:1, :] + g_prev1(g) * cw_ref[1:2, :]) + g * cw_ref[2:3, :])
    act = (jax.nn.gelu(gconv) * u).astype(BF16)
    down = jnp.dot(act, wd_ref[...], preferred_element_type=F32)

    @pl.when(j == 0)
    def _():
        y_ref[...] = x1_ref[...] + down

    @pl.when(j > 0)
    def _():
        y_ref[...] += down

    return g


def _ffn_prompt_kernel(h2_ref, wu_ref, wg_ref, cw_ref, cb_ref, wd_ref, x1_ref, y_ref, tail_ref, prev_s,
                       u0_s, g0_s, u1_s, g1_s, *, tiles_per_seq, nf):
    i = pl.program_id(0)
    j = pl.program_id(1)
    tm = h2_ref.shape[0]
    slots = ((u0_s, g0_s), (u1_s, g1_s))

    def up(u_s, g_s):
        h2 = h2_ref[...]
        u_s[...] = jnp.dot(h2, wu_ref[...], preferred_element_type=F32)
        g_s[...] = jnp.dot(h2, wg_ref[...], preferred_element_type=F32)

    def down(u_s, g_s):
        jt = j - 1
        prev = jnp.where(i % tiles_per_seq == 0, 0.0, prev_s[jt])
        p2, p1 = prev[6:7, :], prev[7:8, :]
        g = g_s[...]
        r = lax.broadcasted_iota(jnp.int32, g.shape, 0)
        g_m1 = jnp.where(r == 0, p1, pltpu.roll(g, 1, axis=0))
        g_m2 = jnp.where(r == 0, p2, jnp.where(r == 1, p1, pltpu.roll(g, 2, axis=0)))
        cw = cw_ref[jt]
        gconv = cb_ref[jt] + ((g_m2 * cw[0:1, :] + g_m1 * cw[1:2, :]) + g * cw[2:3, :])
        act = (jax.nn.gelu(gconv) * u_s[...]).astype(BF16)
        y_ref[...] += jnp.dot(act, wd_ref[...], preferred_element_type=F32)
        prev_s[jt] = g[tm - 8:tm, :]

    @pl.when(j == 0)
    def _():
        y_ref[...] = x1_ref[...]
        up(*slots[0])

    for par in range(2):
        @pl.when((j > 0) & (j < nf) & (j % 2 == par))
        def _(par=par):
            up(*slots[par])
            down(*slots[1 - par])

    @pl.when(j == nf)
    def _():
        down(*slots[(nf - 1) % 2])
        tail_ref[...] = prev_s[...]


def _ffn_sample_kernel(h2_ref, wu_ref, wg_ref, cw_ref, cb_ref, wd_ref, x1_ref, b2_ref, b1_ref, y_ref, g_ref):
    g = _ffn_body(h2_ref, wu_ref, wg_ref, cw_ref, cb_ref, wd_ref, x1_ref, y_ref,
                  lambda g: b2_ref[...], lambda g: b1_ref[...])
    g_ref[...] = g


def _ffn_specs(tm, tf, d, nf):
    return [pl.BlockSpec((tm, d), lambda i, j: (i, 0)),
            pl.BlockSpec((d, tf), lambda i, j: (0, j)),
            pl.BlockSpec((d, tf), lambda i, j: (0, nf + j)),
            pl.BlockSpec((CONV_W, tf), lambda i, j: (0, j)),
            pl.BlockSpec((1, tf), lambda i, j: (0, j)),
            pl.BlockSpec((tf, d), lambda i, j: (j, 0)),
            pl.BlockSpec((tm, d), lambda i, j: (i, 0))]


def _ffn_prompt(h2, x1, w_in, conv_w, conv_b, w_down, seq, *, tm, tf):
    m, d = x1.shape
    assert m % tm == 0 and seq % tm == 0 and D_FF % tf == 0 and tm % 8 == 0
    nf = D_FF // tf
    kern = functools.partial(_ffn_prompt_kernel, tiles_per_seq=seq // tm, nf=nf)
    up_t = lambda j: jnp.minimum(j, nf - 1)
    dn_t = lambda j: jnp.maximum(j - 1, 0)
    y, tails = pl.pallas_call(
        kern,
        grid=(m // tm, nf + 1),
        in_specs=[pl.BlockSpec((tm, d), lambda i, j: (i, 0)),
                  pl.BlockSpec((d, tf), lambda i, j: (0, up_t(j))),
                  pl.BlockSpec((d, tf), lambda i, j: (0, nf + up_t(j))),
                  pl.BlockSpec((nf, CONV_W, tf), lambda i, j: (0, 0, 0)),
                  pl.BlockSpec((nf, 1, tf), lambda i, j: (0, 0, 0)),
                  pl.BlockSpec((tf, d), lambda i, j: (dn_t(j), 0)),
                  pl.BlockSpec((tm, d), lambda i, j: (i, 0))],
        out_specs=[pl.BlockSpec((tm, d), lambda i, j: (i, 0)),
                   pl.BlockSpec((None, nf, 8, tf), lambda i, j: (i, 0, 0, 0))],
        out_shape=[jax.ShapeDtypeStruct((m, d), F32), jax.ShapeDtypeStruct((m // tm, nf, 8, tf), F32)],
        scratch_shapes=[pltpu.VMEM((nf, 8, tf), F32)] + [pltpu.VMEM((tm, tf), F32)] * 4,
        compiler_params=_cparams(("arbitrary", "arbitrary")),
        name="ffn_prompt",
    )(h2, w_in, w_in, conv_w.reshape(CONV_W, nf, tf).swapaxes(0, 1), conv_b.reshape(nf, 1, tf), w_down, x1)
    return y, tails.swapaxes(1, 2).reshape(m // tm, 8, D_FF)


def _ffn_sample(h2, x1, w_in, conv_w, conv_b, w_down, conv_buf, *, tf):
    m, d = x1.shape
    nf = D_FF // tf
    buf2d = conv_buf.reshape(m, (CONV_W - 1) * D_FF)
    return pl.pallas_call(
        _ffn_sample_kernel,
        grid=(1, nf),
        in_specs=_ffn_specs(m, tf, d, nf) + [pl.BlockSpec((m, tf), lambda i, j: (0, j)),
                                             pl.BlockSpec((m, tf), lambda i, j: (0, nf + j))],
        out_specs=[pl.BlockSpec((m, d), lambda i, j: (i, 0)), pl.BlockSpec((m, tf), lambda i, j: (0, j))],
        out_shape=[jax.ShapeDtypeStruct((m, d), F32), jax.ShapeDtypeStruct((m, D_FF), F32)],
        compiler_params=_cparams(("arbitrary", "arbitrary")),
        name="ffn_sample",
    )(h2, w_in, w_in, conv_w, conv_b, w_down, x1, buf2d, buf2d)


IN_PROJ_TN = 512
FFN_TM = 512
FFN_TF = 512


def _qk_gain_row(q_norm, k_norm):
    reps = N_GROUPS * ATT_HEADS
    return jnp.concatenate([jnp.tile(q_norm, reps), jnp.tile(k_norm, reps), jnp.zeros((NA - A_VA,), F32)])[None, :]


ATTN_PROMPT_TILING = ((4, 4, 1), (1, 4, 2), (1, 2, 4))


def _layer(x_prompt, x_sample, caches, norm_mix, w_in, q_norm, k_norm, b_igate, b_fgate, w_proj_att,
           w_proj_mlstm, w_out, norm_ffn, w_ffn_in, conv_w, conv_b, w_ffn_down):
    batch, seq, d = x_prompt.shape
    bd = x_sample.shape[0]
    assert x_sample.shape[1] == 1 and d == D_MODEL
    (ck1, cv1, ck2, cv2, ck3, cv3, st_c, st_n, st_m, st_conv) = caches

    wt_in = jnp.swapaxes(w_in, 0, 1)
    qk_gain = _qk_gain_row(q_norm, k_norm)
    g1 = norm_mix[None, :]
    g2 = norm_ffn[None, :]
    gate_bias = jnp.concatenate([b_igate, b_fgate, jnp.zeros((LANES - 2 * M_HEADS,), F32)])[None, :]
    wpa = w_proj_att.astype(BF16)
    wpm = w_proj_mlstm.astype(BF16)
    wo = w_out.astype(BF16)
    w_ff = w_ffn_in.astype(BF16)
    w_dn = w_ffn_down.astype(BF16)
    cb = conv_b[None, :]

    xp = x_prompt.reshape(batch * seq, d)
    xs = x_sample.reshape(bd, d)

    h_all = _rmsnorm(xp, xs, g1, tm=2048, ts=512)
    (za_p, zb_p, zt_p, zif_p), (za_s, zb_s, zt_s, zif_s) = _in_proj(h_all, wt_in, qk_gain, tn=IN_PROJ_TN, ms=bd)
    o_p, lse_p = zip(*[_attn_prompt_group(za_p, batch, seq, gi, dil, nsub=nsub, hw=hw, rb=rb)
                       for gi, ((_, dil), (nsub, hw, rb)) in enumerate(zip(DIL_PATTERNS, ATTN_PROMPT_TILING))])
    hm_p, p_c, p_n, p_m = _mlstm_prompt(zif_p, zb_p, gate_bias, batch, seq, chunk=256)
    x1_p, h2_p = _merge(o_p, lse_p, hm_p, zt_p, xp, wpa, wpm, wo, g2, tm=256)
    y_p, tails = _ffn_prompt(h2_p, x1_p, w_ff, conv_w, cb, w_dn, seq, tm=FFN_TM, tf=FFN_TF)

    za_p3 = za_p.reshape(batch, seq, NA)
    p_kv = []
    for gi, (win, _) in enumerate(DIL_PATTERNS):
        keep = min(win, seq)
        for col in (A_KA, A_VA):
            lo = col + gi * ATT_GROUP_W
            p_kv.append(za_p3[:, seq - keep:, lo:lo + ATT_GROUP_W].reshape(batch, keep, ATT_HEADS, ATT_HEAD_DIM))
    tiles_per_seq = seq // FFN_TM
    p_conv = tails.reshape(batch, tiles_per_seq, 8, D_FF)[:, -1, 8 - (CONV_W - 1):, :]

    o_s, lse_s = _attn_sample(za_s, [(ck1, cv1), (ck2, cv2), (ck3, cv3)], bb=4)
    hm_s, s_c, s_n, s_m = _mlstm_sample(zif_s, zb_s, gate_bias, st_c, st_n, st_m, bb=4)
    x1_s, h2_s = _merge(o_s, lse_s, hm_s, zt_s, xs, wpa, wpm, wo, g2, tm=bd)
    y_s, g_s = _ffn_sample(h2_s, x1_s, w_ff, conv_w, cb, w_dn, st_conv, tf=FFN_TF)

    s_kv = []
    for gi in range(N_GROUPS):
        for col in (A_KA, A_VA):
            lo = col + gi * ATT_GROUP_W
            s_kv.append(za_s[:, lo:lo + ATT_GROUP_W].reshape(bd, 1, ATT_HEADS, ATT_HEAD_DIM))
    s_conv = jnp.stack([st_conv[:, 1, :], g_s], axis=1)

    p_state = p_kv + [p_c, p_n, p_m, p_conv]
    s_state = s_kv + [s_c, s_n, s_m, s_conv]
    return y_p.reshape(batch, seq, d), y_s.reshape(bd, 1, d), p_state, s_state


def kernel(x_prompt, x_sample, cache_k_w128, cache_v_w128, cache_k_w512, cache_v_w512, cache_k_w2048,
           cache_v_w2048, state_mlstm_C, state_mlstm_n, state_mlstm_m, state_ffn_conv, norm_mix, w_in, q_norm,
           k_norm, b_igate, b_fgate, w_proj_att, w_proj_mlstm, w_out, norm_ffn, w_ffn_in, conv_w, conv_b,
           w_ffn_down):
    assert norm_mix.shape[0] == 1
    caches = [c[0] for c in (cache_k_w128, cache_v_w128, cache_k_w512, cache_v_w512, cache_k_w2048,
                             cache_v_w2048, state_mlstm_C, state_mlstm_n, state_mlstm_m, state_ffn_conv)]
    weights = [w[0] for w in (norm_mix, w_in, q_norm, k_norm, b_igate, b_fgate, w_proj_att, w_proj_mlstm,
                              w_out, norm_ffn, w_ffn_in, conv_w, conv_b, w_ffn_down)]
    y_p, y_s, p_state, s_state = _layer(x_prompt, x_sample, caches, *weights)
    return (y_p, y_s, *[a[None] for a in p_state], *[a[None] for a in s_state])
```

```python
import functools

import jax
import jax.numpy as jnp
from jax import lax
from jax.experimental import pallas as pl
from jax.experimental.pallas import tpu as pltpu

F32 = jnp.float32
BF16 = jnp.bfloat16

RMS_EPS = 1e-6
NEG_INF = -1e30
LANES = 128
VMEM_LIMIT = 56 * 1024 * 1024

D_MODEL = 2048
DIL_PATTERNS = ((128, 1), (512, 4), (2048, 16))
N_GROUPS = 3
ATT_HEADS = 4
ATT_HEAD_DIM = 128
ATT_SPAN = 128
ATT_SCALE = ATT_HEAD_DIM ** -0.5
ATT_GROUP_W = ATT_HEADS * ATT_HEAD_DIM
ATT_WIDTH = N_GROUPS * ATT_GROUP_W
M_HEADS = 4
M_QK_DIM = D_MODEL // (2 * M_HEADS)
M_V_DIM = D_MODEL // M_HEADS
M_QK_WIDTH = M_HEADS * M_QK_DIM
M_V_WIDTH = M_HEADS * M_V_DIM
M_K_SCALE = M_QK_DIM ** -0.5
D_FF = ((8 * D_MODEL // 3 + 255) // 256) * 256
CONV_W = 3
SPLIT_SIZES = (ATT_WIDTH, ATT_WIDTH, ATT_WIDTH, M_QK_WIDTH, M_QK_WIDTH, M_V_WIDTH, M_V_WIDTH,
               M_HEADS, M_HEADS, D_MODEL, D_MODEL)

IN_COLS = sum(SPLIT_SIZES)
A_QA = 0
A_KA = A_QA + ATT_WIDTH
A_VA = A_KA + ATT_WIDTH
NA = A_VA + ATT_WIDTH
B_QM = 0
B_KM = B_QM + M_QK_WIDTH
B_VM = B_KM + M_QK_WIDTH
B_OM = B_VM + M_V_WIDTH
NB = B_OM + M_V_WIDTH
N_IF = 2 * M_HEADS
T_GA = 0
T_GB = T_GA + D_MODEL
NT = T_GB + D_MODEL
assert NA + NB + N_IF + NT == IN_COLS


def _cparams(sem):
    return pltpu.CompilerParams(dimension_semantics=sem, vmem_limit_bytes=VMEM_LIMIT)


def _log_sigmoid(x):
    return jnp.minimum(x, 0.0) - jnp.log(1.0 + jnp.exp(-jnp.abs(x)))


def _rmsnorm_kernel(x_ref, xs_ref, g_ref, h_ref, *, nsub):
    j = pl.program_id(1)
    ts = x_ref.shape[0]

    def norm(x):
        ms = jnp.mean(x * x, axis=-1, keepdims=True)
        return (x * lax.rsqrt(ms + RMS_EPS) * g_ref[...]).astype(h_ref.dtype)

    @pl.when(j < nsub)
    def _():
        h_ref[pl.ds(pl.multiple_of(j * ts, ts), ts), :] = norm(x_ref[...])

    @pl.when(j == nsub)
    def _():
        h_ref[nsub * ts:, :] = norm(xs_ref[...])


def _rmsnorm(x, xs, gain, *, tm, ts):
    m, d = x.shape
    ms = xs.shape[0]
    assert m % tm == 0 and tm % ts == 0
    nsub = tm // ts
    last = m // ts - 1
    return pl.pallas_call(
        functools.partial(_rmsnorm_kernel, nsub=nsub),
        grid=(m // tm, nsub + 1),
        in_specs=[pl.BlockSpec((ts, d), lambda i, j: (jnp.minimum(i * nsub + j, last), 0)),
                  pl.BlockSpec((ms, d), lambda i, j: (0, 0)),
                  pl.BlockSpec((1, d), lambda i, j: (0, 0))],
        out_specs=pl.BlockSpec((None, tm + ms, d), lambda i, j: (i, 0, 0)),
        out_shape=jax.ShapeDtypeStruct((m // tm, tm + ms, d), BF16),
        compiler_params=_cparams(("parallel", "arbitrary")),
        name="rmsnorm",
    )(x, xs, gain)


def _in_proj_kernel(h_ref, w_ref, wn_ref, wif_ref, qkg_ref, za_ref, zb_ref, zt_ref, zif_ref,
                    sa_ref, sb_ref, st_ref, sif_ref, *, n_norm, na, nb, nt):
    j = pl.program_id(1)
    tp = za_ref.shape[0]

    def z_of(w):
        return lax.dot_general(h_ref[...], w.astype(BF16), (((1,), (1,)), ((), ())), preferred_element_type=F32)

    @pl.when(j < n_norm)
    def _():
        z = z_of(w_ref[...])
        for c in range(za_ref.shape[1] // LANES):
            sl = slice(c * LANES, (c + 1) * LANES)
            zc = z[:, sl]
            ms = jnp.mean(zc * zc, axis=-1, keepdims=True)
            zn = zc * lax.rsqrt(ms + RMS_EPS) * qkg_ref[:, sl]
            za_ref[:, sl] = zn[:tp]
            sa_ref[j, :, sl] = zn[tp:]

    @pl.when((j >= n_norm) & (j < na))
    def _():
        z = z_of(w_ref[...])
        za_ref[...] = z[:tp]
        sa_ref[j] = z[tp:]

    @pl.when((j >= na) & (j < na + nb))
    def _():
        z = z_of(w_ref[...])
        zb_ref[...] = z[:tp].astype(zb_ref.dtype)
        sb_ref[j - na] = z[tp:]

    @pl.when((j >= na + nb) & (j < na + nb + nt))
    def _():
        z = z_of(jnp.concatenate([w_ref[N_IF:, :], wn_ref[...]], axis=0))
        zt_ref[...] = z[:tp].astype(zt_ref.dtype)
        st_ref[j - na - nb] = z[tp:]

    @pl.when(j == na + nb + nt)
    def _():
        z = z_of(jnp.concatenate([wif_ref[...], jnp.zeros((LANES - N_IF, wif_ref.shape[1]), F32)], axis=0))
        zif_ref[...] = z[:tp]
        sif_ref[...] = z[tp:]


def _in_proj(h, wt, qk_gain, *, tn, ms):
    ni, th, d = h.shape
    tm = th - ms
    m = ni * tm
    assert wt.shape == (IN_COLS, d) and N_IF == 8
    assert NA % tn == 0 and NB % tn == 0 and NT % tn == 0 and A_VA % tn == 0 and tn % N_IF == 0
    na, nb, nt = NA // tn, NB // tn, NT // tn
    nw = na + nb + nt
    g_if = (NA + NB) // N_IF
    g_tn = tn // N_IF
    kern = functools.partial(_in_proj_kernel, n_norm=A_VA // tn, na=na, nb=nb, nt=nt)
    col_a = lambda j: jnp.minimum(j, na - 1)
    col_b = lambda j: jnp.clip(j - na, 0, nb - 1)
    col_t = lambda j: jnp.clip(j - na - nb, 0, nt - 1)
    out_specs = [pl.BlockSpec((None, tm, tn), lambda i, j: (i, 0, col_a(j))),
                 pl.BlockSpec((None, tm, tn), lambda i, j: (i, 0, col_b(j))),
                 pl.BlockSpec((None, tm, tn), lambda i, j: (i, 0, col_t(j))),
                 pl.BlockSpec((None, tm, LANES), lambda i, j: (i, 0, 0))]
    out_shape = [jax.ShapeDtypeStruct((ni, tm, w), dt)
                 for w, dt in zip((NA, NB, NT, LANES), (F32, BF16, BF16, F32))]
    for n_tiles in (na, nb, nt):
        out_specs.append(pl.BlockSpec((None, n_tiles, ms, tn), lambda i, j: (i, 0, 0, 0)))
        out_shape.append(jax.ShapeDtypeStruct((ni, n_tiles, ms, tn), F32))
    out_specs.append(pl.BlockSpec((None, ms, LANES), lambda i, j: (i, 0, 0)))
    out_shape.append(jax.ShapeDtypeStruct((ni, ms, LANES), F32))
    outs = pl.pallas_call(
        kern,
        grid=(ni, nw + 1),
        in_specs=[
            pl.BlockSpec((None, th, d), lambda i, j: (i, 0, 0)),
            pl.BlockSpec((tn, d), lambda i, j: (jnp.minimum(j, nw - 1), 0)),
            pl.BlockSpec((N_IF, d), lambda i, j: (g_if + g_tn * (jnp.clip(j, na + nb, nw - 1) - (na + nb) + 1), 0)),
            pl.BlockSpec((N_IF, d), lambda i, j: (g_if, 0)),
            pl.BlockSpec((1, tn), lambda i, j: (0, col_a(j))),
        ],
        out_specs=out_specs,
        out_shape=out_shape,
        compiler_params=_cparams(("parallel", "arbitrary")),
        name="in_proj",
    )(h, wt, wt, wt, qk_gain)
    sample = [o[0].swapaxes(0, 1).reshape(ms, -1) for o in outs[4:7]] + [outs[7][0]]
    return [o.reshape(m, o.shape[2]) for o in outs[:4]], sample


def _attn_prompt_kernel(*refs, dil, nsub, hw, rb, n_cast):
    q_refs, kp_refs, kc_refs, vp_refs, vc_refs = [refs[t * hw:(t + 1) * hw] for t in range(5)]
    cast_in = refs[5 * hw:5 * hw + n_cast]
    o_ref, lse_ref = refs[5 * hw + n_cast:5 * hw + n_cast + 2]
    cast_out = refs[5 * hw + n_cast + 2:5 * hw + 2 * n_cast + 2]
    o_refs = refs[5 * hw + 2 * n_cast + 2:]
    for w_ref, wb_ref in zip(cast_in, cast_out):
        wb_ref[...] = w_ref[...].astype(wb_ref.dtype)
    c = pl.program_id(1)
    blk = ATT_SPAN
    step = blk * dil
    qi = lax.broadcasted_iota(jnp.int32, (blk, 2 * blk), 0)
    kj = lax.broadcasted_iota(jnp.int32, (blk, 2 * blk), 1)
    band = (kj >= qi) & (kj <= qi + blk)
    first_bias = jnp.where((kj < blk) & (c == 0), NEG_INF, 0.0)
    lane = lax.broadcasted_iota(jnp.int32, (blk, LANES), 1)

    def rows(base, r):
        return pl.ds(base + r, blk) if dil == 1 else pl.ds(base + r, blk, stride=dil)

    def group(r0, s):
        base = s * step
        ids = [(r0 + r, n) for r in range(rb) for n in range(hw)]

        def stacked(cur_refs, prev_refs):
            parts = []
            for r, n in ids:
                cur = cur_refs[n][rows(base, r), :]
                if prev_refs is None:
                    parts.append(cur)
                else:
                    prv = prev_refs[n][rows(0, r), :] if s == 0 else cur_refs[n][rows(base - step, r), :]
                    parts.append(jnp.concatenate([prv, cur], axis=0))
            return jnp.stack(parts).astype(BF16)

        q = stacked(q_refs, None)
        k = stacked(kc_refs, kp_refs)
        v = stacked(vc_refs, vp_refs)
        sc = jnp.einsum('bqe,bke->bqk', q, k, preferred_element_type=F32) * ATT_SCALE
        if s == 0:
            sc = sc + first_bias
        sc = jnp.where(band, sc, NEG_INF)
        m = jnp.max(sc, axis=-1, keepdims=True)
        p = jnp.exp(sc - m)
        l = jnp.sum(p, axis=-1, keepdims=True)
        o = jnp.einsum('bqk,bke->bqe', p.astype(BF16), v, preferred_element_type=F32) / l
        lse = m + jnp.log(l)
        for r in range(rb):
            lse_tile = jnp.zeros((blk, LANES), F32)
            for n in range(hw):
                b = r * hw + n
                o_refs[n][rows(base, r0 + r), :] = o[b]
                lse_tile = jnp.where(lane == n, lse[b], lse_tile)
            lse_ref[rows(base, r0 + r), :] = lse_tile

    for s in range(nsub):
        if dil == rb:
            group(0, s)
        else:
            def body(it, carry, s=s):
                group(it * rb, s)
                return carry
            lax.fori_loop(0, dil // rb, body, 0)

    for n in range(hw):
        o_ref[:, n * ATT_HEAD_DIM:(n + 1) * ATT_HEAD_DIM] = o_refs[n][...]


def _attn_prompt_group(za, batch, seq, gi, dil, to_cast, *, nsub, hw, rb):
    step = ATT_SPAN * dil
    tc = nsub * step
    assert seq % tc == 0 and ATT_HEADS % hw == 0 and dil % rb == 0
    nh = ATT_HEADS // hw
    e = ATT_HEAD_DIM
    nchunk = seq // tc
    n_steps = batch * nchunk * nh
    bf16_sublanes = 16
    cast_in_specs, cast_out_specs, cast_shapes = [], [], []
    for w in to_cast:
        rows, cols = w.shape
        assert rows % (n_steps * bf16_sublanes) == 0
        spec = pl.BlockSpec((rows // n_steps, cols), lambda b, c, h: ((b * nchunk + c) * nh + h, 0))
        cast_in_specs.append(spec)
        cast_out_specs.append(spec)
        cast_shapes.append(jax.ShapeDtypeStruct((rows, cols), BF16))
    z3 = za.reshape(batch, seq, NA)
    qc, kc, vc = [(col + gi * ATT_GROUP_W) // e for col in (A_QA, A_KA, A_VA)]

    def cur(col):
        return [pl.BlockSpec((None, tc, e), lambda b, c, h, n=n: (b, c, col + h * hw + n)) for n in range(hw)]

    def prev(col):
        return [pl.BlockSpec((None, step, e), lambda b, c, h, n=n: (b, jnp.maximum(c * nsub - 1, 0), col + h * hw + n))
                for n in range(hw)]

    kern = functools.partial(_attn_prompt_kernel, dil=dil, nsub=nsub, hw=hw, rb=rb, n_cast=len(to_cast))
    o, lse, *cast = pl.pallas_call(
        kern,
        grid=(batch, nchunk, nh),
        in_specs=cur(qc) + prev(kc) + cur(kc) + prev(vc) + cur(vc) + cast_in_specs,
        out_specs=[pl.BlockSpec((None, tc, hw * e), lambda b, c, h: (b, c, h)),
                   pl.BlockSpec((None, tc, LANES), lambda b, c, h: (b, c, h))] + cast_out_specs,
        out_shape=[jax.ShapeDtypeStruct((batch, seq, ATT_GROUP_W), F32),
                   jax.ShapeDtypeStruct((batch, seq, nh * LANES), F32)] + cast_shapes,
        scratch_shapes=[pltpu.VMEM((tc, e), F32)] * hw,
        compiler_params=_cparams(("parallel", "parallel", "parallel")),
        name=f"attn_prompt_g{gi}",
    )(*([z3] * (5 * hw)), *to_cast)
    return o.reshape(batch * seq, ATT_GROUP_W), lse.reshape(batch * seq, nh * LANES), cast


def _attn_sample_kernel(z_ref, k1, v1, k2, v2, k3, v3, o1, o2, o3, l1, l2, l3):
    bb = z_ref.shape[0]
    bufs = ((k1, v1, o1, l1), (k2, v2, o2, l2), (k3, v3, o3, l3))
    lane = lax.broadcasted_iota(jnp.int32, (1, LANES), 1)

    def heads(b, col):
        return jnp.concatenate([z_ref[b, :, col + n * ATT_HEAD_DIM:col + (n + 1) * ATT_HEAD_DIM]
                                for n in range(ATT_HEADS)], axis=0)

    def body(b, carry):
        for gi, (k_ref, v_ref, o_ref, l_ref) in enumerate(bufs):
            q = heads(b, A_QA + gi * ATT_GROUP_W)
            k_new = heads(b, A_KA + gi * ATT_GROUP_W)
            v_new = heads(b, A_VA + gi * ATT_GROUP_W)
            kb = k_ref[b].reshape(ATT_SPAN // 2, 2 * ATT_HEADS, ATT_HEAD_DIM)
            vb = v_ref[b].reshape(ATT_SPAN // 2, 2 * ATT_HEADS, ATT_HEAD_DIM)
            q2 = jnp.concatenate([q, q], axis=0)
            s = jnp.sum(kb * q2[None], axis=-1, keepdims=True) * ATT_SCALE
            s_new = jnp.sum(k_new * q, axis=-1, keepdims=True) * ATT_SCALE
            m2 = jnp.max(s, axis=0)
            m = jnp.maximum(jnp.maximum(m2[:ATT_HEADS], m2[ATT_HEADS:]), s_new)
            p = jnp.exp(s - jnp.concatenate([m, m], axis=0)[None])
            p_new = jnp.exp(s_new - m)
            l2 = jnp.sum(p, axis=0)
            l = l2[:ATT_HEADS] + l2[ATT_HEADS:] + p_new
            o2 = jnp.sum(p * vb, axis=0)
            o = (o2[:ATT_HEADS] + o2[ATT_HEADS:] + p_new * v_new) / l
            lse = m + jnp.log(l)
            lse_row = jnp.zeros((1, LANES), F32)
            for n in range(ATT_HEADS):
                o_ref[b, :, n * ATT_HEAD_DIM:(n + 1) * ATT_HEAD_DIM] = o[n:n + 1, :]
                lse_row = jnp.where(lane == n, lse[n:n + 1, :], lse_row)
            l_ref[b] = lse_row
        return carry

    lax.fori_loop(0, bb, body, 0)


def _attn_sample(za_s, caches, *, bb):
    bd = za_s.shape[0]
    assert bd % bb == 0
    ins, in_specs = [za_s.reshape(bd, 1, NA)], [pl.BlockSpec((bb, 1, NA), lambda i: (i, 0, 0))]
    for (k_buf, v_buf), (win, dil) in zip(caches, DIL_PATTERNS):
        assert k_buf.shape[1:] == (ATT_SPAN * dil, ATT_HEADS, ATT_HEAD_DIM)
        for buf in (k_buf, v_buf):
            ins.append(buf.reshape(bd, ATT_SPAN, dil, ATT_HEADS, ATT_HEAD_DIM))
            in_specs.append(pl.BlockSpec((bb, ATT_SPAN, None, ATT_HEADS, ATT_HEAD_DIM),
                                         lambda i: (i, 0, 0, 0, 0)))
    outs = pl.pallas_call(
        _attn_sample_kernel,
        grid=(bd // bb,),
        in_specs=in_specs,
        out_specs=[pl.BlockSpec((bb, 1, ATT_GROUP_W), lambda i: (i, 0, 0))] * 3
                  + [pl.BlockSpec((bb, 1, LANES), lambda i: (i, 0, 0))] * 3,
        out_shape=[jax.ShapeDtypeStruct((bd, 1, ATT_GROUP_W), F32)] * 3
                  + [jax.ShapeDtypeStruct((bd, 1, LANES), F32)] * 3,
        compiler_params=_cparams(("parallel",)),
        name="attn_sample",
    )(*ins)
    return [o[:, 0, :] for o in outs[:3]], [l[:, 0, :] for l in outs[3:]]


def _mlstm_prompt_kernel(q_ref, k_ref, v_ref, om_ref, g_ref, bias_ref, h_ref, c_out, n_out, m_out,
                         c_s, n_s, m_s):
    hd = pl.program_id(1)
    ci = pl.program_id(2)
    L = q_ref.shape[0]

    @pl.when(ci == 0)
    def _():
        c_s[...] = jnp.zeros_like(c_s)
        n_s[...] = jnp.zeros_like(n_s)
        m_s[...] = jnp.zeros_like(m_s)

    gates = g_ref[...] + bias_ref[...]
    lane = lax.broadcasted_iota(jnp.int32, (L, LANES), 1)
    li_col = jnp.sum(jnp.where(lane == hd, gates, 0.0), axis=1, keepdims=True)
    lf_col = _log_sigmoid(jnp.sum(jnp.where(lane == hd + M_HEADS, gates, 0.0), axis=1, keepdims=True))
    gates_t = gates.T
    sub = lax.broadcasted_iota(jnp.int32, (LANES, L), 0)
    li_row = jnp.sum(jnp.where(sub == hd, gates_t, 0.0), axis=0, keepdims=True)
    lf_row = _log_sigmoid(jnp.sum(jnp.where(sub == hd + M_HEADS, gates_t, 0.0), axis=0, keepdims=True))

    ti = lax.broadcasted_iota(jnp.int32, (L, L), 0)
    si = lax.broadcasted_iota(jnp.int32, (L, L), 1)
    causal = si <= ti
    b_col = jnp.sum(jnp.where(causal, lf_row, 0.0), axis=1, keepdims=True)
    b_row = jnp.sum(jnp.where(ti <= si, lf_col, 0.0), axis=0, keepdims=True)
    b_end = jnp.sum(lf_row, axis=1, keepdims=True)

    m_prev = m_s[...]
    dmat = jnp.where(causal, b_col - b_row + li_row, NEG_INF)
    inter = b_col + m_prev
    mt = jnp.maximum(inter, jnp.max(dmat, axis=1, keepdims=True))

    qb = q_ref[...]
    q = qb.astype(F32)
    k = k_ref[...].astype(F32) * M_K_SCALE
    vb = v_ref[...]
    qk = lax.dot_general(qb, k.astype(BF16), (((1,), (1,)), ((), ())), preferred_element_type=F32)
    a = jnp.exp(dmat - mt) * qk
    w_inter = jnp.exp(inter - mt)
    num = (jnp.dot(a.astype(BF16), vb, preferred_element_type=F32)
           + w_inter * jnp.dot(qb, c_s[...].astype(BF16), preferred_element_type=F32))
    den = jnp.sum(a, axis=1, keepdims=True) + w_inter * jnp.sum(q * n_s[...], axis=1, keepdims=True)
    h = num / jnp.maximum(jnp.abs(den), jnp.exp(-mt))
    h_ref[...] = (jax.nn.sigmoid(om_ref[...].astype(F32)) * h).astype(h_ref.dtype)

    g_col = b_end - b_col + li_col
    g_row = b_end - b_row + li_row
    m_new = jnp.maximum(b_end + m_prev, jnp.max(g_row, axis=1, keepdims=True))
    decay = jnp.exp(b_end + m_prev - m_new)
    kw = jnp.exp(g_col - m_new) * k
    c_s[...] = decay * c_s[...] + jnp.dot(kw.T.astype(BF16), vb, preferred_element_type=F32)
    n_s[...] = decay * n_s[...] + jnp.sum(kw, axis=0, keepdims=True)
    m_s[...] = m_new

    @pl.when(ci == pl.num_programs(2) - 1)
    def _():
        c_out[...] = c_s[...]
        n_out[...] = n_s[...]
        m_out[...] = jnp.broadcast_to(m_s[...], m_out.shape)


def _mlstm_prompt(zif, zb, gate_bias, batch, seq, *, chunk):
    assert seq % chunk == 0
    nc = seq // chunk
    zif3 = zif.reshape(batch, seq, LANES)
    zb3 = zb.reshape(batch, seq, NB)
    qk_blk = lambda col: pl.BlockSpec((None, chunk, M_QK_DIM), lambda b, h, c: (b, c, col // M_QK_DIM + h))
    v_blk = lambda col: pl.BlockSpec((None, chunk, M_V_DIM), lambda b, h, c: (b, c, col // M_V_DIM + h))
    hm, c1, n1, m1 = pl.pallas_call(
        _mlstm_prompt_kernel,
        grid=(batch, M_HEADS, nc),
        in_specs=[qk_blk(B_QM), qk_blk(B_KM), v_blk(B_VM), v_blk(B_OM),
                  pl.BlockSpec((None, chunk, LANES), lambda b, h, c: (b, c, 0)),
                  pl.BlockSpec((1, LANES), lambda b, h, c: (0, 0))],
        out_specs=[pl.BlockSpec((None, chunk, M_V_DIM), lambda b, h, c: (b, c, h)),
                   pl.BlockSpec((None, None, M_QK_DIM, M_V_DIM), lambda b, h, c: (b, h, 0, 0)),
                   pl.BlockSpec((None, None, 1, M_QK_DIM), lambda b, h, c: (b, h, 0, 0)),
                   pl.BlockSpec((None, None, 1, LANES), lambda b, h, c: (b, h, 0, 0))],
        out_shape=[jax.ShapeDtypeStruct((batch, seq, M_V_WIDTH), BF16),
                   jax.ShapeDtypeStruct((batch, M_HEADS, M_QK_DIM, M_V_DIM), F32),
                   jax.ShapeDtypeStruct((batch, M_HEADS, 1, M_QK_DIM), F32),
                   jax.ShapeDtypeStruct((batch, M_HEADS, 1, LANES), F32)],
        scratch_shapes=[pltpu.VMEM((M_QK_DIM, M_V_DIM), F32), pltpu.VMEM((1, M_QK_DIM), F32),
                        pltpu.VMEM((1, 1), F32)],
        compiler_params=_cparams(("parallel", "parallel", "arbitrary")),
        name="mlstm_prompt",
    )(zb3, zb3, zb3, zb3, zif3, gate_bias)
    return hm.reshape(batch * seq, M_V_WIDTH), c1, n1[:, :, 0, :], m1[:, :, 0, 0]


def _row_to_col(row):
    n = row.shape[1]
    return jnp.broadcast_to(row, (LANES, n)).T[:, 0:1]


def _mlstm_sample_kernel(zif_ref, zb_ref, bias_ref, c_ref, n_ref, m_ref, h_ref, c_out, n_out, m_out):
    for b in range(zif_ref.shape[0]):
        _mlstm_sample_one(zif_ref.at[b], zb_ref.at[b], bias_ref, c_ref.at[b], n_ref.at[b], m_ref.at[b],
                          h_ref.at[b], c_out.at[b], n_out.at[b], m_out.at[b])


def _mlstm_sample_one(zif_ref, zb_ref, bias_ref, c_ref, n_ref, m_ref, h_ref, c_out, n_out, m_out):
    gates = zif_ref[...] + bias_ref[...]
    lane = lax.broadcasted_iota(jnp.int32, (1, LANES), 1)
    m_row = jnp.zeros((1, LANES), F32)
    for h in range(M_HEADS):
        q = zb_ref[:, B_QM + h * M_QK_DIM:B_QM + (h + 1) * M_QK_DIM].astype(F32)
        k = zb_ref[:, B_KM + h * M_QK_DIM:B_KM + (h + 1) * M_QK_DIM].astype(F32) * M_K_SCALE
        v = zb_ref[:, B_VM + h * M_V_DIM:B_VM + (h + 1) * M_V_DIM].astype(F32)
        om = zb_ref[:, B_OM + h * M_V_DIM:B_OM + (h + 1) * M_V_DIM].astype(F32)
        li = gates[:, h:h + 1]
        lf = _log_sigmoid(gates[:, M_HEADS + h:M_HEADS + h + 1])
        m0 = m_ref[:, h:h + 1]
        c0 = c_ref[h]
        n0 = n_ref[h:h + 1, :]
        inter = lf + m0
        mt = jnp.maximum(inter, li)
        a = jnp.exp(li - mt) * jnp.sum(q * k, axis=1, keepdims=True)
        w_inter = jnp.exp(inter - mt)
        q_c = jnp.sum(_row_to_col(q) * c0, axis=0, keepdims=True)
        num = a * v + w_inter * q_c
        den = a + w_inter * jnp.sum(q * n0, axis=1, keepdims=True)
        hv = num / jnp.maximum(jnp.abs(den), jnp.exp(-mt))
        h_ref[:, h * M_V_DIM:(h + 1) * M_V_DIM] = (jax.nn.sigmoid(om) * hv).astype(h_ref.dtype)
        m_new = jnp.maximum(inter, li)
        decay = jnp.exp(inter - m_new)
        ws = jnp.exp(li - m_new)
        c_out[h] = decay * c0 + _row_to_col(ws * k) * v
        n_out[h:h + 1, :] = decay * n0 + ws * k
        m_row = jnp.where(lane == h, m_new, m_row)
    m_out[...] = m_row


def _mlstm_sample(zif_s, zb_s, gate_bias, c0, n0, m0, *, bb):
    bd = zif_s.shape[0]
    assert bd % bb == 0
    hm, c1, n1, m1 = pl.pallas_call(
        _mlstm_sample_kernel,
        grid=(bd // bb,),
        in_specs=[pl.BlockSpec((bb, 1, LANES), lambda b: (b, 0, 0)),
                  pl.BlockSpec((bb, 1, NB), lambda b: (b, 0, 0)),
                  pl.BlockSpec((1, LANES), lambda b: (0, 0)),
                  pl.BlockSpec((bb, M_HEADS, M_QK_DIM, M_V_DIM), lambda b: (b, 0, 0, 0)),
                  pl.BlockSpec((bb, M_HEADS, M_QK_DIM), lambda b: (b, 0, 0)),
                  pl.BlockSpec((bb, 1, M_HEADS), lambda b: (b, 0, 0))],
        out_specs=[pl.BlockSpec((bb, 1, M_V_WIDTH), lambda b: (b, 0, 0)),
                   pl.BlockSpec((bb, M_HEADS, M_QK_DIM, M_V_DIM), lambda b: (b, 0, 0, 0)),
                   pl.BlockSpec((bb, M_HEADS, M_QK_DIM), lambda b: (b, 0, 0)),
                   pl.BlockSpec((bb, 1, LANES), lambda b: (b, 0, 0))],
        out_shape=[jax.ShapeDtypeStruct((bd, 1, M_V_WIDTH), BF16),
                   jax.ShapeDtypeStruct((bd, M_HEADS, M_QK_DIM, M_V_DIM), F32),
                   jax.ShapeDtypeStruct((bd, M_HEADS, M_QK_DIM), F32),
                   jax.ShapeDtypeStruct((bd, 1, LANES), F32)],
        compiler_params=_cparams(("parallel",)),
        name="mlstm_sample",
    )(zif_s.reshape(bd, 1, LANES), zb_s.reshape(bd, 1, NB), gate_bias, c0, n0, m0.reshape(bd, 1, M_HEADS))
    return hm.reshape(bd, M_V_WIDTH), c1, n1, m1[:, 0, :M_HEADS]


def _merge_kernel(o1, o2, o3, l1, l2, l3, hm_ref, ga_ref, gb_ref, x_ref, wpa_ref, wpm_ref, wo_ref, g2_ref,
                  x1_ref, h2_ref, *, hws):
    os_ = (o1, o2, o3)
    parts = []
    for n in range(ATT_HEADS):
        sl = slice(n * ATT_HEAD_DIM, (n + 1) * ATT_HEAD_DIM)
        ls = []
        for l_ref, hw in zip((l1, l2, l3), hws):
            ln = (n // hw) * LANES + n % hw
            ls.append(l_ref[:, ln:ln + 1])
        mx = jnp.maximum(jnp.maximum(ls[0], ls[1]), ls[2])
        es = [jnp.exp(l - mx) for l in ls]
        tot = es[0] + es[1] + es[2]
        parts.append((es[0] / tot) * o1[:, sl] + (es[1] / tot) * o2[:, sl] + (es[2] / tot) * o3[:, sl])
    att = jnp.concatenate(parts, axis=1).astype(BF16)
    pa = jnp.dot(att, wpa_ref[...], preferred_element_type=F32)
    pm = jnp.dot(hm_ref[...], wpm_ref[...], preferred_element_type=F32)
    merged = (jax.nn.sigmoid(ga_ref[...].astype(F32)) * pa + jax.nn.sigmoid(gb_ref[...].astype(F32)) * pm)
    x1 = x_ref[...] + jnp.dot(merged.astype(BF16), wo_ref[...], preferred_element_type=F32)
    x1_ref[...] = x1
    ms = jnp.mean(x1 * x1, axis=-1, keepdims=True)
    h2_ref[...] = (x1 * lax.rsqrt(ms + RMS_EPS) * g2_ref[...]).astype(BF16)


def _merge(o_list, lse_list, hm, zt, x, wpa, wpm, wo, g2, *, tm):
    m, d = x.shape
    assert m % tm == 0 and zt.shape[1] == NT
    row_blk = lambda w: pl.BlockSpec((tm, w), lambda i: (i, 0))
    const = lambda shape: pl.BlockSpec(shape, lambda i: (0, 0), pipeline_mode=pl.Buffered(1))
    hws = tuple(ATT_HEADS * LANES // l.shape[1] for l in lse_list)
    return pl.pallas_call(
        functools.partial(_merge_kernel, hws=hws),
        grid=(m // tm,),
        in_specs=[row_blk(ATT_GROUP_W)] * 3 + [row_blk(l.shape[1]) for l in lse_list]
                 + [row_blk(M_V_WIDTH),
                    pl.BlockSpec((tm, d), lambda i: (i, T_GA // d)),
                    pl.BlockSpec((tm, d), lambda i: (i, T_GB // d)),
                    row_blk(d),
                    const(wpa.shape), const(wpm.shape), const(wo.shape), const((1, d))],
        out_specs=[row_blk(d), row_blk(d)],
        out_shape=[jax.ShapeDtypeStruct((m, d), F32), jax.ShapeDtypeStruct((m, d), BF16)],
        compiler_params=_cparams(("parallel",)),
        name="merge",
    )(*o_list, *lse_list, hm, zt, zt, x, wpa, wpm, wo, g2)


def _ffn_body(h2_ref, wu_ref, wg_ref, cw_ref, cb_ref, wd_ref, x1_ref, y_ref, g_prev2, g_prev1):
    j = pl.program_id(1)
    h2 = h2_ref[...]
    u = jnp.dot(h2, wu_ref[...], preferred_element_type=F32)
    g = jnp.dot(h2, wg_ref[...], preferred_element_type=F32)
    gconv = cb_ref[...] + ((g_prev2(g) * cw_ref[0:1, :] + g_prev1(g) * cw_ref[1:2, :]) + g * cw_ref[2:3, :])
    act = (jax.nn.gelu(gconv) * u).astype(BF16)
    down = jnp.dot(act, wd_ref[...], preferred_element_type=F32)

    @pl.when(j == 0)
    def _():
        y_ref[...] = x1_ref[...] + down

    @pl.when(j > 0)
    def _():
        y_ref[...] += down

    return g


def _ffn_prompt_kernel(h2_ref, wu_ref, wg_ref, cw_ref, cb_ref, wd_ref, x1_ref, y_ref, tail_ref, prev_s,
                       u0_s, g0_s, u1_s, g1_s, *, tiles_per_seq, nf):
    i = pl.program_id(0)
    j = pl.program_id(1)
    tm = h2_ref.shape[0]
    slots = ((u0_s, g0_s), (u1_s, g1_s))

    def up(u_s, g_s):
        h2 = h2_ref[...]
        u_s[...] = jnp.dot(h2, wu_ref[...], preferred_element_type=F32)
        g_s[...] = jnp.dot(h2, wg_ref[...], preferred_element_type=F32)

    def down(u_s, g_s):
        jt = j - 1
        prev = jnp.where(i % tiles_per_seq == 0, 0.0, prev_s[jt])
        p2, p1 = prev[6:7, :], prev[7:8, :]
        g = g_s[...]
        r = lax.broadcasted_iota(jnp.int32, g.shape, 0)
        g_m1 = jnp.where(r == 0, p1, pltpu.roll(g, 1, axis=0))
        g_m2 = jnp.where(r == 0, p2, jnp.where(r == 1, p1, pltpu.roll(g, 2, axis=0)))
        cw = cw_ref[jt]
        gconv = cb_ref[jt] + ((g_m2 * cw[0:1, :] + g_m1 * cw[1:2, :]) + g * cw[2:3, :])
        act = (jax.nn.gelu(gconv) * u_s[...]).astype(BF16)
        y_ref[...] += jnp.dot(act, wd_ref[...], preferred_element_type=F32)
        prev_s[jt] = g[tm - 8:tm, :]

    @pl.when(j == 0)
    def _():
        y_ref[...] = x1_ref[...]
        up(*slots[0])

    for par in range(2):
        @pl.when((j > 0) & (j < nf) & (j % 2 == par))
        def _(par=par):
            up(*slots[par])
            down(*slots[1 - par])

    @pl.when(j == nf)
    def _():
        down(*slots[(nf - 1) % 2])
        tail_ref[...] = prev_s[...]


def _ffn_sample_kernel(h2_ref, wu_ref, wg_ref, cw_ref, cb_ref, wd_ref, x1_ref, b2_ref, b1_ref, y_ref, g_ref):
    g = _ffn_body(h2_ref, wu_ref, wg_ref, cw_ref, cb_ref, wd_ref, x1_ref, y_ref,
                  lambda g: b2_ref[...], lambda g: b1_ref[...])
    g_ref[...] = g


def _ffn_specs(tm, tf, d, nf):
    return [pl.BlockSpec((tm, d), lambda i, j: (i, 0)),
            pl.BlockSpec((d, tf), lambda i, j: (0, j)),
            pl.BlockSpec((d, tf), lambda i, j: (0, nf + j)),
            pl.BlockSpec((CONV_W, tf), lambda i, j: (0, j)),
            pl.BlockSpec((1, tf), lambda i, j: (0, j)),
            pl.BlockSpec((tf, d), lambda i, j: (j, 0)),
            pl.BlockSpec((tm, d), lambda i, j: (i, 0))]


def _ffn_prompt(h2, x1, w_in, conv_w, conv_b, w_down, seq, *, tm, tf):
    m, d = x1.shape
    assert m % tm == 0 and seq % tm == 0 and D_FF % tf == 0 and tm % 8 == 0
    nf = D_FF // tf
    kern = functools.partial(_ffn_prompt_kernel, tiles_per_seq=seq // tm, nf=nf)
    up_t = lambda j: jnp.minimum(j, nf - 1)
    dn_t = lambda j: jnp.maximum(j - 1, 0)
    y, tails = pl.pallas_call(
        kern,
        grid=(m // tm, nf + 1),
        in_specs=[pl.BlockSpec((tm, d), lambda i, j: (i, 0)),
                  pl.BlockSpec((d, tf), lambda i, j: (0, up_t(j))),
                  pl.BlockSpec((d, tf), lambda i, j: (0, nf + up_t(j))),
                  pl.BlockSpec((nf, CONV_W, tf), lambda i, j: (0, 0, 0)),
                  pl.BlockSpec((nf, 1, tf), lambda i, j: (0, 0, 0)),
                  pl.BlockSpec((tf, d), lambda i, j: (dn_t(j), 0)),
                  pl.BlockSpec((tm, d), lambda i, j: (i, 0))],
        out_specs=[pl.BlockSpec((tm, d), lambda i, j: (i, 0)),
                   pl.BlockSpec((None, nf, 8, tf), lambda i, j: (i, 0, 0, 0))],
        out_shape=[jax.ShapeDtypeStruct((m, d), F32), jax.ShapeDtypeStruct((m // tm, nf, 8, tf), F32)],
        scratch_shapes=[pltpu.VMEM((nf, 8, tf), F32)] + [pltpu.VMEM((tm, tf), F32)] * 4,
        compiler_params=_cparams(("arbitrary", "arbitrary")),
        name="ffn_prompt",
    )(h2, w_in, w_in, conv_w.reshape(CONV_W, nf, tf).swapaxes(0, 1), conv_b.reshape(nf, 1, tf), w_down, x1)
    return y, tails.swapaxes(1, 2).reshape(m // tm, 8, D_FF)


def _ffn_sample(h2, x1, w_in, conv_w, conv_b, w_down, conv_buf, *, tf):
    m, d = x1.shape
    nf = D_FF // tf
    buf2d = conv_buf.reshape(m, (CONV_W - 1) * D_FF)
    return pl.pallas_call(
        _ffn_sample_kernel,
        grid=(1, nf),
        in_specs=_ffn_specs(m, tf, d, nf) + [pl.BlockSpec((m, tf), lambda i, j: (0, j)),
                                             pl.BlockSpec((m, tf), lambda i, j: (0, nf + j))],
        out_specs=[pl.BlockSpec((m, d), lambda i, j: (i, 0)), pl.BlockSpec((m, tf), lambda i, j: (0, j))],
        out_shape=[jax.ShapeDtypeStruct((m, d), F32), jax.ShapeDtypeStruct((m, D_FF), F32)],
        compiler_params=_cparams(("arbitrary", "arbitrary")),
        name="ffn_sample",
    )(h2, w_in, w_in, conv_w, conv_b, w_down, x1, buf2d, buf2d)


IN_PROJ_TN = 512
FFN_TM = 512
FFN_TF = 512


def _qk_gain_row(q_norm, k_norm):
    reps = N_GROUPS * ATT_HEADS
    return jnp.concatenate([jnp.tile(q_norm, reps), jnp.tile(k_norm, reps), jnp.zeros((NA - A_VA,), F32)])[None, :]


ATTN_PROMPT_TILING = ((4, 4, 1), (1, 4, 2), (1, 2, 4))


def _layer(x_prompt, x_sample, caches, norm_mix, w_in, q_norm, k_norm, b_igate, b_fgate, w_proj_att,
           w_proj_mlstm, w_out, norm_ffn, w_ffn_in, conv_w, conv_b, w_ffn_down):
    batch, seq, d = x_prompt.shape
    bd = x_sample.shape[0]
    assert x_sample.shape[1] == 1 and d == D_MODEL
    (ck1, cv1, ck2, cv2, ck3, cv3, st_c, st_n, st_m, st_conv) = caches

    wt_in = jnp.swapaxes(w_in, 0, 1)
    qk_gain = _qk_gain_row(q_norm, k_norm)
    g1 = norm_mix[None, :]
    g2 = norm_ffn[None, :]
    gate_bias = jnp.concatenate([b_igate, b_fgate, jnp.zeros((LANES - 2 * M_HEADS,), F32)])[None, :]
    cb = conv_b[None, :]

    xp = x_prompt.reshape(batch * seq, d)
    xs = x_sample.reshape(bd, d)

    h_all = _rmsnorm(xp, xs, g1, tm=2048, ts=512)
    (za_p, zb_p, zt_p, zif_p), (za_s, zb_s, zt_s, zif_s) = _in_proj(h_all, wt_in, qk_gain, tn=IN_PROJ_TN, ms=bd)
    to_cast = ((w_ffn_in,), (w_ffn_down,), (w_proj_att, w_proj_mlstm, w_out))
    o_p, lse_p, cast = zip(*[_attn_prompt_group(za_p, batch, seq, gi, dil, to_cast[gi], nsub=nsub, hw=hw, rb=rb)
                             for gi, ((_, dil), (nsub, hw, rb)) in enumerate(zip(DIL_PATTERNS, ATTN_PROMPT_TILING))])
    (w_ff,), (w_dn,), (wpa, wpm, wo) = cast
    hm_p, p_c, p_n, p_m = _mlstm_prompt(zif_p, zb_p, gate_bias, batch, seq, chunk=256)
    x1_p, h2_p = _merge(o_p, lse_p, hm_p, zt_p, xp, wpa, wpm, wo, g2, tm=256)
    y_p, tails = _ffn_prompt(h2_p, x1_p, w_ff, conv_w, cb, w_dn, seq, tm=FFN_TM, tf=FFN_TF)

    za_p3 = za_p.reshape(batch, seq, NA)
    p_kv = []
    for gi, (win, _) in enumerate(DIL_PATTERNS):
        keep = min(win, seq)
        for col in (A_KA, A_VA):
            lo = col + gi * ATT_GROUP_W
            p_kv.append(za_p3[:, seq - keep:, lo:lo + ATT_GROUP_W].reshape(batch, keep, ATT_HEADS, ATT_HEAD_DIM))
    tiles_per_seq = seq // FFN_TM
    p_conv = tails.reshape(batch, tiles_per_seq, 8, D_FF)[:, -1, 8 - (CONV_W - 1):, :]

    o_s, lse_s = _attn_sample(za_s, [(ck1, cv1), (ck2, cv2), (ck3, cv3)], bb=4)
    hm_s, s_c, s_n, s_m = _mlstm_sample(zif_s, zb_s, gate_bias, st_c, st_n, st_m, bb=4)
    x1_s, h2_s = _merge(o_s, lse_s, hm_s, zt_s, xs, wpa, wpm, wo, g2, tm=bd)
    y_s, g_s = _ffn_sample(h2_s, x1_s, w_ff, conv_w, cb, w_dn, st_conv, tf=FFN_TF)

    s_kv = []
    for gi in range(N_GROUPS):
        for col in (A_KA, A_VA):
            lo = col + gi * ATT_GROUP_W
            s_kv.append(za_s[:, lo:lo + ATT_GROUP_W].reshape(bd, 1, ATT_HEADS, ATT_HEAD_DIM))
    s_conv = jnp.stack([st_conv[:, 1, :], g_s], axis=1)

    p_state = p_kv + [p_c, p_n, p_m, p_conv]
    s_state = s_kv + [s_c, s_n, s_m, s_conv]
    return y_p.reshape(batch, seq, d), y_s.reshape(bd, 1, d), p_state, s_state


def kernel(x_prompt, x_sample, cache_k_w128, cache_v_w128, cache_k_w512, cache_v_w512, cache_k_w2048,
           cache_v_w2048, state_mlstm_C, state_mlstm_n, state_mlstm_m, state_ffn_conv, norm_mix, w_in, q_norm,
           k_norm, b_igate, b_fgate, w_proj_att, w_proj_mlstm, w_out, norm_ffn, w_ffn_in, conv_w, conv_b,
           w_ffn_down):
    assert norm_mix.shape[0] == 1
    caches = [c[0] for c in (cache_k_w128, cache_v_w128, cache_k_w512, cache_v_w512, cache_k_w2048,
                             cache_v_w2048, state_mlstm_C, state_mlstm_n, state_mlstm_m, state_ffn_conv)]
    weights = [w[0] for w in (norm_mix, w_in, q_norm, k_norm, b_igate, b_fgate, w_proj_att, w_proj_mlstm,
                              w_out, norm_ffn, w_ffn_in, conv_w, conv_b, w_ffn_down)]
    y_p, y_s, p_state, s_state = _layer(x_prompt, x_sample, caches, *weights)
    return (y_p, y_s, *[a[None] for a in p_state], *[a[None] for a in s_state])
```

```python
import functools

import jax
import jax.numpy as jnp
from jax import lax
from jax.experimental import pallas as pl
from jax.experimental.pallas import tpu as pltpu

F32 = jnp.float32
BF16 = jnp.bfloat16

RMS_EPS = 1e-6
NEG_INF = -1e30
LANES = 128
VMEM_LIMIT = 56 * 1024 * 1024

D_MODEL = 2048
DIL_PATTERNS = ((128, 1), (512, 4), (2048, 16))
N_GROUPS = 3
ATT_HEADS = 4
ATT_HEAD_DIM = 128
ATT_SPAN = 128
ATT_SCALE = ATT_HEAD_DIM ** -0.5
ATT_GROUP_W = ATT_HEADS * ATT_HEAD_DIM
ATT_WIDTH = N_GROUPS * ATT_GROUP_W
M_HEADS = 4
M_QK_DIM = D_MODEL // (2 * M_HEADS)
M_V_DIM = D_MODEL // M_HEADS
M_QK_WIDTH = M_HEADS * M_QK_DIM
M_V_WIDTH = M_HEADS * M_V_DIM
M_K_SCALE = M_QK_DIM ** -0.5
D_FF = ((8 * D_MODEL // 3 + 255) // 256) * 256
CONV_W = 3
SPLIT_SIZES = (ATT_WIDTH, ATT_WIDTH, ATT_WIDTH, M_QK_WIDTH, M_QK_WIDTH, M_V_WIDTH, M_V_WIDTH,
               M_HEADS, M_HEADS, D_MODEL, D_MODEL)

IN_COLS = sum(SPLIT_SIZES)
A_QA = 0
A_KA = A_QA + ATT_WIDTH
A_VA = A_KA + ATT_WIDTH
NA = A_VA + ATT_WIDTH
B_QM = 0
B_KM = B_QM + M_QK_WIDTH
B_VM = B_KM + M_QK_WIDTH
B_OM = B_VM + M_V_WIDTH
NB = B_OM + M_V_WIDTH
N_IF = 2 * M_HEADS
T_GA = 0
T_GB = T_GA + D_MODEL
NT = T_GB + D_MODEL
assert NA + NB + N_IF + NT == IN_COLS


def _cparams(sem):
    return pltpu.CompilerParams(dimension_semantics=sem, vmem_limit_bytes=VMEM_LIMIT)


def _log_sigmoid(x):
    return jnp.minimum(x, 0.0) - jnp.log(1.0 + jnp.exp(-jnp.abs(x)))


def _rmsnorm_kernel(x_ref, xs_ref, g_ref, h_ref, *, nsub):
    j = pl.program_id(1)
    ts = x_ref.shape[0]

    def norm(x):
        ms = jnp.mean(x * x, axis=-1, keepdims=True)
        return (x * lax.rsqrt(ms + RMS_EPS) * g_ref[...]).astype(h_ref.dtype)

    @pl.when(j < nsub)
    def _():
        h_ref[pl.ds(pl.multiple_of(j * ts, ts), ts), :] = norm(x_ref[...])

    @pl.when(j == nsub)
    def _():
        h_ref[nsub * ts:, :] = norm(xs_ref[...])


def _rmsnorm(x, xs, gain, *, tm, ts):
    m, d = x.shape
    ms = xs.shape[0]
    assert m % tm == 0 and tm % ts == 0
    nsub = tm // ts
    last = m // ts - 1
    return pl.pallas_call(
        functools.partial(_rmsnorm_kernel, nsub=nsub),
        grid=(m // tm, nsub + 1),
        in_specs=[pl.BlockSpec((ts, d), lambda i, j: (jnp.minimum(i * nsub + j, last), 0)),
                  pl.BlockSpec((ms, d), lambda i, j: (0, 0)),
                  pl.BlockSpec((1, d), lambda i, j: (0, 0))],
        out_specs=pl.BlockSpec((None, tm + ms, d), lambda i, j: (i, 0, 0)),
        out_shape=jax.ShapeDtypeStruct((m // tm, tm + ms, d), BF16),
        compiler_params=_cparams(("parallel", "arbitrary")),
        name="rmsnorm",
    )(x, xs, gain)


def _in_proj_kernel(h_ref, w_ref, wn_ref, wif_ref, qkg_ref, za_ref, zb_ref, zt_ref, zif_ref,
                    sa_ref, sb_ref, st_ref, sif_ref, *, n_norm, na, nb, nt):
    j = pl.program_id(1)
    tp = za_ref.shape[0]

    def z_of(w):
        return lax.dot_general(h_ref[...], w.astype(BF16), (((1,), (1,)), ((), ())), preferred_element_type=F32)

    @pl.when(j < n_norm)
    def _():
        z = z_of(w_ref[...])
        for c in range(za_ref.shape[1] // LANES):
            sl = slice(c * LANES, (c + 1) * LANES)
            zc = z[:, sl]
            ms = jnp.mean(zc * zc, axis=-1, keepdims=True)
            zn = zc * lax.rsqrt(ms + RMS_EPS) * qkg_ref[:, sl]
            za_ref[:, sl] = zn[:tp]
            sa_ref[j, :, sl] = zn[tp:]

    @pl.when((j >= n_norm) & (j < na))
    def _():
        z = z_of(w_ref[...])
        za_ref[...] = z[:tp]
        sa_ref[j] = z[tp:]

    @pl.when((j >= na) & (j < na + nb))
    def _():
        z = z_of(w_ref[...])
        zb_ref[...] = z[:tp].astype(zb_ref.dtype)
        sb_ref[j - na] = z[tp:]

    @pl.when((j >= na + nb) & (j < na + nb + nt))
    def _():
        z = z_of(jnp.concatenate([w_ref[N_IF:, :], wn_ref[...]], axis=0))
        zt_ref[...] = z[:tp].astype(zt_ref.dtype)
        st_ref[j - na - nb] = z[tp:]

    @pl.when(j == na + nb + nt)
    def _():
        z = z_of(jnp.concatenate([wif_ref[...], jnp.zeros((LANES - N_IF, wif_ref.shape[1]), F32)], axis=0))
        zif_ref[...] = z[:tp]
        sif_ref[...] = z[tp:]


def _in_proj(h, wt, qk_gain, *, tn, ms):
    ni, th, d = h.shape
    tm = th - ms
    m = ni * tm
    assert wt.shape == (IN_COLS, d) and N_IF == 8
    assert NA % tn == 0 and NB % tn == 0 and NT % tn == 0 and A_VA % tn == 0 and tn % N_IF == 0
    na, nb, nt = NA // tn, NB // tn, NT // tn
    nw = na + nb + nt
    g_if = (NA + NB) // N_IF
    g_tn = tn // N_IF
    kern = functools.partial(_in_proj_kernel, n_norm=A_VA // tn, na=na, nb=nb, nt=nt)
    col_a = lambda j: jnp.minimum(j, na - 1)
    col_b = lambda j: jnp.clip(j - na, 0, nb - 1)
    col_t = lambda j: jnp.clip(j - na - nb, 0, nt - 1)
    out_specs = [pl.BlockSpec((None, tm, tn), lambda i, j: (i, 0, col_a(j))),
                 pl.BlockSpec((None, tm, tn), lambda i, j: (i, 0, col_b(j))),
                 pl.BlockSpec((None, tm, tn), lambda i, j: (i, 0, col_t(j))),
                 pl.BlockSpec((None, tm, LANES), lambda i, j: (i, 0, 0))]
    out_shape = [jax.ShapeDtypeStruct((ni, tm, w), dt)
                 for w, dt in zip((NA, NB, NT, LANES), (F32, BF16, BF16, F32))]
    for n_tiles in (na, nb, nt):
        out_specs.append(pl.BlockSpec((None, n_tiles, ms, tn), lambda i, j: (i, 0, 0, 0)))
        out_shape.append(jax.ShapeDtypeStruct((ni, n_tiles, ms, tn), F32))
    out_specs.append(pl.BlockSpec((None, ms, LANES), lambda i, j: (i, 0, 0)))
    out_shape.append(jax.ShapeDtypeStruct((ni, ms, LANES), F32))
    outs = pl.pallas_call(
        kern,
        grid=(ni, nw + 1),
        in_specs=[
            pl.BlockSpec((None, th, d), lambda i, j: (i, 0, 0)),
            pl.BlockSpec((tn, d), lambda i, j: (jnp.minimum(j, nw - 1), 0)),
            pl.BlockSpec((N_IF, d), lambda i, j: (g_if + g_tn * (jnp.clip(j, na + nb, nw - 1) - (na + nb) + 1), 0)),
            pl.BlockSpec((N_IF, d), lambda i, j: (g_if, 0)),
            pl.BlockSpec((1, tn), lambda i, j: (0, col_a(j))),
        ],
        out_specs=out_specs,
        out_shape=out_shape,
        compiler_params=_cparams(("parallel", "arbitrary")),
        name="in_proj",
    )(h, wt, wt, wt, qk_gain)
    sample = [o[0].swapaxes(0, 1).reshape(ms, -1) for o in outs[4:7]] + [outs[7][0]]
    return [o.reshape(m, o.shape[2]) for o in outs[:4]], sample


def _attn_prompt_kernel(*refs, dil, nsub, hw, rb, n_cast):
    q_refs, kp_refs, kc_refs, vp_refs, vc_refs = [refs[t * hw:(t + 1) * hw] for t in range(5)]
    cast_in = refs[5 * hw:5 * hw + n_cast]
    o_ref, lse_ref = refs[5 * hw + n_cast:5 * hw + n_cast + 2]
    cast_out = refs[5 * hw + n_cast + 2:5 * hw + 2 * n_cast + 2]
    o_refs = refs[5 * hw + 2 * n_cast + 2:]
    for w_ref, wb_ref in zip(cast_in, cast_out):
        wb_ref[...] = w_ref[...].astype(wb_ref.dtype)
    c = pl.program_id(1)
    blk = ATT_SPAN
    step = blk * dil
    qi = lax.broadcasted_iota(jnp.int32, (blk, 2 * blk), 0)
    kj = lax.broadcasted_iota(jnp.int32, (blk, 2 * blk), 1)
    band = (kj >= qi) & (kj <= qi + blk)
    first_bias = jnp.where((kj < blk) & (c == 0), NEG_INF, 0.0)
    lane = lax.broadcasted_iota(jnp.int32, (blk, LANES), 1)

    def rows(base, r):
        return pl.ds(base + r, blk) if dil == 1 else pl.ds(base + r, blk, stride=dil)

    def group(r0, s):
        base = s * step
        ids = [(r0 + r, n) for r in range(rb) for n in range(hw)]

        def stacked(cur_refs, prev_refs):
            parts = []
            for r, n in ids:
                cur = cur_refs[n][rows(base, r), :]
                if prev_refs is None:
                    parts.append(cur)
                else:
                    prv = prev_refs[n][rows(0, r), :] if s == 0 else cur_refs[n][rows(base - step, r), :]
                    parts.append(jnp.concatenate([prv, cur], axis=0))
            return jnp.stack(parts).astype(BF16)

        q = stacked(q_refs, None)
        k = stacked(kc_refs, kp_refs)
        v = stacked(vc_refs, vp_refs)
        sc = jnp.einsum('bqe,bke->bqk', q, k, preferred_element_type=F32) * ATT_SCALE
        if s == 0:
            sc = sc + first_bias
        sc = jnp.where(band, sc, NEG_INF)
        m = jnp.max(sc, axis=-1, keepdims=True)
        p = jnp.exp(sc - m)
        l = jnp.sum(p, axis=-1, keepdims=True)
        o = jnp.einsum('bqk,bke->bqe', p.astype(BF16), v, preferred_element_type=F32) / l
        lse = m + jnp.log(l)
        for r in range(rb):
            lse_tile = jnp.zeros((blk, LANES), F32)
            for n in range(hw):
                b = r * hw + n
                o_refs[n][rows(base, r0 + r), :] = o[b]
                lse_tile = jnp.where(lane == n, lse[b], lse_tile)
            lse_ref[rows(base, r0 + r), :] = lse_tile

    for s in range(nsub):
        if dil == rb:
            group(0, s)
        else:
            def body(it, carry, s=s):
                group(it * rb, s)
                return carry
            lax.fori_loop(0, dil // rb, body, 0)

    for n in range(hw):
        o_ref[:, n * ATT_HEAD_DIM:(n + 1) * ATT_HEAD_DIM] = o_refs[n][...]


def _attn_prompt_group(za, batch, seq, gi, dil, to_cast, *, nsub, hw, rb):
    step = ATT_SPAN * dil
    tc = nsub * step
    assert seq % tc == 0 and ATT_HEADS % hw == 0 and dil % rb == 0
    nh = ATT_HEADS // hw
    e = ATT_HEAD_DIM
    nchunk = seq // tc
    n_steps = batch * nchunk * nh
    bf16_sublanes = 16
    cast_in_specs, cast_out_specs, cast_shapes = [], [], []
    for w in to_cast:
        rows, cols = w.shape
        assert rows % (n_steps * bf16_sublanes) == 0
        spec = pl.BlockSpec((rows // n_steps, cols), lambda b, c, h: ((b * nchunk + c) * nh + h, 0))
        cast_in_specs.append(spec)
        cast_out_specs.append(spec)
        cast_shapes.append(jax.ShapeDtypeStruct((rows, cols), BF16))
    z3 = za.reshape(batch, seq, NA)
    qc, kc, vc = [(col + gi * ATT_GROUP_W) // e for col in (A_QA, A_KA, A_VA)]

    def cur(col):
        return [pl.BlockSpec((None, tc, e), lambda b, c, h, n=n: (b, c, col + h * hw + n)) for n in range(hw)]

    def prev(col):
        return [pl.BlockSpec((None, step, e), lambda b, c, h, n=n: (b, jnp.maximum(c * nsub - 1, 0), col + h * hw + n))
                for n in range(hw)]

    kern = functools.partial(_attn_prompt_kernel, dil=dil, nsub=nsub, hw=hw, rb=rb, n_cast=len(to_cast))
    o, lse, *cast = pl.pallas_call(
        kern,
        grid=(batch, nchunk, nh),
        in_specs=cur(qc) + prev(kc) + cur(kc) + prev(vc) + cur(vc) + cast_in_specs,
        out_specs=[pl.BlockSpec((None, tc, hw * e), lambda b, c, h: (b, c, h)),
                   pl.BlockSpec((None, tc, LANES), lambda b, c, h: (b, c, h))] + cast_out_specs,
        out_shape=[jax.ShapeDtypeStruct((batch, seq, ATT_GROUP_W), F32),
                   jax.ShapeDtypeStruct((batch, seq, nh * LANES), F32)] + cast_shapes,
        scratch_shapes=[pltpu.VMEM((tc, e), F32)] * hw,
        compiler_params=_cparams(("parallel", "parallel", "parallel")),
        name=f"attn_prompt_g{gi}",
    )(*([z3] * (5 * hw)), *to_cast)
    return o.reshape(batch * seq, ATT_GROUP_W), lse.reshape(batch * seq, nh * LANES), cast


def _attn_sample_kernel(z_ref, k1, v1, k2, v2, k3, v3, o1, o2, o3, l1, l2, l3):
    bb = z_ref.shape[0]
    bufs = ((k1, v1, o1, l1), (k2, v2, o2, l2), (k3, v3, o3, l3))
    lane = lax.broadcasted_iota(jnp.int32, (1, LANES), 1)

    def heads(b, col):
        return jnp.concatenate([z_ref[b, :, col + n * ATT_HEAD_DIM:col + (n + 1) * ATT_HEAD_DIM]
                                for n in range(ATT_HEADS)], axis=0)

    def body(b, carry):
        for gi, (k_ref, v_ref, o_ref, l_ref) in enumerate(bufs):
            q = heads(b, A_QA + gi * ATT_GROUP_W)
            k_new = heads(b, A_KA + gi * ATT_GROUP_W)
            v_new = heads(b, A_VA + gi * ATT_GROUP_W)
            kb = k_ref[b].reshape(ATT_SPAN // 2, 2 * ATT_HEADS, ATT_HEAD_DIM)
            vb = v_ref[b].reshape(ATT_SPAN // 2, 2 * ATT_HEADS, ATT_HEAD_DIM)
            q2 = jnp.concatenate([q, q], axis=0)
            s = jnp.sum(kb * q2[None], axis=-1, keepdims=True) * ATT_SCALE
            s_new = jnp.sum(k_new * q, axis=-1, keepdims=True) * ATT_SCALE
            m2 = jnp.max(s, axis=0)
            m = jnp.maximum(jnp.maximum(m2[:ATT_HEADS], m2[ATT_HEADS:]), s_new)
            p = jnp.exp(s - jnp.concatenate([m, m], axis=0)[None])
            p_new = jnp.exp(s_new - m)
            l2 = jnp.sum(p, axis=0)
            l = l2[:ATT_HEADS] + l2[ATT_HEADS:] + p_new
            o2 = jnp.sum(p * vb, axis=0)
            o = (o2[:ATT_HEADS] + o2[ATT_HEADS:] + p_new * v_new) / l
            lse = m + jnp.log(l)
            lse_row = jnp.zeros((1, LANES), F32)
            for n in range(ATT_HEADS):
                o_ref[b, :, n * ATT_HEAD_DIM:(n + 1) * ATT_HEAD_DIM] = o[n:n + 1, :]
                lse_row = jnp.where(lane == n, lse[n:n + 1, :], lse_row)
            l_ref[b] = lse_row
        return carry

    lax.fori_loop(0, bb, body, 0)


def _attn_sample(za_s, caches, *, bb):
    bd = za_s.shape[0]
    assert bd % bb == 0
    ins, in_specs = [za_s.reshape(bd, 1, NA)], [pl.BlockSpec((bb, 1, NA), lambda i: (i, 0, 0))]
    for (k_buf, v_buf), (win, dil) in zip(caches, DIL_PATTERNS):
        assert k_buf.shape[1:] == (ATT_SPAN * dil, ATT_HEADS, ATT_HEAD_DIM)
        for buf in (k_buf, v_buf):
            ins.append(buf.reshape(bd, ATT_SPAN, dil, ATT_HEADS, ATT_HEAD_DIM))
            in_specs.append(pl.BlockSpec((bb, ATT_SPAN, None, ATT_HEADS, ATT_HEAD_DIM),
                                         lambda i: (i, 0, 0, 0, 0)))
    outs = pl.pallas_call(
        _attn_sample_kernel,
        grid=(bd // bb,),
        in_specs=in_specs,
        out_specs=[pl.BlockSpec((bb, 1, ATT_GROUP_W), lambda i: (i, 0, 0))] * 3
                  + [pl.BlockSpec((bb, 1, LANES), lambda i: (i, 0, 0))] * 3,
        out_shape=[jax.ShapeDtypeStruct((bd, 1, ATT_GROUP_W), F32)] * 3
                  + [jax.ShapeDtypeStruct((bd, 1, LANES), F32)] * 3,
        compiler_params=_cparams(("parallel",)),
        name="attn_sample",
    )(*ins)
    return [o[:, 0, :] for o in outs[:3]], [l[:, 0, :] for l in outs[3:]]


def _mlstm_prompt_kernel(q_ref, k_ref, v_ref, om_ref, g_ref, bias_ref, h_ref, c_out, n_out, m_out,
                         c_s, n_s, m_s):
    hd = pl.program_id(1)
    ci = pl.program_id(2)
    L = q_ref.shape[0]

    @pl.when(ci == 0)
    def _():
        c_s[...] = jnp.zeros_like(c_s)
        n_s[...] = jnp.zeros_like(n_s)
        m_s[...] = jnp.zeros_like(m_s)

    gates = g_ref[...] + bias_ref[...]
    lane = lax.broadcasted_iota(jnp.int32, (L, LANES), 1)
    li_col = jnp.sum(jnp.where(lane == hd, gates, 0.0), axis=1, keepdims=True)
    lf_col = _log_sigmoid(jnp.sum(jnp.where(lane == hd + M_HEADS, gates, 0.0), axis=1, keepdims=True))
    gates_t = gates.T
    sub = lax.broadcasted_iota(jnp.int32, (LANES, L), 0)
    li_row = jnp.sum(jnp.where(sub == hd, gates_t, 0.0), axis=0, keepdims=True)
    lf_row = _log_sigmoid(jnp.sum(jnp.where(sub == hd + M_HEADS, gates_t, 0.0), axis=0, keepdims=True))

    ti = lax.broadcasted_iota(jnp.int32, (L, L), 0)
    si = lax.broadcasted_iota(jnp.int32, (L, L), 1)
    causal = si <= ti
    b_col = jnp.sum(jnp.where(causal, lf_row, 0.0), axis=1, keepdims=True)
    b_row = jnp.sum(jnp.where(ti <= si, lf_col, 0.0), axis=0, keepdims=True)
    b_end = jnp.sum(lf_row, axis=1, keepdims=True)

    m_prev = m_s[...]
    dmat = jnp.where(causal, b_col - b_row + li_row, NEG_INF)
    inter = b_col + m_prev
    mt = jnp.maximum(inter, jnp.max(dmat, axis=1, keepdims=True))

    qb = q_ref[...]
    q = qb.astype(F32)
    k = k_ref[...].astype(F32) * M_K_SCALE
    vb = v_ref[...]
    qk = lax.dot_general(qb, k.astype(BF16), (((1,), (1,)), ((), ())), preferred_element_type=F32)
    a = jnp.exp(dmat - mt) * qk
    w_inter = jnp.exp(inter - mt)
    num = (jnp.dot(a.astype(BF16), vb, preferred_element_type=F32)
           + w_inter * jnp.dot(qb, c_s[...].astype(BF16), preferred_element_type=F32))
    den = jnp.sum(a, axis=1, keepdims=True) + w_inter * jnp.sum(q * n_s[...], axis=1, keepdims=True)
    h = num / jnp.maximum(jnp.abs(den), jnp.exp(-mt))
    h_ref[...] = (jax.nn.sigmoid(om_ref[...].astype(F32)) * h).astype(h_ref.dtype)

    g_col = b_end - b_col + li_col
    g_row = b_end - b_row + li_row
    m_new = jnp.maximum(b_end + m_prev, jnp.max(g_row, axis=1, keepdims=True))
    decay = jnp.exp(b_end + m_prev - m_new)
    kw = jnp.exp(g_col - m_new) * k
    c_s[...] = decay * c_s[...] + jnp.dot(kw.T.astype(BF16), vb, preferred_element_type=F32)
    n_s[...] = decay * n_s[...] + jnp.sum(kw, axis=0, keepdims=True)
    m_s[...] = m_new

    @pl.when(ci == pl.num_programs(2) - 1)
    def _():
        c_out[...] = c_s[...]
        n_out[...] = n_s[...]
        m_out[...] = jnp.broadcast_to(m_s[...], m_out.shape)


def _mlstm_prompt(zif, zb, gate_bias, batch, seq, *, chunk):
    assert seq % chunk == 0
    nc = seq // chunk
    zif3 = zif.reshape(batch, seq, LANES)
    zb3 = zb.reshape(batch, seq, NB)
    qk_blk = lambda col: pl.BlockSpec((None, chunk, M_QK_DIM), lambda b, h, c: (b, c, col // M_QK_DIM + h))
    v_blk = lambda col: pl.BlockSpec((None, chunk, M_V_DIM), lambda b, h, c: (b, c, col // M_V_DIM + h))
    hm, c1, n1, m1 = pl.pallas_call(
        _mlstm_prompt_kernel,
        grid=(batch, M_HEADS, nc),
        in_specs=[qk_blk(B_QM), qk_blk(B_KM), v_blk(B_VM), v_blk(B_OM),
                  pl.BlockSpec((None, chunk, LANES), lambda b, h, c: (b, c, 0)),
                  pl.BlockSpec((1, LANES), lambda b, h, c: (0, 0))],
        out_specs=[pl.BlockSpec((None, chunk, M_V_DIM), lambda b, h, c: (b, c, h)),
                   pl.BlockSpec((None, None, M_QK_DIM, M_V_DIM), lambda b, h, c: (b, h, 0, 0)),
                   pl.BlockSpec((None, None, 1, M_QK_DIM), lambda b, h, c: (b, h, 0, 0)),
                   pl.BlockSpec((None, None, 1, LANES), lambda b, h, c: (b, h, 0, 0))],
        out_shape=[jax.ShapeDtypeStruct((batch, seq, M_V_WIDTH), BF16),
                   jax.ShapeDtypeStruct((batch, M_HEADS, M_QK_DIM, M_V_DIM), F32),
                   jax.ShapeDtypeStruct((batch, M_HEADS, 1, M_QK_DIM), F32),
                   jax.ShapeDtypeStruct((batch, M_HEADS, 1, LANES), F32)],
        scratch_shapes=[pltpu.VMEM((M_QK_DIM, M_V_DIM), F32), pltpu.VMEM((1, M_QK_DIM), F32),
                        pltpu.VMEM((1, 1), F32)],
        compiler_params=_cparams(("parallel", "parallel", "arbitrary")),
        name="mlstm_prompt",
    )(zb3, zb3, zb3, zb3, zif3, gate_bias)
    return hm.reshape(batch * seq, M_V_WIDTH), c1, n1[:, :, 0, :], m1[:, :, 0, 0]


def _row_to_col(row):
    n = row.shape[1]
    return jnp.broadcast_to(row, (LANES, n)).T[:, 0:1]


def _mlstm_sample_kernel(zif_ref, zb_ref, bias_ref, c_ref, n_ref, m_ref, h_ref, c_out, n_out, m_out):
    for b in range(zif_ref.shape[0]):
        _mlstm_sample_one(zif_ref.at[b], zb_ref.at[b], bias_ref, c_ref.at[b], n_ref.at[b], m_ref.at[b],
                          h_ref.at[b], c_out.at[b], n_out.at[b], m_out.at[b])


def _mlstm_sample_one(zif_ref, zb_ref, bias_ref, c_ref, n_ref, m_ref, h_ref, c_out, n_out, m_out):
    gates = zif_ref[...] + bias_ref[...]
    lane = lax.broadcasted_iota(jnp.int32, (1, LANES), 1)
    m_row = jnp.zeros((1, LANES), F32)
    for h in range(M_HEADS):
        q = zb_ref[:, B_QM + h * M_QK_DIM:B_QM + (h + 1) * M_QK_DIM].astype(F32)
        k = zb_ref[:, B_KM + h * M_QK_DIM:B_KM + (h + 1) * M_QK_DIM].astype(F32) * M_K_SCALE
        v = zb_ref[:, B_VM + h * M_V_DIM:B_VM + (h + 1) * M_V_DIM].astype(F32)
        om = zb_ref[:, B_OM + h * M_V_DIM:B_OM + (h + 1) * M_V_DIM].astype(F32)
        li = gates[:, h:h + 1]
        lf = _log_sigmoid(gates[:, M_HEADS + h:M_HEADS + h + 1])
        m0 = m_ref[:, h:h + 1]
        c0 = c_ref[h]
        n0 = n_ref[h:h + 1, :]
        inter = lf + m0
        mt = jnp.maximum(inter, li)
        a = jnp.exp(li - mt) * jnp.sum(q * k, axis=1, keepdims=True)
        w_inter = jnp.exp(inter - mt)
        q_c = jnp.sum(_row_to_col(q) * c0, axis=0, keepdims=True)
        num = a * v + w_inter * q_c
        den = a + w_inter * jnp.sum(q * n0, axis=1, keepdims=True)
        hv = num / jnp.maximum(jnp.abs(den), jnp.exp(-mt))
        h_ref[:, h * M_V_DIM:(h + 1) * M_V_DIM] = (jax.nn.sigmoid(om) * hv).astype(h_ref.dtype)
        m_new = jnp.maximum(inter, li)
        decay = jnp.exp(inter - m_new)
        ws = jnp.exp(li - m_new)
        c_out[h] = decay * c0 + _row_to_col(ws * k) * v
        n_out[h:h + 1, :] = decay * n0 + ws * k
        m_row = jnp.where(lane == h, m_new, m_row)
    m_out[...] = m_row


def _mlstm_sample(zif_s, zb_s, gate_bias, c0, n0, m0, *, bb):
    bd = zif_s.shape[0]
    assert bd % bb == 0
    hm, c1, n1, m1 = pl.pallas_call(
        _mlstm_sample_kernel,
        grid=(bd // bb,),
        in_specs=[pl.BlockSpec((bb, 1, LANES), lambda b: (b, 0, 0)),
                  pl.BlockSpec((bb, 1, NB), lambda b: (b, 0, 0)),
                  pl.BlockSpec((1, LANES), lambda b: (0, 0)),
                  pl.BlockSpec((bb, M_HEADS, M_QK_DIM, M_V_DIM), lambda b: (b, 0, 0, 0)),
                  pl.BlockSpec((bb, M_HEADS, M_QK_DIM), lambda b: (b, 0, 0)),
                  pl.BlockSpec((bb, 1, M_HEADS), lambda b: (b, 0, 0))],
        out_specs=[pl.BlockSpec((bb, 1, M_V_WIDTH), lambda b: (b, 0, 0)),
                   pl.BlockSpec((bb, M_HEADS, M_QK_DIM, M_V_DIM), lambda b: (b, 0, 0, 0)),
                   pl.BlockSpec((bb, M_HEADS, M_QK_DIM), lambda b: (b, 0, 0)),
                   pl.BlockSpec((bb, 1, LANES), lambda b: (b, 0, 0))],
        out_shape=[jax.ShapeDtypeStruct((bd, 1, M_V_WIDTH), BF16),
                   jax.ShapeDtypeStruct((bd, M_HEADS, M_QK_DIM, M_V_DIM), F32),
                   jax.ShapeDtypeStruct((bd, M_HEADS, M_QK_DIM), F32),
                   jax.ShapeDtypeStruct((bd, 1, LANES), F32)],
        compiler_params=_cparams(("parallel",)),
        name="mlstm_sample",
    )(zif_s.reshape(bd, 1, LANES), zb_s.reshape(bd, 1, NB), gate_bias, c0, n0, m0.reshape(bd, 1, M_HEADS))
    return hm.reshape(bd, M_V_WIDTH), c1, n1, m1[:, 0, :M_HEADS]


def _merge_kernel(o1, o2, o3, l1, l2, l3, hm_ref, ga_ref, gb_ref, x_ref, wpa_ref, wpm_ref, wo_ref, g2_ref,
                  *rest, hws):
    n_cast = (len(rest) - 2) // 2
    x1_ref, h2_ref = rest[n_cast:n_cast + 2]
    for w_ref, wb_ref in zip(rest[:n_cast], rest[n_cast + 2:]):
        wb_ref[...] = w_ref[...].astype(wb_ref.dtype)
    parts = []
    for n in range(ATT_HEADS):
        sl = slice(n * ATT_HEAD_DIM, (n + 1) * ATT_HEAD_DIM)
        ls = []
        for l_ref, hw in zip((l1, l2, l3), hws):
            ln = (n // hw) * LANES + n % hw
            ls.append(l_ref[:, ln:ln + 1])
        mx = jnp.maximum(jnp.maximum(ls[0], ls[1]), ls[2])
        es = [jnp.exp(l - mx) for l in ls]
        tot = es[0] + es[1] + es[2]
        parts.append((es[0] / tot) * o1[:, sl] + (es[1] / tot) * o2[:, sl] + (es[2] / tot) * o3[:, sl])
    att = jnp.concatenate(parts, axis=1).astype(BF16)
    pa = jnp.dot(att, wpa_ref[...], preferred_element_type=F32)
    pm = jnp.dot(hm_ref[...], wpm_ref[...], preferred_element_type=F32)
    merged = (jax.nn.sigmoid(ga_ref[...].astype(F32)) * pa + jax.nn.sigmoid(gb_ref[...].astype(F32)) * pm)
    x1 = x_ref[...] + jnp.dot(merged.astype(BF16), wo_ref[...], preferred_element_type=F32)
    x1_ref[...] = x1
    ms = jnp.mean(x1 * x1, axis=-1, keepdims=True)
    h2_ref[...] = (x1 * lax.rsqrt(ms + RMS_EPS) * g2_ref[...]).astype(BF16)


def _merge(o_list, lse_list, hm, zt, x, wpa, wpm, wo, g2, to_cast=(), *, tm):
    m, d = x.shape
    assert m % tm == 0 and zt.shape[1] == NT
    n_steps = m // tm
    row_blk = lambda w: pl.BlockSpec((tm, w), lambda i: (i, 0))
    const = lambda shape: pl.BlockSpec(shape, lambda i: (0, 0), pipeline_mode=pl.Buffered(1))
    hws = tuple(ATT_HEADS * LANES // l.shape[1] for l in lse_list)
    cast_specs, cast_shapes = [], []
    for w in to_cast:
        rows, cols = w.shape
        assert rows % (n_steps * 16) == 0
        cast_specs.append(pl.BlockSpec((rows // n_steps, cols), lambda i: (i, 0)))
        cast_shapes.append(jax.ShapeDtypeStruct((rows, cols), BF16))
    x1, h2, *cast = pl.pallas_call(
        functools.partial(_merge_kernel, hws=hws),
        grid=(n_steps,),
        in_specs=[row_blk(ATT_GROUP_W)] * 3 + [row_blk(l.shape[1]) for l in lse_list]
                 + [row_blk(M_V_WIDTH),
                    pl.BlockSpec((tm, d), lambda i: (i, T_GA // d)),
                    pl.BlockSpec((tm, d), lambda i: (i, T_GB // d)),
                    row_blk(d),
                    const(wpa.shape), const(wpm.shape), const(wo.shape), const((1, d))] + cast_specs,
        out_specs=[row_blk(d), row_blk(d)] + cast_specs,
        out_shape=[jax.ShapeDtypeStruct((m, d), F32), jax.ShapeDtypeStruct((m, d), BF16)] + cast_shapes,
        compiler_params=_cparams(("parallel",)),
        name="merge",
    )(*o_list, *lse_list, hm, zt, zt, x, wpa, wpm, wo, g2, *to_cast)
    return x1, h2, cast


def _ffn_body(h2_ref, wu_ref, wg_ref, cw_ref, cb_ref, wd_ref, x1_ref, y_ref, g_prev2, g_prev1):
    j = pl.program_id(1)
    h2 = h2_ref[...]
    u = jnp.dot(h2, wu_ref[...], preferred_element_type=F32)
    g = jnp.dot(h2, wg_ref[...], preferred_element_type=F32)
    gconv = cb_ref[...] + ((g_prev2(g) * cw_ref[0:1, :] + g_prev1(g) * cw_ref[1:2, :]) + g * cw_ref[2:3, :])
    act = (jax.nn.gelu(gconv) * u).astype(BF16)
    down = jnp.dot(act, wd_ref[...], preferred_element_type=F32)

    @pl.when(j == 0)
    def _():
        y_ref[...] = x1_ref[...] + down

    @pl.when(j > 0)
    def _():
        y_ref[...] += down

    return g


def _ffn_prompt_kernel(h2_ref, wu_ref, wg_ref, cw_ref, cb_ref, wd_ref, x1_ref, y_ref, tail_ref, prev_s,
                       u0_s, g0_s, u1_s, g1_s, *, tiles_per_seq, nf):
    i = pl.program_id(0)
    j = pl.program_id(1)
    tm = h2_ref.shape[0]
    slots = ((u0_s, g0_s), (u1_s, g1_s))

    def up(u_s, g_s):
        h2 = h2_ref[...]
        u_s[...] = jnp.dot(h2, wu_ref[...], preferred_element_type=F32)
        g_s[...] = jnp.dot(h2, wg_ref[...], preferred_element_type=F32)

    def down(u_s, g_s):
        jt = j - 1
        prev = jnp.where(i % tiles_per_seq == 0, 0.0, prev_s[jt])
        p2, p1 = prev[6:7, :], prev[7:8, :]
        g = g_s[...]
        r = lax.broadcasted_iota(jnp.int32, g.shape, 0)
        g_m1 = jnp.where(r == 0, p1, pltpu.roll(g, 1, axis=0))
        g_m2 = jnp.where(r == 0, p2, jnp.where(r == 1, p1, pltpu.roll(g, 2, axis=0)))
        cw = cw_ref[jt]
        gconv = cb_ref[jt] + ((g_m2 * cw[0:1, :] + g_m1 * cw[1:2, :]) + g * cw[2:3, :])
        act = (jax.nn.gelu(gconv) * u_s[...]).astype(BF16)
        y_ref[...] += jnp.dot(act, wd_ref[...], preferred_element_type=F32)
        prev_s[jt] = g[tm - 8:tm, :]

    @pl.when(j == 0)
    def _():
        y_ref[...] = x1_ref[...]
        up(*slots[0])

    for par in range(2):
        @pl.when((j > 0) & (j < nf) & (j % 2 == par))
        def _(par=par):
            up(*slots[par])
            down(*slots[1 - par])

    @pl.when(j == nf)
    def _():
        down(*slots[(nf - 1) % 2])
        tail_ref[...] = prev_s[...]


def _ffn_sample_kernel(h2_ref, wu_ref, wg_ref, cw_ref, cb_ref, wd_ref, x1_ref, b2_ref, b1_ref, y_ref, g_ref):
    g = _ffn_body(h2_ref, wu_ref, wg_ref, cw_ref, cb_ref, wd_ref, x1_ref, y_ref,
                  lambda g: b2_ref[...], lambda g: b1_ref[...])
    g_ref[...] = g


def _ffn_specs(tm, tf, d, nf):
    return [pl.BlockSpec((tm, d), lambda i, j: (i, 0)),
            pl.BlockSpec((d, tf), lambda i, j: (0, j)),
            pl.BlockSpec((d, tf), lambda i, j: (0, nf + j)),
            pl.BlockSpec((CONV_W, tf), lambda i, j: (0, j)),
            pl.BlockSpec((1, tf), lambda i, j: (0, j)),
            pl.BlockSpec((tf, d), lambda i, j: (j, 0)),
            pl.BlockSpec((tm, d), lambda i, j: (i, 0))]


def _ffn_prompt(h2, x1, w_in, conv_w, conv_b, w_down, seq, *, tm, tf):
    m, d = x1.shape
    assert m % tm == 0 and seq % tm == 0 and D_FF % tf == 0 and tm % 8 == 0
    nf = D_FF // tf
    kern = functools.partial(_ffn_prompt_kernel, tiles_per_seq=seq // tm, nf=nf)
    up_t = lambda j: jnp.minimum(j, nf - 1)
    dn_t = lambda j: jnp.maximum(j - 1, 0)
    y, tails = pl.pallas_call(
        kern,
        grid=(m // tm, nf + 1),
        in_specs=[pl.BlockSpec((tm, d), lambda i, j: (i, 0)),
                  pl.BlockSpec((d, tf), lambda i, j: (0, up_t(j))),
                  pl.BlockSpec((d, tf), lambda i, j: (0, nf + up_t(j))),
                  pl.BlockSpec((nf, CONV_W, tf), lambda i, j: (0, 0, 0)),
                  pl.BlockSpec((nf, 1, tf), lambda i, j: (0, 0, 0)),
                  pl.BlockSpec((tf, d), lambda i, j: (dn_t(j), 0)),
                  pl.BlockSpec((tm, d), lambda i, j: (i, 0))],
        out_specs=[pl.BlockSpec((tm, d), lambda i, j: (i, 0)),
                   pl.BlockSpec((None, nf, 8, tf), lambda i, j: (i, 0, 0, 0))],
        out_shape=[jax.ShapeDtypeStruct((m, d), F32), jax.ShapeDtypeStruct((m // tm, nf, 8, tf), F32)],
        scratch_shapes=[pltpu.VMEM((nf, 8, tf), F32)] + [pltpu.VMEM((tm, tf), F32)] * 4,
        compiler_params=_cparams(("arbitrary", "arbitrary")),
        name="ffn_prompt",
    )(h2, w_in, w_in, conv_w.reshape(CONV_W, nf, tf).swapaxes(0, 1), conv_b.reshape(nf, 1, tf), w_down, x1)
    return y, tails.swapaxes(1, 2).reshape(m // tm, 8, D_FF)


def _ffn_sample(h2, x1, w_in, conv_w, conv_b, w_down, conv_buf, *, tf):
    m, d = x1.shape
    nf = D_FF // tf
    buf2d = conv_buf.reshape(m, (CONV_W - 1) * D_FF)
    return pl.pallas_call(
        _ffn_sample_kernel,
        grid=(1, nf),
        in_specs=_ffn_specs(m, tf, d, nf) + [pl.BlockSpec((m, tf), lambda i, j: (0, j)),
                                             pl.BlockSpec((m, tf), lambda i, j: (0, nf + j))],
        out_specs=[pl.BlockSpec((m, d), lambda i, j: (i, 0)), pl.BlockSpec((m, tf), lambda i, j: (0, j))],
        out_shape=[jax.ShapeDtypeStruct((m, d), F32), jax.ShapeDtypeStruct((m, D_FF), F32)],
        compiler_params=_cparams(("arbitrary", "arbitrary")),
        name="ffn_sample",
    )(h2, w_in, w_in, conv_w, conv_b, w_down, x1, buf2d, buf2d)


IN_PROJ_TN = 512
FFN_TM = 512
FFN_TF = 512


def _qk_gain_row(q_norm, k_norm):
    reps = N_GROUPS * ATT_HEADS
    return jnp.concatenate([jnp.tile(q_norm, reps), jnp.tile(k_norm, reps), jnp.zeros((NA - A_VA,), F32)])[None, :]


ATTN_PROMPT_TILING = ((4, 4, 1), (1, 4, 2), (1, 2, 4))


def _layer(x_prompt, x_sample, caches, norm_mix, w_in, q_norm, k_norm, b_igate, b_fgate, w_proj_att,
           w_proj_mlstm, w_out, norm_ffn, w_ffn_in, conv_w, conv_b, w_ffn_down):
    batch, seq, d = x_prompt.shape
    bd = x_sample.shape[0]
    assert x_sample.shape[1] == 1 and d == D_MODEL
    (ck1, cv1, ck2, cv2, ck3, cv3, st_c, st_n, st_m, st_conv) = caches

    wt_in = jnp.swapaxes(w_in, 0, 1)
    qk_gain = _qk_gain_row(q_norm, k_norm)
    g1 = norm_mix[None, :]
    g2 = norm_ffn[None, :]
    gate_bias = jnp.concatenate([b_igate, b_fgate, jnp.zeros((LANES - 2 * M_HEADS,), F32)])[None, :]
    cb = conv_b[None, :]

    xp = x_prompt.reshape(batch * seq, d)
    xs = x_sample.reshape(bd, d)

    h_all = _rmsnorm(xp, xs, g1, tm=2048, ts=512)
    (za_p, zb_p, zt_p, zif_p), (za_s, zb_s, zt_s, zif_s) = _in_proj(h_all, wt_in, qk_gain, tn=IN_PROJ_TN, ms=bd)
    to_cast = ((), (w_ffn_down,), (w_proj_att, w_proj_mlstm, w_out))
    o_p, lse_p, cast = zip(*[_attn_prompt_group(za_p, batch, seq, gi, dil, to_cast[gi], nsub=nsub, hw=hw, rb=rb)
                             for gi, ((_, dil), (nsub, hw, rb)) in enumerate(zip(DIL_PATTERNS, ATTN_PROMPT_TILING))])
    _, (w_dn,), (wpa, wpm, wo) = cast
    hm_p, p_c, p_n, p_m = _mlstm_prompt(zif_p, zb_p, gate_bias, batch, seq, chunk=256)
    x1_p, h2_p, (w_ff,) = _merge(o_p, lse_p, hm_p, zt_p, xp, wpa, wpm, wo, g2, (w_ffn_in,), tm=256)
    y_p, tails = _ffn_prompt(h2_p, x1_p, w_ff, conv_w, cb, w_dn, seq, tm=FFN_TM, tf=FFN_TF)

    za_p3 = za_p.reshape(batch, seq, NA)
    p_kv = []
    for gi, (win, _) in enumerate(DIL_PATTERNS):
        keep = min(win, seq)
        for col in (A_KA, A_VA):
            lo = col + gi * ATT_GROUP_W
            p_kv.append(za_p3[:, seq - keep:, lo:lo + ATT_GROUP_W].reshape(batch, keep, ATT_HEADS, ATT_HEAD_DIM))
    tiles_per_seq = seq // FFN_TM
    p_conv = tails.reshape(batch, tiles_per_seq, 8, D_FF)[:, -1, 8 - (CONV_W - 1):, :]

    o_s, lse_s = _attn_sample(za_s, [(ck1, cv1), (ck2, cv2), (ck3, cv3)], bb=4)
    hm_s, s_c, s_n, s_m = _mlstm_sample(zif_s, zb_s, gate_bias, st_c, st_n, st_m, bb=4)
    x1_s, h2_s, _ = _merge(o_s, lse_s, hm_s, zt_s, xs, wpa, wpm, wo, g2, tm=bd)
    y_s, g_s = _ffn_sample(h2_s, x1_s, w_ff, conv_w, cb, w_dn, st_conv, tf=FFN_TF)

    s_kv = []
    for gi in range(N_GROUPS):
        for col in (A_KA, A_VA):
            lo = col + gi * ATT_GROUP_W
            s_kv.append(za_s[:, lo:lo + ATT_GROUP_W].reshape(bd, 1, ATT_HEADS, ATT_HEAD_DIM))
    s_conv = jnp.stack([st_conv[:, 1, :], g_s], axis=1)

    p_state = p_kv + [p_c, p_n, p_m, p_conv]
    s_state = s_kv + [s_c, s_n, s_m, s_conv]
    return y_p.reshape(batch, seq, d), y_s.reshape(bd, 1, d), p_state, s_state


def kernel(x_prompt, x_sample, cache_k_w128, cache_v_w128, cache_k_w512, cache_v_w512, cache_k_w2048,
           cache_v_w2048, state_mlstm_C, state_mlstm_n, state_mlstm_m, state_ffn_conv, norm_mix, w_in, q_norm,
           k_norm, b_igate, b_fgate, w_proj_att, w_proj_mlstm, w_out, norm_ffn, w_ffn_in, conv_w, conv_b,
           w_ffn_down):
    assert norm_mix.shape[0] == 1
    caches = [c[0] for c in (cache_k_w128, cache_v_w128, cache_k_w512, cache_v_w512, cache_k_w2048,
                             cache_v_w2048, state_mlstm_C, state_mlstm_n, state_mlstm_m, state_ffn_conv)]
    weights = [w[0] for w in (norm_mix, w_in, q_norm, k_norm, b_igate, b_fgate, w_proj_att, w_proj_mlstm,
                              w_out, norm_ffn, w_ffn_in, conv_w, conv_b, w_ffn_down)]
    y_p, y_s, p_state, s_state = _layer(x_prompt, x_sample, caches, *weights)
    return (y_p, y_s, *[a[None] for a in p_state], *[a[None] for a in s_state])
```

```python
import functools

import jax
import jax.numpy as jnp
from jax import lax
from jax.experimental import pallas as pl
from jax.experimental.pallas import tpu as pltpu

F32 = jnp.float32
BF16 = jnp.bfloat16

RMS_EPS = 1e-6
NEG_INF = -1e30
LANES = 128
VMEM_LIMIT = 56 * 1024 * 1024

D_MODEL = 2048
DIL_PATTERNS = ((128, 1), (512, 4), (2048, 16))
N_GROUPS = 3
ATT_HEADS = 4
ATT_HEAD_DIM = 128
ATT_SPAN = 128
ATT_SCALE = ATT_HEAD_DIM ** -0.5
ATT_GROUP_W = ATT_HEADS * ATT_HEAD_DIM
ATT_WIDTH = N_GROUPS * ATT_GROUP_W
M_HEADS = 4
M_QK_DIM = D_MODEL // (2 * M_HEADS)
M_V_DIM = D_MODEL // M_HEADS
M_QK_WIDTH = M_HEADS * M_QK_DIM
M_V_WIDTH = M_HEADS * M_V_DIM
M_K_SCALE = M_QK_DIM ** -0.5
D_FF = ((8 * D_MODEL // 3 + 255) // 256) * 256
CONV_W = 3
SPLIT_SIZES = (ATT_WIDTH, ATT_WIDTH, ATT_WIDTH, M_QK_WIDTH, M_QK_WIDTH, M_V_WIDTH, M_V_WIDTH,
               M_HEADS, M_HEADS, D_MODEL, D_MODEL)

IN_COLS = sum(SPLIT_SIZES)
A_QA = 0
A_KA = A_QA + ATT_WIDTH
A_VA = A_KA + ATT_WIDTH
NA = A_VA + ATT_WIDTH
B_QM = 0
B_KM = B_QM + M_QK_WIDTH
B_VM = B_KM + M_QK_WIDTH
B_OM = B_VM + M_V_WIDTH
NB = B_OM + M_V_WIDTH
N_IF = 2 * M_HEADS
T_GA = 0
T_GB = T_GA + D_MODEL
NT = T_GB + D_MODEL
assert NA + NB + N_IF + NT == IN_COLS


def _cparams(sem):
    return pltpu.CompilerParams(dimension_semantics=sem, vmem_limit_bytes=VMEM_LIMIT)


def _log_sigmoid(x):
    return jnp.minimum(x, 0.0) - jnp.log(1.0 + jnp.exp(-jnp.abs(x)))


def _rmsnorm_kernel(x_ref, xs_ref, g_ref, h_ref, *, nsub):
    j = pl.program_id(1)
    ts = x_ref.shape[0]

    def norm(x):
        ms = jnp.mean(x * x, axis=-1, keepdims=True)
        return (x * lax.rsqrt(ms + RMS_EPS) * g_ref[...]).astype(h_ref.dtype)

    @pl.when(j < nsub)
    def _():
        h_ref[pl.ds(pl.multiple_of(j * ts, ts), ts), :] = norm(x_ref[...])

    @pl.when(j == nsub)
    def _():
        h_ref[nsub * ts:, :] = norm(xs_ref[...])


def _rmsnorm(x, xs, gain, *, tm, ts):
    m, d = x.shape
    ms = xs.shape[0]
    assert m % tm == 0 and tm % ts == 0
    nsub = tm // ts
    last = m // ts - 1
    return pl.pallas_call(
        functools.partial(_rmsnorm_kernel, nsub=nsub),
        grid=(m // tm, nsub + 1),
        in_specs=[pl.BlockSpec((ts, d), lambda i, j: (jnp.minimum(i * nsub + j, last), 0)),
                  pl.BlockSpec((ms, d), lambda i, j: (0, 0)),
                  pl.BlockSpec((1, d), lambda i, j: (0, 0))],
        out_specs=pl.BlockSpec((None, tm + ms, d), lambda i, j: (i, 0, 0)),
        out_shape=jax.ShapeDtypeStruct((m // tm, tm + ms, d), BF16),
        compiler_params=_cparams(("parallel", "arbitrary")),
        name="rmsnorm",
    )(x, xs, gain)


def _in_proj_kernel(h_ref, w_ref, wn_ref, wif_ref, qkg_ref, za_ref, zb_ref, zt_ref, zif_ref,
                    sa_ref, sb_ref, st_ref, sif_ref, *, n_norm, na, nb, nt):
    j = pl.program_id(1)
    tp = za_ref.shape[0]

    def z_of(w):
        return lax.dot_general(h_ref[...], w.astype(BF16), (((1,), (1,)), ((), ())), preferred_element_type=F32)

    @pl.when(j < n_norm)
    def _():
        z = z_of(w_ref[...])
        for c in range(za_ref.shape[1] // LANES):
            sl = slice(c * LANES, (c + 1) * LANES)
            zc = z[:, sl]
            ms = jnp.mean(zc * zc, axis=-1, keepdims=True)
            zn = zc * lax.rsqrt(ms + RMS_EPS) * qkg_ref[:, sl]
            za_ref[:, sl] = zn[:tp]
            sa_ref[j, :, sl] = zn[tp:]

    @pl.when((j >= n_norm) & (j < na))
    def _():
        z = z_of(w_ref[...])
        za_ref[...] = z[:tp]
        sa_ref[j] = z[tp:]

    @pl.when((j >= na) & (j < na + nb))
    def _():
        z = z_of(w_ref[...])
        zb_ref[...] = z[:tp].astype(zb_ref.dtype)
        sb_ref[j - na] = z[tp:]

    @pl.when((j >= na + nb) & (j < na + nb + nt))
    def _():
        z = z_of(jnp.concatenate([w_ref[N_IF:, :], wn_ref[...]], axis=0))
        zt_ref[...] = z[:tp].astype(zt_ref.dtype)
        st_ref[j - na - nb] = z[tp:]

    @pl.when(j == na + nb + nt)
    def _():
        z = z_of(jnp.concatenate([wif_ref[...], jnp.zeros((LANES - N_IF, wif_ref.shape[1]), F32)], axis=0))
        zif_ref[...] = z[:tp]
        sif_ref[...] = z[tp:]


def _in_proj(h, wt, qk_gain, *, tn, ms):
    ni, th, d = h.shape
    tm = th - ms
    m = ni * tm
    assert wt.shape == (IN_COLS, d) and N_IF == 8
    assert NA % tn == 0 and NB % tn == 0 and NT % tn == 0 and A_VA % tn == 0 and tn % N_IF == 0
    na, nb, nt = NA // tn, NB // tn, NT // tn
    nw = na + nb + nt
    g_if = (NA + NB) // N_IF
    g_tn = tn // N_IF
    kern = functools.partial(_in_proj_kernel, n_norm=A_VA // tn, na=na, nb=nb, nt=nt)
    col_a = lambda j: jnp.minimum(j, na - 1)
    col_b = lambda j: jnp.clip(j - na, 0, nb - 1)
    col_t = lambda j: jnp.clip(j - na - nb, 0, nt - 1)
    out_specs = [pl.BlockSpec((None, tm, tn), lambda i, j: (i, 0, col_a(j))),
                 pl.BlockSpec((None, tm, tn), lambda i, j: (i, 0, col_b(j))),
                 pl.BlockSpec((None, tm, tn), lambda i, j: (i, 0, col_t(j))),
                 pl.BlockSpec((None, tm, LANES), lambda i, j: (i, 0, 0))]
    out_shape = [jax.ShapeDtypeStruct((ni, tm, w), dt)
                 for w, dt in zip((NA, NB, NT, LANES), (F32, BF16, BF16, F32))]
    for n_tiles in (na, nb, nt):
        out_specs.append(pl.BlockSpec((None, n_tiles, ms, tn), lambda i, j: (i, 0, 0, 0)))
        out_shape.append(jax.ShapeDtypeStruct((ni, n_tiles, ms, tn), F32))
    out_specs.append(pl.BlockSpec((None, ms, LANES), lambda i, j: (i, 0, 0)))
    out_shape.append(jax.ShapeDtypeStruct((ni, ms, LANES), F32))
    outs = pl.pallas_call(
        kern,
        grid=(ni, nw + 1),
        in_specs=[
            pl.BlockSpec((None, th, d), lambda i, j: (i, 0, 0)),
            pl.BlockSpec((tn, d), lambda i, j: (jnp.minimum(j, nw - 1), 0)),
            pl.BlockSpec((N_IF, d), lambda i, j: (g_if + g_tn * (jnp.clip(j, na + nb, nw - 1) - (na + nb) + 1), 0)),
            pl.BlockSpec((N_IF, d), lambda i, j: (g_if, 0)),
            pl.BlockSpec((1, tn), lambda i, j: (0, col_a(j))),
        ],
        out_specs=out_specs,
        out_shape=out_shape,
        compiler_params=_cparams(("parallel", "arbitrary")),
        name="in_proj",
    )(h, wt, wt, wt, qk_gain)
    sample = [o[0].swapaxes(0, 1).reshape(ms, -1) for o in outs[4:7]] + [outs[7][0]]
    return [o.reshape(m, o.shape[2]) for o in outs[:4]], sample


def _attn_prompt_kernel(*refs, dil, nsub, hw, rb, n_cast):
    q_refs, kp_refs, kc_refs, vp_refs, vc_refs = [refs[t * hw:(t + 1) * hw] for t in range(5)]
    cast_in = refs[5 * hw:5 * hw + n_cast]
    o_ref, lse_ref = refs[5 * hw + n_cast:5 * hw + n_cast + 2]
    cast_out = refs[5 * hw + n_cast + 2:5 * hw + 2 * n_cast + 2]
    o_refs = refs[5 * hw + 2 * n_cast + 2:]
    for w_ref, wb_ref in zip(cast_in, cast_out):
        wb_ref[...] = w_ref[...].astype(wb_ref.dtype)
    c = pl.program_id(1)
    blk = ATT_SPAN
    step = blk * dil
    qi = lax.broadcasted_iota(jnp.int32, (blk, 2 * blk), 0)
    kj = lax.broadcasted_iota(jnp.int32, (blk, 2 * blk), 1)
    band = (kj >= qi) & (kj <= qi + blk)
    first_bias = jnp.where((kj < blk) & (c == 0), NEG_INF, 0.0)
    lane = lax.broadcasted_iota(jnp.int32, (blk, LANES), 1)

    def rows(base, r):
        return pl.ds(base + r, blk) if dil == 1 else pl.ds(base + r, blk, stride=dil)

    def group(r0, s):
        base = s * step
        ids = [(r0 + r, n) for r in range(rb) for n in range(hw)]

        def stacked(cur_refs, prev_refs):
            parts = []
            for r, n in ids:
                cur = cur_refs[n][rows(base, r), :]
                if prev_refs is None:
                    parts.append(cur)
                else:
                    prv = prev_refs[n][rows(0, r), :] if s == 0 else cur_refs[n][rows(base - step, r), :]
                    parts.append(jnp.concatenate([prv, cur], axis=0))
            return jnp.stack(parts).astype(BF16)

        q = stacked(q_refs, None)
        k = stacked(kc_refs, kp_refs)
        v = stacked(vc_refs, vp_refs)
        sc = jnp.einsum('bqe,bke->bqk', q, k, preferred_element_type=F32) * ATT_SCALE
        if s == 0:
            sc = sc + first_bias
        sc = jnp.where(band, sc, NEG_INF)
        m = jnp.max(sc, axis=-1, keepdims=True)
        p = jnp.exp(sc - m)
        l = jnp.sum(p, axis=-1, keepdims=True)
        o = jnp.einsum('bqk,bke->bqe', p.astype(BF16), v, preferred_element_type=F32) / l
        lse = m + jnp.log(l)
        for r in range(rb):
            lse_tile = jnp.zeros((blk, LANES), F32)
            for n in range(hw):
                b = r * hw + n
                o_refs[n][rows(base, r0 + r), :] = o[b]
                lse_tile = jnp.where(lane == n, lse[b], lse_tile)
            lse_ref[rows(base, r0 + r), :] = lse_tile

    for s in range(nsub):
        if dil == rb:
            group(0, s)
        else:
            def body(it, carry, s=s):
                group(it * rb, s)
                return carry
            lax.fori_loop(0, dil // rb, body, 0)

    for n in range(hw):
        o_ref[:, n * ATT_HEAD_DIM:(n + 1) * ATT_HEAD_DIM] = o_refs[n][...]


def _attn_prompt_group(za, batch, seq, gi, dil, to_cast, *, nsub, hw, rb):
    step = ATT_SPAN * dil
    tc = nsub * step
    assert seq % tc == 0 and ATT_HEADS % hw == 0 and dil % rb == 0
    nh = ATT_HEADS // hw
    e = ATT_HEAD_DIM
    nchunk = seq // tc
    n_steps = batch * nchunk * nh
    bf16_sublanes = 16
    cast_in_specs, cast_out_specs, cast_shapes = [], [], []
    for w in to_cast:
        rows, cols = w.shape
        assert rows % (n_steps * bf16_sublanes) == 0
        spec = pl.BlockSpec((rows // n_steps, cols), lambda b, c, h: ((b * nchunk + c) * nh + h, 0))
        cast_in_specs.append(spec)
        cast_out_specs.append(spec)
        cast_shapes.append(jax.ShapeDtypeStruct((rows, cols), BF16))
    z3 = za.reshape(batch, seq, NA)
    qc, kc, vc = [(col + gi * ATT_GROUP_W) // e for col in (A_QA, A_KA, A_VA)]

    def cur(col):
        return [pl.BlockSpec((None, tc, e), lambda b, c, h, n=n: (b, c, col + h * hw + n)) for n in range(hw)]

    def prev(col):
        return [pl.BlockSpec((None, step, e), lambda b, c, h, n=n: (b, jnp.maximum(c * nsub - 1, 0), col + h * hw + n))
                for n in range(hw)]

    kern = functools.partial(_attn_prompt_kernel, dil=dil, nsub=nsub, hw=hw, rb=rb, n_cast=len(to_cast))
    o, lse, *cast = pl.pallas_call(
        kern,
        grid=(batch, nchunk, nh),
        in_specs=cur(qc) + prev(kc) + cur(kc) + prev(vc) + cur(vc) + cast_in_specs,
        out_specs=[pl.BlockSpec((None, tc, hw * e), lambda b, c, h: (b, c, h)),
                   pl.BlockSpec((None, tc, LANES), lambda b, c, h: (b, c, h))] + cast_out_specs,
        out_shape=[jax.ShapeDtypeStruct((batch, seq, ATT_GROUP_W), F32),
                   jax.ShapeDtypeStruct((batch, seq, nh * LANES), F32)] + cast_shapes,
        scratch_shapes=[pltpu.VMEM((tc, e), F32)] * hw,
        compiler_params=_cparams(("parallel", "parallel", "parallel")),
        name=f"attn_prompt_g{gi}",
    )(*([z3] * (5 * hw)), *to_cast)
    return o.reshape(batch * seq, ATT_GROUP_W), lse.reshape(batch * seq, nh * LANES), cast


def _attn_sample_kernel(z_ref, k1, v1, k2, v2, k3, v3, o1, o2, o3, l1, l2, l3):
    bb = z_ref.shape[0]
    bufs = ((k1, v1, o1, l1), (k2, v2, o2, l2), (k3, v3, o3, l3))
    lane = lax.broadcasted_iota(jnp.int32, (1, LANES), 1)

    def heads(b, col):
        return jnp.concatenate([z_ref[b, :, col + n * ATT_HEAD_DIM:col + (n + 1) * ATT_HEAD_DIM]
                                for n in range(ATT_HEADS)], axis=0)

    def body(b, carry):
        for gi, (k_ref, v_ref, o_ref, l_ref) in enumerate(bufs):
            q = heads(b, A_QA + gi * ATT_GROUP_W)
            k_new = heads(b, A_KA + gi * ATT_GROUP_W)
            v_new = heads(b, A_VA + gi * ATT_GROUP_W)
            kb = k_ref[b].reshape(ATT_SPAN // 2, 2 * ATT_HEADS, ATT_HEAD_DIM)
            vb = v_ref[b].reshape(ATT_SPAN // 2, 2 * ATT_HEADS, ATT_HEAD_DIM)
            q2 = jnp.concatenate([q, q], axis=0)
            s = jnp.sum(kb * q2[None], axis=-1, keepdims=True) * ATT_SCALE
            s_new = jnp.sum(k_new * q, axis=-1, keepdims=True) * ATT_SCALE
            m2 = jnp.max(s, axis=0)
            m = jnp.maximum(jnp.maximum(m2[:ATT_HEADS], m2[ATT_HEADS:]), s_new)
            p = jnp.exp(s - jnp.concatenate([m, m], axis=0)[None])
            p_new = jnp.exp(s_new - m)
            l2 = jnp.sum(p, axis=0)
            l = l2[:ATT_HEADS] + l2[ATT_HEADS:] + p_new
            o2 = jnp.sum(p * vb, axis=0)
            o = (o2[:ATT_HEADS] + o2[ATT_HEADS:] + p_new * v_new) / l
            lse = m + jnp.log(l)
            lse_row = jnp.zeros((1, LANES), F32)
            for n in range(ATT_HEADS):
                o_ref[b, :, n * ATT_HEAD_DIM:(n + 1) * ATT_HEAD_DIM] = o[n:n + 1, :]
                lse_row = jnp.where(lane == n, lse[n:n + 1, :], lse_row)
            l_ref[b] = lse_row
        return carry

    lax.fori_loop(0, bb, body, 0)


def _attn_sample(za_s, caches, *, bb):
    bd = za_s.shape[0]
    assert bd % bb == 0
    ins, in_specs = [za_s.reshape(bd, 1, NA)], [pl.BlockSpec((bb, 1, NA), lambda i: (i, 0, 0))]
    for (k_buf, v_buf), (win, dil) in zip(caches, DIL_PATTERNS):
        assert k_buf.shape[1:] == (ATT_SPAN * dil, ATT_HEADS, ATT_HEAD_DIM)
        for buf in (k_buf, v_buf):
            ins.append(buf.reshape(bd, ATT_SPAN, dil, ATT_HEADS, ATT_HEAD_DIM))
            in_specs.append(pl.BlockSpec((bb, ATT_SPAN, None, ATT_HEADS, ATT_HEAD_DIM),
                                         lambda i: (i, 0, 0, 0, 0)))
    outs = pl.pallas_call(
        _attn_sample_kernel,
        grid=(bd // bb,),
        in_specs=in_specs,
        out_specs=[pl.BlockSpec((bb, 1, ATT_GROUP_W), lambda i: (i, 0, 0))] * 3
                  + [pl.BlockSpec((bb, 1, LANES), lambda i: (i, 0, 0))] * 3,
        out_shape=[jax.ShapeDtypeStruct((bd, 1, ATT_GROUP_W), F32)] * 3
                  + [jax.ShapeDtypeStruct((bd, 1, LANES), F32)] * 3,
        compiler_params=_cparams(("parallel",)),
        name="attn_sample",
    )(*ins)
    return [o[:, 0, :] for o in outs[:3]], [l[:, 0, :] for l in outs[3:]]


def _mlstm_prompt_kernel(q_ref, k_ref, v_ref, om_ref, g_ref, bias_ref, h_ref, c_out, n_out, m_out,
                         c_s, n_s, m_s):
    hh = c_s.shape[0]
    ci = pl.program_id(2)
    L = q_ref.shape[0]

    @pl.when(ci == 0)
    def _():
        c_s[...] = jnp.zeros_like(c_s)
        n_s[...] = jnp.zeros_like(n_s)
        m_s[...] = jnp.zeros_like(m_s)

    gates = g_ref[...] + bias_ref[...]
    gates_t = gates.T
    lane = lax.broadcasted_iota(jnp.int32, (L, LANES), 1)
    sub = lax.broadcasted_iota(jnp.int32, (LANES, L), 0)
    ti = lax.broadcasted_iota(jnp.int32, (L, L), 0)
    si = lax.broadcasted_iota(jnp.int32, (L, L), 1)
    causal = si <= ti

    for j in range(hh):
        hd = pl.program_id(1) * hh + j
        qs = slice(j * M_QK_DIM, (j + 1) * M_QK_DIM)
        vs = slice(j * M_V_DIM, (j + 1) * M_V_DIM)
        li_col = jnp.sum(jnp.where(lane == hd, gates, 0.0), axis=1, keepdims=True)
        lf_col = _log_sigmoid(jnp.sum(jnp.where(lane == hd + M_HEADS, gates, 0.0), axis=1, keepdims=True))
        li_row = jnp.sum(jnp.where(sub == hd, gates_t, 0.0), axis=0, keepdims=True)
        lf_row = _log_sigmoid(jnp.sum(jnp.where(sub == hd + M_HEADS, gates_t, 0.0), axis=0, keepdims=True))

        b_col = jnp.sum(jnp.where(causal, lf_row, 0.0), axis=1, keepdims=True)
        b_row = jnp.sum(jnp.where(ti <= si, lf_col, 0.0), axis=0, keepdims=True)
        b_end = jnp.sum(lf_row, axis=1, keepdims=True)

        m_prev = m_s[j]
        dmat = jnp.where(causal, b_col - b_row + li_row, NEG_INF)
        inter = b_col + m_prev
        mt = jnp.maximum(inter, jnp.max(dmat, axis=1, keepdims=True))

        qb = q_ref[:, qs]
        q = qb.astype(F32)
        k = k_ref[:, qs].astype(F32) * M_K_SCALE
        vb = v_ref[:, vs]
        qk = lax.dot_general(qb, k.astype(BF16), (((1,), (1,)), ((), ())), preferred_element_type=F32)
        a = jnp.exp(dmat - mt) * qk
        w_inter = jnp.exp(inter - mt)
        num = (jnp.dot(a.astype(BF16), vb, preferred_element_type=F32)
               + w_inter * jnp.dot(qb, c_s[j].astype(BF16), preferred_element_type=F32))
        den = jnp.sum(a, axis=1, keepdims=True) + w_inter * jnp.sum(q * n_s[j], axis=1, keepdims=True)
        h = num / jnp.maximum(jnp.abs(den), jnp.exp(-mt))
        h_ref[:, vs] = (jax.nn.sigmoid(om_ref[:, vs].astype(F32)) * h).astype(h_ref.dtype)

        g_col = b_end - b_col + li_col
        g_row = b_end - b_row + li_row
        m_new = jnp.maximum(b_end + m_prev, jnp.max(g_row, axis=1, keepdims=True))
        decay = jnp.exp(b_end + m_prev - m_new)
        kw = jnp.exp(g_col - m_new) * k
        c_s[j] = decay * c_s[j] + jnp.dot(kw.T.astype(BF16), vb, preferred_element_type=F32)
        n_s[j] = decay * n_s[j] + jnp.sum(kw, axis=0, keepdims=True)
        m_s[j] = m_new

    @pl.when(ci == pl.num_programs(2) - 1)
    def _():
        c_out[...] = c_s[...]
        n_out[...] = n_s[...]
        m_out[...] = jnp.broadcast_to(m_s[...], m_out.shape)


def _mlstm_prompt(zif, zb, gate_bias, batch, seq, *, chunk, hh):
    assert seq % chunk == 0 and M_HEADS % hh == 0
    nc = seq // chunk
    zif3 = zif.reshape(batch, seq, LANES)
    zb3 = zb.reshape(batch, seq, NB)
    wq, wv = hh * M_QK_DIM, hh * M_V_DIM
    qk_blk = lambda col: pl.BlockSpec((None, chunk, wq), lambda b, h, c: (b, c, col // wq + h))
    v_blk = lambda col: pl.BlockSpec((None, chunk, wv), lambda b, h, c: (b, c, col // wv + h))
    hm, c1, n1, m1 = pl.pallas_call(
        _mlstm_prompt_kernel,
        grid=(batch, M_HEADS // hh, nc),
        in_specs=[qk_blk(B_QM), qk_blk(B_KM), v_blk(B_VM), v_blk(B_OM),
                  pl.BlockSpec((None, chunk, LANES), lambda b, h, c: (b, c, 0)),
                  pl.BlockSpec((1, LANES), lambda b, h, c: (0, 0))],
        out_specs=[pl.BlockSpec((None, chunk, wv), lambda b, h, c: (b, c, h)),
                   pl.BlockSpec((None, hh, M_QK_DIM, M_V_DIM), lambda b, h, c: (b, h, 0, 0)),
                   pl.BlockSpec((None, hh, 1, M_QK_DIM), lambda b, h, c: (b, h, 0, 0)),
                   pl.BlockSpec((None, hh, 1, LANES), lambda b, h, c: (b, h, 0, 0))],
        out_shape=[jax.ShapeDtypeStruct((batch, seq, M_V_WIDTH), BF16),
                   jax.ShapeDtypeStruct((batch, M_HEADS, M_QK_DIM, M_V_DIM), F32),
                   jax.ShapeDtypeStruct((batch, M_HEADS, 1, M_QK_DIM), F32),
                   jax.ShapeDtypeStruct((batch, M_HEADS, 1, LANES), F32)],
        scratch_shapes=[pltpu.VMEM((hh, M_QK_DIM, M_V_DIM), F32), pltpu.VMEM((hh, 1, M_QK_DIM), F32),
                        pltpu.VMEM((hh, 1, 1), F32)],
        compiler_params=_cparams(("parallel", "parallel", "arbitrary")),
        name="mlstm_prompt",
    )(zb3, zb3, zb3, zb3, zif3, gate_bias)
    return hm.reshape(batch * seq, M_V_WIDTH), c1, n1[:, :, 0, :], m1[:, :, 0, 0]


def _rows_to_cols(rows):
    rep = LANES // len(rows)
    n = rows[0].shape[1]
    t = jnp.concatenate([jnp.broadcast_to(r, (rep, n)) for r in rows], axis=0).T
    return [t[:, i * rep:i * rep + 1] for i in range(len(rows))]


def _mlstm_sample_kernel(zif_ref, zb_ref, bias_ref, c_ref, n_ref, m_ref, h_ref, c_out, n_out, m_out):
    for b in range(zif_ref.shape[0]):
        _mlstm_sample_one(zif_ref.at[b], zb_ref.at[b], bias_ref, c_ref.at[b], n_ref.at[b], m_ref.at[b],
                          h_ref.at[b], c_out.at[b], n_out.at[b], m_out.at[b])


def _mlstm_sample_one(zif_ref, zb_ref, bias_ref, c_ref, n_ref, m_ref, h_ref, c_out, n_out, m_out):
    gates = zif_ref[...] + bias_ref[...]
    lane = lax.broadcasted_iota(jnp.int32, (1, LANES), 1)
    m_row = jnp.zeros((1, LANES), F32)
    qs, wks, inters, m_news = [], [], [], []
    for h in range(M_HEADS):
        k = zb_ref[:, B_KM + h * M_QK_DIM:B_KM + (h + 1) * M_QK_DIM].astype(F32) * M_K_SCALE
        li = gates[:, h:h + 1]
        lf = _log_sigmoid(gates[:, M_HEADS + h:M_HEADS + h + 1])
        inter = lf + m_ref[:, h:h + 1]
        m_new = jnp.maximum(inter, li)
        qs.append(zb_ref[:, B_QM + h * M_QK_DIM:B_QM + (h + 1) * M_QK_DIM].astype(F32))
        wks.append(jnp.exp(li - m_new) * k)
        inters.append(inter)
        m_news.append(m_new)
    cols = _rows_to_cols(qs + wks)
    for h in range(M_HEADS):
        q, wk, inter, m_new = qs[h], wks[h], inters[h], m_news[h]
        v = zb_ref[:, B_VM + h * M_V_DIM:B_VM + (h + 1) * M_V_DIM].astype(F32)
        om = zb_ref[:, B_OM + h * M_V_DIM:B_OM + (h + 1) * M_V_DIM].astype(F32)
        c0 = c_ref[h]
        n0 = n_ref[h:h + 1, :]
        a = jnp.sum(q * wk, axis=1, keepdims=True)
        w_inter = jnp.exp(inter - m_new)
        q_c = jnp.sum(cols[h] * c0, axis=0, keepdims=True)
        num = a * v + w_inter * q_c
        den = a + w_inter * jnp.sum(q * n0, axis=1, keepdims=True)
        hv = num / jnp.maximum(jnp.abs(den), jnp.exp(-m_new))
        h_ref[:, h * M_V_DIM:(h + 1) * M_V_DIM] = (jax.nn.sigmoid(om) * hv).astype(h_ref.dtype)
        c_out[h] = w_inter * c0 + cols[M_HEADS + h] * v
        n_out[h:h + 1, :] = w_inter * n0 + wk
        m_row = jnp.where(lane == h, m_new, m_row)
    m_out[...] = m_row


def _mlstm_sample(zif_s, zb_s, gate_bias, c0, n0, m0, *, bb):
    bd = zif_s.shape[0]
    assert bd % bb == 0
    hm, c1, n1, m1 = pl.pallas_call(
        _mlstm_sample_kernel,
        grid=(bd // bb,),
        in_specs=[pl.BlockSpec((bb, 1, LANES), lambda b: (b, 0, 0)),
                  pl.BlockSpec((bb, 1, NB), lambda b: (b, 0, 0)),
                  pl.BlockSpec((1, LANES), lambda b: (0, 0)),
                  pl.BlockSpec((bb, M_HEADS, M_QK_DIM, M_V_DIM), lambda b: (b, 0, 0, 0)),
                  pl.BlockSpec((bb, M_HEADS, M_QK_DIM), lambda b: (b, 0, 0)),
                  pl.BlockSpec((bb, 1, M_HEADS), lambda b: (b, 0, 0))],
        out_specs=[pl.BlockSpec((bb, 1, M_V_WIDTH), lambda b: (b, 0, 0)),
                   pl.BlockSpec((bb, M_HEADS, M_QK_DIM, M_V_DIM), lambda b: (b, 0, 0, 0)),
                   pl.BlockSpec((bb, M_HEADS, M_QK_DIM), lambda b: (b, 0, 0)),
                   pl.BlockSpec((bb, 1, LANES), lambda b: (b, 0, 0))],
        out_shape=[jax.ShapeDtypeStruct((bd, 1, M_V_WIDTH), BF16),
                   jax.ShapeDtypeStruct((bd, M_HEADS, M_QK_DIM, M_V_DIM), F32),
                   jax.ShapeDtypeStruct((bd, M_HEADS, M_QK_DIM), F32),
                   jax.ShapeDtypeStruct((bd, 1, LANES), F32)],
        compiler_params=_cparams(("parallel",)),
        name="mlstm_sample",
    )(zif_s.reshape(bd, 1, LANES), zb_s.reshape(bd, 1, NB), gate_bias, c0, n0, m0.reshape(bd, 1, M_HEADS))
    return hm.reshape(bd, M_V_WIDTH), c1, n1, m1[:, 0, :M_HEADS]


def _merge_kernel(o1, o2, o3, l1, l2, l3, hm_ref, ga_ref, gb_ref, x_ref, wpa_ref, wpm_ref, wo_ref, g2_ref,
                  *rest, hws):
    n_cast = (len(rest) - 2) // 2
    x1_ref, h2_ref = rest[n_cast:n_cast + 2]
    for w_ref, wb_ref in zip(rest[:n_cast], rest[n_cast + 2:]):
        wb_ref[...] = w_ref[...].astype(wb_ref.dtype)
    parts = []
    for n in range(ATT_HEADS):
        sl = slice(n * ATT_HEAD_DIM, (n + 1) * ATT_HEAD_DIM)
        ls = []
        for l_ref, hw in zip((l1, l2, l3), hws):
            ln = (n // hw) * LANES + n % hw
            ls.append(l_ref[:, ln:ln + 1])
        mx = jnp.maximum(jnp.maximum(ls[0], ls[1]), ls[2])
        es = [jnp.exp(l - mx) for l in ls]
        tot = es[0] + es[1] + es[2]
        parts.append((es[0] / tot) * o1[:, sl] + (es[1] / tot) * o2[:, sl] + (es[2] / tot) * o3[:, sl])
    att = jnp.concatenate(parts, axis=1).astype(BF16)
    pa = jnp.dot(att, wpa_ref[...], preferred_element_type=F32)
    pm = jnp.dot(hm_ref[...], wpm_ref[...], preferred_element_type=F32)
    merged = (jax.nn.sigmoid(ga_ref[...].astype(F32)) * pa + jax.nn.sigmoid(gb_ref[...].astype(F32)) * pm)
    x1 = x_ref[...] + jnp.dot(merged.astype(BF16), wo_ref[...], preferred_element_type=F32)
    x1_ref[...] = x1
    ms = jnp.mean(x1 * x1, axis=-1, keepdims=True)
    h2_ref[...] = (x1 * lax.rsqrt(ms + RMS_EPS) * g2_ref[...]).astype(BF16)


def _merge(o_list, lse_list, hm, zt, x, wpa, wpm, wo, g2, to_cast=(), *, tm):
    m, d = x.shape
    assert m % tm == 0 and zt.shape[1] == NT
    n_steps = m // tm
    row_blk = lambda w: pl.BlockSpec((tm, w), lambda i: (i, 0))
    const = lambda shape: pl.BlockSpec(shape, lambda i: (0, 0), pipeline_mode=pl.Buffered(1))
    hws = tuple(ATT_HEADS * LANES // l.shape[1] for l in lse_list)
    cast_specs, cast_shapes = [], []
    for w in to_cast:
        rows, cols = w.shape
        assert rows % (n_steps * 16) == 0
        cast_specs.append(pl.BlockSpec((rows // n_steps, cols), lambda i: (i, 0)))
        cast_shapes.append(jax.ShapeDtypeStruct((rows, cols), BF16))
    x1, h2, *cast = pl.pallas_call(
        functools.partial(_merge_kernel, hws=hws),
        grid=(n_steps,),
        in_specs=[row_blk(ATT_GROUP_W)] * 3 + [row_blk(l.shape[1]) for l in lse_list]
                 + [row_blk(M_V_WIDTH),
                    pl.BlockSpec((tm, d), lambda i: (i, T_GA // d)),
                    pl.BlockSpec((tm, d), lambda i: (i, T_GB // d)),
                    row_blk(d),
                    const(wpa.shape), const(wpm.shape), const(wo.shape), const((1, d))] + cast_specs,
        out_specs=[row_blk(d), row_blk(d)] + cast_specs,
        out_shape=[jax.ShapeDtypeStruct((m, d), F32), jax.ShapeDtypeStruct((m, d), BF16)] + cast_shapes,
        compiler_params=_cparams(("parallel",)),
        name="merge",
    )(*o_list, *lse_list, hm, zt, zt, x, wpa, wpm, wo, g2, *to_cast)
    return x1, h2, cast


def _ffn_body(h2_ref, wu_ref, wg_ref, cw_ref, cb_ref, wd_ref, x1_ref, y_ref, g_prev2, g_prev1):
    j = pl.program_id(1)
    h2 = h2_ref[...]
    u = jnp.dot(h2, wu_ref[...], preferred_element_type=F32)
    g = jnp.dot(h2, wg_ref[...], preferred_element_type=F32)
    gconv = cb_ref[...] + ((g_prev2(g) * cw_ref[0:1, :] + g_prev1(g) * cw_ref[1:2, :]) + g * cw_ref[2:3, :])
    act = (jax.nn.gelu(gconv) * u).astype(BF16)
    down = jnp.dot(act, wd_ref[...], preferred_element_type=F32)

    @pl.when(j == 0)
    def _():
        y_ref[...] = x1_ref[...] + down

    @pl.when(j > 0)
    def _():
        y_ref[...] += down

    return g


def _ffn_prompt_kernel(h2_ref, wu_ref, wg_ref, cw_ref, cb_ref, wd_ref, x1_ref, y_ref, tail_ref, prev_s,
                       u0_s, g0_s, u1_s, g1_s, *, tiles_per_seq, nf):
    i = pl.program_id(0)
    j = pl.program_id(1)
    tm = h2_ref.shape[0]
    slots = ((u0_s, g0_s), (u1_s, g1_s))

    def up(u_s, g_s):
        h2 = h2_ref[...]
        u_s[...] = jnp.dot(h2, wu_ref[...], preferred_element_type=F32)
        g_s[...] = jnp.dot(h2, wg_ref[...], preferred_element_type=F32)

    def down(u_s, g_s):
        jt = j - 1
        prev = jnp.where(i % tiles_per_seq == 0, 0.0, prev_s[jt])
        p2, p1 = prev[6:7, :], prev[7:8, :]
        g = g_s[...]
        r = lax.broadcasted_iota(jnp.int32, g.shape, 0)
        g_m1 = jnp.where(r == 0, p1, pltpu.roll(g, 1, axis=0))
        g_m2 = jnp.where(r == 0, p2, jnp.where(r == 1, p1, pltpu.roll(g, 2, axis=0)))
        cw = cw_ref[jt]
        gconv = cb_ref[jt] + ((g_m2 * cw[0:1, :] + g_m1 * cw[1:2, :]) + g * cw[2:3, :])
        act = (jax.nn.gelu(gconv) * u_s[...]).astype(BF16)
        y_ref[...] += jnp.dot(act, wd_ref[...], preferred_element_type=F32)
        prev_s[jt] = g[tm - 8:tm, :]

    @pl.when(j == 0)
    def _():
        y_ref[...] = x1_ref[...]
        up(*slots[0])

    for par in range(2):
        @pl.when((j > 0) & (j < nf) & (j % 2 == par))
        def _(par=par):
            up(*slots[par])
            down(*slots[1 - par])

    @pl.when(j == nf)
    def _():
        down(*slots[(nf - 1) % 2])
        tail_ref[...] = prev_s[...]


def _ffn_sample_kernel(h2_ref, wu_ref, wg_ref, cw_ref, cb_ref, wd_ref, x1_ref, b2_ref, b1_ref, y_ref, g_ref):
    g = _ffn_body(h2_ref, wu_ref, wg_ref, cw_ref, cb_ref, wd_ref, x1_ref, y_ref,
                  lambda g: b2_ref[...], lambda g: b1_ref[...])
    g_ref[...] = g


def _ffn_specs(tm, tf, d, nf):
    return [pl.BlockSpec((tm, d), lambda i, j: (i, 0)),
            pl.BlockSpec((d, tf), lambda i, j: (0, j)),
            pl.BlockSpec((d, tf), lambda i, j: (0, nf + j)),
            pl.BlockSpec((CONV_W, tf), lambda i, j: (0, j)),
            pl.BlockSpec((1, tf), lambda i, j: (0, j)),
            pl.BlockSpec((tf, d), lambda i, j: (j, 0)),
            pl.BlockSpec((tm, d), lambda i, j: (i, 0))]


def _ffn_prompt(h2, x1, w_in, conv_w, conv_b, w_down, seq, *, tm, tf):
    m, d = x1.shape
    assert m % tm == 0 and seq % tm == 0 and D_FF % tf == 0 and tm % 8 == 0
    nf = D_FF // tf
    kern = functools.partial(_ffn_prompt_kernel, tiles_per_seq=seq // tm, nf=nf)
    up_t = lambda j: jnp.minimum(j, nf - 1)
    dn_t = lambda j: jnp.maximum(j - 1, 0)
    y, tails = pl.pallas_call(
        kern,
        grid=(m // tm, nf + 1),
        in_specs=[pl.BlockSpec((tm, d), lambda i, j: (i, 0)),
                  pl.BlockSpec((d, tf), lambda i, j: (0, up_t(j))),
                  pl.BlockSpec((d, tf), lambda i, j: (0, nf + up_t(j))),
                  pl.BlockSpec((nf, CONV_W, tf), lambda i, j: (0, 0, 0)),
                  pl.BlockSpec((nf, 1, tf), lambda i, j: (0, 0, 0)),
                  pl.BlockSpec((tf, d), lambda i, j: (dn_t(j), 0)),
                  pl.BlockSpec((tm, d), lambda i, j: (i, 0))],
        out_specs=[pl.BlockSpec((tm, d), lambda i, j: (i, 0)),
                   pl.BlockSpec((None, nf, 8, tf), lambda i, j: (i, 0, 0, 0))],
        out_shape=[jax.ShapeDtypeStruct((m, d), F32), jax.ShapeDtypeStruct((m // tm, nf, 8, tf), F32)],
        scratch_shapes=[pltpu.VMEM((nf, 8, tf), F32)] + [pltpu.VMEM((tm, tf), F32)] * 4,
        compiler_params=_cparams(("arbitrary", "arbitrary")),
        name="ffn_prompt",
    )(h2, w_in, w_in, conv_w.reshape(CONV_W, nf, tf).swapaxes(0, 1), conv_b.reshape(nf, 1, tf), w_down, x1)
    return y, tails.swapaxes(1, 2).reshape(m // tm, 8, D_FF)


def _ffn_sample(h2, x1, w_in, conv_w, conv_b, w_down, conv_buf, *, tf):
    m, d = x1.shape
    nf = D_FF // tf
    buf2d = conv_buf.reshape(m, (CONV_W - 1) * D_FF)
    return pl.pallas_call(
        _ffn_sample_kernel,
        grid=(1, nf),
        in_specs=_ffn_specs(m, tf, d, nf) + [pl.BlockSpec((m, tf), lambda i, j: (0, j)),
                                             pl.BlockSpec((m, tf), lambda i, j: (0, nf + j))],
        out_specs=[pl.BlockSpec((m, d), lambda i, j: (i, 0)), pl.BlockSpec((m, tf), lambda i, j: (0, j))],
        out_shape=[jax.ShapeDtypeStruct((m, d), F32), jax.ShapeDtypeStruct((m, D_FF), F32)],
        compiler_params=_cparams(("arbitrary", "arbitrary")),
        name="ffn_sample",
    )(h2, w_in, w_in, conv_w, conv_b, w_down, x1, buf2d, buf2d)


IN_PROJ_TN = 512
FFN_TM = 512
FFN_TF = 512


def _qk_gain_row(q_norm, k_norm):
    reps = N_GROUPS * ATT_HEADS
    return jnp.concatenate([jnp.tile(q_norm, reps), jnp.tile(k_norm, reps), jnp.zeros((NA - A_VA,), F32)])[None, :]


ATTN_PROMPT_TILING = ((4, 4, 1), (1, 4, 2), (1, 2, 4))


def _layer(x_prompt, x_sample, caches, norm_mix, w_in, q_norm, k_norm, b_igate, b_fgate, w_proj_att,
           w_proj_mlstm, w_out, norm_ffn, w_ffn_in, conv_w, conv_b, w_ffn_down):
    batch, seq, d = x_prompt.shape
    bd = x_sample.shape[0]
    assert x_sample.shape[1] == 1 and d == D_MODEL
    (ck1, cv1, ck2, cv2, ck3, cv3, st_c, st_n, st_m, st_conv) = caches

    wt_in = jnp.swapaxes(w_in, 0, 1)
    qk_gain = _qk_gain_row(q_norm, k_norm)
    g1 = norm_mix[None, :]
    g2 = norm_ffn[None, :]
    gate_bias = jnp.concatenate([b_igate, b_fgate, jnp.zeros((LANES - 2 * M_HEADS,), F32)])[None, :]
    cb = conv_b[None, :]

    xp = x_prompt.reshape(batch * seq, d)
    xs = x_sample.reshape(bd, d)

    h_all = _rmsnorm(xp, xs, g1, tm=2048, ts=512)
    (za_p, zb_p, zt_p, zif_p), (za_s, zb_s, zt_s, zif_s) = _in_proj(h_all, wt_in, qk_gain, tn=IN_PROJ_TN, ms=bd)
    to_cast = ((), (), (w_proj_att, w_proj_mlstm, w_out))
    o_p, lse_p, cast = zip(*[_attn_prompt_group(za_p, batch, seq, gi, dil, to_cast[gi], nsub=nsub, hw=hw, rb=rb)
                             for gi, ((_, dil), (nsub, hw, rb)) in enumerate(zip(DIL_PATTERNS, ATTN_PROMPT_TILING))])
    _, _, (wpa, wpm, wo) = cast
    hm_p, p_c, p_n, p_m = _mlstm_prompt(zif_p, zb_p, gate_bias, batch, seq, chunk=256, hh=2)
    x1_p, h2_p, (w_ff, w_dn) = _merge(o_p, lse_p, hm_p, zt_p, xp, wpa, wpm, wo, g2, (w_ffn_in, w_ffn_down), tm=256)
    y_p, tails = _ffn_prompt(h2_p, x1_p, w_ff, conv_w, cb, w_dn, seq, tm=FFN_TM, tf=FFN_TF)

    za_p3 = za_p.reshape(batch, seq, NA)
    p_kv = []
    for gi, (win, _) in enumerate(DIL_PATTERNS):
        keep = min(win, seq)
        for col in (A_KA, A_VA):
            lo = col + gi * ATT_GROUP_W
            p_kv.append(za_p3[:, seq - keep:, lo:lo + ATT_GROUP_W].reshape(batch, keep, ATT_HEADS, ATT_HEAD_DIM))
    tiles_per_seq = seq // FFN_TM
    p_conv = tails.reshape(batch, tiles_per_seq, 8, D_FF)[:, -1, 8 - (CONV_W - 1):, :]

    o_s, lse_s = _attn_sample(za_s, [(ck1, cv1), (ck2, cv2), (ck3, cv3)], bb=4)
    hm_s, s_c, s_n, s_m = _mlstm_sample(zif_s, zb_s, gate_bias, st_c, st_n, st_m, bb=4)
    x1_s, h2_s, _ = _merge(o_s, lse_s, hm_s, zt_s, xs, wpa, wpm, wo, g2, tm=bd)
    y_s, g_s = _ffn_sample(h2_s, x1_s, w_ff, conv_w, cb, w_dn, st_conv, tf=FFN_TF)

    s_kv = []
    for gi in range(N_GROUPS):
        for col in (A_KA, A_VA):
            lo = col + gi * ATT_GROUP_W
            s_kv.append(za_s[:, lo:lo + ATT_GROUP_W].reshape(bd, 1, ATT_HEADS, ATT_HEAD_DIM))
    s_conv = jnp.stack([st_conv[:, 1, :], g_s], axis=1)

    p_state = p_kv + [p_c, p_n, p_m, p_conv]
    s_state = s_kv + [s_c, s_n, s_m, s_conv]
    return y_p.reshape(batch, seq, d), y_s.reshape(bd, 1, d), p_state, s_state


def kernel(x_prompt, x_sample, cache_k_w128, cache_v_w128, cache_k_w512, cache_v_w512, cache_k_w2048,
           cache_v_w2048, state_mlstm_C, state_mlstm_n, state_mlstm_m, state_ffn_conv, norm_mix, w_in, q_norm,
           k_norm, b_igate, b_fgate, w_proj_att, w_proj_mlstm, w_out, norm_ffn, w_ffn_in, conv_w, conv_b,
           w_ffn_down):
    assert norm_mix.shape[0] == 1
    caches = [c[0] for c in (cache_k_w128, cache_v_w128, cache_k_w512, cache_v_w512, cache_k_w2048,
                             cache_v_w2048, state_mlstm_C, state_mlstm_n, state_mlstm_m, state_ffn_conv)]
    weights = [w[0] for w in (norm_mix, w_in, q_norm, k_norm, b_igate, b_fgate, w_proj_att, w_proj_mlstm,
                              w_out, norm_ffn, w_ffn_in, conv_w, conv_b, w_ffn_down)]
    y_p, y_s, p_state, s_state = _layer(x_prompt, x_sample, caches, *weights)
    return (y_p, y_s, *[a[None] for a in p_state], *[a[None] for a in s_state])
```

```python
import functools

import jax
import jax.numpy as jnp
from jax import lax
from jax.experimental import pallas as pl
from jax.experimental.pallas import tpu as pltpu

F32 = jnp.float32
BF16 = jnp.bfloat16

RMS_EPS = 1e-6
NEG_INF = -1e30
LANES = 128
VMEM_LIMIT = 56 * 1024 * 1024

D_MODEL = 2048
DIL_PATTERNS = ((128, 1), (512, 4), (2048, 16))
N_GROUPS = 3
ATT_HEADS = 4
ATT_HEAD_DIM = 128
ATT_SPAN = 128
ATT_SCALE = ATT_HEAD_DIM ** -0.5
ATT_GROUP_W = ATT_HEADS * ATT_HEAD_DIM
ATT_WIDTH = N_GROUPS * ATT_GROUP_W
M_HEADS = 4
M_QK_DIM = D_MODEL // (2 * M_HEADS)
M_V_DIM = D_MODEL // M_HEADS
M_QK_WIDTH = M_HEADS * M_QK_DIM
M_V_WIDTH = M_HEADS * M_V_DIM
M_K_SCALE = M_QK_DIM ** -0.5
D_FF = ((8 * D_MODEL // 3 + 255) // 256) * 256
CONV_W = 3
SPLIT_SIZES = (ATT_WIDTH, ATT_WIDTH, ATT_WIDTH, M_QK_WIDTH, M_QK_WIDTH, M_V_WIDTH, M_V_WIDTH,
               M_HEADS, M_HEADS, D_MODEL, D_MODEL)

IN_COLS = sum(SPLIT_SIZES)
A_QA = 0
A_KA = A_QA + ATT_WIDTH
A_VA = A_KA + ATT_WIDTH
NA = A_VA + ATT_WIDTH
B_QM = 0
B_KM = B_QM + M_QK_WIDTH
B_VM = B_KM + M_QK_WIDTH
B_OM = B_VM + M_V_WIDTH
NB = B_OM + M_V_WIDTH
N_IF = 2 * M_HEADS
T_GA = 0
T_GB = T_GA + D_MODEL
NT = T_GB + D_MODEL
assert NA + NB + N_IF + NT == IN_COLS


def _cparams(sem):
    return pltpu.CompilerParams(dimension_semantics=sem, vmem_limit_bytes=VMEM_LIMIT)


def _log_sigmoid(x):
    return jnp.minimum(x, 0.0) - jnp.log(1.0 + jnp.exp(-jnp.abs(x)))


def _rmsnorm_kernel(x_ref, xs_ref, g_ref, h_ref, *, nsub):
    j = pl.program_id(1)
    ts = x_ref.shape[0]

    def norm(x):
        ms = jnp.mean(x * x, axis=-1, keepdims=True)
        return (x * lax.rsqrt(ms + RMS_EPS) * g_ref[...]).astype(h_ref.dtype)

    @pl.when(j < nsub)
    def _():
        h_ref[pl.ds(pl.multiple_of(j * ts, ts), ts), :] = norm(x_ref[...])

    @pl.when(j == nsub)
    def _():
        h_ref[nsub * ts:, :] = norm(xs_ref[...])


def _rmsnorm(x, xs, gain, *, tm, ts):
    m, d = x.shape
    ms = xs.shape[0]
    assert m % tm == 0 and tm % ts == 0
    nsub = tm // ts
    last = m // ts - 1
    return pl.pallas_call(
        functools.partial(_rmsnorm_kernel, nsub=nsub),
        grid=(m // tm, nsub + 1),
        in_specs=[pl.BlockSpec((ts, d), lambda i, j: (jnp.minimum(i * nsub + j, last), 0)),
                  pl.BlockSpec((ms, d), lambda i, j: (0, 0)),
                  pl.BlockSpec((1, d), lambda i, j: (0, 0))],
        out_specs=pl.BlockSpec((None, tm + ms, d), lambda i, j: (i, 0, 0)),
        out_shape=jax.ShapeDtypeStruct((m // tm, tm + ms, d), BF16),
        compiler_params=_cparams(("parallel", "arbitrary")),
        name="rmsnorm",
    )(x, xs, gain)


def _in_proj_kernel(h_ref, w_ref, wn_ref, wif_ref, qkg_ref, za_ref, zb_ref, zt_ref, zif_ref,
                    sa_ref, sb_ref, st_ref, sif_ref, *, n_norm, na, nb, nt):
    j = pl.program_id(1)
    tp = za_ref.shape[0]

    def z_of(w):
        return lax.dot_general(h_ref[...], w.astype(BF16), (((1,), (1,)), ((), ())), preferred_element_type=F32)

    @pl.when(j < n_norm)
    def _():
        z = z_of(w_ref[...])
        for c in range(za_ref.shape[1] // LANES):
            sl = slice(c * LANES, (c + 1) * LANES)
            zc = z[:, sl]
            ms = jnp.mean(zc * zc, axis=-1, keepdims=True)
            zn = zc * lax.rsqrt(ms + RMS_EPS) * qkg_ref[:, sl]
            za_ref[:, sl] = zn[:tp]
            sa_ref[j, :, sl] = zn[tp:]

    @pl.when((j >= n_norm) & (j < na))
    def _():
        z = z_of(w_ref[...])
        za_ref[...] = z[:tp]
        sa_ref[j] = z[tp:]

    @pl.when((j >= na) & (j < na + nb))
    def _():
        z = z_of(w_ref[...])
        zb_ref[...] = z[:tp].astype(zb_ref.dtype)
        sb_ref[j - na] = z[tp:]

    @pl.when((j >= na + nb) & (j < na + nb + nt))
    def _():
        z = z_of(jnp.concatenate([w_ref[N_IF:, :], wn_ref[...]], axis=0))
        zt_ref[...] = z[:tp].astype(zt_ref.dtype)
        st_ref[j - na - nb] = z[tp:]

    @pl.when(j == na + nb + nt)
    def _():
        z = z_of(jnp.concatenate([wif_ref[...], jnp.zeros((LANES - N_IF, wif_ref.shape[1]), F32)], axis=0))
        zif_ref[...] = z[:tp]
        sif_ref[...] = z[tp:]


def _in_proj(h, wt, qk_gain, *, tn, ms):
    ni, th, d = h.shape
    tm = th - ms
    m = ni * tm
    assert wt.shape == (IN_COLS, d) and N_IF == 8
    assert NA % tn == 0 and NB % tn == 0 and NT % tn == 0 and A_VA % tn == 0 and tn % N_IF == 0
    na, nb, nt = NA // tn, NB // tn, NT // tn
    nw = na + nb + nt
    g_if = (NA + NB) // N_IF
    g_tn = tn // N_IF
    kern = functools.partial(_in_proj_kernel, n_norm=A_VA // tn, na=na, nb=nb, nt=nt)
    col_a = lambda j: jnp.minimum(j, na - 1)
    col_b = lambda j: jnp.clip(j - na, 0, nb - 1)
    col_t = lambda j: jnp.clip(j - na - nb, 0, nt - 1)
    out_specs = [pl.BlockSpec((None, tm, tn), lambda i, j: (i, 0, col_a(j))),
                 pl.BlockSpec((None, tm, tn), lambda i, j: (i, 0, col_b(j))),
                 pl.BlockSpec((None, tm, tn), lambda i, j: (i, 0, col_t(j))),
                 pl.BlockSpec((None, tm, LANES), lambda i, j: (i, 0, 0))]
    out_shape = [jax.ShapeDtypeStruct((ni, tm, w), dt)
                 for w, dt in zip((NA, NB, NT, LANES), (F32, BF16, BF16, F32))]
    for n_tiles in (na, nb, nt):
        out_specs.append(pl.BlockSpec((None, n_tiles, ms, tn), lambda i, j: (i, 0, 0, 0)))
        out_shape.append(jax.ShapeDtypeStruct((ni, n_tiles, ms, tn), F32))
    out_specs.append(pl.BlockSpec((None, ms, LANES), lambda i, j: (i, 0, 0)))
    out_shape.append(jax.ShapeDtypeStruct((ni, ms, LANES), F32))
    outs = pl.pallas_call(
        kern,
        grid=(ni, nw + 1),
        in_specs=[
            pl.BlockSpec((None, th, d), lambda i, j: (i, 0, 0)),
            pl.BlockSpec((tn, d), lambda i, j: (jnp.minimum(j, nw - 1), 0)),
            pl.BlockSpec((N_IF, d), lambda i, j: (g_if + g_tn * (jnp.clip(j, na + nb, nw - 1) - (na + nb) + 1), 0)),
            pl.BlockSpec((N_IF, d), lambda i, j: (g_if, 0)),
            pl.BlockSpec((1, tn), lambda i, j: (0, col_a(j))),
        ],
        out_specs=out_specs,
        out_shape=out_shape,
        compiler_params=_cparams(("parallel", "arbitrary")),
        name="in_proj",
    )(h, wt, wt, wt, qk_gain)
    sample = [o[0].swapaxes(0, 1).reshape(ms, -1) for o in outs[4:7]] + [outs[7][0]]
    return [o.reshape(m, o.shape[2]) for o in outs[:4]], sample


def _attn_prompt_kernel(*refs, dil, nsub, hw, rb, n_cast):
    q_refs, kp_refs, kc_refs, vp_refs, vc_refs = [refs[t * hw:(t + 1) * hw] for t in range(5)]
    cast_in = refs[5 * hw:5 * hw + n_cast]
    o_ref, lse_ref = refs[5 * hw + n_cast:5 * hw + n_cast + 2]
    cast_out = refs[5 * hw + n_cast + 2:5 * hw + 2 * n_cast + 2]
    o_refs = refs[5 * hw + 2 * n_cast + 2:]
    for w_ref, wb_ref in zip(cast_in, cast_out):
        wb_ref[...] = w_ref[...].astype(wb_ref.dtype)
    c = pl.program_id(1)
    blk = ATT_SPAN
    step = blk * dil
    qi = lax.broadcasted_iota(jnp.int32, (blk, 2 * blk), 0)
    kj = lax.broadcasted_iota(jnp.int32, (blk, 2 * blk), 1)
    band = (kj >= qi) & (kj <= qi + blk)
    first_bias = jnp.where((kj < blk) & (c == 0), NEG_INF, 0.0)
    lane = lax.broadcasted_iota(jnp.int32, (blk, LANES), 1)

    def rows(base, r):
        return pl.ds(base + r, blk) if dil == 1 else pl.ds(base + r, blk, stride=dil)

    def group(r0, s):
        base = s * step
        ids = [(r0 + r, n) for r in range(rb) for n in range(hw)]

        def stacked(cur_refs, prev_refs):
            parts = []
            for r, n in ids:
                cur = cur_refs[n][rows(base, r), :]
                if prev_refs is None:
                    parts.append(cur)
                else:
                    prv = prev_refs[n][rows(0, r), :] if s == 0 else cur_refs[n][rows(base - step, r), :]
                    parts.append(jnp.concatenate([prv, cur], axis=0))
            return jnp.stack(parts).astype(BF16)

        q = stacked(q_refs, None)
        k = stacked(kc_refs, kp_refs)
        v = stacked(vc_refs, vp_refs)
        sc = jnp.einsum('bqe,bke->bqk', q, k, preferred_element_type=F32) * ATT_SCALE
        if s == 0:
            sc = sc + first_bias
        sc = jnp.where(band, sc, NEG_INF)
        m = jnp.max(sc, axis=-1, keepdims=True)
        p = jnp.exp(sc - m)
        l = jnp.sum(p, axis=-1, keepdims=True)
        o = jnp.einsum('bqk,bke->bqe', p.astype(BF16), v, preferred_element_type=F32) / l
        lse = m + jnp.log(l)
        for r in range(rb):
            lse_tile = jnp.zeros((blk, LANES), F32)
            for n in range(hw):
                b = r * hw + n
                o_refs[n][rows(base, r0 + r), :] = o[b]
                lse_tile = jnp.where(lane == n, lse[b], lse_tile)
            lse_ref[rows(base, r0 + r), :] = lse_tile

    for s in range(nsub):
        if dil == rb:
            group(0, s)
        else:
            def body(it, carry, s=s):
                group(it * rb, s)
                return carry
            lax.fori_loop(0, dil // rb, body, 0)

    for n in range(hw):
        o_ref[:, n * ATT_HEAD_DIM:(n + 1) * ATT_HEAD_DIM] = o_refs[n][...]


def _attn_prompt_group(za, batch, seq, gi, dil, to_cast, *, nsub, hw, rb):
    step = ATT_SPAN * dil
    tc = nsub * step
    assert seq % tc == 0 and ATT_HEADS % hw == 0 and dil % rb == 0
    nh = ATT_HEADS // hw
    e = ATT_HEAD_DIM
    nchunk = seq // tc
    n_steps = batch * nchunk * nh
    bf16_sublanes = 16
    cast_in_specs, cast_out_specs, cast_shapes = [], [], []
    for w in to_cast:
        rows, cols = w.shape
        assert rows % (n_steps * bf16_sublanes) == 0
        spec = pl.BlockSpec((rows // n_steps, cols), lambda b, c, h: ((b * nchunk + c) * nh + h, 0))
        cast_in_specs.append(spec)
        cast_out_specs.append(spec)
        cast_shapes.append(jax.ShapeDtypeStruct((rows, cols), BF16))
    z3 = za.reshape(batch, seq, NA)
    qc, kc, vc = [(col + gi * ATT_GROUP_W) // e for col in (A_QA, A_KA, A_VA)]

    def cur(col):
        return [pl.BlockSpec((None, tc, e), lambda b, c, h, n=n: (b, c, col + h * hw + n)) for n in range(hw)]

    def prev(col):
        return [pl.BlockSpec((None, step, e), lambda b, c, h, n=n: (b, jnp.maximum(c * nsub - 1, 0), col + h * hw + n))
                for n in range(hw)]

    kern = functools.partial(_attn_prompt_kernel, dil=dil, nsub=nsub, hw=hw, rb=rb, n_cast=len(to_cast))
    o, lse, *cast = pl.pallas_call(
        kern,
        grid=(batch, nchunk, nh),
        in_specs=cur(qc) + prev(kc) + cur(kc) + prev(vc) + cur(vc) + cast_in_specs,
        out_specs=[pl.BlockSpec((None, tc, hw * e), lambda b, c, h: (b, c, h)),
                   pl.BlockSpec((None, tc, LANES), lambda b, c, h: (b, c, h))] + cast_out_specs,
        out_shape=[jax.ShapeDtypeStruct((batch, seq, ATT_GROUP_W), F32),
                   jax.ShapeDtypeStruct((batch, seq, nh * LANES), F32)] + cast_shapes,
        scratch_shapes=[pltpu.VMEM((tc, e), F32)] * hw,
        compiler_params=_cparams(("parallel", "parallel", "parallel")),
        name=f"attn_prompt_g{gi}",
    )(*([z3] * (5 * hw)), *to_cast)
    return o.reshape(batch * seq, ATT_GROUP_W), lse.reshape(batch * seq, nh * LANES), cast


def _attn_sample_kernel(z_ref, k1, v1, k2, v2, k3, v3, o1, o2, o3, l1, l2, l3):
    bb = z_ref.shape[0]
    bufs = ((k1, v1, o1, l1), (k2, v2, o2, l2), (k3, v3, o3, l3))
    lane = lax.broadcasted_iota(jnp.int32, (1, LANES), 1)

    def heads(b, col):
        return jnp.concatenate([z_ref[b, :, col + n * ATT_HEAD_DIM:col + (n + 1) * ATT_HEAD_DIM]
                                for n in range(ATT_HEADS)], axis=0)

    def body(b, carry):
        for gi, (k_ref, v_ref, o_ref, l_ref) in enumerate(bufs):
            q = heads(b, A_QA + gi * ATT_GROUP_W)
            k_new = heads(b, A_KA + gi * ATT_GROUP_W)
            v_new = heads(b, A_VA + gi * ATT_GROUP_W)
            kb = k_ref[b].reshape(ATT_SPAN // 2, 2 * ATT_HEADS, ATT_HEAD_DIM)
            vb = v_ref[b].reshape(ATT_SPAN // 2, 2 * ATT_HEADS, ATT_HEAD_DIM)
            q2 = jnp.concatenate([q, q], axis=0)
            s = jnp.sum(kb * q2[None], axis=-1, keepdims=True) * ATT_SCALE
            s_new = jnp.sum(k_new * q, axis=-1, keepdims=True) * ATT_SCALE
            m2 = jnp.max(s, axis=0)
            m = jnp.maximum(jnp.maximum(m2[:ATT_HEADS], m2[ATT_HEADS:]), s_new)
            p = jnp.exp(s - jnp.concatenate([m, m], axis=0)[None])
            p_new = jnp.exp(s_new - m)
            l2 = jnp.sum(p, axis=0)
            l = l2[:ATT_HEADS] + l2[ATT_HEADS:] + p_new
            o2 = jnp.sum(p * vb, axis=0)
            o = (o2[:ATT_HEADS] + o2[ATT_HEADS:] + p_new * v_new) / l
            lse = m + jnp.log(l)
            lse_row = jnp.zeros((1, LANES), F32)
            for n in range(ATT_HEADS):
                o_ref[b, :, n * ATT_HEAD_DIM:(n + 1) * ATT_HEAD_DIM] = o[n:n + 1, :]
                lse_row = jnp.where(lane == n, lse[n:n + 1, :], lse_row)
            l_ref[b] = lse_row
        return carry

    lax.fori_loop(0, bb, body, 0)


def _attn_sample(za_s, caches, *, bb):
    bd = za_s.shape[0]
    assert bd % bb == 0
    ins, in_specs = [za_s.reshape(bd, 1, NA)], [pl.BlockSpec((bb, 1, NA), lambda i: (i, 0, 0))]
    for (k_buf, v_buf), (win, dil) in zip(caches, DIL_PATTERNS):
        assert k_buf.shape[1:] == (ATT_SPAN * dil, ATT_HEADS, ATT_HEAD_DIM)
        for buf in (k_buf, v_buf):
            ins.append(buf.reshape(bd, ATT_SPAN, dil, ATT_HEADS, ATT_HEAD_DIM))
            in_specs.append(pl.BlockSpec((bb, ATT_SPAN, None, ATT_HEADS, ATT_HEAD_DIM),
                                         lambda i: (i, 0, 0, 0, 0)))
    outs = pl.pallas_call(
        _attn_sample_kernel,
        grid=(bd // bb,),
        in_specs=in_specs,
        out_specs=[pl.BlockSpec((bb, 1, ATT_GROUP_W), lambda i: (i, 0, 0))] * 3
                  + [pl.BlockSpec((bb, 1, LANES), lambda i: (i, 0, 0))] * 3,
        out_shape=[jax.ShapeDtypeStruct((bd, 1, ATT_GROUP_W), F32)] * 3
                  + [jax.ShapeDtypeStruct((bd, 1, LANES), F32)] * 3,
        compiler_params=_cparams(("parallel",)),
        name="attn_sample",
    )(*ins)
    return [o[:, 0, :] for o in outs[:3]], [l[:, 0, :] for l in outs[3:]]


def _mlstm_prompt_kernel(q_ref, k_ref, v_ref, om_ref, g_ref, bias_ref, *rest):
    n_cast = (len(rest) - 7) // 2
    h_ref, c_out, n_out, m_out = rest[n_cast:n_cast + 4]
    c_s, n_s, m_s = rest[2 * n_cast + 4:]
    for w_ref, wb_ref in zip(rest[:n_cast], rest[n_cast + 4:2 * n_cast + 4]):
        wb_ref[...] = w_ref[...].astype(wb_ref.dtype)
    hh = c_s.shape[0]
    ci = pl.program_id(2)
    L = q_ref.shape[0]

    @pl.when(ci == 0)
    def _():
        c_s[...] = jnp.zeros_like(c_s)
        n_s[...] = jnp.zeros_like(n_s)
        m_s[...] = jnp.zeros_like(m_s)

    gates = g_ref[...] + bias_ref[...]
    gates_t = gates.T
    lane = lax.broadcasted_iota(jnp.int32, (L, LANES), 1)
    sub = lax.broadcasted_iota(jnp.int32, (LANES, L), 0)
    ti = lax.broadcasted_iota(jnp.int32, (L, L), 0)
    si = lax.broadcasted_iota(jnp.int32, (L, L), 1)
    causal = si <= ti

    for j in range(hh):
        hd = pl.program_id(1) * hh + j
        qs = slice(j * M_QK_DIM, (j + 1) * M_QK_DIM)
        vs = slice(j * M_V_DIM, (j + 1) * M_V_DIM)
        li_col = jnp.sum(jnp.where(lane == hd, gates, 0.0), axis=1, keepdims=True)
        lf_col = _log_sigmoid(jnp.sum(jnp.where(lane == hd + M_HEADS, gates, 0.0), axis=1, keepdims=True))
        li_row = jnp.sum(jnp.where(sub == hd, gates_t, 0.0), axis=0, keepdims=True)
        lf_row = _log_sigmoid(jnp.sum(jnp.where(sub == hd + M_HEADS, gates_t, 0.0), axis=0, keepdims=True))

        b_col = jnp.sum(jnp.where(causal, lf_row, 0.0), axis=1, keepdims=True)
        b_row = jnp.sum(jnp.where(ti <= si, lf_col, 0.0), axis=0, keepdims=True)
        b_end = jnp.sum(lf_row, axis=1, keepdims=True)

        m_prev = m_s[j]
        dmat = jnp.where(causal, b_col - b_row + li_row, NEG_INF)
        inter = b_col + m_prev
        mt = jnp.maximum(inter, jnp.max(dmat, axis=1, keepdims=True))

        qb = q_ref[:, qs]
        q = qb.astype(F32)
        k = k_ref[:, qs].astype(F32) * M_K_SCALE
        vb = v_ref[:, vs]
        qk = lax.dot_general(qb, k.astype(BF16), (((1,), (1,)), ((), ())), preferred_element_type=F32)
        a = jnp.exp(dmat - mt) * qk
        w_inter = jnp.exp(inter - mt)
        num = (jnp.dot(a.astype(BF16), vb, preferred_element_type=F32)
               + w_inter * jnp.dot(qb, c_s[j].astype(BF16), preferred_element_type=F32))
        den = jnp.sum(a, axis=1, keepdims=True) + w_inter * jnp.sum(q * n_s[j], axis=1, keepdims=True)
        h = num / jnp.maximum(jnp.abs(den), jnp.exp(-mt))
        h_ref[:, vs] = (jax.nn.sigmoid(om_ref[:, vs].astype(F32)) * h).astype(h_ref.dtype)

        g_col = b_end - b_col + li_col
        g_row = b_end - b_row + li_row
        m_new = jnp.maximum(b_end + m_prev, jnp.max(g_row, axis=1, keepdims=True))
        decay = jnp.exp(b_end + m_prev - m_new)
        kw = jnp.exp(g_col - m_new) * k
        c_s[j] = decay * c_s[j] + jnp.dot(kw.T.astype(BF16), vb, preferred_element_type=F32)
        n_s[j] = decay * n_s[j] + jnp.sum(kw, axis=0, keepdims=True)
        m_s[j] = m_new

    @pl.when(ci == pl.num_programs(2) - 1)
    def _():
        c_out[...] = c_s[...]
        n_out[...] = n_s[...]
        m_out[...] = jnp.broadcast_to(m_s[...], m_out.shape)


def _mlstm_prompt(zif, zb, gate_bias, batch, seq, to_cast=(), *, chunk, hh):
    assert seq % chunk == 0 and M_HEADS % hh == 0
    nc = seq // chunk
    nhg = M_HEADS // hh
    n_steps = batch * nhg * nc
    zif3 = zif.reshape(batch, seq, LANES)
    zb3 = zb.reshape(batch, seq, NB)
    wq, wv = hh * M_QK_DIM, hh * M_V_DIM
    qk_blk = lambda col: pl.BlockSpec((None, chunk, wq), lambda b, h, c: (b, c, col // wq + h))
    v_blk = lambda col: pl.BlockSpec((None, chunk, wv), lambda b, h, c: (b, c, col // wv + h))
    cast_specs, cast_shapes = [], []
    for w in to_cast:
        rows, cols = w.shape
        assert rows % (n_steps * 16) == 0
        cast_specs.append(pl.BlockSpec((rows // n_steps, cols), lambda b, h, c: ((b * nhg + h) * nc + c, 0)))
        cast_shapes.append(jax.ShapeDtypeStruct((rows, cols), BF16))
    hm, c1, n1, m1, *cast = pl.pallas_call(
        _mlstm_prompt_kernel,
        grid=(batch, nhg, nc),
        in_specs=[qk_blk(B_QM), qk_blk(B_KM), v_blk(B_VM), v_blk(B_OM),
                  pl.BlockSpec((None, chunk, LANES), lambda b, h, c: (b, c, 0)),
                  pl.BlockSpec((1, LANES), lambda b, h, c: (0, 0))] + cast_specs,
        out_specs=[pl.BlockSpec((None, chunk, wv), lambda b, h, c: (b, c, h)),
                   pl.BlockSpec((None, hh, M_QK_DIM, M_V_DIM), lambda b, h, c: (b, h, 0, 0)),
                   pl.BlockSpec((None, hh, 1, M_QK_DIM), lambda b, h, c: (b, h, 0, 0)),
                   pl.BlockSpec((None, hh, 1, LANES), lambda b, h, c: (b, h, 0, 0))] + cast_specs,
        out_shape=[jax.ShapeDtypeStruct((batch, seq, M_V_WIDTH), BF16),
                   jax.ShapeDtypeStruct((batch, M_HEADS, M_QK_DIM, M_V_DIM), F32),
                   jax.ShapeDtypeStruct((batch, M_HEADS, 1, M_QK_DIM), F32),
                   jax.ShapeDtypeStruct((batch, M_HEADS, 1, LANES), F32)] + cast_shapes,
        scratch_shapes=[pltpu.VMEM((hh, M_QK_DIM, M_V_DIM), F32), pltpu.VMEM((hh, 1, M_QK_DIM), F32),
                        pltpu.VMEM((hh, 1, 1), F32)],
        compiler_params=_cparams(("parallel", "parallel", "arbitrary")),
        name="mlstm_prompt",
    )(zb3, zb3, zb3, zb3, zif3, gate_bias, *to_cast)
    return hm.reshape(batch * seq, M_V_WIDTH), c1, n1[:, :, 0, :], m1[:, :, 0, 0], cast


def _rows_to_cols(rows):
    rep = LANES // len(rows)
    n = rows[0].shape[1]
    t = jnp.concatenate([jnp.broadcast_to(r, (rep, n)) for r in rows], axis=0).T
    return [t[:, i * rep:i * rep + 1] for i in range(len(rows))]


def _mlstm_sample_kernel(zif_ref, zb_ref, bias_ref, c_ref, n_ref, m_ref, h_ref, c_out, n_out, m_out):
    for b in range(zif_ref.shape[0]):
        _mlstm_sample_one(zif_ref.at[b], zb_ref.at[b], bias_ref, c_ref.at[b], n_ref.at[b], m_ref.at[b],
                          h_ref.at[b], c_out.at[b], n_out.at[b], m_out.at[b])


def _mlstm_sample_one(zif_ref, zb_ref, bias_ref, c_ref, n_ref, m_ref, h_ref, c_out, n_out, m_out):
    gates = zif_ref[...] + bias_ref[...]
    lane = lax.broadcasted_iota(jnp.int32, (1, LANES), 1)
    m_row = jnp.zeros((1, LANES), F32)
    qs, wks, inters, m_news = [], [], [], []
    for h in range(M_HEADS):
        k = zb_ref[:, B_KM + h * M_QK_DIM:B_KM + (h + 1) * M_QK_DIM].astype(F32) * M_K_SCALE
        li = gates[:, h:h + 1]
        lf = _log_sigmoid(gates[:, M_HEADS + h:M_HEADS + h + 1])
        inter = lf + m_ref[:, h:h + 1]
        m_new = jnp.maximum(inter, li)
        qs.append(zb_ref[:, B_QM + h * M_QK_DIM:B_QM + (h + 1) * M_QK_DIM].astype(F32))
        wks.append(jnp.exp(li - m_new) * k)
        inters.append(inter)
        m_news.append(m_new)
    cols = _rows_to_cols(qs + wks)
    for h in range(M_HEADS):
        q, wk, inter, m_new = qs[h], wks[h], inters[h], m_news[h]
        v = zb_ref[:, B_VM + h * M_V_DIM:B_VM + (h + 1) * M_V_DIM].astype(F32)
        om = zb_ref[:, B_OM + h * M_V_DIM:B_OM + (h + 1) * M_V_DIM].astype(F32)
        c0 = c_ref[h]
        n0 = n_ref[h:h + 1, :]
        a = jnp.sum(q * wk, axis=1, keepdims=True)
        w_inter = jnp.exp(inter - m_new)
        q_c = jnp.sum(cols[h] * c0, axis=0, keepdims=True)
        num = a * v + w_inter * q_c
        den = a + w_inter * jnp.sum(q * n0, axis=1, keepdims=True)
        hv = num / jnp.maximum(jnp.abs(den), jnp.exp(-m_new))
        h_ref[:, h * M_V_DIM:(h + 1) * M_V_DIM] = (jax.nn.sigmoid(om) * hv).astype(h_ref.dtype)
        c_out[h] = w_inter * c0 + cols[M_HEADS + h] * v
        n_out[h:h + 1, :] = w_inter * n0 + wk
        m_row = jnp.where(lane == h, m_new, m_row)
    m_out[...] = m_row


def _mlstm_sample(zif_s, zb_s, gate_bias, c0, n0, m0, *, bb):
    bd = zif_s.shape[0]
    assert bd % bb == 0
    hm, c1, n1, m1 = pl.pallas_call(
        _mlstm_sample_kernel,
        grid=(bd // bb,),
        in_specs=[pl.BlockSpec((bb, 1, LANES), lambda b: (b, 0, 0)),
                  pl.BlockSpec((bb, 1, NB), lambda b: (b, 0, 0)),
                  pl.BlockSpec((1, LANES), lambda b: (0, 0)),
                  pl.BlockSpec((bb, M_HEADS, M_QK_DIM, M_V_DIM), lambda b: (b, 0, 0, 0)),
                  pl.BlockSpec((bb, M_HEADS, M_QK_DIM), lambda b: (b, 0, 0)),
                  pl.BlockSpec((bb, 1, M_HEADS), lambda b: (b, 0, 0))],
        out_specs=[pl.BlockSpec((bb, 1, M_V_WIDTH), lambda b: (b, 0, 0)),
                   pl.BlockSpec((bb, M_HEADS, M_QK_DIM, M_V_DIM), lambda b: (b, 0, 0, 0)),
                   pl.BlockSpec((bb, M_HEADS, M_QK_DIM), lambda b: (b, 0, 0)),
                   pl.BlockSpec((bb, 1, LANES), lambda b: (b, 0, 0))],
        out_shape=[jax.ShapeDtypeStruct((bd, 1, M_V_WIDTH), BF16),
                   jax.ShapeDtypeStruct((bd, M_HEADS, M_QK_DIM, M_V_DIM), F32),
                   jax.ShapeDtypeStruct((bd, M_HEADS, M_QK_DIM), F32),
                   jax.ShapeDtypeStruct((bd, 1, LANES), F32)],
        compiler_params=_cparams(("parallel",)),
        name="mlstm_sample",
    )(zif_s.reshape(bd, 1, LANES), zb_s.reshape(bd, 1, NB), gate_bias, c0, n0, m0.reshape(bd, 1, M_HEADS))
    return hm.reshape(bd, M_V_WIDTH), c1, n1, m1[:, 0, :M_HEADS]


def _merge_kernel(o1, o2, o3, l1, l2, l3, hm_ref, ga_ref, gb_ref, x_ref, wpa_ref, wpm_ref, wo_ref, g2_ref,
                  *rest, hws):
    n_cast = (len(rest) - 2) // 2
    x1_ref, h2_ref = rest[n_cast:n_cast + 2]
    for w_ref, wb_ref in zip(rest[:n_cast], rest[n_cast + 2:]):
        wb_ref[...] = w_ref[...].astype(wb_ref.dtype)
    parts = []
    for n in range(ATT_HEADS):
        sl = slice(n * ATT_HEAD_DIM, (n + 1) * ATT_HEAD_DIM)
        ls = []
        for l_ref, hw in zip((l1, l2, l3), hws):
            ln = (n // hw) * LANES + n % hw
            ls.append(l_ref[:, ln:ln + 1])
        mx = jnp.maximum(jnp.maximum(ls[0], ls[1]), ls[2])
        es = [jnp.exp(l - mx) for l in ls]
        tot = es[0] + es[1] + es[2]
        parts.append((es[0] / tot) * o1[:, sl] + (es[1] / tot) * o2[:, sl] + (es[2] / tot) * o3[:, sl])
    att = jnp.concatenate(parts, axis=1).astype(BF16)
    pa = jnp.dot(att, wpa_ref[...], preferred_element_type=F32)
    pm = jnp.dot(hm_ref[...], wpm_ref[...], preferred_element_type=F32)
    merged = (jax.nn.sigmoid(ga_ref[...].astype(F32)) * pa + jax.nn.sigmoid(gb_ref[...].astype(F32)) * pm)
    x1 = x_ref[...] + jnp.dot(merged.astype(BF16), wo_ref[...], preferred_element_type=F32)
    x1_ref[...] = x1
    ms = jnp.mean(x1 * x1, axis=-1, keepdims=True)
    h2_ref[...] = (x1 * lax.rsqrt(ms + RMS_EPS) * g2_ref[...]).astype(BF16)


def _merge(o_list, lse_list, hm, zt, x, wpa, wpm, wo, g2, to_cast=(), *, tm):
    m, d = x.shape
    assert m % tm == 0 and zt.shape[1] == NT
    n_steps = m // tm
    row_blk = lambda w: pl.BlockSpec((tm, w), lambda i: (i, 0))
    const = lambda shape: pl.BlockSpec(shape, lambda i: (0, 0), pipeline_mode=pl.Buffered(1))
    hws = tuple(ATT_HEADS * LANES // l.shape[1] for l in lse_list)
    cast_specs, cast_shapes = [], []
    for w in to_cast:
        rows, cols = w.shape
        assert rows % (n_steps * 16) == 0
        cast_specs.append(pl.BlockSpec((rows // n_steps, cols), lambda i: (i, 0)))
        cast_shapes.append(jax.ShapeDtypeStruct((rows, cols), BF16))
    x1, h2, *cast = pl.pallas_call(
        functools.partial(_merge_kernel, hws=hws),
        grid=(n_steps,),
        in_specs=[row_blk(ATT_GROUP_W)] * 3 + [row_blk(l.shape[1]) for l in lse_list]
                 + [row_blk(M_V_WIDTH),
                    pl.BlockSpec((tm, d), lambda i: (i, T_GA // d)),
                    pl.BlockSpec((tm, d), lambda i: (i, T_GB // d)),
                    row_blk(d),
                    const(wpa.shape), const(wpm.shape), const(wo.shape), const((1, d))] + cast_specs,
        out_specs=[row_blk(d), row_blk(d)] + cast_specs,
        out_shape=[jax.ShapeDtypeStruct((m, d), F32), jax.ShapeDtypeStruct((m, d), BF16)] + cast_shapes,
        compiler_params=_cparams(("parallel",)),
        name="merge",
    )(*o_list, *lse_list, hm, zt, zt, x, wpa, wpm, wo, g2, *to_cast)
    return x1, h2, cast


def _ffn_body(h2_ref, wu_ref, wg_ref, cw_ref, cb_ref, wd_ref, x1_ref, y_ref, g_prev2, g_prev1):
    j = pl.program_id(1)
    h2 = h2_ref[...]
    u = jnp.dot(h2, wu_ref[...], preferred_element_type=F32)
    g = jnp.dot(h2, wg_ref[...], preferred_element_type=F32)
    gconv = cb_ref[...] + ((g_prev2(g) * cw_ref[0:1, :] + g_prev1(g) * cw_ref[1:2, :]) + g * cw_ref[2:3, :])
    act = (jax.nn.gelu(gconv) * u).astype(BF16)
    down = jnp.dot(act, wd_ref[...], preferred_element_type=F32)

    @pl.when(j == 0)
    def _():
        y_ref[...] = x1_ref[...] + down

    @pl.when(j > 0)
    def _():
        y_ref[...] += down

    return g


def _ffn_prompt_kernel(h2_ref, wu_ref, wg_ref, cw_ref, cb_ref, wd_ref, x1_ref, y_ref, tail_ref, prev_s,
                       u0_s, g0_s, u1_s, g1_s, *, tiles_per_seq, nf):
    i = pl.program_id(0)
    j = pl.program_id(1)
    tm = h2_ref.shape[0]
    slots = ((u0_s, g0_s), (u1_s, g1_s))

    def up(u_s, g_s):
        h2 = h2_ref[...]
        u_s[...] = jnp.dot(h2, wu_ref[...], preferred_element_type=F32)
        g_s[...] = jnp.dot(h2, wg_ref[...], preferred_element_type=F32)

    def down(u_s, g_s):
        jt = j - 1
        prev = jnp.where(i % tiles_per_seq == 0, 0.0, prev_s[jt])
        p2, p1 = prev[6:7, :], prev[7:8, :]
        g = g_s[...]
        r = lax.broadcasted_iota(jnp.int32, g.shape, 0)
        g_m1 = jnp.where(r == 0, p1, pltpu.roll(g, 1, axis=0))
        g_m2 = jnp.where(r == 0, p2, jnp.where(r == 1, p1, pltpu.roll(g, 2, axis=0)))
        cw = cw_ref[jt]
        gconv = cb_ref[jt] + ((g_m2 * cw[0:1, :] + g_m1 * cw[1:2, :]) + g * cw[2:3, :])
        act = (jax.nn.gelu(gconv) * u_s[...]).astype(BF16)
        y_ref[...] += jnp.dot(act, wd_ref[...], preferred_element_type=F32)
        prev_s[jt] = g[tm - 8:tm, :]

    @pl.when(j == 0)
    def _():
        y_ref[...] = x1_ref[...]
        up(*slots[0])

    for par in range(2):
        @pl.when((j > 0) & (j < nf) & (j % 2 == par))
        def _(par=par):
            up(*slots[par])
            down(*slots[1 - par])

    @pl.when(j == nf)
    def _():
        down(*slots[(nf - 1) % 2])
        tail_ref[...] = prev_s[...]


def _ffn_sample_kernel(h2_ref, wu_ref, wg_ref, cw_ref, cb_ref, wd_ref, x1_ref, b2_ref, b1_ref, y_ref, g_ref):
    g = _ffn_body(h2_ref, wu_ref, wg_ref, cw_ref, cb_ref, wd_ref, x1_ref, y_ref,
                  lambda g: b2_ref[...], lambda g: b1_ref[...])
    g_ref[...] = g


def _ffn_specs(tm, tf, d, nf):
    return [pl.BlockSpec((tm, d), lambda i, j: (i, 0)),
            pl.BlockSpec((d, tf), lambda i, j: (0, j)),
            pl.BlockSpec((d, tf), lambda i, j: (0, nf + j)),
            pl.BlockSpec((CONV_W, tf), lambda i, j: (0, j)),
            pl.BlockSpec((1, tf), lambda i, j: (0, j)),
            pl.BlockSpec((tf, d), lambda i, j: (j, 0)),
            pl.BlockSpec((tm, d), lambda i, j: (i, 0))]


def _ffn_prompt(h2, x1, w_in, conv_w, conv_b, w_down, seq, *, tm, tf):
    m, d = x1.shape
    assert m % tm == 0 and seq % tm == 0 and D_FF % tf == 0 and tm % 8 == 0
    nf = D_FF // tf
    kern = functools.partial(_ffn_prompt_kernel, tiles_per_seq=seq // tm, nf=nf)
    up_t = lambda j: jnp.minimum(j, nf - 1)
    dn_t = lambda j: jnp.maximum(j - 1, 0)
    y, tails = pl.pallas_call(
        kern,
        grid=(m // tm, nf + 1),
        in_specs=[pl.BlockSpec((tm, d), lambda i, j: (i, 0)),
                  pl.BlockSpec((d, tf), lambda i, j: (0, up_t(j))),
                  pl.BlockSpec((d, tf), lambda i, j: (0, nf + up_t(j))),
                  pl.BlockSpec((nf, CONV_W, tf), lambda i, j: (0, 0, 0)),
                  pl.BlockSpec((nf, 1, tf), lambda i, j: (0, 0, 0)),
                  pl.BlockSpec((tf, d), lambda i, j: (dn_t(j), 0)),
                  pl.BlockSpec((tm, d), lambda i, j: (i, 0))],
        out_specs=[pl.BlockSpec((tm, d), lambda i, j: (i, 0)),
                   pl.BlockSpec((None, nf, 8, tf), lambda i, j: (i, 0, 0, 0))],
        out_shape=[jax.ShapeDtypeStruct((m, d), F32), jax.ShapeDtypeStruct((m // tm, nf, 8, tf), F32)],
        scratch_shapes=[pltpu.VMEM((nf, 8, tf), F32)] + [pltpu.VMEM((tm, tf), F32)] * 4,
        compiler_params=_cparams(("arbitrary", "arbitrary")),
        name="ffn_prompt",
    )(h2, w_in, w_in, conv_w.reshape(CONV_W, nf, tf).swapaxes(0, 1), conv_b.reshape(nf, 1, tf), w_down, x1)
    return y, tails.swapaxes(1, 2).reshape(m // tm, 8, D_FF)


def _ffn_sample(h2, x1, w_in, conv_w, conv_b, w_down, conv_buf, *, tf):
    m, d = x1.shape
    nf = D_FF // tf
    buf2d = conv_buf.reshape(m, (CONV_W - 1) * D_FF)
    return pl.pallas_call(
        _ffn_sample_kernel,
        grid=(1, nf),
        in_specs=_ffn_specs(m, tf, d, nf) + [pl.BlockSpec((m, tf), lambda i, j: (0, j)),
                                             pl.BlockSpec((m, tf), lambda i, j: (0, nf + j))],
        out_specs=[pl.BlockSpec((m, d), lambda i, j: (i, 0)), pl.BlockSpec((m, tf), lambda i, j: (0, j))],
        out_shape=[jax.ShapeDtypeStruct((m, d), F32), jax.ShapeDtypeStruct((m, D_FF), F32)],
        compiler_params=_cparams(("arbitrary", "arbitrary")),
        name="ffn_sample",
    )(h2, w_in, w_in, conv_w, conv_b, w_down, x1, buf2d, buf2d)


IN_PROJ_TN = 512
FFN_TM = 512
FFN_TF = 512
FFN_SAMPLE_TF = D_FF // 4


def _split_heads_kernel(*refs):
    n = len(refs) // 2
    for src, dst in zip(refs[:n], refs[n:]):
        for h in range(ATT_HEADS):
            dst[:, h, :] = src[:, h * ATT_HEAD_DIM:(h + 1) * ATT_HEAD_DIM]


def _split_heads(za3, col_blocks, row0, rows, *, tr):
    batch = za3.shape[0]
    assert rows % tr == 0 and row0 % tr == 0
    out_spec = pl.BlockSpec((None, tr, ATT_HEADS, ATT_HEAD_DIM), lambda b, r: (b, r, 0, 0))
    return pl.pallas_call(
        _split_heads_kernel,
        grid=(batch, rows // tr),
        in_specs=[pl.BlockSpec((None, tr, ATT_GROUP_W), lambda b, r, c=c: (b, row0 // tr + r, c)) for c in col_blocks],
        out_specs=[out_spec] * len(col_blocks),
        out_shape=[jax.ShapeDtypeStruct((batch, rows, ATT_HEADS, ATT_HEAD_DIM), F32)] * len(col_blocks),
        compiler_params=_cparams(("parallel", "parallel")),
        name="split_heads",
    )(*([za3] * len(col_blocks)))


def _qk_gain_row(q_norm, k_norm):
    reps = N_GROUPS * ATT_HEADS
    return jnp.concatenate([jnp.tile(q_norm, reps), jnp.tile(k_norm, reps), jnp.zeros((NA - A_VA,), F32)])[None, :]


ATTN_PROMPT_TILING = ((4, 4, 1), (1, 4, 2), (1, 2, 4))


def _layer(x_prompt, x_sample, caches, norm_mix, w_in, q_norm, k_norm, b_igate, b_fgate, w_proj_att,
           w_proj_mlstm, w_out, norm_ffn, w_ffn_in, conv_w, conv_b, w_ffn_down):
    batch, seq, d = x_prompt.shape
    bd = x_sample.shape[0]
    assert x_sample.shape[1] == 1 and d == D_MODEL
    (ck1, cv1, ck2, cv2, ck3, cv3, st_c, st_n, st_m, st_conv) = caches

    wt_in = jnp.swapaxes(w_in, 0, 1)
    qk_gain = _qk_gain_row(q_norm, k_norm)
    g1 = norm_mix[None, :]
    g2 = norm_ffn[None, :]
    gate_bias = jnp.concatenate([b_igate, b_fgate, jnp.zeros((LANES - 2 * M_HEADS,), F32)])[None, :]
    cb = conv_b[None, :]

    xp = x_prompt.reshape(batch * seq, d)
    xs = x_sample.reshape(bd, d)

    h_all = _rmsnorm(xp, xs, g1, tm=2048, ts=1024)
    (za_p, zb_p, zt_p, zif_p), (za_s, zb_s, zt_s, zif_s) = _in_proj(h_all, wt_in, qk_gain, tn=IN_PROJ_TN, ms=bd)
    o_p, lse_p, _ = zip(*[_attn_prompt_group(za_p, batch, seq, gi, dil, (), nsub=nsub, hw=hw, rb=rb)
                          for gi, ((_, dil), (nsub, hw, rb)) in enumerate(zip(DIL_PATTERNS, ATTN_PROMPT_TILING))])
    hm_p, p_c, p_n, p_m, (wpa, wpm, wo) = _mlstm_prompt(zif_p, zb_p, gate_bias, batch, seq,
                                                        (w_proj_att, w_proj_mlstm, w_out), chunk=256, hh=4)
    x1_p, h2_p, (w_ff, w_dn) = _merge(o_p, lse_p, hm_p, zt_p, xp, wpa, wpm, wo, g2, (w_ffn_in, w_ffn_down), tm=256)
    y_p, tails = _ffn_prompt(h2_p, x1_p, w_ff, conv_w, cb, w_dn, seq, tm=FFN_TM, tf=FFN_TF)

    za_p3 = za_p.reshape(batch, seq, NA)
    p_kv = []
    for gi, (win, _) in enumerate(DIL_PATTERNS):
        keep = min(win, seq)
        cols = [(col + gi * ATT_GROUP_W) // ATT_GROUP_W for col in (A_KA, A_VA)]
        p_kv += _split_heads(za_p3, cols, seq - keep, keep, tr=min(keep, 512))
    tiles_per_seq = seq // FFN_TM
    p_conv = tails.reshape(batch, tiles_per_seq, 8, D_FF)[:, -1, 8 - (CONV_W - 1):, :]

    o_s, lse_s = _attn_sample(za_s, [(ck1, cv1), (ck2, cv2), (ck3, cv3)], bb=8)
    hm_s, s_c, s_n, s_m = _mlstm_sample(zif_s, zb_s, gate_bias, st_c, st_n, st_m, bb=4)
    x1_s, h2_s, _ = _merge(o_s, lse_s, hm_s, zt_s, xs, wpa, wpm, wo, g2, tm=bd)
    y_s, g_s = _ffn_sample(h2_s, x1_s, w_ff, conv_w, cb, w_dn, st_conv, tf=FFN_SAMPLE_TF)

    cols = [(col + gi * ATT_GROUP_W) // ATT_GROUP_W for gi in range(N_GROUPS) for col in (A_KA, A_VA)]
    s_kv = [a.reshape(bd, 1, ATT_HEADS, ATT_HEAD_DIM) for a in _split_heads(za_s[None], cols, 0, bd, tr=bd)]
    s_conv = jnp.stack([st_conv[:, 1, :], g_s], axis=1)

    p_state = p_kv + [p_c, p_n, p_m, p_conv]
    s_state = s_kv + [s_c, s_n, s_m, s_conv]
    return y_p.reshape(batch, seq, d), y_s.reshape(bd, 1, d), p_state, s_state


def kernel(x_prompt, x_sample, cache_k_w128, cache_v_w128, cache_k_w512, cache_v_w512, cache_k_w2048,
           cache_v_w2048, state_mlstm_C, state_mlstm_n, state_mlstm_m, state_ffn_conv, norm_mix, w_in, q_norm,
           k_norm, b_igate, b_fgate, w_proj_att, w_proj_mlstm, w_out, norm_ffn, w_ffn_in, conv_w, conv_b,
           w_ffn_down):
    assert norm_mix.shape[0] == 1
    caches = [c[0] for c in (cache_k_w128, cache_v_w128, cache_k_w512, cache_v_w512, cache_k_w2048,
                             cache_v_w2048, state_mlstm_C, state_mlstm_n, state_mlstm_m, state_ffn_conv)]
    weights = [w[0] for w in (norm_mix, w_in, q_norm, k_norm, b_igate, b_fgate, w_proj_att, w_proj_mlstm,
                              w_out, norm_ffn, w_ffn_in, conv_w, conv_b, w_ffn_down)]
    y_p, y_s, p_state, s_state = _layer(x_prompt, x_sample, caches, *weights)
    return (y_p, y_s, *[a[None] for a in p_state], *[a[None] for a in s_state])
```

```python
import functools

import jax
import jax.numpy as jnp
from jax import lax
from jax.experimental import pallas as pl
from jax.experimental.pallas import tpu as pltpu

F32 = jnp.float32
BF16 = jnp.bfloat16

RMS_EPS = 1e-6
NEG_INF = -1e30
LANES = 128
SUBLANES = 8
BF16_SUBLANES = 16
VMEM_LIMIT = 56 * 1024 * 1024

D_MODEL = 2048
DIL_PATTERNS = ((128, 1), (512, 4), (2048, 16))
N_GROUPS = 3
ATT_HEADS = 4
ATT_HEAD_DIM = 128
ATT_SPAN = 128
ATT_SCALE = ATT_HEAD_DIM ** -0.5
ATT_GROUP_W = ATT_HEADS * ATT_HEAD_DIM
ATT_WIDTH = N_GROUPS * ATT_GROUP_W
M_HEADS = 4
M_QK_DIM = D_MODEL // (2 * M_HEADS)
M_V_DIM = D_MODEL // M_HEADS
M_QK_WIDTH = M_HEADS * M_QK_DIM
M_V_WIDTH = M_HEADS * M_V_DIM
M_K_SCALE = M_QK_DIM ** -0.5
D_FF = ((8 * D_MODEL // 3 + 255) // 256) * 256
CONV_W = 3
SPLIT_SIZES = (ATT_WIDTH, ATT_WIDTH, ATT_WIDTH, M_QK_WIDTH, M_QK_WIDTH, M_V_WIDTH, M_V_WIDTH,
               M_HEADS, M_HEADS, D_MODEL, D_MODEL)

IN_COLS = sum(SPLIT_SIZES)
A_QA = 0
A_KA = A_QA + ATT_WIDTH
A_VA = A_KA + ATT_WIDTH
NA = A_VA + ATT_WIDTH
B_QM = 0
B_KM = B_QM + M_QK_WIDTH
B_VM = B_KM + M_QK_WIDTH
B_OM = B_VM + M_V_WIDTH
NB = B_OM + M_V_WIDTH
N_IF = 2 * M_HEADS
T_GA = 0
T_GB = T_GA + D_MODEL
NT = T_GB + D_MODEL
assert NA + NB + N_IF + NT == IN_COLS


def _cparams(sem):
    return pltpu.CompilerParams(dimension_semantics=sem, vmem_limit_bytes=VMEM_LIMIT)


def _log_sigmoid(x):
    return jnp.minimum(x, 0.0) - jnp.log(1.0 + jnp.exp(-jnp.abs(x)))


def _rmsnorm_kernel(x_ref, xs_ref, g_ref, h_ref, *, nsub):
    j = pl.program_id(1)
    ts = x_ref.shape[0]

    def norm(x):
        ms = jnp.mean(x * x, axis=-1, keepdims=True)
        return (x * lax.rsqrt(ms + RMS_EPS) * g_ref[...]).astype(h_ref.dtype)

    @pl.when(j < nsub)
    def _():
        h_ref[pl.ds(pl.multiple_of(j * ts, ts), ts), :] = norm(x_ref[...])

    @pl.when(j == nsub)
    def _():
        h_ref[nsub * ts:, :] = norm(xs_ref[...])


def _rmsnorm(x, xs, gain, *, tm, ts):
    m, d = x.shape
    ms = xs.shape[0]
    assert m % tm == 0 and tm % ts == 0
    nsub = tm // ts
    last = m // ts - 1
    return pl.pallas_call(
        functools.partial(_rmsnorm_kernel, nsub=nsub),
        grid=(m // tm, nsub + 1),
        in_specs=[pl.BlockSpec((ts, d), lambda i, j: (jnp.minimum(i * nsub + j, last), 0)),
                  pl.BlockSpec((ms, d), lambda i, j: (0, 0)),
                  pl.BlockSpec((1, d), lambda i, j: (0, 0))],
        out_specs=pl.BlockSpec((None, tm + ms, d), lambda i, j: (i, 0, 0)),
        out_shape=jax.ShapeDtypeStruct((m // tm, tm + ms, d), BF16),
        compiler_params=_cparams(("parallel", "arbitrary")),
        name="rmsnorm",
    )(x, xs, gain)


def _in_proj_kernel(h_ref, w_ref, wn_ref, wif_ref, qkg_ref, za_ref, zb_ref, zt_ref, zif_ref,
                    sa_ref, sb_ref, st_ref, sif_ref, *, n_norm, na, nb, nt):
    j = pl.program_id(1)
    tp = za_ref.shape[0]

    def z_of(w):
        return lax.dot_general(h_ref[...], w.astype(BF16), (((1,), (1,)), ((), ())), preferred_element_type=F32)

    @pl.when(j < n_norm)
    def _():
        z = z_of(w_ref[...])
        for c in range(za_ref.shape[1] // LANES):
            sl = slice(c * LANES, (c + 1) * LANES)
            zc = z[:, sl]
            ms = jnp.mean(zc * zc, axis=-1, keepdims=True)
            zn = zc * lax.rsqrt(ms + RMS_EPS) * qkg_ref[:, sl]
            za_ref[:, sl] = zn[:tp]
            sa_ref[j, :, sl] = zn[tp:]

    @pl.when((j >= n_norm) & (j < na))
    def _():
        z = z_of(w_ref[...])
        za_ref[...] = z[:tp]
        sa_ref[j] = z[tp:]

    @pl.when((j >= na) & (j < na + nb))
    def _():
        z = z_of(w_ref[...])
        zb_ref[...] = z[:tp].astype(zb_ref.dtype)
        sb_ref[j - na] = z[tp:]

    @pl.when((j >= na + nb) & (j < na + nb + nt))
    def _():
        z = z_of(jnp.concatenate([w_ref[N_IF:, :], wn_ref[...]], axis=0))
        zt_ref[...] = z[:tp].astype(zt_ref.dtype)
        st_ref[j - na - nb] = z[tp:]

    @pl.when(j == na + nb + nt)
    def _():
        z = z_of(jnp.concatenate([wif_ref[...], jnp.zeros((LANES - N_IF, wif_ref.shape[1]), F32)], axis=0))
        zif_ref[...] = z[:tp]
        sif_ref[...] = z[tp:]


def _in_proj(h, wt, qk_gain, *, tn, ms):
    ni, th, d = h.shape
    tm = th - ms
    m = ni * tm
    assert wt.shape == (IN_COLS, d) and N_IF == 8
    assert NA % tn == 0 and NB % tn == 0 and NT % tn == 0 and A_VA % tn == 0 and tn % N_IF == 0
    na, nb, nt = NA // tn, NB // tn, NT // tn
    nw = na + nb + nt
    g_if = (NA + NB) // N_IF
    g_tn = tn // N_IF
    kern = functools.partial(_in_proj_kernel, n_norm=A_VA // tn, na=na, nb=nb, nt=nt)
    col_a = lambda j: jnp.minimum(j, na - 1)
    col_b = lambda j: jnp.clip(j - na, 0, nb - 1)
    col_t = lambda j: jnp.clip(j - na - nb, 0, nt - 1)
    out_specs = [pl.BlockSpec((None, tm, tn), lambda i, j: (i, 0, col_a(j))),
                 pl.BlockSpec((None, tm, tn), lambda i, j: (i, 0, col_b(j))),
                 pl.BlockSpec((None, tm, tn), lambda i, j: (i, 0, col_t(j))),
                 pl.BlockSpec((None, tm, LANES), lambda i, j: (i, 0, 0))]
    out_shape = [jax.ShapeDtypeStruct((ni, tm, w), dt)
                 for w, dt in zip((NA, NB, NT, LANES), (F32, BF16, BF16, F32))]
    for n_tiles in (na, nb, nt):
        out_specs.append(pl.BlockSpec((None, n_tiles, ms, tn), lambda i, j: (i, 0, 0, 0)))
        out_shape.append(jax.ShapeDtypeStruct((ni, n_tiles, ms, tn), F32))
    out_specs.append(pl.BlockSpec((None, ms, LANES), lambda i, j: (i, 0, 0)))
    out_shape.append(jax.ShapeDtypeStruct((ni, ms, LANES), F32))
    outs = pl.pallas_call(
        kern,
        grid=(ni, nw + 1),
        in_specs=[
            pl.BlockSpec((None, th, d), lambda i, j: (i, 0, 0)),
            pl.BlockSpec((tn, d), lambda i, j: (jnp.minimum(j, nw - 1), 0)),
            pl.BlockSpec((N_IF, d), lambda i, j: (g_if + g_tn * (jnp.clip(j, na + nb, nw - 1) - (na + nb) + 1), 0)),
            pl.BlockSpec((N_IF, d), lambda i, j: (g_if, 0)),
            pl.BlockSpec((1, tn), lambda i, j: (0, col_a(j))),
        ],
        out_specs=out_specs,
        out_shape=out_shape,
        compiler_params=_cparams(("parallel", "arbitrary")),
        name="in_proj",
    )(h, wt, wt, wt, qk_gain)
    sample = [o[0].swapaxes(0, 1).reshape(ms, -1) for o in outs[4:7]] + [outs[7][0]]
    return [o.reshape(m, o.shape[2]) for o in outs[:4]], sample


def _attn_prompt_kernel(*refs, dil, nsub, hw, rb):
    q_refs, kp_refs, kc_refs, vp_refs, vc_refs = [refs[t * hw:(t + 1) * hw] for t in range(5)]
    o_ref, lse_ref = refs[5 * hw:5 * hw + 2]
    o_refs = refs[5 * hw + 2:]
    c = pl.program_id(1)
    blk = ATT_SPAN
    step = blk * dil
    qi = lax.broadcasted_iota(jnp.int32, (blk, 2 * blk), 0)
    kj = lax.broadcasted_iota(jnp.int32, (blk, 2 * blk), 1)
    band = (kj >= qi) & (kj <= qi + blk)
    first_bias = jnp.where((kj < blk) & (c == 0), NEG_INF, 0.0)
    lane = lax.broadcasted_iota(jnp.int32, (blk, LANES), 1)

    def rows(base, r):
        return pl.ds(base + r, blk) if dil == 1 else pl.ds(base + r, blk, stride=dil)

    def group(r0, s):
        base = s * step
        ids = [(r0 + r, n) for r in range(rb) for n in range(hw)]

        def stacked(cur_refs, prev_refs):
            parts = []
            for r, n in ids:
                cur = cur_refs[n][rows(base, r), :]
                if prev_refs is None:
                    parts.append(cur)
                else:
                    prv = prev_refs[n][rows(0, r), :] if s == 0 else cur_refs[n][rows(base - step, r), :]
                    parts.append(jnp.concatenate([prv, cur], axis=0))
            return jnp.stack(parts).astype(BF16)

        q = stacked(q_refs, None)
        k = stacked(kc_refs, kp_refs)
        v = stacked(vc_refs, vp_refs)
        sc = jnp.einsum('bqe,bke->bqk', q, k, preferred_element_type=F32) * ATT_SCALE
        if s == 0:
            sc = sc + first_bias
        sc = jnp.where(band, sc, NEG_INF)
        m = jnp.max(sc, axis=-1, keepdims=True)
        p = jnp.exp(sc - m)
        l = jnp.sum(p, axis=-1, keepdims=True)
        o = jnp.einsum('bqk,bke->bqe', p.astype(BF16), v, preferred_element_type=F32) / l
        lse = m + jnp.log(l)
        for r in range(rb):
            lse_tile = jnp.zeros((blk, LANES), F32)
            for n in range(hw):
                b = r * hw + n
                o_refs[n][rows(base, r0 + r), :] = o[b]
                lse_tile = jnp.where(lane == n, lse[b], lse_tile)
            lse_ref[rows(base, r0 + r), :] = lse_tile

    for s in range(nsub):
        if dil == rb:
            group(0, s)
        else:
            def body(it, carry, s=s):
                group(it * rb, s)
                return carry
            lax.fori_loop(0, dil // rb, body, 0)

    for n in range(hw):
        o_ref[:, n * ATT_HEAD_DIM:(n + 1) * ATT_HEAD_DIM] = o_refs[n][...]


def _attn_prompt_group(za, batch, seq, gi, dil, *, nsub, hw, rb):
    step = ATT_SPAN * dil
    tc = nsub * step
    assert seq % tc == 0 and ATT_HEADS % hw == 0 and dil % rb == 0
    nh = ATT_HEADS // hw
    e = ATT_HEAD_DIM
    z3 = za.reshape(batch, seq, NA)
    qc, kc, vc = [(col + gi * ATT_GROUP_W) // e for col in (A_QA, A_KA, A_VA)]

    def cur(col):
        return [pl.BlockSpec((None, tc, e), lambda b, c, h, n=n: (b, c, col + h * hw + n)) for n in range(hw)]

    def prev(col):
        return [pl.BlockSpec((None, step, e), lambda b, c, h, n=n: (b, jnp.maximum(c * nsub - 1, 0), col + h * hw + n))
                for n in range(hw)]

    kern = functools.partial(_attn_prompt_kernel, dil=dil, nsub=nsub, hw=hw, rb=rb)
    o, lse = pl.pallas_call(
        kern,
        grid=(batch, seq // tc, nh),
        in_specs=cur(qc) + prev(kc) + cur(kc) + prev(vc) + cur(vc),
        out_specs=[pl.BlockSpec((None, tc, hw * e), lambda b, c, h: (b, c, h)),
                   pl.BlockSpec((None, tc, LANES), lambda b, c, h: (b, c, h))],
        out_shape=[jax.ShapeDtypeStruct((batch, seq, ATT_GROUP_W), F32),
                   jax.ShapeDtypeStruct((batch, seq, nh * LANES), F32)],
        scratch_shapes=[pltpu.VMEM((tc, e), F32)] * hw,
        compiler_params=_cparams(("parallel", "parallel", "parallel")),
        name=f"attn_prompt_g{gi}",
    )(*([z3] * (5 * hw)))
    return o.reshape(batch * seq, ATT_GROUP_W), lse.reshape(batch * seq, nh * LANES)


def _attn_sample_kernel(z_ref, k1, v1, k2, v2, k3, v3, o1, o2, o3, l1, l2, l3):
    bb = z_ref.shape[0]
    bufs = ((k1, v1, o1, l1), (k2, v2, o2, l2), (k3, v3, o3, l3))
    lane = lax.broadcasted_iota(jnp.int32, (1, LANES), 1)

    def heads(b, col):
        return jnp.concatenate([z_ref[b, :, col + n * ATT_HEAD_DIM:col + (n + 1) * ATT_HEAD_DIM]
                                for n in range(ATT_HEADS)], axis=0)

    def body(b, carry):
        for gi, (k_ref, v_ref, o_ref, l_ref) in enumerate(bufs):
            q = heads(b, A_QA + gi * ATT_GROUP_W)
            k_new = heads(b, A_KA + gi * ATT_GROUP_W)
            v_new = heads(b, A_VA + gi * ATT_GROUP_W)
            kb = k_ref[b].reshape(ATT_SPAN // 2, 2 * ATT_HEADS, ATT_HEAD_DIM)
            vb = v_ref[b].reshape(ATT_SPAN // 2, 2 * ATT_HEADS, ATT_HEAD_DIM)
            q2 = jnp.concatenate([q, q], axis=0)
            s = jnp.sum(kb * q2[None], axis=-1, keepdims=True) * ATT_SCALE
            s_new = jnp.sum(k_new * q, axis=-1, keepdims=True) * ATT_SCALE
            m2 = jnp.max(s, axis=0)
            m = jnp.maximum(jnp.maximum(m2[:ATT_HEADS], m2[ATT_HEADS:]), s_new)
            p = jnp.exp(s - jnp.concatenate([m, m], axis=0)[None])
            p_new = jnp.exp(s_new - m)
            l2 = jnp.sum(p, axis=0)
            l = l2[:ATT_HEADS] + l2[ATT_HEADS:] + p_new
            o2 = jnp.sum(p * vb, axis=0)
            o = (o2[:ATT_HEADS] + o2[ATT_HEADS:] + p_new * v_new) / l
            lse = m + jnp.log(l)
            lse_row = jnp.zeros((1, LANES), F32)
            for n in range(ATT_HEADS):
                o_ref[b, :, n * ATT_HEAD_DIM:(n + 1) * ATT_HEAD_DIM] = o[n:n + 1, :]
                lse_row = jnp.where(lane == n, lse[n:n + 1, :], lse_row)
            l_ref[b] = lse_row
        return carry

    lax.fori_loop(0, bb, body, 0)


def _attn_sample(za_s, caches, *, bb):
    bd = za_s.shape[0]
    assert bd % bb == 0
    ins, in_specs = [za_s.reshape(bd, 1, NA)], [pl.BlockSpec((bb, 1, NA), lambda i: (i, 0, 0))]
    for (k_buf, v_buf), (win, dil) in zip(caches, DIL_PATTERNS):
        assert k_buf.shape[1:] == (ATT_SPAN * dil, ATT_HEADS, ATT_HEAD_DIM)
        for buf in (k_buf, v_buf):
            ins.append(buf.reshape(bd, ATT_SPAN, dil, ATT_HEADS, ATT_HEAD_DIM))
            in_specs.append(pl.BlockSpec((bb, ATT_SPAN, None, ATT_HEADS, ATT_HEAD_DIM),
                                         lambda i: (i, 0, 0, 0, 0)))
    outs = pl.pallas_call(
        _attn_sample_kernel,
        grid=(bd // bb,),
        in_specs=in_specs,
        out_specs=[pl.BlockSpec((bb, 1, ATT_GROUP_W), lambda i: (i, 0, 0))] * 3
                  + [pl.BlockSpec((bb, 1, LANES), lambda i: (i, 0, 0))] * 3,
        out_shape=[jax.ShapeDtypeStruct((bd, 1, ATT_GROUP_W), F32)] * 3
                  + [jax.ShapeDtypeStruct((bd, 1, LANES), F32)] * 3,
        compiler_params=_cparams(("parallel",)),
        name="attn_sample",
    )(*ins)
    return [o[:, 0, :] for o in outs[:3]], [l[:, 0, :] for l in outs[3:]]


def _mlstm_prompt_kernel(q_ref, k_ref, v_ref, om_ref, g_ref, bias_ref, *rest):
    n_cast = (len(rest) - 7) // 2
    h_ref, c_out, n_out, m_out = rest[n_cast:n_cast + 4]
    c_s, n_s, m_s = rest[2 * n_cast + 4:]
    for w_ref, wb_ref in zip(rest[:n_cast], rest[n_cast + 4:2 * n_cast + 4]):
        wb_ref[...] = w_ref[...].astype(wb_ref.dtype)
    hh = c_s.shape[0]
    ci = pl.program_id(2)
    L = q_ref.shape[0]

    @pl.when(ci == 0)
    def _():
        c_s[...] = jnp.zeros_like(c_s)
        n_s[...] = jnp.zeros_like(n_s)
        m_s[...] = jnp.zeros_like(m_s)

    gates = g_ref[...] + bias_ref[...]
    gates_t = gates.T
    lane = lax.broadcasted_iota(jnp.int32, (L, LANES), 1)
    sub = lax.broadcasted_iota(jnp.int32, (LANES, L), 0)
    ti = lax.broadcasted_iota(jnp.int32, (L, L), 0)
    si = lax.broadcasted_iota(jnp.int32, (L, L), 1)
    causal = si <= ti

    for j in range(hh):
        hd = pl.program_id(1) * hh + j
        qs = slice(j * M_QK_DIM, (j + 1) * M_QK_DIM)
        vs = slice(j * M_V_DIM, (j + 1) * M_V_DIM)
        li_col = jnp.sum(jnp.where(lane == hd, gates, 0.0), axis=1, keepdims=True)
        lf_col = _log_sigmoid(jnp.sum(jnp.where(lane == hd + M_HEADS, gates, 0.0), axis=1, keepdims=True))
        li_row = jnp.sum(jnp.where(sub == hd, gates_t, 0.0), axis=0, keepdims=True)
        lf_row = _log_sigmoid(jnp.sum(jnp.where(sub == hd + M_HEADS, gates_t, 0.0), axis=0, keepdims=True))

        b_col = jnp.sum(jnp.where(causal, lf_row, 0.0), axis=1, keepdims=True)
        b_row = jnp.sum(jnp.where(ti <= si, lf_col, 0.0), axis=0, keepdims=True)
        b_end = jnp.sum(lf_row, axis=1, keepdims=True)

        m_prev = m_s[j]
        dmat = jnp.where(causal, b_col - b_row + li_row, NEG_INF)
        inter = b_col + m_prev
        mt = jnp.maximum(inter, jnp.max(dmat, axis=1, keepdims=True))

        qb = q_ref[:, qs]
        q = qb.astype(F32)
        k = k_ref[:, qs].astype(F32) * M_K_SCALE
        vb = v_ref[:, vs]
        qk = lax.dot_general(qb, k.astype(BF16), (((1,), (1,)), ((), ())), preferred_element_type=F32)
        a = jnp.exp(dmat - mt) * qk
        w_inter = jnp.exp(inter - mt)
        num = (jnp.dot(a.astype(BF16), vb, preferred_element_type=F32)
               + w_inter * jnp.dot(qb, c_s[j].astype(BF16), preferred_element_type=F32))
        den = jnp.sum(a, axis=1, keepdims=True) + w_inter * jnp.sum(q * n_s[j], axis=1, keepdims=True)
        h = num / jnp.maximum(jnp.abs(den), jnp.exp(-mt))
        h_ref[:, vs] = (jax.nn.sigmoid(om_ref[:, vs].astype(F32)) * h).astype(h_ref.dtype)

        g_col = b_end - b_col + li_col
        g_row = b_end - b_row + li_row
        m_new = jnp.maximum(b_end + m_prev, jnp.max(g_row, axis=1, keepdims=True))
        decay = jnp.exp(b_end + m_prev - m_new)
        kw = jnp.exp(g_col - m_new) * k
        c_s[j] = decay * c_s[j] + jnp.dot(kw.T.astype(BF16), vb, preferred_element_type=F32)
        n_s[j] = decay * n_s[j] + jnp.sum(kw, axis=0, keepdims=True)
        m_s[j] = m_new

    @pl.when(ci == pl.num_programs(2) - 1)
    def _():
        c_out[...] = c_s[...]
        n_out[...] = n_s[...]
        m_out[...] = jnp.broadcast_to(m_s[...], m_out.shape)


def _mlstm_prompt(zif, zb, gate_bias, batch, seq, to_cast=(), *, chunk, hh):
    assert seq % chunk == 0 and M_HEADS % hh == 0
    nc = seq // chunk
    nhg = M_HEADS // hh
    n_steps = batch * nhg * nc
    zif3 = zif.reshape(batch, seq, LANES)
    zb3 = zb.reshape(batch, seq, NB)
    wq, wv = hh * M_QK_DIM, hh * M_V_DIM
    qk_blk = lambda col: pl.BlockSpec((None, chunk, wq), lambda b, h, c: (b, c, col // wq + h))
    v_blk = lambda col: pl.BlockSpec((None, chunk, wv), lambda b, h, c: (b, c, col // wv + h))
    cast_specs, cast_shapes = [], []
    for w in to_cast:
        rows, cols = w.shape
        assert rows % (n_steps * BF16_SUBLANES) == 0
        cast_specs.append(pl.BlockSpec((rows // n_steps, cols), lambda b, h, c: ((b * nhg + h) * nc + c, 0)))
        cast_shapes.append(jax.ShapeDtypeStruct((rows, cols), BF16))
    hm, c1, n1, m1, *cast = pl.pallas_call(
        _mlstm_prompt_kernel,
        grid=(batch, nhg, nc),
        in_specs=[qk_blk(B_QM), qk_blk(B_KM), v_blk(B_VM), v_blk(B_OM),
                  pl.BlockSpec((None, chunk, LANES), lambda b, h, c: (b, c, 0)),
                  pl.BlockSpec((1, LANES), lambda b, h, c: (0, 0))] + cast_specs,
        out_specs=[pl.BlockSpec((None, chunk, wv), lambda b, h, c: (b, c, h)),
                   pl.BlockSpec((None, hh, M_QK_DIM, M_V_DIM), lambda b, h, c: (b, h, 0, 0)),
                   pl.BlockSpec((None, hh, 1, M_QK_DIM), lambda b, h, c: (b, h, 0, 0)),
                   pl.BlockSpec((None, hh, 1, LANES), lambda b, h, c: (b, h, 0, 0))] + cast_specs,
        out_shape=[jax.ShapeDtypeStruct((batch, seq, M_V_WIDTH), BF16),
                   jax.ShapeDtypeStruct((batch, M_HEADS, M_QK_DIM, M_V_DIM), F32),
                   jax.ShapeDtypeStruct((batch, M_HEADS, 1, M_QK_DIM), F32),
                   jax.ShapeDtypeStruct((batch, M_HEADS, 1, LANES), F32)] + cast_shapes,
        scratch_shapes=[pltpu.VMEM((hh, M_QK_DIM, M_V_DIM), F32), pltpu.VMEM((hh, 1, M_QK_DIM), F32),
                        pltpu.VMEM((hh, 1, 1), F32)],
        compiler_params=_cparams(("parallel", "parallel", "arbitrary")),
        name="mlstm_prompt",
    )(zb3, zb3, zb3, zb3, zif3, gate_bias, *to_cast)
    return hm.reshape(batch * seq, M_V_WIDTH), c1, n1[:, :, 0, :], m1[:, :, 0, 0], cast


def _rows_to_cols(rows):
    rep = LANES // len(rows)
    n = rows[0].shape[1]
    t = jnp.concatenate([jnp.broadcast_to(r, (rep, n)) for r in rows], axis=0).T
    return [t[:, i * rep:i * rep + 1] for i in range(len(rows))]


def _mlstm_sample_kernel(zif_ref, zb_ref, bias_ref, c_ref, n_ref, m_ref, h_ref, c_out, n_out, m_out):
    for b in range(zif_ref.shape[0]):
        _mlstm_sample_one(zif_ref.at[b], zb_ref.at[b], bias_ref, c_ref.at[b], n_ref.at[b], m_ref.at[b],
                          h_ref.at[b], c_out.at[b], n_out.at[b], m_out.at[b])


def _mlstm_sample_one(zif_ref, zb_ref, bias_ref, c_ref, n_ref, m_ref, h_ref, c_out, n_out, m_out):
    gates = zif_ref[...] + bias_ref[...]
    lane = lax.broadcasted_iota(jnp.int32, (1, LANES), 1)
    m_row = jnp.zeros((1, LANES), F32)
    qs, wks, inters, m_news = [], [], [], []
    for h in range(M_HEADS):
        k = zb_ref[:, B_KM + h * M_QK_DIM:B_KM + (h + 1) * M_QK_DIM].astype(F32) * M_K_SCALE
        li = gates[:, h:h + 1]
        lf = _log_sigmoid(gates[:, M_HEADS + h:M_HEADS + h + 1])
        inter = lf + m_ref[:, h:h + 1]
        m_new = jnp.maximum(inter, li)
        qs.append(zb_ref[:, B_QM + h * M_QK_DIM:B_QM + (h + 1) * M_QK_DIM].astype(F32))
        wks.append(jnp.exp(li - m_new) * k)
        inters.append(inter)
        m_news.append(m_new)
    cols = _rows_to_cols(qs + wks)
    for h in range(M_HEADS):
        q, wk, inter, m_new = qs[h], wks[h], inters[h], m_news[h]
        v = zb_ref[:, B_VM + h * M_V_DIM:B_VM + (h + 1) * M_V_DIM].astype(F32)
        om = zb_ref[:, B_OM + h * M_V_DIM:B_OM + (h + 1) * M_V_DIM].astype(F32)
        c0 = c_ref[h]
        n0 = n_ref[h:h + 1, :]
        a = jnp.sum(q * wk, axis=1, keepdims=True)
        w_inter = jnp.exp(inter - m_new)
        q_c = jnp.sum(cols[h] * c0, axis=0, keepdims=True)
        num = a * v + w_inter * q_c
        den = a + w_inter * jnp.sum(q * n0, axis=1, keepdims=True)
        hv = num / jnp.maximum(jnp.abs(den), jnp.exp(-m_new))
        h_ref[:, h * M_V_DIM:(h + 1) * M_V_DIM] = (jax.nn.sigmoid(om) * hv).astype(h_ref.dtype)
        c_out[h] = w_inter * c0 + cols[M_HEADS + h] * v
        n_out[h:h + 1, :] = w_inter * n0 + wk
        m_row = jnp.where(lane == h, m_new, m_row)
    m_out[...] = m_row


def _mlstm_sample(zif_s, zb_s, gate_bias, c0, n0, m0, *, bb):
    bd = zif_s.shape[0]
    assert bd % bb == 0
    hm, c1, n1, m1 = pl.pallas_call(
        _mlstm_sample_kernel,
        grid=(bd // bb,),
        in_specs=[pl.BlockSpec((bb, 1, LANES), lambda b: (b, 0, 0)),
                  pl.BlockSpec((bb, 1, NB), lambda b: (b, 0, 0)),
                  pl.BlockSpec((1, LANES), lambda b: (0, 0)),
                  pl.BlockSpec((bb, M_HEADS, M_QK_DIM, M_V_DIM), lambda b: (b, 0, 0, 0)),
                  pl.BlockSpec((bb, M_HEADS, M_QK_DIM), lambda b: (b, 0, 0)),
                  pl.BlockSpec((bb, 1, M_HEADS), lambda b: (b, 0, 0))],
        out_specs=[pl.BlockSpec((bb, 1, M_V_WIDTH), lambda b: (b, 0, 0)),
                   pl.BlockSpec((bb, M_HEADS, M_QK_DIM, M_V_DIM), lambda b: (b, 0, 0, 0)),
                   pl.BlockSpec((bb, M_HEADS, M_QK_DIM), lambda b: (b, 0, 0)),
                   pl.BlockSpec((bb, 1, LANES), lambda b: (b, 0, 0))],
        out_shape=[jax.ShapeDtypeStruct((bd, 1, M_V_WIDTH), BF16),
                   jax.ShapeDtypeStruct((bd, M_HEADS, M_QK_DIM, M_V_DIM), F32),
                   jax.ShapeDtypeStruct((bd, M_HEADS, M_QK_DIM), F32),
                   jax.ShapeDtypeStruct((bd, 1, LANES), F32)],
        compiler_params=_cparams(("parallel",)),
        name="mlstm_sample",
    )(zif_s.reshape(bd, 1, LANES), zb_s.reshape(bd, 1, NB), gate_bias, c0, n0, m0.reshape(bd, 1, M_HEADS))
    return hm.reshape(bd, M_V_WIDTH), c1, n1, m1[:, 0, :M_HEADS]


def _merge_kernel(o1, o2, o3, l1, l2, l3, hm_ref, ga_ref, gb_ref, x_ref, wpa_ref, wpm_ref, wo_ref, g2_ref,
                  *rest, hws):
    n_cast = (len(rest) - 2) // 2
    x1_ref, h2_ref = rest[n_cast:n_cast + 2]
    for w_ref, wb_ref in zip(rest[:n_cast], rest[n_cast + 2:]):
        wb_ref[...] = w_ref[...].astype(wb_ref.dtype)
    parts = []
    for n in range(ATT_HEADS):
        sl = slice(n * ATT_HEAD_DIM, (n + 1) * ATT_HEAD_DIM)
        ls = []
        for l_ref, hw in zip((l1, l2, l3), hws):
            ln = (n // hw) * LANES + n % hw
            ls.append(l_ref[:, ln:ln + 1])
        mx = jnp.maximum(jnp.maximum(ls[0], ls[1]), ls[2])
        es = [jnp.exp(l - mx) for l in ls]
        tot = es[0] + es[1] + es[2]
        parts.append((es[0] / tot) * o1[:, sl] + (es[1] / tot) * o2[:, sl] + (es[2] / tot) * o3[:, sl])
    att = jnp.concatenate(parts, axis=1).astype(BF16)
    pa = jnp.dot(att, wpa_ref[...], preferred_element_type=F32)
    pm = jnp.dot(hm_ref[...], wpm_ref[...], preferred_element_type=F32)
    merged = (jax.nn.sigmoid(ga_ref[...].astype(F32)) * pa + jax.nn.sigmoid(gb_ref[...].astype(F32)) * pm)
    x1 = x_ref[...] + jnp.dot(merged.astype(BF16), wo_ref[...], preferred_element_type=F32)
    x1_ref[...] = x1
    ms = jnp.mean(x1 * x1, axis=-1, keepdims=True)
    h2_ref[...] = (x1 * lax.rsqrt(ms + RMS_EPS) * g2_ref[...]).astype(BF16)


def _merge(o_list, lse_list, hm, zt, x, wpa, wpm, wo, g2, to_cast=(), *, tm):
    m, d = x.shape
    assert m % tm == 0 and zt.shape[1] == NT
    n_steps = m // tm
    row_blk = lambda w: pl.BlockSpec((tm, w), lambda i: (i, 0))
    const = lambda shape: pl.BlockSpec(shape, lambda i: (0, 0), pipeline_mode=pl.Buffered(1))
    hws = tuple(ATT_HEADS * LANES // l.shape[1] for l in lse_list)
    cast_specs, cast_shapes = [], []
    for w in to_cast:
        rows, cols = w.shape
        assert rows % (n_steps * BF16_SUBLANES) == 0
        cast_specs.append(pl.BlockSpec((rows // n_steps, cols), lambda i: (i, 0)))
        cast_shapes.append(jax.ShapeDtypeStruct((rows, cols), BF16))
    x1, h2, *cast = pl.pallas_call(
        functools.partial(_merge_kernel, hws=hws),
        grid=(n_steps,),
        in_specs=[row_blk(ATT_GROUP_W)] * 3 + [row_blk(l.shape[1]) for l in lse_list]
                 + [row_blk(M_V_WIDTH),
                    pl.BlockSpec((tm, d), lambda i: (i, T_GA // d)),
                    pl.BlockSpec((tm, d), lambda i: (i, T_GB // d)),
                    row_blk(d),
                    const(wpa.shape), const(wpm.shape), const(wo.shape), const((1, d))] + cast_specs,
        out_specs=[row_blk(d), row_blk(d)] + cast_specs,
        out_shape=[jax.ShapeDtypeStruct((m, d), F32), jax.ShapeDtypeStruct((m, d), BF16)] + cast_shapes,
        compiler_params=_cparams(("parallel",)),
        name="merge",
    )(*o_list, *lse_list, hm, zt, zt, x, wpa, wpm, wo, g2, *to_cast)
    return x1, h2, cast


def _ffn_body(h2_ref, wu_ref, wg_ref, cw_ref, cb_ref, wd_ref, x1_ref, y_ref, g_prev2, g_prev1):
    j = pl.program_id(1)
    h2 = h2_ref[...]
    u = jnp.dot(h2, wu_ref[...], preferred_element_type=F32)
    g = jnp.dot(h2, wg_ref[...], preferred_element_type=F32)
    gconv = cb_ref[...] + ((g_prev2(g) * cw_ref[0:1, :] + g_prev1(g) * cw_ref[1:2, :]) + g * cw_ref[2:3, :])
    act = (jax.nn.gelu(gconv) * u).astype(BF16)
    down = jnp.dot(act, wd_ref[...], preferred_element_type=F32)

    @pl.when(j == 0)
    def _():
        y_ref[...] = x1_ref[...] + down

    @pl.when(j > 0)
    def _():
        y_ref[...] += down

    return g


def _ffn_prompt_kernel(h2_ref, wu_ref, wg_ref, cw_ref, cb_ref, wd_ref, x1_ref, y_ref, tail_ref, prev_s,
                       u0_s, g0_s, u1_s, g1_s, *, tiles_per_seq, nf):
    i = pl.program_id(0)
    j = pl.program_id(1)
    tm = h2_ref.shape[0]
    slots = ((u0_s, g0_s), (u1_s, g1_s))

    def up(u_s, g_s):
        h2 = h2_ref[...]
        u_s[...] = jnp.dot(h2, wu_ref[...], preferred_element_type=F32)
        g_s[...] = jnp.dot(h2, wg_ref[...], preferred_element_type=F32)

    def down(u_s, g_s):
        jt = j - 1
        prev = jnp.where(i % tiles_per_seq == 0, 0.0, prev_s[jt])
        p2, p1 = prev[SUBLANES - 2:SUBLANES - 1, :], prev[SUBLANES - 1:SUBLANES, :]
        g = g_s[...]
        r = lax.broadcasted_iota(jnp.int32, g.shape, 0)
        g_m1 = jnp.where(r == 0, p1, pltpu.roll(g, 1, axis=0))
        g_m2 = jnp.where(r == 0, p2, jnp.where(r == 1, p1, pltpu.roll(g, 2, axis=0)))
        cw = cw_ref[jt]
        gconv = cb_ref[jt] + ((g_m2 * cw[0:1, :] + g_m1 * cw[1:2, :]) + g * cw[2:3, :])
        act = (jax.nn.gelu(gconv) * u_s[...]).astype(BF16)
        y_ref[...] += jnp.dot(act, wd_ref[...], preferred_element_type=F32)
        prev_s[jt] = g[tm - SUBLANES:tm, :]

    @pl.when(j == 0)
    def _():
        y_ref[...] = x1_ref[...]
        up(*slots[0])

    for par in range(2):
        @pl.when((j > 0) & (j < nf) & (j % 2 == par))
        def _(par=par):
            up(*slots[par])
            down(*slots[1 - par])

    @pl.when(j == nf)
    def _():
        down(*slots[(nf - 1) % 2])
        tail_ref[...] = prev_s[...]


def _ffn_sample_kernel(h2_ref, wu_ref, wg_ref, cw_ref, cb_ref, wd_ref, x1_ref, b2_ref, b1_ref, y_ref, g_ref):
    g = _ffn_body(h2_ref, wu_ref, wg_ref, cw_ref, cb_ref, wd_ref, x1_ref, y_ref,
                  lambda g: b2_ref[...], lambda g: b1_ref[...])
    g_ref[...] = g


def _ffn_specs(tm, tf, d, nf):
    return [pl.BlockSpec((tm, d), lambda i, j: (i, 0)),
            pl.BlockSpec((d, tf), lambda i, j: (0, j)),
            pl.BlockSpec((d, tf), lambda i, j: (0, nf + j)),
            pl.BlockSpec((CONV_W, tf), lambda i, j: (0, j)),
            pl.BlockSpec((1, tf), lambda i, j: (0, j)),
            pl.BlockSpec((tf, d), lambda i, j: (j, 0)),
            pl.BlockSpec((tm, d), lambda i, j: (i, 0))]


def _ffn_prompt(h2, x1, w_in, conv_w, conv_b, w_down, seq, *, tm, tf):
    m, d = x1.shape
    assert m % tm == 0 and seq % tm == 0 and D_FF % tf == 0 and tm % 8 == 0
    nf = D_FF // tf
    kern = functools.partial(_ffn_prompt_kernel, tiles_per_seq=seq // tm, nf=nf)
    up_t = lambda j: jnp.minimum(j, nf - 1)
    dn_t = lambda j: jnp.maximum(j - 1, 0)
    y, tails = pl.pallas_call(
        kern,
        grid=(m // tm, nf + 1),
        in_specs=[pl.BlockSpec((tm, d), lambda i, j: (i, 0)),
                  pl.BlockSpec((d, tf), lambda i, j: (0, up_t(j))),
                  pl.BlockSpec((d, tf), lambda i, j: (0, nf + up_t(j))),
                  pl.BlockSpec((nf, CONV_W, tf), lambda i, j: (0, 0, 0)),
                  pl.BlockSpec((nf, 1, tf), lambda i, j: (0, 0, 0)),
                  pl.BlockSpec((tf, d), lambda i, j: (dn_t(j), 0)),
                  pl.BlockSpec((tm, d), lambda i, j: (i, 0))],
        out_specs=[pl.BlockSpec((tm, d), lambda i, j: (i, 0)),
                   pl.BlockSpec((None, nf, SUBLANES, tf), lambda i, j: (i, 0, 0, 0))],
        out_shape=[jax.ShapeDtypeStruct((m, d), F32), jax.ShapeDtypeStruct((m // tm, nf, SUBLANES, tf), F32)],
        scratch_shapes=[pltpu.VMEM((nf, SUBLANES, tf), F32)] + [pltpu.VMEM((tm, tf), F32)] * 4,
        compiler_params=_cparams(("arbitrary", "arbitrary")),
        name="ffn_prompt",
    )(h2, w_in, w_in, conv_w.reshape(CONV_W, nf, tf).swapaxes(0, 1), conv_b.reshape(nf, 1, tf), w_down, x1)
    return y, tails.swapaxes(1, 2).reshape(m // tm, SUBLANES, D_FF)


def _ffn_sample(h2, x1, w_in, conv_w, conv_b, w_down, conv_buf, *, tf):
    m, d = x1.shape
    nf = D_FF // tf
    buf2d = conv_buf.reshape(m, (CONV_W - 1) * D_FF)
    return pl.pallas_call(
        _ffn_sample_kernel,
        grid=(1, nf),
        in_specs=_ffn_specs(m, tf, d, nf) + [pl.BlockSpec((m, tf), lambda i, j: (0, j)),
                                             pl.BlockSpec((m, tf), lambda i, j: (0, nf + j))],
        out_specs=[pl.BlockSpec((m, d), lambda i, j: (i, 0)), pl.BlockSpec((m, tf), lambda i, j: (0, j))],
        out_shape=[jax.ShapeDtypeStruct((m, d), F32), jax.ShapeDtypeStruct((m, D_FF), F32)],
        compiler_params=_cparams(("arbitrary", "arbitrary")),
        name="ffn_sample",
    )(h2, w_in, w_in, conv_w, conv_b, w_down, x1, buf2d, buf2d)


IN_PROJ_TM = 2048
IN_PROJ_TN = 512
RMSNORM_TS = 1024
MLSTM_CHUNK = 512
MLSTM_HH = 4
MERGE_TM = 256
ATTN_SAMPLE_BB = 8
MLSTM_SAMPLE_BB = 4
FFN_TM = 512
FFN_TF = 512
FFN_SAMPLE_TF = D_FF // 4


def _split_heads_kernel(*refs):
    n = len(refs) // 2
    for src, dst in zip(refs[:n], refs[n:]):
        for h in range(ATT_HEADS):
            dst[:, h, :] = src[:, h * ATT_HEAD_DIM:(h + 1) * ATT_HEAD_DIM]


def _split_heads(za3, col_blocks, row0, rows, *, tr):
    batch = za3.shape[0]
    assert rows % tr == 0 and row0 % tr == 0
    out_spec = pl.BlockSpec((None, tr, ATT_HEADS, ATT_HEAD_DIM), lambda b, r: (b, r, 0, 0))
    return pl.pallas_call(
        _split_heads_kernel,
        grid=(batch, rows // tr),
        in_specs=[pl.BlockSpec((None, tr, ATT_GROUP_W), lambda b, r, c=c: (b, row0 // tr + r, c)) for c in col_blocks],
        out_specs=[out_spec] * len(col_blocks),
        out_shape=[jax.ShapeDtypeStruct((batch, rows, ATT_HEADS, ATT_HEAD_DIM), F32)] * len(col_blocks),
        compiler_params=_cparams(("parallel", "parallel")),
        name="split_heads",
    )(*([za3] * len(col_blocks)))


def _qk_gain_row(q_norm, k_norm):
    reps = N_GROUPS * ATT_HEADS
    return jnp.concatenate([jnp.tile(q_norm, reps), jnp.tile(k_norm, reps), jnp.zeros((NA - A_VA,), F32)])[None, :]


ATTN_PROMPT_TILING = ((4, 4, 1), (1, 4, 2), (1, 2, 4))


def _layer(x_prompt, x_sample, caches, norm_mix, w_in, q_norm, k_norm, b_igate, b_fgate, w_proj_att,
           w_proj_mlstm, w_out, norm_ffn, w_ffn_in, conv_w, conv_b, w_ffn_down):
    batch, seq, d = x_prompt.shape
    bd = x_sample.shape[0]
    assert x_sample.shape[1] == 1 and d == D_MODEL
    (ck1, cv1, ck2, cv2, ck3, cv3, st_c, st_n, st_m, st_conv) = caches

    wt_in = jnp.swapaxes(w_in, 0, 1)
    qk_gain = _qk_gain_row(q_norm, k_norm)
    g1 = norm_mix[None, :]
    g2 = norm_ffn[None, :]
    gate_bias = jnp.concatenate([b_igate, b_fgate, jnp.zeros((LANES - 2 * M_HEADS,), F32)])[None, :]
    cb = conv_b[None, :]

    xp = x_prompt.reshape(batch * seq, d)
    xs = x_sample.reshape(bd, d)

    h_all = _rmsnorm(xp, xs, g1, tm=IN_PROJ_TM, ts=RMSNORM_TS)
    (za_p, zb_p, zt_p, zif_p), (za_s, zb_s, zt_s, zif_s) = _in_proj(h_all, wt_in, qk_gain, tn=IN_PROJ_TN, ms=bd)
    o_p, lse_p = zip(*[_attn_prompt_group(za_p, batch, seq, gi, dil, nsub=nsub, hw=hw, rb=rb)
                       for gi, ((_, dil), (nsub, hw, rb)) in enumerate(zip(DIL_PATTERNS, ATTN_PROMPT_TILING))])
    hm_p, p_c, p_n, p_m, (wpa, wpm, wo) = _mlstm_prompt(zif_p, zb_p, gate_bias, batch, seq,
                                                        (w_proj_att, w_proj_mlstm, w_out), chunk=MLSTM_CHUNK,
                                                        hh=MLSTM_HH)
    x1_p, h2_p, (w_ff, w_dn) = _merge(o_p, lse_p, hm_p, zt_p, xp, wpa, wpm, wo, g2, (w_ffn_in, w_ffn_down),
                                      tm=MERGE_TM)
    y_p, tails = _ffn_prompt(h2_p, x1_p, w_ff, conv_w, cb, w_dn, seq, tm=FFN_TM, tf=FFN_TF)

    za_p3 = za_p.reshape(batch, seq, NA)
    p_kv = []
    for gi, (win, _) in enumerate(DIL_PATTERNS):
        keep = min(win, seq)
        cols = [(col + gi * ATT_GROUP_W) // ATT_GROUP_W for col in (A_KA, A_VA)]
        p_kv += _split_heads(za_p3, cols, seq - keep, keep, tr=min(keep, 512))
    tiles_per_seq = seq // FFN_TM
    p_conv = tails.reshape(batch, tiles_per_seq, SUBLANES, D_FF)[:, -1, SUBLANES - (CONV_W - 1):, :]

    o_s, lse_s = _attn_sample(za_s, [(ck1, cv1), (ck2, cv2), (ck3, cv3)], bb=ATTN_SAMPLE_BB)
    hm_s, s_c, s_n, s_m = _mlstm_sample(zif_s, zb_s, gate_bias, st_c, st_n, st_m, bb=MLSTM_SAMPLE_BB)
    x1_s, h2_s, _ = _merge(o_s, lse_s, hm_s, zt_s, xs, wpa, wpm, wo, g2, tm=bd)
    y_s, g_s = _ffn_sample(h2_s, x1_s, w_ff, conv_w, cb, w_dn, st_conv, tf=FFN_SAMPLE_TF)

    cols = [(col + gi * ATT_GROUP_W) // ATT_GROUP_W for gi in range(N_GROUPS) for col in (A_KA, A_VA)]
    s_kv = [a.reshape(bd, 1, ATT_HEADS, ATT_HEAD_DIM) for a in _split_heads(za_s[None], cols, 0, bd, tr=bd)]
    s_conv = jnp.stack([st_conv[:, 1, :], g_s], axis=1)

    p_state = p_kv + [p_c, p_n, p_m, p_conv]
    s_state = s_kv + [s_c, s_n, s_m, s_conv]
    return y_p.reshape(batch, seq, d), y_s.reshape(bd, 1, d), p_state, s_state


def kernel(x_prompt, x_sample, cache_k_w128, cache_v_w128, cache_k_w512, cache_v_w512, cache_k_w2048,
           cache_v_w2048, state_mlstm_C, state_mlstm_n, state_mlstm_m, state_ffn_conv, norm_mix, w_in, q_norm,
           k_norm, b_igate, b_fgate, w_proj_att, w_proj_mlstm, w_out, norm_ffn, w_ffn_in, conv_w, conv_b,
           w_ffn_down):
    assert norm_mix.shape[0] == 1
    caches = [c[0] for c in (cache_k_w128, cache_v_w128, cache_k_w512, cache_v_w512, cache_k_w2048,
                             cache_v_w2048, state_mlstm_C, state_mlstm_n, state_mlstm_m, state_ffn_conv)]
    weights = [w[0] for w in (norm_mix, w_in, q_norm, k_norm, b_igate, b_fgate, w_proj_att, w_proj_mlstm,
                              w_out, norm_ffn, w_ffn_in, conv_w, conv_b, w_ffn_down)]
    y_p, y_s, p_state, s_state = _layer(x_prompt, x_sample, caches, *weights)
    return (y_p, y_s, *[a[None] for a in p_state], *[a[None] for a in s_state])
```

```python
import functools

import jax
import jax.numpy as jnp
from jax import lax
from jax.experimental import pallas as pl
from jax.experimental.pallas import tpu as pltpu

F32 = jnp.float32
BF16 = jnp.bfloat16

RMS_EPS = 1e-6
NEG_INF = -1e30
LANES = 128
SUBLANES = 8
BF16_SUBLANES = 16
VMEM_LIMIT = 56 * 1024 * 1024

D_MODEL = 2048
DIL_PATTERNS = ((128, 1), (512, 4), (2048, 16))
N_GROUPS = 3
ATT_HEADS = 4
ATT_HEAD_DIM = 128
ATT_SPAN = 128
ATT_SCALE = ATT_HEAD_DIM ** -0.5
ATT_GROUP_W = ATT_HEADS * ATT_HEAD_DIM
ATT_WIDTH = N_GROUPS * ATT_GROUP_W
M_HEADS = 4
M_QK_DIM = D_MODEL // (2 * M_HEADS)
M_V_DIM = D_MODEL // M_HEADS
M_QK_WIDTH = M_HEADS * M_QK_DIM
M_V_WIDTH = M_HEADS * M_V_DIM
M_K_SCALE = M_QK_DIM ** -0.5
D_FF = ((8 * D_MODEL // 3 + 255) // 256) * 256
CONV_W = 3
SPLIT_SIZES = (ATT_WIDTH, ATT_WIDTH, ATT_WIDTH, M_QK_WIDTH, M_QK_WIDTH, M_V_WIDTH, M_V_WIDTH,
               M_HEADS, M_HEADS, D_MODEL, D_MODEL)

IN_COLS = sum(SPLIT_SIZES)
A_QA = 0
A_KA = A_QA + ATT_WIDTH
A_VA = A_KA + ATT_WIDTH
NA = A_VA + ATT_WIDTH
B_QM = 0
B_KM = B_QM + M_QK_WIDTH
B_VM = B_KM + M_QK_WIDTH
B_OM = B_VM + M_V_WIDTH
NB = B_OM + M_V_WIDTH
N_IF = 2 * M_HEADS
T_GA = 0
T_GB = T_GA + D_MODEL
NT = T_GB + D_MODEL
assert NA + NB + N_IF + NT == IN_COLS


def _cparams(sem):
    return pltpu.CompilerParams(dimension_semantics=sem, vmem_limit_bytes=VMEM_LIMIT)


def _log_sigmoid(x):
    return jnp.minimum(x, 0.0) - jnp.log(1.0 + jnp.exp(-jnp.abs(x)))


def _rmsnorm_kernel(x_ref, xs_ref, g_ref, h_ref, *, nsub):
    j = pl.program_id(1)
    ts = x_ref.shape[0]

    def norm(x):
        ms = jnp.mean(x * x, axis=-1, keepdims=True)
        return (x * lax.rsqrt(ms + RMS_EPS) * g_ref[...]).astype(h_ref.dtype)

    @pl.when(j < nsub)
    def _():
        h_ref[pl.ds(pl.multiple_of(j * ts, ts), ts), :] = norm(x_ref[...])

    @pl.when(j == nsub)
    def _():
        h_ref[nsub * ts:, :] = norm(xs_ref[...])


def _rmsnorm(x, xs, gain, *, tm, ts):
    m, d = x.shape
    ms = xs.shape[0]
    assert m % tm == 0 and tm % ts == 0
    nsub = tm // ts
    last = m // ts - 1
    return pl.pallas_call(
        functools.partial(_rmsnorm_kernel, nsub=nsub),
        grid=(m // tm, nsub + 1),
        in_specs=[pl.BlockSpec((ts, d), lambda i, j: (jnp.minimum(i * nsub + j, last), 0)),
                  pl.BlockSpec((ms, d), lambda i, j: (0, 0)),
                  pl.BlockSpec((1, d), lambda i, j: (0, 0))],
        out_specs=pl.BlockSpec((None, tm + ms, d), lambda i, j: (i, 0, 0)),
        out_shape=jax.ShapeDtypeStruct((m // tm, tm + ms, d), BF16),
        compiler_params=_cparams(("parallel", "arbitrary")),
        name="rmsnorm",
    )(x, xs, gain)


def _in_proj_kernel(h_ref, w_ref, wn_ref, wif_ref, qkg_ref, za_ref, zb_ref, zt_ref, zif_ref,
                    sa_ref, sb_ref, st_ref, sif_ref, *, n_norm, na, nb, nt):
    j = pl.program_id(1)
    tp = za_ref.shape[0]

    def z_of(w):
        return lax.dot_general(h_ref[...], w.astype(BF16), (((1,), (1,)), ((), ())), preferred_element_type=F32)

    @pl.when(j < n_norm)
    def _():
        z = z_of(w_ref[...])
        for c in range(za_ref.shape[1] // LANES):
            sl = slice(c * LANES, (c + 1) * LANES)
            zc = z[:, sl]
            ms = jnp.mean(zc * zc, axis=-1, keepdims=True)
            zn = zc * lax.rsqrt(ms + RMS_EPS) * qkg_ref[:, sl]
            za_ref[:, sl] = zn[:tp]
            sa_ref[j, :, sl] = zn[tp:]

    @pl.when((j >= n_norm) & (j < na))
    def _():
        z = z_of(w_ref[...])
        za_ref[...] = z[:tp]
        sa_ref[j] = z[tp:]

    @pl.when((j >= na) & (j < na + nb))
    def _():
        z = z_of(w_ref[...])
        zb_ref[...] = z[:tp].astype(zb_ref.dtype)
        sb_ref[j - na] = z[tp:]

    @pl.when((j >= na + nb) & (j < na + nb + nt))
    def _():
        z = z_of(jnp.concatenate([w_ref[N_IF:, :], wn_ref[...]], axis=0))
        zt_ref[...] = z[:tp].astype(zt_ref.dtype)
        st_ref[j - na - nb] = z[tp:]

    @pl.when(j == na + nb + nt)
    def _():
        z = z_of(jnp.concatenate([wif_ref[...], jnp.zeros((LANES - N_IF, wif_ref.shape[1]), F32)], axis=0))
        zif_ref[...] = z[:tp]
        sif_ref[...] = z[tp:]


def _in_proj(h, wt, qk_gain, *, tn, ms):
    ni, th, d = h.shape
    tm = th - ms
    m = ni * tm
    assert wt.shape == (IN_COLS, d) and N_IF == 8
    assert NA % tn == 0 and NB % tn == 0 and NT % tn == 0 and A_VA % tn == 0 and tn % N_IF == 0
    na, nb, nt = NA // tn, NB // tn, NT // tn
    nw = na + nb + nt
    g_if = (NA + NB) // N_IF
    g_tn = tn // N_IF
    kern = functools.partial(_in_proj_kernel, n_norm=A_VA // tn, na=na, nb=nb, nt=nt)
    col_a = lambda j: jnp.minimum(j, na - 1)
    col_b = lambda j: jnp.clip(j - na, 0, nb - 1)
    col_t = lambda j: jnp.clip(j - na - nb, 0, nt - 1)
    out_specs = [pl.BlockSpec((None, tm, tn), lambda i, j: (i, 0, col_a(j))),
                 pl.BlockSpec((None, tm, tn), lambda i, j: (i, 0, col_b(j))),
                 pl.BlockSpec((None, tm, tn), lambda i, j: (i, 0, col_t(j))),
                 pl.BlockSpec((None, tm, LANES), lambda i, j: (i, 0, 0))]
    out_shape = [jax.ShapeDtypeStruct((ni, tm, w), dt)
                 for w, dt in zip((NA, NB, NT, LANES), (F32, BF16, BF16, F32))]
    for n_tiles in (na, nb, nt):
        out_specs.append(pl.BlockSpec((None, n_tiles, ms, tn), lambda i, j: (i, 0, 0, 0)))
        out_shape.append(jax.ShapeDtypeStruct((ni, n_tiles, ms, tn), F32))
    out_specs.append(pl.BlockSpec((None, ms, LANES), lambda i, j: (i, 0, 0)))
    out_shape.append(jax.ShapeDtypeStruct((ni, ms, LANES), F32))
    outs = pl.pallas_call(
        kern,
        grid=(ni, nw + 1),
        in_specs=[
            pl.BlockSpec((None, th, d), lambda i, j: (i, 0, 0)),
            pl.BlockSpec((tn, d), lambda i, j: (jnp.minimum(j, nw - 1), 0)),
            pl.BlockSpec((N_IF, d), lambda i, j: (g_if + g_tn * (jnp.clip(j, na + nb, nw - 1) - (na + nb) + 1), 0)),
            pl.BlockSpec((N_IF, d), lambda i, j: (g_if, 0)),
            pl.BlockSpec((1, tn), lambda i, j: (0, col_a(j))),
        ],
        out_specs=out_specs,
        out_shape=out_shape,
        compiler_params=_cparams(("parallel", "arbitrary")),
        name="in_proj",
    )(h, wt, wt, wt, qk_gain)
    sample = [o[0].swapaxes(0, 1).reshape(ms, -1) for o in outs[4:7]] + [outs[7][0]]
    return [o.reshape(m, o.shape[2]) for o in outs[:4]], sample


def _attn_prompt_kernel(*refs, dil, nsub, hw, rb):
    q_refs, kp_refs, kc_refs, vp_refs, vc_refs = [refs[t * hw:(t + 1) * hw] for t in range(5)]
    o_ref, lse_ref = refs[5 * hw:5 * hw + 2]
    o_refs = refs[5 * hw + 2:]
    c = pl.program_id(1)
    blk = ATT_SPAN
    step = blk * dil
    qi = lax.broadcasted_iota(jnp.int32, (blk, 2 * blk), 0)
    kj = lax.broadcasted_iota(jnp.int32, (blk, 2 * blk), 1)
    band = (kj >= qi) & (kj <= qi + blk)
    first_bias = jnp.where((kj < blk) & (c == 0), NEG_INF, 0.0)
    lane = lax.broadcasted_iota(jnp.int32, (blk, LANES), 1)

    def rows(base, r):
        return pl.ds(base + r, blk) if dil == 1 else pl.ds(base + r, blk, stride=dil)

    def group(r0, s):
        base = s * step
        ids = [(r0 + r, n) for r in range(rb) for n in range(hw)]

        def stacked(cur_refs, prev_refs):
            parts = []
            for r, n in ids:
                cur = cur_refs[n][rows(base, r), :]
                if prev_refs is None:
                    parts.append(cur)
                else:
                    prv = prev_refs[n][rows(0, r), :] if s == 0 else cur_refs[n][rows(base - step, r), :]
                    parts.append(jnp.concatenate([prv, cur], axis=0))
            return jnp.stack(parts).astype(BF16)

        q = stacked(q_refs, None)
        k = stacked(kc_refs, kp_refs)
        v = stacked(vc_refs, vp_refs)
        sc = jnp.einsum('bqe,bke->bqk', q, k, preferred_element_type=F32) * ATT_SCALE
        if s == 0:
            sc = sc + first_bias
        sc = jnp.where(band, sc, NEG_INF)
        m = jnp.max(sc, axis=-1, keepdims=True)
        p = jnp.exp(sc - m)
        l = jnp.sum(p, axis=-1, keepdims=True)
        o = jnp.einsum('bqk,bke->bqe', p.astype(BF16), v, preferred_element_type=F32) / l
        lse = m + jnp.log(l)
        for r in range(rb):
            lse_tile = jnp.zeros((blk, LANES), F32)
            for n in range(hw):
                b = r * hw + n
                o_refs[n][rows(base, r0 + r), :] = o[b]
                lse_tile = jnp.where(lane == n, lse[b], lse_tile)
            lse_ref[rows(base, r0 + r), :] = lse_tile

    for s in range(nsub):
        if dil == rb:
            group(0, s)
        else:
            def body(it, carry, s=s):
                group(it * rb, s)
                return carry
            lax.fori_loop(0, dil // rb, body, 0)

    for n in range(hw):
        o_ref[:, n * ATT_HEAD_DIM:(n + 1) * ATT_HEAD_DIM] = o_refs[n][...]


def _attn_prompt_group(za, batch, seq, gi, dil, *, nsub, hw, rb):
    step = ATT_SPAN * dil
    tc = nsub * step
    assert seq % tc == 0 and ATT_HEADS % hw == 0 and dil % rb == 0
    nh = ATT_HEADS // hw
    e = ATT_HEAD_DIM
    z3 = za.reshape(batch, seq, NA)
    qc, kc, vc = [(col + gi * ATT_GROUP_W) // e for col in (A_QA, A_KA, A_VA)]

    def cur(col):
        return [pl.BlockSpec((None, tc, e), lambda b, c, h, n=n: (b, c, col + h * hw + n)) for n in range(hw)]

    def prev(col):
        return [pl.BlockSpec((None, step, e), lambda b, c, h, n=n: (b, jnp.maximum(c * nsub - 1, 0), col + h * hw + n))
                for n in range(hw)]

    kern = functools.partial(_attn_prompt_kernel, dil=dil, nsub=nsub, hw=hw, rb=rb)
    o, lse = pl.pallas_call(
        kern,
        grid=(batch, seq // tc, nh),
        in_specs=cur(qc) + prev(kc) + cur(kc) + prev(vc) + cur(vc),
        out_specs=[pl.BlockSpec((None, tc, hw * e), lambda b, c, h: (b, c, h)),
                   pl.BlockSpec((None, tc, LANES), lambda b, c, h: (b, c, h))],
        out_shape=[jax.ShapeDtypeStruct((batch, seq, ATT_GROUP_W), F32),
                   jax.ShapeDtypeStruct((batch, seq, nh * LANES), F32)],
        scratch_shapes=[pltpu.VMEM((tc, e), F32)] * hw,
        compiler_params=_cparams(("parallel", "parallel", "parallel")),
        name=f"attn_prompt_g{gi}",
    )(*([z3] * (5 * hw)))
    return o.reshape(batch * seq, ATT_GROUP_W), lse.reshape(batch * seq, nh * LANES)


def _attn_sample_kernel(z_ref, k1, v1, k2, v2, k3, v3, o1, o2, o3, l1, l2, l3):
    bb = z_ref.shape[0]
    bufs = ((k1, v1, o1, l1), (k2, v2, o2, l2), (k3, v3, o3, l3))
    lane = lax.broadcasted_iota(jnp.int32, (1, LANES), 1)

    def heads(b, col):
        return jnp.concatenate([z_ref[b, :, col + n * ATT_HEAD_DIM:col + (n + 1) * ATT_HEAD_DIM]
                                for n in range(ATT_HEADS)], axis=0)

    def body(b, carry):
        for gi, (k_ref, v_ref, o_ref, l_ref) in enumerate(bufs):
            q = heads(b, A_QA + gi * ATT_GROUP_W)
            k_new = heads(b, A_KA + gi * ATT_GROUP_W)
            v_new = heads(b, A_VA + gi * ATT_GROUP_W)
            kb = k_ref[b].reshape(ATT_SPAN // 2, 2 * ATT_HEADS, ATT_HEAD_DIM)
            vb = v_ref[b].reshape(ATT_SPAN // 2, 2 * ATT_HEADS, ATT_HEAD_DIM)
            q2 = jnp.concatenate([q, q], axis=0)
            s = jnp.sum(kb * q2[None], axis=-1, keepdims=True) * ATT_SCALE
            s_new = jnp.sum(k_new * q, axis=-1, keepdims=True) * ATT_SCALE
            m2 = jnp.max(s, axis=0)
            m = jnp.maximum(jnp.maximum(m2[:ATT_HEADS], m2[ATT_HEADS:]), s_new)
            p = jnp.exp(s - jnp.concatenate([m, m], axis=0)[None])
            p_new = jnp.exp(s_new - m)
            l2 = jnp.sum(p, axis=0)
            l = l2[:ATT_HEADS] + l2[ATT_HEADS:] + p_new
            o2 = jnp.sum(p * vb, axis=0)
            o = (o2[:ATT_HEADS] + o2[ATT_HEADS:] + p_new * v_new) / l
            lse = m + jnp.log(l)
            lse_row = jnp.zeros((1, LANES), F32)
            for n in range(ATT_HEADS):
                o_ref[b, :, n * ATT_HEAD_DIM:(n + 1) * ATT_HEAD_DIM] = o[n:n + 1, :]
                lse_row = jnp.where(lane == n, lse[n:n + 1, :], lse_row)
            l_ref[b] = lse_row
        return carry

    lax.fori_loop(0, bb, body, 0)


def _attn_sample(za_s, caches, *, bb):
    bd = za_s.shape[0]
    assert bd % bb == 0
    ins, in_specs = [za_s.reshape(bd, 1, NA)], [pl.BlockSpec((bb, 1, NA), lambda i: (i, 0, 0))]
    for (k_buf, v_buf), (win, dil) in zip(caches, DIL_PATTERNS):
        assert k_buf.shape[1:] == (ATT_SPAN * dil, ATT_HEADS, ATT_HEAD_DIM)
        for buf in (k_buf, v_buf):
            ins.append(buf.reshape(bd, ATT_SPAN, dil, ATT_HEADS, ATT_HEAD_DIM))
            in_specs.append(pl.BlockSpec((bb, ATT_SPAN, None, ATT_HEADS, ATT_HEAD_DIM),
                                         lambda i: (i, 0, 0, 0, 0)))
    outs = pl.pallas_call(
        _attn_sample_kernel,
        grid=(bd // bb,),
        in_specs=in_specs,
        out_specs=[pl.BlockSpec((bb, 1, ATT_GROUP_W), lambda i: (i, 0, 0))] * 3
                  + [pl.BlockSpec((bb, 1, LANES), lambda i: (i, 0, 0))] * 3,
        out_shape=[jax.ShapeDtypeStruct((bd, 1, ATT_GROUP_W), F32)] * 3
                  + [jax.ShapeDtypeStruct((bd, 1, LANES), F32)] * 3,
        compiler_params=_cparams(("parallel",)),
        name="attn_sample",
    )(*ins)
    return [o[:, 0, :] for o in outs[:3]], [l[:, 0, :] for l in outs[3:]]


def _mlstm_prompt_kernel(q_ref, k_ref, v_ref, om_ref, g_ref, bias_ref, *rest):
    n_cast = (len(rest) - 7) // 2
    h_ref, c_out, n_out, m_out = rest[n_cast:n_cast + 4]
    c_s, n_s, m_s = rest[2 * n_cast + 4:]
    for w_ref, wb_ref in zip(rest[:n_cast], rest[n_cast + 4:2 * n_cast + 4]):
        wb_ref[...] = w_ref[...].astype(wb_ref.dtype)
    hh = c_s.shape[0]
    ci = pl.program_id(2)
    L = q_ref.shape[0]

    @pl.when(ci == 0)
    def _():
        c_s[...] = jnp.zeros_like(c_s)
        n_s[...] = jnp.zeros_like(n_s)
        m_s[...] = jnp.zeros_like(m_s)

    gates = g_ref[...] + bias_ref[...]
    gates_t = gates.T
    lane = lax.broadcasted_iota(jnp.int32, (L, LANES), 1)
    sub = lax.broadcasted_iota(jnp.int32, (LANES, L), 0)
    ti = lax.broadcasted_iota(jnp.int32, (L, L), 0)
    si = lax.broadcasted_iota(jnp.int32, (L, L), 1)
    causal = si <= ti

    for j in range(hh):
        hd = pl.program_id(1) * hh + j
        qs = slice(j * M_QK_DIM, (j + 1) * M_QK_DIM)
        vs = slice(j * M_V_DIM, (j + 1) * M_V_DIM)
        li_col = jnp.sum(jnp.where(lane == hd, gates, 0.0), axis=1, keepdims=True)
        lf_col = _log_sigmoid(jnp.sum(jnp.where(lane == hd + M_HEADS, gates, 0.0), axis=1, keepdims=True))
        li_row = jnp.sum(jnp.where(sub == hd, gates_t, 0.0), axis=0, keepdims=True)
        lf_row = _log_sigmoid(jnp.sum(jnp.where(sub == hd + M_HEADS, gates_t, 0.0), axis=0, keepdims=True))

        b_col = jnp.sum(jnp.where(causal, lf_row, 0.0), axis=1, keepdims=True)
        b_row = jnp.sum(jnp.where(ti <= si, lf_col, 0.0), axis=0, keepdims=True)
        b_end = jnp.sum(lf_row, axis=1, keepdims=True)

        m_prev = m_s[j]
        dmat = jnp.where(causal, b_col - b_row + li_row, NEG_INF)
        inter = b_col + m_prev
        mt = jnp.maximum(inter, jnp.max(dmat, axis=1, keepdims=True))

        qb = q_ref[:, qs]
        q = qb.astype(F32)
        k = k_ref[:, qs].astype(F32) * M_K_SCALE
        vb = v_ref[:, vs]
        qk = lax.dot_general(qb, k.astype(BF16), (((1,), (1,)), ((), ())), preferred_element_type=F32)
        a = jnp.exp(dmat - mt) * qk
        w_inter = jnp.exp(inter - mt)
        num = (jnp.dot(a.astype(BF16), vb, preferred_element_type=F32)
               + w_inter * jnp.dot(qb, c_s[j].astype(BF16), preferred_element_type=F32))
        den = jnp.sum(a, axis=1, keepdims=True) + w_inter * jnp.sum(q * n_s[j], axis=1, keepdims=True)
        h = num / jnp.maximum(jnp.abs(den), jnp.exp(-mt))
        h_ref[:, vs] = (jax.nn.sigmoid(om_ref[:, vs].astype(F32)) * h).astype(h_ref.dtype)

        g_col = b_end - b_col + li_col
        g_row = b_end - b_row + li_row
        m_new = jnp.maximum(b_end + m_prev, jnp.max(g_row, axis=1, keepdims=True))
        decay = jnp.exp(b_end + m_prev - m_new)
        kw = jnp.exp(g_col - m_new) * k
        c_s[j] = decay * c_s[j] + jnp.dot(kw.T.astype(BF16), vb, preferred_element_type=F32)
        n_s[j] = decay * n_s[j] + jnp.sum(kw, axis=0, keepdims=True)
        m_s[j] = m_new

    @pl.when(ci == pl.num_programs(2) - 1)
    def _():
        c_out[...] = c_s[...]
        n_out[...] = n_s[...]
        m_out[...] = jnp.broadcast_to(m_s[...], m_out.shape)


def _mlstm_prompt(zif, zb, gate_bias, batch, seq, to_cast=(), *, chunk, hh):
    assert seq % chunk == 0 and M_HEADS % hh == 0
    nc = seq // chunk
    nhg = M_HEADS // hh
    n_steps = batch * nhg * nc
    zif3 = zif.reshape(batch, seq, LANES)
    zb3 = zb.reshape(batch, seq, NB)
    wq, wv = hh * M_QK_DIM, hh * M_V_DIM
    qk_blk = lambda col: pl.BlockSpec((None, chunk, wq), lambda b, h, c: (b, c, col // wq + h))
    v_blk = lambda col: pl.BlockSpec((None, chunk, wv), lambda b, h, c: (b, c, col // wv + h))
    cast_specs, cast_shapes = [], []
    for w in to_cast:
        rows, cols = w.shape
        assert rows % (n_steps * BF16_SUBLANES) == 0
        cast_specs.append(pl.BlockSpec((rows // n_steps, cols), lambda b, h, c: ((b * nhg + h) * nc + c, 0)))
        cast_shapes.append(jax.ShapeDtypeStruct((rows, cols), BF16))
    hm, c1, n1, m1, *cast = pl.pallas_call(
        _mlstm_prompt_kernel,
        grid=(batch, nhg, nc),
        in_specs=[qk_blk(B_QM), qk_blk(B_KM), v_blk(B_VM), v_blk(B_OM),
                  pl.BlockSpec((None, chunk, LANES), lambda b, h, c: (b, c, 0)),
                  pl.BlockSpec((1, LANES), lambda b, h, c: (0, 0))] + cast_specs,
        out_specs=[pl.BlockSpec((None, chunk, wv), lambda b, h, c: (b, c, h)),
                   pl.BlockSpec((None, hh, M_QK_DIM, M_V_DIM), lambda b, h, c: (b, h, 0, 0)),
                   pl.BlockSpec((None, hh, 1, M_QK_DIM), lambda b, h, c: (b, h, 0, 0)),
                   pl.BlockSpec((None, hh, 1, LANES), lambda b, h, c: (b, h, 0, 0))] + cast_specs,
        out_shape=[jax.ShapeDtypeStruct((batch, seq, M_V_WIDTH), BF16),
                   jax.ShapeDtypeStruct((batch, M_HEADS, M_QK_DIM, M_V_DIM), F32),
                   jax.ShapeDtypeStruct((batch, M_HEADS, 1, M_QK_DIM), F32),
                   jax.ShapeDtypeStruct((batch, M_HEADS, 1, LANES), F32)] + cast_shapes,
        scratch_shapes=[pltpu.VMEM((hh, M_QK_DIM, M_V_DIM), F32), pltpu.VMEM((hh, 1, M_QK_DIM), F32),
                        pltpu.VMEM((hh, 1, 1), F32)],
        compiler_params=_cparams(("parallel", "parallel", "arbitrary")),
        name="mlstm_prompt",
    )(zb3, zb3, zb3, zb3, zif3, gate_bias, *to_cast)
    return hm.reshape(batch * seq, M_V_WIDTH), c1, n1[:, :, 0, :], m1[:, :, 0, 0], cast


def _rows_to_cols(rows):
    rep = LANES // len(rows)
    n = rows[0].shape[1]
    t = jnp.concatenate([jnp.broadcast_to(r, (rep, n)) for r in rows], axis=0).T
    return [t[:, i * rep:i * rep + 1] for i in range(len(rows))]


def _mlstm_sample_kernel(zif_ref, zb_ref, bias_ref, c_ref, n_ref, m_ref, h_ref, c_out, n_out, m_out):
    for b in range(zif_ref.shape[0]):
        _mlstm_sample_one(zif_ref.at[b], zb_ref.at[b], bias_ref, c_ref.at[b], n_ref.at[b], m_ref.at[b],
                          h_ref.at[b], c_out.at[b], n_out.at[b], m_out.at[b])


def _mlstm_sample_one(zif_ref, zb_ref, bias_ref, c_ref, n_ref, m_ref, h_ref, c_out, n_out, m_out):
    gates = zif_ref[...] + bias_ref[...]
    lane = lax.broadcasted_iota(jnp.int32, (1, LANES), 1)
    m_row = jnp.zeros((1, LANES), F32)
    qs, wks, inters, m_news = [], [], [], []
    for h in range(M_HEADS):
        k = zb_ref[:, B_KM + h * M_QK_DIM:B_KM + (h + 1) * M_QK_DIM].astype(F32) * M_K_SCALE
        li = gates[:, h:h + 1]
        lf = _log_sigmoid(gates[:, M_HEADS + h:M_HEADS + h + 1])
        inter = lf + m_ref[:, h:h + 1]
        m_new = jnp.maximum(inter, li)
        qs.append(zb_ref[:, B_QM + h * M_QK_DIM:B_QM + (h + 1) * M_QK_DIM].astype(F32))
        wks.append(jnp.exp(li - m_new) * k)
        inters.append(inter)
        m_news.append(m_new)
    cols = _rows_to_cols(qs + wks)
    for h in range(M_HEADS):
        q, wk, inter, m_new = qs[h], wks[h], inters[h], m_news[h]
        v = zb_ref[:, B_VM + h * M_V_DIM:B_VM + (h + 1) * M_V_DIM].astype(F32)
        om = zb_ref[:, B_OM + h * M_V_DIM:B_OM + (h + 1) * M_V_DIM].astype(F32)
        c0 = c_ref[h]
        n0 = n_ref[h:h + 1, :]
        a = jnp.sum(q * wk, axis=1, keepdims=True)
        w_inter = jnp.exp(inter - m_new)
        q_c = jnp.sum(cols[h] * c0, axis=0, keepdims=True)
        num = a * v + w_inter * q_c
        den = a + w_inter * jnp.sum(q * n0, axis=1, keepdims=True)
        hv = num / jnp.maximum(jnp.abs(den), jnp.exp(-m_new))
        h_ref[:, h * M_V_DIM:(h + 1) * M_V_DIM] = (jax.nn.sigmoid(om) * hv).astype(h_ref.dtype)
        c_out[h] = w_inter * c0 + cols[M_HEADS + h] * v
        n_out[h:h + 1, :] = w_inter * n0 + wk
        m_row = jnp.where(lane == h, m_new, m_row)
    m_out[...] = m_row


def _mlstm_sample(zif_s, zb_s, gate_bias, c0, n0, m0, *, bb):
    bd = zif_s.shape[0]
    assert bd % bb == 0
    hm, c1, n1, m1 = pl.pallas_call(
        _mlstm_sample_kernel,
        grid=(bd // bb,),
        in_specs=[pl.BlockSpec((bb, 1, LANES), lambda b: (b, 0, 0)),
                  pl.BlockSpec((bb, 1, NB), lambda b: (b, 0, 0)),
                  pl.BlockSpec((1, LANES), lambda b: (0, 0)),
                  pl.BlockSpec((bb, M_HEADS, M_QK_DIM, M_V_DIM), lambda b: (b, 0, 0, 0)),
                  pl.BlockSpec((bb, M_HEADS, M_QK_DIM), lambda b: (b, 0, 0)),
                  pl.BlockSpec((bb, 1, M_HEADS), lambda b: (b, 0, 0))],
        out_specs=[pl.BlockSpec((bb, 1, M_V_WIDTH), lambda b: (b, 0, 0)),
                   pl.BlockSpec((bb, M_HEADS, M_QK_DIM, M_V_DIM), lambda b: (b, 0, 0, 0)),
                   pl.BlockSpec((bb, M_HEADS, M_QK_DIM), lambda b: (b, 0, 0)),
                   pl.BlockSpec((bb, 1, LANES), lambda b: (b, 0, 0))],
        out_shape=[jax.ShapeDtypeStruct((bd, 1, M_V_WIDTH), BF16),
                   jax.ShapeDtypeStruct((bd, M_HEADS, M_QK_DIM, M_V_DIM), F32),
                   jax.ShapeDtypeStruct((bd, M_HEADS, M_QK_DIM), F32),
                   jax.ShapeDtypeStruct((bd, 1, LANES), F32)],
        compiler_params=_cparams(("parallel",)),
        name="mlstm_sample",
    )(zif_s.reshape(bd, 1, LANES), zb_s.reshape(bd, 1, NB), gate_bias, c0, n0, m0.reshape(bd, 1, M_HEADS))
    return hm.reshape(bd, M_V_WIDTH), c1, n1, m1[:, 0, :M_HEADS]


def _merge_kernel(o1, o2, o3, l1, l2, l3, hm_ref, ga_ref, gb_ref, x_ref, wpa_ref, wpm_ref, wo_ref, g2_ref,
                  *rest, hws):
    n_cast = (len(rest) - 2) // 2
    x1_ref, h2_ref = rest[n_cast:n_cast + 2]
    for w_ref, wb_ref in zip(rest[:n_cast], rest[n_cast + 2:]):
        wb_ref[...] = w_ref[...].astype(wb_ref.dtype)
    parts = []
    for n in range(ATT_HEADS):
        sl = slice(n * ATT_HEAD_DIM, (n + 1) * ATT_HEAD_DIM)
        ls = []
        for l_ref, hw in zip((l1, l2, l3), hws):
            ln = (n // hw) * LANES + n % hw
            ls.append(l_ref[:, ln:ln + 1])
        mx = jnp.maximum(jnp.maximum(ls[0], ls[1]), ls[2])
        es = [jnp.exp(l - mx) for l in ls]
        tot = es[0] + es[1] + es[2]
        parts.append((es[0] / tot) * o1[:, sl] + (es[1] / tot) * o2[:, sl] + (es[2] / tot) * o3[:, sl])
    att = jnp.concatenate(parts, axis=1).astype(BF16)
    pa = jnp.dot(att, wpa_ref[...], preferred_element_type=F32)
    pm = jnp.dot(hm_ref[...], wpm_ref[...], preferred_element_type=F32)
    merged = (jax.nn.sigmoid(ga_ref[...].astype(F32)) * pa + jax.nn.sigmoid(gb_ref[...].astype(F32)) * pm)
    x1 = x_ref[...] + jnp.dot(merged.astype(BF16), wo_ref[...], preferred_element_type=F32)
    x1_ref[...] = x1
    ms = jnp.mean(x1 * x1, axis=-1, keepdims=True)
    h2_ref[...] = (x1 * lax.rsqrt(ms + RMS_EPS) * g2_ref[...]).astype(BF16)


def _merge(o_list, lse_list, hm, zt, x, wpa, wpm, wo, g2, to_cast=(), *, tm):
    m, d = x.shape
    assert m % tm == 0 and zt.shape[1] == NT
    n_steps = m // tm
    row_blk = lambda w: pl.BlockSpec((tm, w), lambda i: (i, 0))
    const = lambda shape: pl.BlockSpec(shape, lambda i: (0, 0), pipeline_mode=pl.Buffered(1))
    hws = tuple(ATT_HEADS * LANES // l.shape[1] for l in lse_list)
    cast_specs, cast_shapes = [], []
    for w in to_cast:
        rows, cols = w.shape
        assert rows % (n_steps * BF16_SUBLANES) == 0
        cast_specs.append(pl.BlockSpec((rows // n_steps, cols), lambda i: (i, 0)))
        cast_shapes.append(jax.ShapeDtypeStruct((rows, cols), BF16))
    x1, h2, *cast = pl.pallas_call(
        functools.partial(_merge_kernel, hws=hws),
        grid=(n_steps,),
        in_specs=[row_blk(ATT_GROUP_W)] * 3 + [row_blk(l.shape[1]) for l in lse_list]
                 + [row_blk(M_V_WIDTH),
                    pl.BlockSpec((tm, d), lambda i: (i, T_GA // d)),
                    pl.BlockSpec((tm, d), lambda i: (i, T_GB // d)),
                    row_blk(d),
                    const(wpa.shape), const(wpm.shape), const(wo.shape), const((1, d))] + cast_specs,
        out_specs=[row_blk(d), row_blk(d)] + cast_specs,
        out_shape=[jax.ShapeDtypeStruct((m, d), F32), jax.ShapeDtypeStruct((m, d), BF16)] + cast_shapes,
        compiler_params=_cparams(("parallel",)),
        name="merge",
    )(*o_list, *lse_list, hm, zt, zt, x, wpa, wpm, wo, g2, *to_cast)
    return x1, h2, cast


def _ffn_body(h2_ref, wu_ref, wg_ref, cw_ref, cb_ref, wd_ref, x1_ref, y_ref, g_prev2, g_prev1):
    j = pl.program_id(1)
    h2 = h2_ref[...]
    u = jnp.dot(h2, wu_ref[...], preferred_element_type=F32)
    g = jnp.dot(h2, wg_ref[...], preferred_element_type=F32)
    gconv = cb_ref[...] + ((g_prev2(g) * cw_ref[0:1, :] + g_prev1(g) * cw_ref[1:2, :]) + g * cw_ref[2:3, :])
    act = (jax.nn.gelu(gconv) * u).astype(BF16)
    down = jnp.dot(act, wd_ref[...], preferred_element_type=F32)

    @pl.when(j == 0)
    def _():
        y_ref[...] = x1_ref[...] + down

    @pl.when(j > 0)
    def _():
        y_ref[...] += down

    return g


def _ffn_prompt_kernel(h2_ref, wu_ref, wg_ref, cw_ref, cb_ref, wd_ref, x1_ref, y_ref, tail_ref, prev_s,
                       u0_s, g0_s, u1_s, g1_s, *, tiles_per_seq, nf):
    i = pl.program_id(0)
    j = pl.program_id(1)
    tm = h2_ref.shape[0]
    slots = ((u0_s, g0_s), (u1_s, g1_s))

    def up(u_s, g_s):
        h2 = h2_ref[...]
        u_s[...] = jnp.dot(h2, wu_ref[...], preferred_element_type=F32)
        g_s[...] = jnp.dot(h2, wg_ref[...], preferred_element_type=F32)

    def down(u_s, g_s):
        jt = j - 1
        prev = jnp.where(i % tiles_per_seq == 0, 0.0, prev_s[jt])
        p2, p1 = prev[SUBLANES - 2:SUBLANES - 1, :], prev[SUBLANES - 1:SUBLANES, :]
        g = g_s[...]
        r = lax.broadcasted_iota(jnp.int32, g.shape, 0)
        g_m1 = jnp.where(r == 0, p1, pltpu.roll(g, 1, axis=0))
        g_m2 = jnp.where(r == 0, p2, jnp.where(r == 1, p1, pltpu.roll(g, 2, axis=0)))
        cw = cw_ref[jt]
        gconv = cb_ref[jt] + ((g_m2 * cw[0:1, :] + g_m1 * cw[1:2, :]) + g * cw[2:3, :])
        act = (jax.nn.gelu(gconv) * u_s[...]).astype(BF16)
        y_ref[...] += jnp.dot(act, wd_ref[...], preferred_element_type=F32)
        prev_s[jt] = g[tm - SUBLANES:tm, :]

    @pl.when(j == 0)
    def _():
        y_ref[...] = x1_ref[...]
        up(*slots[0])

    for par in range(2):
        @pl.when((j > 0) & (j < nf) & (j % 2 == par))
        def _(par=par):
            up(*slots[par])
            down(*slots[1 - par])

    @pl.when(j == nf)
    def _():
        down(*slots[(nf - 1) % 2])
        tail_ref[...] = prev_s[...]


def _ffn_sample_kernel(h2_ref, wu_ref, wg_ref, cw_ref, cb_ref, wd_ref, x1_ref, b2_ref, b1_ref, y_ref, g_ref):
    g = _ffn_body(h2_ref, wu_ref, wg_ref, cw_ref, cb_ref, wd_ref, x1_ref, y_ref,
                  lambda g: b2_ref[...], lambda g: b1_ref[...])
    g_ref[...] = g


def _ffn_specs(tm, tf, d, nf):
    return [pl.BlockSpec((tm, d), lambda i, j: (i, 0)),
            pl.BlockSpec((d, tf), lambda i, j: (0, j)),
            pl.BlockSpec((d, tf), lambda i, j: (0, nf + j)),
            pl.BlockSpec((CONV_W, tf), lambda i, j: (0, j)),
            pl.BlockSpec((1, tf), lambda i, j: (0, j)),
            pl.BlockSpec((tf, d), lambda i, j: (j, 0)),
            pl.BlockSpec((tm, d), lambda i, j: (i, 0))]


def _ffn_prompt(h2, x1, w_in, conv_w, conv_b, w_down, seq, *, tm, tf):
    m, d = x1.shape
    assert m % tm == 0 and seq % tm == 0 and D_FF % tf == 0 and tm % 8 == 0
    nf = D_FF // tf
    kern = functools.partial(_ffn_prompt_kernel, tiles_per_seq=seq // tm, nf=nf)
    up_t = lambda j: jnp.minimum(j, nf - 1)
    dn_t = lambda j: jnp.maximum(j - 1, 0)
    y, tails = pl.pallas_call(
        kern,
        grid=(m // tm, nf + 1),
        in_specs=[pl.BlockSpec((tm, d), lambda i, j: (i, 0)),
                  pl.BlockSpec((d, tf), lambda i, j: (0, up_t(j))),
                  pl.BlockSpec((d, tf), lambda i, j: (0, nf + up_t(j))),
                  pl.BlockSpec((nf, CONV_W, tf), lambda i, j: (0, 0, 0)),
                  pl.BlockSpec((nf, 1, tf), lambda i, j: (0, 0, 0)),
                  pl.BlockSpec((tf, d), lambda i, j: (dn_t(j), 0)),
                  pl.BlockSpec((tm, d), lambda i, j: (i, 0))],
        out_specs=[pl.BlockSpec((tm, d), lambda i, j: (i, 0)),
                   pl.BlockSpec((None, nf, SUBLANES, tf), lambda i, j: (i, 0, 0, 0))],
        out_shape=[jax.ShapeDtypeStruct((m, d), F32), jax.ShapeDtypeStruct((m // tm, nf, SUBLANES, tf), F32)],
        scratch_shapes=[pltpu.VMEM((nf, SUBLANES, tf), F32)] + [pltpu.VMEM((tm, tf), F32)] * 4,
        compiler_params=_cparams(("arbitrary", "arbitrary")),
        name="ffn_prompt",
    )(h2, w_in, w_in, conv_w.reshape(CONV_W, nf, tf).swapaxes(0, 1), conv_b.reshape(nf, 1, tf), w_down, x1)
    return y, tails.swapaxes(1, 2).reshape(m // tm, SUBLANES, D_FF)


def _ffn_sample(h2, x1, w_in, conv_w, conv_b, w_down, conv_buf, *, tf):
    m, d = x1.shape
    nf = D_FF // tf
    buf2d = conv_buf.reshape(m, (CONV_W - 1) * D_FF)
    return pl.pallas_call(
        _ffn_sample_kernel,
        grid=(1, nf),
        in_specs=_ffn_specs(m, tf, d, nf) + [pl.BlockSpec((m, tf), lambda i, j: (0, j)),
                                             pl.BlockSpec((m, tf), lambda i, j: (0, nf + j))],
        out_specs=[pl.BlockSpec((m, d), lambda i, j: (i, 0)), pl.BlockSpec((m, tf), lambda i, j: (0, j))],
        out_shape=[jax.ShapeDtypeStruct((m, d), F32), jax.ShapeDtypeStruct((m, D_FF), F32)],
        compiler_params=_cparams(("arbitrary", "arbitrary")),
        name="ffn_sample",
    )(h2, w_in, w_in, conv_w, conv_b, w_down, x1, buf2d, buf2d)


IN_PROJ_TM = 2048
IN_PROJ_TN = 512
RMSNORM_TS = 1024
MLSTM_CHUNK = 256
MLSTM_HH = 4
MERGE_TM = 256
ATTN_SAMPLE_BB = 8
MLSTM_SAMPLE_BB = 4
FFN_TM = 512
FFN_TF = 512
FFN_SAMPLE_TF = D_FF // 4


def _split_heads_kernel(*refs):
    n = len(refs) // 2
    for src, dst in zip(refs[:n], refs[n:]):
        for h in range(ATT_HEADS):
            dst[:, h, :] = src[:, h * ATT_HEAD_DIM:(h + 1) * ATT_HEAD_DIM]


def _split_heads(za3, col_blocks, row0, rows, *, tr):
    batch = za3.shape[0]
    assert rows % tr == 0 and row0 % tr == 0
    out_spec = pl.BlockSpec((None, tr, ATT_HEADS, ATT_HEAD_DIM), lambda b, r: (b, r, 0, 0))
    return pl.pallas_call(
        _split_heads_kernel,
        grid=(batch, rows // tr),
        in_specs=[pl.BlockSpec((None, tr, ATT_GROUP_W), lambda b, r, c=c: (b, row0 // tr + r, c)) for c in col_blocks],
        out_specs=[out_spec] * len(col_blocks),
        out_shape=[jax.ShapeDtypeStruct((batch, rows, ATT_HEADS, ATT_HEAD_DIM), F32)] * len(col_blocks),
        compiler_params=_cparams(("parallel", "parallel")),
        name="split_heads",
    )(*([za3] * len(col_blocks)))


def _qk_gain_row(q_norm, k_norm):
    reps = N_GROUPS * ATT_HEADS
    return jnp.concatenate([jnp.tile(q_norm, reps), jnp.tile(k_norm, reps), jnp.zeros((NA - A_VA,), F32)])[None, :]


ATTN_PROMPT_TILING = ((4, 4, 1), (1, 4, 2), (1, 2, 4))


def _layer(x_prompt, x_sample, caches, norm_mix, w_in, q_norm, k_norm, b_igate, b_fgate, w_proj_att,
           w_proj_mlstm, w_out, norm_ffn, w_ffn_in, conv_w, conv_b, w_ffn_down):
    batch, seq, d = x_prompt.shape
    bd = x_sample.shape[0]
    assert x_sample.shape[1] == 1 and d == D_MODEL
    (ck1, cv1, ck2, cv2, ck3, cv3, st_c, st_n, st_m, st_conv) = caches

    wt_in = jnp.swapaxes(w_in, 0, 1)
    qk_gain = _qk_gain_row(q_norm, k_norm)
    g1 = norm_mix[None, :]
    g2 = norm_ffn[None, :]
    gate_bias = jnp.concatenate([b_igate, b_fgate, jnp.zeros((LANES - 2 * M_HEADS,), F32)])[None, :]
    cb = conv_b[None, :]

    xp = x_prompt.reshape(batch * seq, d)
    xs = x_sample.reshape(bd, d)

    h_all = _rmsnorm(xp, xs, g1, tm=IN_PROJ_TM, ts=RMSNORM_TS)
    (za_p, zb_p, zt_p, zif_p), (za_s, zb_s, zt_s, zif_s) = _in_proj(h_all, wt_in, qk_gain, tn=IN_PROJ_TN, ms=bd)
    o_p, lse_p = zip(*[_attn_prompt_group(za_p, batch, seq, gi, dil, nsub=nsub, hw=hw, rb=rb)
                       for gi, ((_, dil), (nsub, hw, rb)) in enumerate(zip(DIL_PATTERNS, ATTN_PROMPT_TILING))])
    hm_p, p_c, p_n, p_m, (wpa, wpm, wo) = _mlstm_prompt(zif_p, zb_p, gate_bias, batch, seq,
                                                        (w_proj_att, w_proj_mlstm, w_out), chunk=MLSTM_CHUNK,
                                                        hh=MLSTM_HH)
    x1_p, h2_p, (w_ff, w_dn) = _merge(o_p, lse_p, hm_p, zt_p, xp, wpa, wpm, wo, g2, (w_ffn_in, w_ffn_down),
                                      tm=MERGE_TM)
    y_p, tails = _ffn_prompt(h2_p, x1_p, w_ff, conv_w, cb, w_dn, seq, tm=FFN_TM, tf=FFN_TF)

    za_p3 = za_p.reshape(batch, seq, NA)
    p_kv = []
    for gi, (win, _) in enumerate(DIL_PATTERNS):
        keep = min(win, seq)
        cols = [(col + gi * ATT_GROUP_W) // ATT_GROUP_W for col in (A_KA, A_VA)]
        p_kv += _split_heads(za_p3, cols, seq - keep, keep, tr=min(keep, 512))
    tiles_per_seq = seq // FFN_TM
    p_conv = tails.reshape(batch, tiles_per_seq, SUBLANES, D_FF)[:, -1, SUBLANES - (CONV_W - 1):, :]

    o_s, lse_s = _attn_sample(za_s, [(ck1, cv1), (ck2, cv2), (ck3, cv3)], bb=ATTN_SAMPLE_BB)
    hm_s, s_c, s_n, s_m = _mlstm_sample(zif_s, zb_s, gate_bias, st_c, st_n, st_m, bb=MLSTM_SAMPLE_BB)
    x1_s, h2_s, _ = _merge(o_s, lse_s, hm_s, zt_s, xs, wpa, wpm, wo, g2, tm=bd)
    y_s, g_s = _ffn_sample(h2_s, x1_s, w_ff, conv_w, cb, w_dn, st_conv, tf=FFN_SAMPLE_TF)

    cols = [(col + gi * ATT_GROUP_W) // ATT_GROUP_W for gi in range(N_GROUPS) for col in (A_KA, A_VA)]
    s_kv = [a.reshape(bd, 1, ATT_HEADS, ATT_HEAD_DIM) for a in _split_heads(za_s[None], cols, 0, bd, tr=bd)]
    s_conv = jnp.stack([st_conv[:, 1, :], g_s], axis=1)

    p_state = p_kv + [p_c, p_n, p_m, p_conv]
    s_state = s_kv + [s_c, s_n, s_m, s_conv]
    return y_p.reshape(batch, seq, d), y_s.reshape(bd, 1, d), p_state, s_state


def kernel(x_prompt, x_sample, cache_k_w128, cache_v_w128, cache_k_w512, cache_v_w512, cache_k_w2048,
           cache_v_w2048, state_mlstm_C, state_mlstm_n, state_mlstm_m, state_ffn_conv, norm_mix, w_in, q_norm,
           k_norm, b_igate, b_fgate, w_proj_att, w_proj_mlstm, w_out, norm_ffn, w_ffn_in, conv_w, conv_b,
           w_ffn_down):
    assert norm_mix.shape[0] == 1
    caches = [c[0] for c in (cache_k_w128, cache_v_w128, cache_k_w512, cache_v_w512, cache_k_w2048,
                             cache_v_w2048, state_mlstm_C, state_mlstm_n, state_mlstm_m, state_ffn_conv)]
    weights = [w[0] for w in (norm_mix, w_in, q_norm, k_norm, b_igate, b_fgate, w_proj_att, w_proj_mlstm,
                              w_out, norm_ffn, w_ffn_in, conv_w, conv_b, w_ffn_down)]
    y_p, y_s, p_state, s_state = _layer(x_prompt, x_sample, caches, *weights)
    return (y_p, y_s, *[a[None] for a in p_state], *[a[None] for a in s_state])
```

```python
import functools

import jax
import jax.numpy as jnp
from jax import lax
from jax.experimental import pallas as pl
from jax.experimental.pallas import tpu as pltpu

F32 = jnp.float32
BF16 = jnp.bfloat16

RMS_EPS = 1e-6
NEG_INF = -1e30
LANES = 128
SUBLANES = 8
BF16_SUBLANES = 16
VMEM_LIMIT = 56 * 1024 * 1024

D_MODEL = 2048
DIL_PATTERNS = ((128, 1), (512, 4), (2048, 16))
N_GROUPS = 3
ATT_HEADS = 4
ATT_HEAD_DIM = 128
ATT_SPAN = 128
ATT_SCALE = ATT_HEAD_DIM ** -0.5
ATT_GROUP_W = ATT_HEADS * ATT_HEAD_DIM
ATT_WIDTH = N_GROUPS * ATT_GROUP_W
M_HEADS = 4
M_QK_DIM = D_MODEL // (2 * M_HEADS)
M_V_DIM = D_MODEL // M_HEADS
M_QK_WIDTH = M_HEADS * M_QK_DIM
M_V_WIDTH = M_HEADS * M_V_DIM
M_K_SCALE = M_QK_DIM ** -0.5
D_FF = ((8 * D_MODEL // 3 + 255) // 256) * 256
CONV_W = 3
SPLIT_SIZES = (ATT_WIDTH, ATT_WIDTH, ATT_WIDTH, M_QK_WIDTH, M_QK_WIDTH, M_V_WIDTH, M_V_WIDTH,
               M_HEADS, M_HEADS, D_MODEL, D_MODEL)

IN_COLS = sum(SPLIT_SIZES)
A_QA = 0
A_KA = A_QA + ATT_WIDTH
A_VA = A_KA + ATT_WIDTH
NA = A_VA + ATT_WIDTH
B_QM = 0
B_KM = B_QM + M_QK_WIDTH
B_VM = B_KM + M_QK_WIDTH
B_OM = B_VM + M_V_WIDTH
NB = B_OM + M_V_WIDTH
N_IF = 2 * M_HEADS
T_GA = 0
T_GB = T_GA + D_MODEL
NT = T_GB + D_MODEL
assert NA + NB + N_IF + NT == IN_COLS


def _cparams(sem):
    return pltpu.CompilerParams(dimension_semantics=sem, vmem_limit_bytes=VMEM_LIMIT)


def _log_sigmoid(x):
    return jnp.minimum(x, 0.0) - jnp.log(1.0 + jnp.exp(-jnp.abs(x)))


def _rmsnorm_kernel(x_ref, xs_ref, g_ref, h_ref, *, nsub):
    j = pl.program_id(1)
    ts = x_ref.shape[0]

    def norm(x):
        ms = jnp.mean(x * x, axis=-1, keepdims=True)
        return (x * lax.rsqrt(ms + RMS_EPS) * g_ref[...]).astype(h_ref.dtype)

    @pl.when(j < nsub)
    def _():
        h_ref[pl.ds(pl.multiple_of(j * ts, ts), ts), :] = norm(x_ref[...])

    @pl.when(j == nsub)
    def _():
        h_ref[nsub * ts:, :] = norm(xs_ref[...])


def _rmsnorm(x, xs, gain, *, tm, ts):
    m, d = x.shape
    ms = xs.shape[0]
    assert m % tm == 0 and tm % ts == 0
    nsub = tm // ts
    last = m // ts - 1
    return pl.pallas_call(
        functools.partial(_rmsnorm_kernel, nsub=nsub),
        grid=(m // tm, nsub + 1),
        in_specs=[pl.BlockSpec((ts, d), lambda i, j: (jnp.minimum(i * nsub + j, last), 0)),
                  pl.BlockSpec((ms, d), lambda i, j: (0, 0)),
                  pl.BlockSpec((1, d), lambda i, j: (0, 0))],
        out_specs=pl.BlockSpec((None, tm + ms, d), lambda i, j: (i, 0, 0)),
        out_shape=jax.ShapeDtypeStruct((m // tm, tm + ms, d), BF16),
        compiler_params=_cparams(("parallel", "arbitrary")),
        name="rmsnorm",
    )(x, xs, gain)


def _in_proj_kernel(h_ref, w_ref, wn_ref, wif_ref, qkg_ref, za_ref, zb_ref, zt_ref, zif_ref,
                    sa_ref, sb_ref, st_ref, sif_ref, *, n_norm, na, nb, nt):
    j = pl.program_id(1)
    tp = za_ref.shape[0]

    def z_of(w):
        return lax.dot_general(h_ref[...], w.astype(BF16), (((1,), (1,)), ((), ())), preferred_element_type=F32)

    @pl.when(j < n_norm)
    def _():
        z = z_of(w_ref[...])
        for c in range(za_ref.shape[1] // LANES):
            sl = slice(c * LANES, (c + 1) * LANES)
            zc = z[:, sl]
            ms = jnp.mean(zc * zc, axis=-1, keepdims=True)
            zn = zc * lax.rsqrt(ms + RMS_EPS) * qkg_ref[:, sl]
            za_ref[:, sl] = zn[:tp]
            sa_ref[j, :, sl] = zn[tp:]

    @pl.when((j >= n_norm) & (j < na))
    def _():
        z = z_of(w_ref[...])
        za_ref[...] = z[:tp]
        sa_ref[j] = z[tp:]

    @pl.when((j >= na) & (j < na + nb))
    def _():
        z = z_of(w_ref[...])
        zb_ref[...] = z[:tp].astype(zb_ref.dtype)
        sb_ref[j - na] = z[tp:]

    @pl.when((j >= na + nb) & (j < na + nb + nt))
    def _():
        z = z_of(jnp.concatenate([w_ref[N_IF:, :], wn_ref[...]], axis=0))
        zt_ref[...] = z[:tp].astype(zt_ref.dtype)
        st_ref[j - na - nb] = z[tp:]

    @pl.when(j == na + nb + nt)
    def _():
        z = z_of(jnp.concatenate([wif_ref[...], jnp.zeros((LANES - N_IF, wif_ref.shape[1]), F32)], axis=0))
        zif_ref[...] = z[:tp]
        sif_ref[...] = z[tp:]


def _in_proj(h, wt, qk_gain, *, tn, ms):
    ni, th, d = h.shape
    tm = th - ms
    m = ni * tm
    assert wt.shape == (IN_COLS, d) and N_IF == 8
    assert NA % tn == 0 and NB % tn == 0 and NT % tn == 0 and A_VA % tn == 0 and tn % N_IF == 0
    na, nb, nt = NA // tn, NB // tn, NT // tn
    nw = na + nb + nt
    g_if = (NA + NB) // N_IF
    g_tn = tn // N_IF
    kern = functools.partial(_in_proj_kernel, n_norm=A_VA // tn, na=na, nb=nb, nt=nt)
    col_a = lambda j: jnp.minimum(j, na - 1)
    col_b = lambda j: jnp.clip(j - na, 0, nb - 1)
    col_t = lambda j: jnp.clip(j - na - nb, 0, nt - 1)
    out_specs = [pl.BlockSpec((None, tm, tn), lambda i, j: (i, 0, col_a(j))),
                 pl.BlockSpec((None, tm, tn), lambda i, j: (i, 0, col_b(j))),
                 pl.BlockSpec((None, tm, tn), lambda i, j: (i, 0, col_t(j))),
                 pl.BlockSpec((None, tm, LANES), lambda i, j: (i, 0, 0))]
    out_shape = [jax.ShapeDtypeStruct((ni, tm, w), dt)
                 for w, dt in zip((NA, NB, NT, LANES), (F32, BF16, BF16, F32))]
    for n_tiles in (na, nb, nt):
        out_specs.append(pl.BlockSpec((None, n_tiles, ms, tn), lambda i, j: (i, 0, 0, 0)))
        out_shape.append(jax.ShapeDtypeStruct((ni, n_tiles, ms, tn), F32))
    out_specs.append(pl.BlockSpec((None, ms, LANES), lambda i, j: (i, 0, 0)))
    out_shape.append(jax.ShapeDtypeStruct((ni, ms, LANES), F32))
    outs = pl.pallas_call(
        kern,
        grid=(ni, nw + 1),
        in_specs=[
            pl.BlockSpec((None, th, d), lambda i, j: (i, 0, 0)),
            pl.BlockSpec((tn, d), lambda i, j: (jnp.minimum(j, nw - 1), 0)),
            pl.BlockSpec((N_IF, d), lambda i, j: (g_if + g_tn * (jnp.clip(j, na + nb, nw - 1) - (na + nb) + 1), 0)),
            pl.BlockSpec((N_IF, d), lambda i, j: (g_if, 0)),
            pl.BlockSpec((1, tn), lambda i, j: (0, col_a(j))),
        ],
        out_specs=out_specs,
        out_shape=out_shape,
        compiler_params=_cparams(("parallel", "arbitrary")),
        name="in_proj",
    )(h, wt, wt, wt, qk_gain)
    sample = [o[0].swapaxes(0, 1).reshape(ms, -1) for o in outs[4:7]] + [outs[7][0]]
    return [o.reshape(m, o.shape[2]) for o in outs[:4]], sample


def _attn_prompt_kernel(*refs, dil, nsub, hw, rb):
    q_refs, kp_refs, kc_refs, vp_refs, vc_refs = [refs[t * hw:(t + 1) * hw] for t in range(5)]
    o_ref, lse_ref = refs[5 * hw:5 * hw + 2]
    o_refs = refs[5 * hw + 2:]
    c = pl.program_id(1)
    blk = ATT_SPAN
    step = blk * dil
    qi = lax.broadcasted_iota(jnp.int32, (blk, 2 * blk), 0)
    kj = lax.broadcasted_iota(jnp.int32, (blk, 2 * blk), 1)
    band = (kj >= qi) & (kj <= qi + blk)
    first_bias = jnp.where((kj < blk) & (c == 0), NEG_INF, 0.0)
    lane = lax.broadcasted_iota(jnp.int32, (blk, LANES), 1)

    def rows(base, r):
        return pl.ds(base + r, blk) if dil == 1 else pl.ds(base + r, blk, stride=dil)

    def group(r0, s):
        base = s * step
        ids = [(r0 + r, n) for r in range(rb) for n in range(hw)]

        def stacked(cur_refs, prev_refs):
            parts = []
            for r, n in ids:
                cur = cur_refs[n][rows(base, r), :]
                if prev_refs is None:
                    parts.append(cur)
                else:
                    prv = prev_refs[n][rows(0, r), :] if s == 0 else cur_refs[n][rows(base - step, r), :]
                    parts.append(jnp.concatenate([prv, cur], axis=0))
            return jnp.stack(parts).astype(BF16)

        q = stacked(q_refs, None)
        k = stacked(kc_refs, kp_refs)
        v = stacked(vc_refs, vp_refs)
        sc = jnp.einsum('bqe,bke->bqk', q, k, preferred_element_type=F32) * ATT_SCALE
        if s == 0:
            sc = sc + first_bias
        sc = jnp.where(band, sc, NEG_INF)
        m = jnp.max(sc, axis=-1, keepdims=True)
        p = jnp.exp(sc - m)
        l = jnp.sum(p, axis=-1, keepdims=True)
        o = jnp.einsum('bqk,bke->bqe', p.astype(BF16), v, preferred_element_type=F32) / l
        lse = m + jnp.log(l)
        for r in range(rb):
            lse_tile = jnp.zeros((blk, LANES), F32)
            for n in range(hw):
                b = r * hw + n
                o_refs[n][rows(base, r0 + r), :] = o[b]
                lse_tile = jnp.where(lane == n, lse[b], lse_tile)
            lse_ref[rows(base, r0 + r), :] = lse_tile

    for s in range(nsub):
        if dil == rb:
            group(0, s)
        else:
            def body(it, carry, s=s):
                group(it * rb, s)
                return carry
            lax.fori_loop(0, dil // rb, body, 0)

    for n in range(hw):
        o_ref[:, n * ATT_HEAD_DIM:(n + 1) * ATT_HEAD_DIM] = o_refs[n][...]


def _attn_prompt_group(za, batch, seq, gi, dil, *, nsub, hw, rb):
    step = ATT_SPAN * dil
    tc = nsub * step
    assert seq % tc == 0 and ATT_HEADS % hw == 0 and dil % rb == 0
    nh = ATT_HEADS // hw
    e = ATT_HEAD_DIM
    z3 = za.reshape(batch, seq, NA)
    qc, kc, vc = [(col + gi * ATT_GROUP_W) // e for col in (A_QA, A_KA, A_VA)]

    def cur(col):
        return [pl.BlockSpec((None, tc, e), lambda b, c, h, n=n: (b, c, col + h * hw + n)) for n in range(hw)]

    def prev(col):
        return [pl.BlockSpec((None, step, e), lambda b, c, h, n=n: (b, jnp.maximum(c * nsub - 1, 0), col + h * hw + n))
                for n in range(hw)]

    kern = functools.partial(_attn_prompt_kernel, dil=dil, nsub=nsub, hw=hw, rb=rb)
    o, lse = pl.pallas_call(
        kern,
        grid=(batch, seq // tc, nh),
        in_specs=cur(qc) + prev(kc) + cur(kc) + prev(vc) + cur(vc),
        out_specs=[pl.BlockSpec((None, tc, hw * e), lambda b, c, h: (b, c, h)),
                   pl.BlockSpec((None, tc, LANES), lambda b, c, h: (b, c, h))],
        out_shape=[jax.ShapeDtypeStruct((batch, seq, ATT_GROUP_W), F32),
                   jax.ShapeDtypeStruct((batch, seq, nh * LANES), F32)],
        scratch_shapes=[pltpu.VMEM((tc, e), F32)] * hw,
        compiler_params=_cparams(("parallel", "parallel", "parallel")),
        name=f"attn_prompt_g{gi}",
    )(*([z3] * (5 * hw)))
    return o.reshape(batch * seq, ATT_GROUP_W), lse.reshape(batch * seq, nh * LANES)


def _attn_sample_kernel(z_ref, k1, v1, k2, v2, k3, v3, o1, o2, o3, l1, l2, l3):
    bb = z_ref.shape[0]
    bufs = ((k1, v1, o1, l1), (k2, v2, o2, l2), (k3, v3, o3, l3))
    lane = lax.broadcasted_iota(jnp.int32, (1, LANES), 1)

    def heads(b, col):
        return jnp.concatenate([z_ref[b, :, col + n * ATT_HEAD_DIM:col + (n + 1) * ATT_HEAD_DIM]
                                for n in range(ATT_HEADS)], axis=0)

    def body(b, carry):
        for gi, (k_ref, v_ref, o_ref, l_ref) in enumerate(bufs):
            q = heads(b, A_QA + gi * ATT_GROUP_W)
            k_new = heads(b, A_KA + gi * ATT_GROUP_W)
            v_new = heads(b, A_VA + gi * ATT_GROUP_W)
            kb = k_ref[b].reshape(ATT_SPAN // 2, 2 * ATT_HEADS, ATT_HEAD_DIM)
            vb = v_ref[b].reshape(ATT_SPAN // 2, 2 * ATT_HEADS, ATT_HEAD_DIM)
            q2 = jnp.concatenate([q, q], axis=0)
            s = jnp.sum(kb * q2[None], axis=-1, keepdims=True) * ATT_SCALE
            s_new = jnp.sum(k_new * q, axis=-1, keepdims=True) * ATT_SCALE
            m2 = jnp.max(s, axis=0)
            m = jnp.maximum(jnp.maximum(m2[:ATT_HEADS], m2[ATT_HEADS:]), s_new)
            p = jnp.exp(s - jnp.concatenate([m, m], axis=0)[None])
            p_new = jnp.exp(s_new - m)
            l2 = jnp.sum(p, axis=0)
            l = l2[:ATT_HEADS] + l2[ATT_HEADS:] + p_new
            o2 = jnp.sum(p * vb, axis=0)
            o = (o2[:ATT_HEADS] + o2[ATT_HEADS:] + p_new * v_new) / l
            lse = m + jnp.log(l)
            lse_row = jnp.zeros((1, LANES), F32)
            for n in range(ATT_HEADS):
                o_ref[b, :, n * ATT_HEAD_DIM:(n + 1) * ATT_HEAD_DIM] = o[n:n + 1, :]
                lse_row = jnp.where(lane == n, lse[n:n + 1, :], lse_row)
            l_ref[b] = lse_row
        return carry

    lax.fori_loop(0, bb, body, 0)


def _attn_sample(za_s, caches, *, bb):
    bd = za_s.shape[0]
    assert bd % bb == 0
    ins, in_specs = [za_s.reshape(bd, 1, NA)], [pl.BlockSpec((bb, 1, NA), lambda i: (i, 0, 0))]
    for (k_buf, v_buf), (win, dil) in zip(caches, DIL_PATTERNS):
        assert k_buf.shape[1:] == (ATT_SPAN * dil, ATT_HEADS, ATT_HEAD_DIM)
        for buf in (k_buf, v_buf):
            ins.append(buf.reshape(bd, ATT_SPAN, dil, ATT_HEADS, ATT_HEAD_DIM))
            in_specs.append(pl.BlockSpec((bb, ATT_SPAN, None, ATT_HEADS, ATT_HEAD_DIM),
                                         lambda i: (i, 0, 0, 0, 0)))
    outs = pl.pallas_call(
        _attn_sample_kernel,
        grid=(bd // bb,),
        in_specs=in_specs,
        out_specs=[pl.BlockSpec((bb, 1, ATT_GROUP_W), lambda i: (i, 0, 0))] * 3
                  + [pl.BlockSpec((bb, 1, LANES), lambda i: (i, 0, 0))] * 3,
        out_shape=[jax.ShapeDtypeStruct((bd, 1, ATT_GROUP_W), F32)] * 3
                  + [jax.ShapeDtypeStruct((bd, 1, LANES), F32)] * 3,
        compiler_params=_cparams(("parallel",)),
        name="attn_sample",
    )(*ins)
    return [o[:, 0, :] for o in outs[:3]], [l[:, 0, :] for l in outs[3:]]


def _mlstm_prompt_kernel(q_ref, k_ref, v_ref, om_ref, g_ref, bias_ref, *rest):
    n_cast = (len(rest) - 7) // 2
    h_ref, c_out, n_out, m_out = rest[n_cast:n_cast + 4]
    c_s, n_s, m_s = rest[2 * n_cast + 4:]
    for w_ref, wb_ref in zip(rest[:n_cast], rest[n_cast + 4:2 * n_cast + 4]):
        wb_ref[...] = w_ref[...].astype(wb_ref.dtype)
    hh = c_s.shape[0]
    ci = pl.program_id(2)
    L = q_ref.shape[0]

    @pl.when(ci == 0)
    def _():
        c_s[...] = jnp.zeros_like(c_s)
        n_s[...] = jnp.zeros_like(n_s)
        m_s[...] = jnp.zeros_like(m_s)

    gates = g_ref[...] + bias_ref[...]
    gates_t = gates.T
    lane = lax.broadcasted_iota(jnp.int32, (L, LANES), 1)
    sub = lax.broadcasted_iota(jnp.int32, (LANES, L), 0)
    ti = lax.broadcasted_iota(jnp.int32, (L, L), 0)
    si = lax.broadcasted_iota(jnp.int32, (L, L), 1)
    causal = si <= ti

    for j in range(hh):
        hd = pl.program_id(1) * hh + j
        qs = slice(j * M_QK_DIM, (j + 1) * M_QK_DIM)
        vs = slice(j * M_V_DIM, (j + 1) * M_V_DIM)
        li_col = jnp.sum(jnp.where(lane == hd, gates, 0.0), axis=1, keepdims=True)
        lf_col = _log_sigmoid(jnp.sum(jnp.where(lane == hd + M_HEADS, gates, 0.0), axis=1, keepdims=True))
        li_row = jnp.sum(jnp.where(sub == hd, gates_t, 0.0), axis=0, keepdims=True)
        lf_row = _log_sigmoid(jnp.sum(jnp.where(sub == hd + M_HEADS, gates_t, 0.0), axis=0, keepdims=True))

        b_col = jnp.sum(jnp.where(causal, lf_row, 0.0), axis=1, keepdims=True)
        b_row = jnp.sum(jnp.where(ti <= si, lf_col, 0.0), axis=0, keepdims=True)
        b_end = jnp.sum(lf_row, axis=1, keepdims=True)

        m_prev = m_s[j]
        dmat = jnp.where(causal, b_col - b_row + li_row, NEG_INF)
        inter = b_col + m_prev
        mt = jnp.maximum(inter, jnp.max(dmat, axis=1, keepdims=True))

        qb = q_ref[:, qs]
        q = qb.astype(F32)
        k = k_ref[:, qs].astype(F32) * M_K_SCALE
        vb = v_ref[:, vs]
        qk = lax.dot_general(qb, k.astype(BF16), (((1,), (1,)), ((), ())), preferred_element_type=F32)
        a = jnp.exp(dmat - mt) * qk
        w_inter = jnp.exp(inter - mt)
        num = (jnp.dot(a.astype(BF16), vb, preferred_element_type=F32)
               + w_inter * jnp.dot(qb, c_s[j].astype(BF16), preferred_element_type=F32))
        den = jnp.sum(a, axis=1, keepdims=True) + w_inter * jnp.sum(q * n_s[j], axis=1, keepdims=True)
        h = num / jnp.maximum(jnp.abs(den), jnp.exp(-mt))
        h_ref[:, vs] = (jax.nn.sigmoid(om_ref[:, vs].astype(F32)) * h).astype(h_ref.dtype)

        g_col = b_end - b_col + li_col
        g_row = b_end - b_row + li_row
        m_new = jnp.maximum(b_end + m_prev, jnp.max(g_row, axis=1, keepdims=True))
        decay = jnp.exp(b_end + m_prev - m_new)
        kw = jnp.exp(g_col - m_new) * k
        c_s[j] = decay * c_s[j] + jnp.dot(kw.T.astype(BF16), vb, preferred_element_type=F32)
        n_s[j] = decay * n_s[j] + jnp.sum(kw, axis=0, keepdims=True)
        m_s[j] = m_new

    @pl.when(ci == pl.num_programs(2) - 1)
    def _():
        c_out[...] = c_s[...]
        n_out[...] = n_s[...]
        m_out[...] = jnp.broadcast_to(m_s[...], m_out.shape)


def _mlstm_prompt(zif, zb, gate_bias, batch, seq, to_cast=(), *, chunk, hh):
    assert seq % chunk == 0 and M_HEADS % hh == 0
    nc = seq // chunk
    nhg = M_HEADS // hh
    n_steps = batch * nhg * nc
    zif3 = zif.reshape(batch, seq, LANES)
    zb3 = zb.reshape(batch, seq, NB)
    wq, wv = hh * M_QK_DIM, hh * M_V_DIM
    qk_blk = lambda col: pl.BlockSpec((None, chunk, wq), lambda b, h, c: (b, c, col // wq + h))
    v_blk = lambda col: pl.BlockSpec((None, chunk, wv), lambda b, h, c: (b, c, col // wv + h))
    cast_specs, cast_shapes = [], []
    for w in to_cast:
        rows, cols = w.shape
        assert rows % (n_steps * BF16_SUBLANES) == 0
        cast_specs.append(pl.BlockSpec((rows // n_steps, cols), lambda b, h, c: ((b * nhg + h) * nc + c, 0)))
        cast_shapes.append(jax.ShapeDtypeStruct((rows, cols), BF16))
    hm, c1, n1, m1, *cast = pl.pallas_call(
        _mlstm_prompt_kernel,
        grid=(batch, nhg, nc),
        in_specs=[qk_blk(B_QM), qk_blk(B_KM), v_blk(B_VM), v_blk(B_OM),
                  pl.BlockSpec((None, chunk, LANES), lambda b, h, c: (b, c, 0)),
                  pl.BlockSpec((1, LANES), lambda b, h, c: (0, 0))] + cast_specs,
        out_specs=[pl.BlockSpec((None, chunk, wv), lambda b, h, c: (b, c, h)),
                   pl.BlockSpec((None, hh, M_QK_DIM, M_V_DIM), lambda b, h, c: (b, h, 0, 0)),
                   pl.BlockSpec((None, hh, 1, M_QK_DIM), lambda b, h, c: (b, h, 0, 0)),
                   pl.BlockSpec((None, hh, 1, LANES), lambda b, h, c: (b, h, 0, 0))] + cast_specs,
        out_shape=[jax.ShapeDtypeStruct((batch, seq, M_V_WIDTH), BF16),
                   jax.ShapeDtypeStruct((batch, M_HEADS, M_QK_DIM, M_V_DIM), F32),
                   jax.ShapeDtypeStruct((batch, M_HEADS, 1, M_QK_DIM), F32),
                   jax.ShapeDtypeStruct((batch, M_HEADS, 1, LANES), F32)] + cast_shapes,
        scratch_shapes=[pltpu.VMEM((hh, M_QK_DIM, M_V_DIM), F32), pltpu.VMEM((hh, 1, M_QK_DIM), F32),
                        pltpu.VMEM((hh, 1, 1), F32)],
        compiler_params=_cparams(("parallel", "parallel", "arbitrary")),
        name="mlstm_prompt",
    )(zb3, zb3, zb3, zb3, zif3, gate_bias, *to_cast)
    return hm.reshape(batch * seq, M_V_WIDTH), c1, n1[:, :, 0, :], m1[:, :, 0, 0], cast


def _rows_to_cols(rows):
    rep = LANES // len(rows)
    n = rows[0].shape[1]
    t = jnp.concatenate([jnp.broadcast_to(r, (rep, n)) for r in rows], axis=0).T
    return [t[:, i * rep:i * rep + 1] for i in range(len(rows))]


def _mlstm_sample_kernel(zif_ref, zb_ref, bias_ref, c_ref, n_ref, m_ref, h_ref, c_out, n_out, m_out):
    for b in range(zif_ref.shape[0]):
        _mlstm_sample_one(zif_ref.at[b], zb_ref.at[b], bias_ref, c_ref.at[b], n_ref.at[b], m_ref.at[b],
                          h_ref.at[b], c_out.at[b], n_out.at[b], m_out.at[b])


def _mlstm_sample_one(zif_ref, zb_ref, bias_ref, c_ref, n_ref, m_ref, h_ref, c_out, n_out, m_out):
    gates = zif_ref[...] + bias_ref[...]
    lane = lax.broadcasted_iota(jnp.int32, (1, LANES), 1)
    m_row = jnp.zeros((1, LANES), F32)
    qs, wks, inters, m_news = [], [], [], []
    for h in range(M_HEADS):
        k = zb_ref[:, B_KM + h * M_QK_DIM:B_KM + (h + 1) * M_QK_DIM].astype(F32) * M_K_SCALE
        li = gates[:, h:h + 1]
        lf = _log_sigmoid(gates[:, M_HEADS + h:M_HEADS + h + 1])
        inter = lf + m_ref[:, h:h + 1]
        m_new = jnp.maximum(inter, li)
        qs.append(zb_ref[:, B_QM + h * M_QK_DIM:B_QM + (h + 1) * M_QK_DIM].astype(F32))
        wks.append(jnp.exp(li - m_new) * k)
        inters.append(inter)
        m_news.append(m_new)
    cols = _rows_to_cols(qs + wks)
    for h in range(M_HEADS):
        q, wk, inter, m_new = qs[h], wks[h], inters[h], m_news[h]
        v = zb_ref[:, B_VM + h * M_V_DIM:B_VM + (h + 1) * M_V_DIM].astype(F32)
        om = zb_ref[:, B_OM + h * M_V_DIM:B_OM + (h + 1) * M_V_DIM].astype(F32)
        c0 = c_ref[h]
        n0 = n_ref[h:h + 1, :]
        a = jnp.sum(q * wk, axis=1, keepdims=True)
        w_inter = jnp.exp(inter - m_new)
        q_c = jnp.sum(cols[h] * c0, axis=0, keepdims=True)
        num = a * v + w_inter * q_c
        den = a + w_inter * jnp.sum(q * n0, axis=1, keepdims=True)
        hv = num / jnp.maximum(jnp.abs(den), jnp.exp(-m_new))
        h_ref[:, h * M_V_DIM:(h + 1) * M_V_DIM] = (jax.nn.sigmoid(om) * hv).astype(h_ref.dtype)
        c_out[h] = w_inter * c0 + cols[M_HEADS + h] * v
        n_out[h:h + 1, :] = w_inter * n0 + wk
        m_row = jnp.where(lane == h, m_new, m_row)
    m_out[...] = m_row


def _mlstm_sample(zif_s, zb_s, gate_bias, c0, n0, m0, *, bb):
    bd = zif_s.shape[0]
    assert bd % bb == 0
    hm, c1, n1, m1 = pl.pallas_call(
        _mlstm_sample_kernel,
        grid=(bd // bb,),
        in_specs=[pl.BlockSpec((bb, 1, LANES), lambda b: (b, 0, 0)),
                  pl.BlockSpec((bb, 1, NB), lambda b: (b, 0, 0)),
                  pl.BlockSpec((1, LANES), lambda b: (0, 0)),
                  pl.BlockSpec((bb, M_HEADS, M_QK_DIM, M_V_DIM), lambda b: (b, 0, 0, 0)),
                  pl.BlockSpec((bb, M_HEADS, M_QK_DIM), lambda b: (b, 0, 0)),
                  pl.BlockSpec((bb, 1, M_HEADS), lambda b: (b, 0, 0))],
        out_specs=[pl.BlockSpec((bb, 1, M_V_WIDTH), lambda b: (b, 0, 0)),
                   pl.BlockSpec((bb, M_HEADS, M_QK_DIM, M_V_DIM), lambda b: (b, 0, 0, 0)),
                   pl.BlockSpec((bb, M_HEADS, M_QK_DIM), lambda b: (b, 0, 0)),
                   pl.BlockSpec((bb, 1, LANES), lambda b: (b, 0, 0))],
        out_shape=[jax.ShapeDtypeStruct((bd, 1, M_V_WIDTH), BF16),
                   jax.ShapeDtypeStruct((bd, M_HEADS, M_QK_DIM, M_V_DIM), F32),
                   jax.ShapeDtypeStruct((bd, M_HEADS, M_QK_DIM), F32),
                   jax.ShapeDtypeStruct((bd, 1, LANES), F32)],
        compiler_params=_cparams(("parallel",)),
        name="mlstm_sample",
    )(zif_s.reshape(bd, 1, LANES), zb_s.reshape(bd, 1, NB), gate_bias, c0, n0, m0.reshape(bd, 1, M_HEADS))
    return hm.reshape(bd, M_V_WIDTH), c1, n1, m1[:, 0, :M_HEADS]


def _merge_kernel(o1, o2, o3, l1, l2, l3, hm_ref, ga_ref, gb_ref, x_ref, wpa_ref, wpm_ref, wo_ref, g2_ref,
                  *rest, hws):
    n_cast = (len(rest) - 3) // 2
    x1_ref, h2_ref = rest[n_cast:n_cast + 2]
    merged_s = rest[-1]
    for w_ref, wb_ref in zip(rest[:n_cast], rest[n_cast + 2:-1]):
        wb_ref[...] = w_ref[...].astype(wb_ref.dtype)
    parts = []
    for n in range(ATT_HEADS):
        sl = slice(n * ATT_HEAD_DIM, (n + 1) * ATT_HEAD_DIM)
        ls = []
        for l_ref, hw in zip((l1, l2, l3), hws):
            ln = (n // hw) * LANES + n % hw
            ls.append(l_ref[:, ln:ln + 1])
        mx = jnp.maximum(jnp.maximum(ls[0], ls[1]), ls[2])
        es = [jnp.exp(l - mx) for l in ls]
        tot = es[0] + es[1] + es[2]
        parts.append((es[0] / tot) * o1[:, sl] + (es[1] / tot) * o2[:, sl] + (es[2] / tot) * o3[:, sl])
    att = jnp.concatenate(parts, axis=1).astype(BF16)
    hm = hm_ref[...]
    for c in range(x_ref.shape[1] // MERGE_TC):
        cs = slice(c * MERGE_TC, (c + 1) * MERGE_TC)
        pa = jnp.dot(att, wpa_ref[:, cs], preferred_element_type=F32)
        pm = jnp.dot(hm, wpm_ref[:, cs], preferred_element_type=F32)
        merged_s[:, cs] = (jax.nn.sigmoid(ga_ref[:, cs].astype(F32)) * pa
                           + jax.nn.sigmoid(gb_ref[:, cs].astype(F32)) * pm).astype(BF16)
    x1 = x_ref[...] + jnp.dot(merged_s[...], wo_ref[...], preferred_element_type=F32)
    x1_ref[...] = x1
    ms = jnp.mean(x1 * x1, axis=-1, keepdims=True)
    h2_ref[...] = (x1 * lax.rsqrt(ms + RMS_EPS) * g2_ref[...]).astype(BF16)


def _merge(o_list, lse_list, hm, zt, x, wpa, wpm, wo, g2, to_cast=(), *, tm):
    m, d = x.shape
    assert m % tm == 0 and zt.shape[1] == NT
    n_steps = m // tm
    row_blk = lambda w: pl.BlockSpec((tm, w), lambda i: (i, 0))
    const = lambda shape: pl.BlockSpec(shape, lambda i: (0, 0), pipeline_mode=pl.Buffered(1))
    hws = tuple(ATT_HEADS * LANES // l.shape[1] for l in lse_list)
    cast_specs, cast_shapes = [], []
    for w in to_cast:
        rows, cols = w.shape
        assert rows % (n_steps * BF16_SUBLANES) == 0
        cast_specs.append(pl.BlockSpec((rows // n_steps, cols), lambda i: (i, 0)))
        cast_shapes.append(jax.ShapeDtypeStruct((rows, cols), BF16))
    x1, h2, *cast = pl.pallas_call(
        functools.partial(_merge_kernel, hws=hws),
        grid=(n_steps,),
        in_specs=[row_blk(ATT_GROUP_W)] * 3 + [row_blk(l.shape[1]) for l in lse_list]
                 + [row_blk(M_V_WIDTH),
                    pl.BlockSpec((tm, d), lambda i: (i, T_GA // d)),
                    pl.BlockSpec((tm, d), lambda i: (i, T_GB // d)),
                    row_blk(d),
                    const(wpa.shape), const(wpm.shape), const(wo.shape), const((1, d))] + cast_specs,
        out_specs=[row_blk(d), row_blk(d)] + cast_specs,
        out_shape=[jax.ShapeDtypeStruct((m, d), F32), jax.ShapeDtypeStruct((m, d), BF16)] + cast_shapes,
        scratch_shapes=[pltpu.VMEM((tm, d), BF16)],
        compiler_params=_cparams(("parallel",)),
        name="merge",
    )(*o_list, *lse_list, hm, zt, zt, x, wpa, wpm, wo, g2, *to_cast)
    return x1, h2, cast


def _ffn_body(h2_ref, wu_ref, wg_ref, cw_ref, cb_ref, wd_ref, x1_ref, y_ref, g_prev2, g_prev1):
    j = pl.program_id(1)
    h2 = h2_ref[...]
    u = jnp.dot(h2, wu_ref[...], preferred_element_type=F32)
    g = jnp.dot(h2, wg_ref[...], preferred_element_type=F32)
    gconv = cb_ref[...] + ((g_prev2(g) * cw_ref[0:1, :] + g_prev1(g) * cw_ref[1:2, :]) + g * cw_ref[2:3, :])
    act = (jax.nn.gelu(gconv) * u).astype(BF16)
    down = jnp.dot(act, wd_ref[...], preferred_element_type=F32)

    @pl.when(j == 0)
    def _():
        y_ref[...] = x1_ref[...] + down

    @pl.when(j > 0)
    def _():
        y_ref[...] += down

    return g


def _ffn_prompt_kernel(h2_ref, wu_ref, wg_ref, cw_ref, cb_ref, wd_ref, x1_ref, y_ref, tail_ref, prev_s,
                       u0_s, g0_s, u1_s, g1_s, *, tiles_per_seq, nf):
    i = pl.program_id(0)
    j = pl.program_id(1)
    tm = h2_ref.shape[0]
    slots = ((u0_s, g0_s), (u1_s, g1_s))

    def up(u_s, g_s):
        h2 = h2_ref[...]
        u_s[...] = jnp.dot(h2, wu_ref[...], preferred_element_type=F32)
        g_s[...] = jnp.dot(h2, wg_ref[...], preferred_element_type=F32)

    def down(u_s, g_s):
        jt = j - 1
        prev = jnp.where(i % tiles_per_seq == 0, 0.0, prev_s[jt])
        p2, p1 = prev[SUBLANES - 2:SUBLANES - 1, :], prev[SUBLANES - 1:SUBLANES, :]
        g = g_s[...]
        r = lax.broadcasted_iota(jnp.int32, g.shape, 0)
        g_m1 = jnp.where(r == 0, p1, pltpu.roll(g, 1, axis=0))
        g_m2 = jnp.where(r == 0, p2, jnp.where(r == 1, p1, pltpu.roll(g, 2, axis=0)))
        cw = cw_ref[jt]
        gconv = cb_ref[jt] + ((g_m2 * cw[0:1, :] + g_m1 * cw[1:2, :]) + g * cw[2:3, :])
        act = (jax.nn.gelu(gconv) * u_s[...]).astype(BF16)
        y_ref[...] += jnp.dot(act, wd_ref[...], preferred_element_type=F32)
        prev_s[jt] = g[tm - SUBLANES:tm, :]

    @pl.when(j == 0)
    def _():
        y_ref[...] = x1_ref[...]
        up(*slots[0])

    for par in range(2):
        @pl.when((j > 0) & (j < nf) & (j % 2 == par))
        def _(par=par):
            up(*slots[par])
            down(*slots[1 - par])

    @pl.when(j == nf)
    def _():
        down(*slots[(nf - 1) % 2])
        tail_ref[...] = prev_s[...]


def _ffn_sample_kernel(h2_ref, wu_ref, wg_ref, cw_ref, cb_ref, wd_ref, x1_ref, b2_ref, b1_ref, y_ref, g_ref):
    g = _ffn_body(h2_ref, wu_ref, wg_ref, cw_ref, cb_ref, wd_ref, x1_ref, y_ref,
                  lambda g: b2_ref[...], lambda g: b1_ref[...])
    g_ref[...] = g


def _ffn_specs(tm, tf, d, nf):
    return [pl.BlockSpec((tm, d), lambda i, j: (i, 0)),
            pl.BlockSpec((d, tf), lambda i, j: (0, j)),
            pl.BlockSpec((d, tf), lambda i, j: (0, nf + j)),
            pl.BlockSpec((CONV_W, tf), lambda i, j: (0, j)),
            pl.BlockSpec((1, tf), lambda i, j: (0, j)),
            pl.BlockSpec((tf, d), lambda i, j: (j, 0)),
            pl.BlockSpec((tm, d), lambda i, j: (i, 0))]


def _ffn_prompt(h2, x1, w_in, conv_w, conv_b, w_down, seq, *, tm, tf):
    m, d = x1.shape
    assert m % tm == 0 and seq % tm == 0 and D_FF % tf == 0 and tm % 8 == 0
    nf = D_FF // tf
    kern = functools.partial(_ffn_prompt_kernel, tiles_per_seq=seq // tm, nf=nf)
    up_t = lambda j: jnp.minimum(j, nf - 1)
    dn_t = lambda j: jnp.maximum(j - 1, 0)
    y, tails = pl.pallas_call(
        kern,
        grid=(m // tm, nf + 1),
        in_specs=[pl.BlockSpec((tm, d), lambda i, j: (i, 0)),
                  pl.BlockSpec((d, tf), lambda i, j: (0, up_t(j))),
                  pl.BlockSpec((d, tf), lambda i, j: (0, nf + up_t(j))),
                  pl.BlockSpec((nf, CONV_W, tf), lambda i, j: (0, 0, 0)),
                  pl.BlockSpec((nf, 1, tf), lambda i, j: (0, 0, 0)),
                  pl.BlockSpec((tf, d), lambda i, j: (dn_t(j), 0)),
                  pl.BlockSpec((tm, d), lambda i, j: (i, 0))],
        out_specs=[pl.BlockSpec((tm, d), lambda i, j: (i, 0)),
                   pl.BlockSpec((None, nf, SUBLANES, tf), lambda i, j: (i, 0, 0, 0))],
        out_shape=[jax.ShapeDtypeStruct((m, d), F32), jax.ShapeDtypeStruct((m // tm, nf, SUBLANES, tf), F32)],
        scratch_shapes=[pltpu.VMEM((nf, SUBLANES, tf), F32)] + [pltpu.VMEM((tm, tf), F32)] * 4,
        compiler_params=_cparams(("arbitrary", "arbitrary")),
        name="ffn_prompt",
    )(h2, w_in, w_in, conv_w.reshape(CONV_W, nf, tf).swapaxes(0, 1), conv_b.reshape(nf, 1, tf), w_down, x1)
    return y, tails.swapaxes(1, 2).reshape(m // tm, SUBLANES, D_FF)


def _ffn_sample(h2, x1, w_in, conv_w, conv_b, w_down, conv_buf, *, tf):
    m, d = x1.shape
    nf = D_FF // tf
    buf2d = conv_buf.reshape(m, (CONV_W - 1) * D_FF)
    return pl.pallas_call(
        _ffn_sample_kernel,
        grid=(1, nf),
        in_specs=_ffn_specs(m, tf, d, nf) + [pl.BlockSpec((m, tf), lambda i, j: (0, j)),
                                             pl.BlockSpec((m, tf), lambda i, j: (0, nf + j))],
        out_specs=[pl.BlockSpec((m, d), lambda i, j: (i, 0)), pl.BlockSpec((m, tf), lambda i, j: (0, j))],
        out_shape=[jax.ShapeDtypeStruct((m, d), F32), jax.ShapeDtypeStruct((m, D_FF), F32)],
        compiler_params=_cparams(("arbitrary", "arbitrary")),
        name="ffn_sample",
    )(h2, w_in, w_in, conv_w, conv_b, w_down, x1, buf2d, buf2d)


IN_PROJ_TM = 2048
IN_PROJ_TN = 512
RMSNORM_TS = 1024
MLSTM_CHUNK = 256
MLSTM_HH = 4
MERGE_TM = 256
MERGE_TC = 512
ATTN_SAMPLE_BB = 8
MLSTM_SAMPLE_BB = 4
FFN_TM = 512
FFN_TF = 512
FFN_SAMPLE_TF = D_FF // 4


def _split_heads_kernel(*refs):
    n = len(refs) // 2
    for src, dst in zip(refs[:n], refs[n:]):
        for h in range(ATT_HEADS):
            dst[:, h, :] = src[:, h * ATT_HEAD_DIM:(h + 1) * ATT_HEAD_DIM]


def _split_heads(za3, col_blocks, row0, rows, *, tr):
    batch = za3.shape[0]
    assert rows % tr == 0 and row0 % tr == 0
    out_spec = pl.BlockSpec((None, tr, ATT_HEADS, ATT_HEAD_DIM), lambda b, r: (b, r, 0, 0))
    return pl.pallas_call(
        _split_heads_kernel,
        grid=(batch, rows // tr),
        in_specs=[pl.BlockSpec((None, tr, ATT_GROUP_W), lambda b, r, c=c: (b, row0 // tr + r, c)) for c in col_blocks],
        out_specs=[out_spec] * len(col_blocks),
        out_shape=[jax.ShapeDtypeStruct((batch, rows, ATT_HEADS, ATT_HEAD_DIM), F32)] * len(col_blocks),
        compiler_params=_cparams(("parallel", "parallel")),
        name="split_heads",
    )(*([za3] * len(col_blocks)))


def _qk_gain_row(q_norm, k_norm):
    reps = N_GROUPS * ATT_HEADS
    return jnp.concatenate([jnp.tile(q_norm, reps), jnp.tile(k_norm, reps), jnp.zeros((NA - A_VA,), F32)])[None, :]


ATTN_PROMPT_TILING = ((4, 4, 1), (1, 4, 2), (1, 2, 4))


def _layer(x_prompt, x_sample, caches, norm_mix, w_in, q_norm, k_norm, b_igate, b_fgate, w_proj_att,
           w_proj_mlstm, w_out, norm_ffn, w_ffn_in, conv_w, conv_b, w_ffn_down):
    batch, seq, d = x_prompt.shape
    bd = x_sample.shape[0]
    assert x_sample.shape[1] == 1 and d == D_MODEL
    (ck1, cv1, ck2, cv2, ck3, cv3, st_c, st_n, st_m, st_conv) = caches

    wt_in = jnp.swapaxes(w_in, 0, 1)
    qk_gain = _qk_gain_row(q_norm, k_norm)
    g1 = norm_mix[None, :]
    g2 = norm_ffn[None, :]
    gate_bias = jnp.concatenate([b_igate, b_fgate, jnp.zeros((LANES - 2 * M_HEADS,), F32)])[None, :]
    cb = conv_b[None, :]

    xp = x_prompt.reshape(batch * seq, d)
    xs = x_sample.reshape(bd, d)

    h_all = _rmsnorm(xp, xs, g1, tm=IN_PROJ_TM, ts=RMSNORM_TS)
    (za_p, zb_p, zt_p, zif_p), (za_s, zb_s, zt_s, zif_s) = _in_proj(h_all, wt_in, qk_gain, tn=IN_PROJ_TN, ms=bd)
    o_p, lse_p = zip(*[_attn_prompt_group(za_p, batch, seq, gi, dil, nsub=nsub, hw=hw, rb=rb)
                       for gi, ((_, dil), (nsub, hw, rb)) in enumerate(zip(DIL_PATTERNS, ATTN_PROMPT_TILING))])
    hm_p, p_c, p_n, p_m, (wpa, wpm, wo) = _mlstm_prompt(zif_p, zb_p, gate_bias, batch, seq,
                                                        (w_proj_att, w_proj_mlstm, w_out), chunk=MLSTM_CHUNK,
                                                        hh=MLSTM_HH)
    x1_p, h2_p, (w_ff, w_dn) = _merge(o_p, lse_p, hm_p, zt_p, xp, wpa, wpm, wo, g2, (w_ffn_in, w_ffn_down),
                                      tm=MERGE_TM)
    y_p, tails = _ffn_prompt(h2_p, x1_p, w_ff, conv_w, cb, w_dn, seq, tm=FFN_TM, tf=FFN_TF)

    za_p3 = za_p.reshape(batch, seq, NA)
    p_kv = []
    for gi, (win, _) in enumerate(DIL_PATTERNS):
        keep = min(win, seq)
        cols = [(col + gi * ATT_GROUP_W) // ATT_GROUP_W for col in (A_KA, A_VA)]
        p_kv += _split_heads(za_p3, cols, seq - keep, keep, tr=min(keep, 512))
    tiles_per_seq = seq // FFN_TM
    p_conv = tails.reshape(batch, tiles_per_seq, SUBLANES, D_FF)[:, -1, SUBLANES - (CONV_W - 1):, :]

    o_s, lse_s = _attn_sample(za_s, [(ck1, cv1), (ck2, cv2), (ck3, cv3)], bb=ATTN_SAMPLE_BB)
    hm_s, s_c, s_n, s_m = _mlstm_sample(zif_s, zb_s, gate_bias, st_c, st_n, st_m, bb=MLSTM_SAMPLE_BB)
    x1_s, h2_s, _ = _merge(o_s, lse_s, hm_s, zt_s, xs, wpa, wpm, wo, g2, tm=bd)
    y_s, g_s = _ffn_sample(h2_s, x1_s, w_ff, conv_w, cb, w_dn, st_conv, tf=FFN_SAMPLE_TF)

    cols = [(col + gi * ATT_GROUP_W) // ATT_GROUP_W for gi in range(N_GROUPS) for col in (A_KA, A_VA)]
    s_kv = [a.reshape(bd, 1, ATT_HEADS, ATT_HEAD_DIM) for a in _split_heads(za_s[None], cols, 0, bd, tr=bd)]
    s_conv = jnp.stack([st_conv[:, 1, :], g_s], axis=1)

    p_state = p_kv + [p_c, p_n, p_m, p_conv]
    s_state = s_kv + [s_c, s_n, s_m, s_conv]
    return y_p.reshape(batch, seq, d), y_s.reshape(bd, 1, d), p_state, s_state


def kernel(x_prompt, x_sample, cache_k_w128, cache_v_w128, cache_k_w512, cache_v_w512, cache_k_w2048,
           cache_v_w2048, state_mlstm_C, state_mlstm_n, state_mlstm_m, state_ffn_conv, norm_mix, w_in, q_norm,
           k_norm, b_igate, b_fgate, w_proj_att, w_proj_mlstm, w_out, norm_ffn, w_ffn_in, conv_w, conv_b,
           w_ffn_down):
    assert norm_mix.shape[0] == 1
    caches = [c[0] for c in (cache_k_w128, cache_v_w128, cache_k_w512, cache_v_w512, cache_k_w2048,
                             cache_v_w2048, state_mlstm_C, state_mlstm_n, state_mlstm_m, state_ffn_conv)]
    weights = [w[0] for w in (norm_mix, w_in, q_norm, k_norm, b_igate, b_fgate, w_proj_att, w_proj_mlstm,
                              w_out, norm_ffn, w_ffn_in, conv_w, conv_b, w_ffn_down)]
    y_p, y_s, p_state, s_state = _layer(x_prompt, x_sample, caches, *weights)
    return (y_p, y_s, *[a[None] for a in p_state], *[a[None] for a in s_state])
```
